```python
import math
import jax, jax.numpy as jnp
from jax import lax
import numpy as np

D_MODEL = 1024
BATCH = 32
SEQ = 256
DEPTH = 2
DEC_BATCH = 4
DEC_SEQ = 2048
PAST_LEN = 512

GRID_W = 64
GROUP_W = D_MODEL // 4
MIX_W = 4 * GROUP_W
A_HEADS = 4
A_DQK = GROUP_W // (2 * A_HEADS)
A_DV = GROUP_W // A_HEADS
ROPE_BASE = 10000.0
HY_CH = GROUP_W
HY_ORDER = 2
HY_BANDS = 8
HY_FEAT = 1 + 2 * HY_BANDS
HY_HID = 64
C_HEADS = 4
C_DH = GROUP_W // C_HEADS
NA_KH = 8
NA_KW = 16
NA_QCOLS = 16
POOL_WINDOWS = (2, 4, 8, 16)
POOL_GC = GROUP_W // 4
A_PROJ = 3 * GROUP_W
B_PROJ = 3 * HY_CH
C_PROJ = 3 * GROUP_W
D_PROJ = GROUP_W
PROJ_W = A_PROJ + B_PROJ + C_PROJ + D_PROJ
FFN_DENSE = 2816
N_EXPERTS = 8
TOP_K = 2
FFN_EXPERT = 3584
N_DENSE = (DEPTH + 1) // 2
N_MOE = DEPTH // 2
Q_BLOCK = 128
EPS = 1e-6
NEG = -1e30

kernel_name = 'hybrid_diff_hyena_na_pool_prefix_trunk'

f32 = jnp.float32


def rmsnorm(x, g):
    xf = x.astype(f32)
    y = xf * lax.rsqrt(jnp.mean(xf * xf, axis=-1, keepdims=True) + EPS)
    return (y * g.astype(f32)).astype(x.dtype)


def axial_rope(x):
    L, d = x.shape[1], x.shape[-1]
    t = jnp.arange(L)
    row = (t // GRID_W).astype(f32)
    col = (t % GRID_W).astype(f32)
    nf = d // 4
    inv = ROPE_BASE ** (-jnp.arange(nf, dtype=f32) / nf)

    def rot(xh, pos):
        ang = pos[:, None] * inv
        cos = jnp.cos(ang)[None, :, None, :]
        sin = jnp.sin(ang)[None, :, None, :]
        x1, x2 = xh[..., :nf], xh[..., nf:]
        return jnp.concatenate([x1 * cos - x2 * sin, x1 * sin + x2 * cos], axis=-1)

    xf = x.astype(f32)
    half = d // 2
    out = jnp.concatenate([rot(xf[..., :half], row), rot(xf[..., half:], col)], axis=-1)
    return out.astype(x.dtype)


def _to_blocks(a):
    B, L = a.shape[:2]
    return a.reshape((B, L // Q_BLOCK, Q_BLOCK) + a.shape[2:]).swapaxes(0, 1)


def _from_blocks(a):
    nb, B, qb = a.shape[:3]
    return a.swapaxes(0, 1).reshape((B, nb * qb) + a.shape[3:])


def diff_attention(q, k, v, lam):
    scale = A_DQK ** -0.5

    def block(qb):
        s = jnp.einsum('bqhmd,bkhmd->bhmqk', qb, k).astype(f32) * scale
        p = jax.nn.softmax(s, axis=-1)
        pd = p[:, :, 0] - lam * p[:, :, 1]
        return jnp.einsum('bhqk,bkhe->bqhe', pd.astype(v.dtype), v)

    return _from_blocks(lax.map(block, _to_blocks(q)))


def softmax_attention(q, k, v):
    scale = q.shape[-1] ** -0.5

    def block(qb):
        s = jnp.einsum('bqhd,bkhd->bhqk', qb, k).astype(f32) * scale
        p = jax.nn.softmax(s, axis=-1).astype(v.dtype)
        return jnp.einsum('bhqk,bkhd->bqhd', p, v)

    return _from_blocks(lax.map(block, _to_blocks(q)))


def na_attention(q, k, v, k_ctx, v_ctx, rpb):
    B, L, H, d = q.shape
    R = L // GRID_W
    kh = min(NA_KH, R)
    cb = GRID_W // NA_QCOLS
    band = NA_KW + NA_QCOLS
    rows = jnp.arange(R)
    key_rows = jnp.clip(rows - kh // 2, 0, R - kh)[:, None] + jnp.arange(kh)[None, :]
    qcol = jnp.arange(GRID_W).reshape(cb, NA_QCOLS)
    band_cols = (jnp.clip(jnp.arange(cb) * NA_QCOLS - NA_KW // 2, 0, GRID_W - band)[:, None]
                 + jnp.arange(band)[None, :])
    win_start = jnp.clip(qcol - NA_KW // 2, 0, GRID_W - NA_KW)
    bc = band_cols[:, None, :]
    valid = (bc >= win_start[..., None]) & (bc < win_start[..., None] + NA_KW)
    tok = key_rows[:, None, :, None] * GRID_W + band_cols[None, :, None, :]
    kg = k[:, tok]
    vg = v[:, tok]
    qg = q.reshape(B, R, cb, NA_QCOLS, H, d)
    scale = d ** -0.5
    s_loc = jnp.einsum('brjqhd,brjawhd->bhrjqaw', qg, kg).astype(f32) * scale
    dy = key_rows - rows[:, None] + NA_KH - 1
    dx = jnp.clip(bc - qcol[..., None] + NA_KW - 1, 0, 2 * NA_KW - 2)
    bias = rpb.astype(f32)[:, dy[:, None, None, :, None], dx[None, :, :, None, :]]
    s_loc = jnp.where(valid[:, :, None, :], s_loc + bias, NEG)
    s_ctx = jnp.einsum('brjqhd,bnhd->bhrjqn', qg, k_ctx).astype(f32) * scale
    nloc = kh * band
    s = jnp.concatenate([s_loc.reshape(s_loc.shape[:5] + (nloc,)), s_ctx], axis=-1)
    p = jax.nn.softmax(s, axis=-1).astype(v.dtype)
    p_loc = p[..., :nloc].reshape(s_loc.shape)
    p_ctx = p[..., nloc:]
    o = (jnp.einsum('bhrjqaw,brjawhd->brjqhd', p_loc, vg)
         + jnp.einsum('bhrjqn,bnhd->brjqhd', p_ctx, v_ctx))
    return o.reshape(B, L, H, d)


def short_conv3(u, w):
    up = jnp.pad(u, ((0, 0), (1, 1), (0, 0)))
    return up[:, :-2] * w[0] + up[:, 1:-1] * w[1] + up[:, 2:] * w[2]


def hyena_filters(L, w1, b1, w2, b2, w3, b3, freq, decay):
    t = (jnp.arange(L, dtype=f32) / L)[:, None]
    bands = jnp.arange(1, HY_BANDS + 1, dtype=f32)[None, :]
    ang = 2.0 * math.pi * bands * t
    feat = jnp.concatenate([t, jnp.sin(ang), jnp.cos(ang)], axis=-1)
    fr = freq.astype(f32)
    z = jnp.sin(fr[0] * (feat @ w1.astype(f32) + b1.astype(f32)))
    z = jnp.sin(fr[1] * (z @ w2.astype(f32) + b2.astype(f32)))
    z = (z @ w3.astype(f32) + b3.astype(f32)).reshape(L, HY_ORDER, 2, HY_CH)
    z = z * jnp.exp(-t[:, :, None, None] * jnp.abs(decay.astype(f32)))
    fwd = z[:, :, 0]
    bwd = z[:, :, 1]
    filt = jnp.concatenate([fwd, jnp.zeros((1, HY_ORDER, HY_CH), f32), bwd[:0:-1]], axis=0)
    return filt / jnp.sum(jnp.abs(filt), axis=0, keepdims=True)


def hyena(u, conv_w, w1, b1, w2, b2, w3, b3, freq, decay, dbias):
    L = u.shape[1]
    u = short_conv3(u, conv_w)
    x1, x2, v = jnp.split(u, 3, axis=-1)
    filt = hyena_filters(L, w1, b1, w2, b2, w3, b3, freq, decay)
    filt_f = jnp.fft.rfft(filt, axis=0)

    def longconv(z, o):
        zf = jnp.fft.rfft(z.astype(f32), n=2 * L, axis=1)
        y = jnp.fft.irfft(zf * filt_f[None, :, o, :], n=2 * L, axis=1)[:, :L]
        return y.astype(z.dtype) + dbias[o] * z

    z = x1 * longconv(v, 0)
    return x2 * longconv(z, 1)


def pool_mixer(u, w, scale):
    B, L, _ = u.shape
    uf = u.astype(f32)
    cs = jnp.concatenate([jnp.zeros((B, 1, GROUP_W), f32), jnp.cumsum(uf, axis=1)], axis=1)
    t = jnp.arange(L)
    outs = []
    for g, win in enumerate(POOL_WINDOWS):
        lo = jnp.clip(t - win // 2, 0, L)
        hi = jnp.clip(t - win // 2 + win, 0, L)
        sl = slice(g * POOL_GC, (g + 1) * POOL_GC)
        mean = (cs[:, hi, sl] - cs[:, lo, sl]) / (hi - lo).astype(f32)[None, :, None]
        outs.append(mean - uf[:, :, sl])
    pooled = jnp.stack(outs, axis=2)
    y = jnp.einsum('blgc,gce->blge', pooled, w.astype(f32)).reshape(B, L, GROUP_W)
    return (y * scale.astype(f32)).astype(u.dtype)


def token_mixers(h, lp, layer, ctx_kv):
    B, L, _ = h.shape
    proj = h @ lp['w_in']
    pa, pb, pc, pd = jnp.split(proj, [A_PROJ, A_PROJ + B_PROJ, A_PROJ + B_PROJ + C_PROJ], axis=-1)
    qa, ka, va = jnp.split(pa, 3, axis=-1)
    qa = qa.reshape(B, L, A_HEADS, 2, A_DQK)
    ka = ka.reshape(B, L, A_HEADS, 2, A_DQK)
    va = va.reshape(B, L, A_HEADS, A_DV)
    lam_init = 0.8 - 0.6 * math.exp(-0.3 * layer)
    lv = lp['diff_lam'].astype(f32)
    lam = jnp.exp(jnp.sum(lv[0] * lv[1])) - jnp.exp(jnp.sum(lv[2] * lv[3])) + lam_init
    qc, kc, vc = jnp.split(pc, 3, axis=-1)
    qc = qc.reshape(B, L, C_HEADS, C_DH)
    kc = kc.reshape(B, L, C_HEADS, C_DH)
    vc = vc.reshape(B, L, C_HEADS, C_DH)
    if ctx_kv is None:
        oa = diff_attention(qa, ka, va, lam)
        oc = softmax_attention(qc, kc, vc)
        new_kv = (ka.reshape(B, L, A_HEADS, 2 * A_DQK), va, kc, vc)
    else:
        cak, cav, cck, ccv = ctx_kv
        N = cak.shape[1]
        qa_r = axial_rope(qa.reshape(B, L, 2 * A_HEADS, A_DQK)).reshape(qa.shape)
        ka_r = axial_rope(ka.reshape(B, L, 2 * A_HEADS, A_DQK)).reshape(ka.shape)
        k_all = jnp.concatenate([ka_r, cak.reshape(B, N, A_HEADS, 2, A_DQK)], axis=1)
        v_all = jnp.concatenate([va, cav], axis=1)
        oa = diff_attention(qa_r, k_all, v_all, lam)
        oc = na_attention(qc, kc, vc, cck, ccv, lp['na_rpb'])
        new_kv = None
    oa = rmsnorm(oa, lp['diff_subln_g']) * (1.0 - lam_init)
    ob = hyena(pb, *lp['hy'])
    od = pool_mixer(pd, lp['pool_w'], lp['pool_scale'])
    mixed = jnp.concatenate([oa.reshape(B, L, GROUP_W), ob, oc.reshape(B, L, GROUP_W), od], axis=-1)
    return mixed @ lp['w_out'], new_kv


def swiglu(h, w1, w3, w2):
    return (jax.nn.silu(h @ w1) * (h @ w3)) @ w2


def moe_swiglu(h, router, w1, w3, w2):
    B, L, D = h.shape
    t = h.reshape(B * L, D)
    logits = (t @ router).astype(f32)
    top_v, top_i = lax.top_k(logits, TOP_K)
    gates = jax.nn.softmax(top_v, axis=-1)
    comb = jnp.sum(jax.nn.one_hot(top_i, N_EXPERTS, dtype=f32) * gates[..., None], axis=1)
    out = jnp.zeros_like(t)
    for e in range(N_EXPERTS):
        out = out + comb[:, e:e + 1].astype(t.dtype) * swiglu(t, w1[e], w3[e], w2[e])
    return out.reshape(B, L, D)


def trunk_layer(x, cond, layer, lp, ctx_kv):
    mod = (jax.nn.silu(cond) @ lp['ada_w'] + lp['ada_b'])[:, None, :]
    sh1, sc1, g1, sh2, sc2, g2 = jnp.split(mod, 6, axis=-1)
    h = rmsnorm(x, lp['norm1_g']) * (1.0 + sc1) + sh1
    mix, new_kv = token_mixers(h, lp, layer, ctx_kv)
    x = x + g1 * mix
    h = rmsnorm(x, lp['norm2_g']) * (1.0 + sc2) + sh2
    if 'moe' in lp:
        f = moe_swiglu(h, *lp['moe'])
    else:
        f = swiglu(h, *lp['ffn'])
    x = x + g2 * f
    return x, new_kv


def setup_inputs(seed: int = 0) -> dict:
    key = jax.random.key(seed)
    keys = iter(jax.random.split(key, 48))
    D = D_MODEL

    def nrm(shape, scale=1.0):
        return jax.random.normal(next(keys), shape, f32) * scale

    decay_base = jnp.broadcast_to(jnp.linspace(3.0, 15.0, HY_CH, dtype=f32), (DEPTH, HY_ORDER, 2, HY_CH))
    return {
        'x_prompt': nrm((BATCH, SEQ, D)),
        'x_sample': nrm((DEC_BATCH, DEC_SEQ, D)),
        'cache_diff_k': nrm((DEC_BATCH, DEPTH, PAST_LEN, A_HEADS, 2 * A_DQK)),
        'cache_diff_v': nrm((DEC_BATCH, DEPTH, PAST_LEN, A_HEADS, A_DV)),
        'cache_na_k': nrm((DEC_BATCH, DEPTH, PAST_LEN, C_HEADS, C_DH)),
        'cache_na_v': nrm((DEC_BATCH, DEPTH, PAST_LEN, C_HEADS, C_DH)),
        'c': nrm((DEC_BATCH, D)),
        'c_ctx': nrm((D,)),
        'norm1_g': 1.0 + nrm((DEPTH, D), 0.02),
        'norm2_g': 1.0 + nrm((DEPTH, D), 0.02),
        'final_g': 1.0 + nrm((D,), 0.02),
        'ada_w': nrm((DEPTH, D, 6 * D), 0.5 * D ** -0.5),
        'ada_b': nrm((DEPTH, 6 * D), 0.02),
        'w_in': nrm((DEPTH, D, PROJ_W), D ** -0.5),
        'w_out': nrm((DEPTH, MIX_W, D), MIX_W ** -0.5),
        'diff_lam': nrm((DEPTH, 4, A_DQK), 0.1),
        'diff_subln_g': 1.0 + nrm((DEPTH, A_DV), 0.02),
        'hy_conv': nrm((DEPTH, 3, 3 * HY_CH), 0.5),
        'hy_w1': nrm((DEPTH, HY_FEAT, HY_HID), 1.0),
        'hy_b1': nrm((DEPTH, HY_HID), 0.1),
        'hy_w2': nrm((DEPTH, HY_HID, HY_HID), 2.0 * HY_HID ** -0.5),
        'hy_b2': nrm((DEPTH, HY_HID), 0.1),
        'hy_w3': nrm((DEPTH, HY_HID, HY_ORDER * 2 * HY_CH), HY_HID ** -0.5),
        'hy_b3': nrm((DEPTH, HY_ORDER * 2 * HY_CH), 0.1),
        'hy_freq': 1.0 + nrm((DEPTH, 2, HY_HID), 0.1),
        'hy_decay': decay_base + nrm((DEPTH, HY_ORDER, 2, HY_CH), 0.1),
        'hy_dbias': nrm((DEPTH, HY_ORDER, HY_CH), 0.5),
        'na_rpb': nrm((DEPTH, C_HEADS, 2 * NA_KH - 1, 2 * NA_KW - 1), 0.1),
        'pool_w': nrm((DEPTH, 4, POOL_GC, POOL_GC), POOL_GC ** -0.5),
        'pool_scale': 0.5 + nrm((DEPTH, GROUP_W), 0.05),
        'ffn_w1': nrm((N_DENSE, D, FFN_DENSE), D ** -0.5),
        'ffn_w3': nrm((N_DENSE, D, FFN_DENSE), D ** -0.5),
        'ffn_w2': nrm((N_DENSE, FFN_DENSE, D), FFN_DENSE ** -0.5),
        'moe_router': nrm((N_MOE, D, N_EXPERTS), D ** -0.5),
        'moe_w1': nrm((N_MOE, N_EXPERTS, D, FFN_EXPERT), D ** -0.5),
        'moe_w3': nrm((N_MOE, N_EXPERTS, D, FFN_EXPERT), D ** -0.5),
        'moe_w2': nrm((N_MOE, N_EXPERTS, FFN_EXPERT, D), FFN_EXPERT ** -0.5),
    }


def reference(x_prompt, x_sample, cache_diff_k, cache_diff_v, cache_na_k, cache_na_v, c, c_ctx,
              norm1_g, norm2_g, final_g, ada_w, ada_b, w_in, w_out, diff_lam, diff_subln_g,
              hy_conv, hy_w1, hy_b1, hy_w2, hy_b2, hy_w3, hy_b3, hy_freq, hy_decay, hy_dbias,
              na_rpb, pool_w, pool_scale, ffn_w1, ffn_w3, ffn_w2,
              moe_router, moe_w1, moe_w3, moe_w2):
    xp = x_prompt
    xs = x_sample
    cond_ctx = c_ctx[None, :]
    diff_k, diff_v, na_k, na_v = [], [], [], []
    for l in range(DEPTH):
        lp = {
            'norm1_g': norm1_g[l], 'norm2_g': norm2_g[l],
            'ada_w': ada_w[l], 'ada_b': ada_b[l],
            'w_in': w_in[l], 'w_out': w_out[l],
            'diff_lam': diff_lam[l], 'diff_subln_g': diff_subln_g[l],
            'hy': (hy_conv[l], hy_w1[l], hy_b1[l], hy_w2[l], hy_b2[l], hy_w3[l], hy_b3[l],
                   hy_freq[l], hy_decay[l], hy_dbias[l]),
            'na_rpb': na_rpb[l],
            'pool_w': pool_w[l], 'pool_scale': pool_scale[l],
        }
        if l % 2 == 0:
            lp['ffn'] = (ffn_w1[l // 2], ffn_w3[l // 2], ffn_w2[l // 2])
        else:
            lp['moe'] = (moe_router[l // 2], moe_w1[l // 2], moe_w3[l // 2], moe_w2[l // 2])
        xp, (ka, va, kc, vc) = trunk_layer(xp, cond_ctx, l, lp, None)
        diff_k.append(ka)
        diff_v.append(va)
        na_k.append(kc)
        na_v.append(vc)
        ctx = (cache_diff_k[:, l], cache_diff_v[:, l], cache_na_k[:, l], cache_na_v[:, l])
        xs, _ = trunk_layer(xs, c, l, lp, ctx)
    y_prompt = rmsnorm(xp, final_g)
    y_sample = rmsnorm(xs, final_g)
    new_diff_k = jnp.stack(diff_k, axis=1)
    new_diff_v = jnp.stack(diff_v, axis=1)
    new_na_k = jnp.stack(na_k, axis=1)
    new_na_v = jnp.stack(na_v, axis=1)
    return (y_prompt, y_sample, new_diff_k, new_diff_v, new_na_k, new_na_v)
```

```python
import functools
import math

import numpy as np
import jax
import jax.numpy as jnp
from jax import lax
from jax.experimental import pallas as pl
from jax.experimental.pallas import tpu as pltpu

D_MODEL = 1024
BATCH = 32
SEQ = 256
DEPTH = 2
DEC_BATCH = 4
DEC_SEQ = 2048
PAST_LEN = 512
GRID_W = 64
GROUP_W = D_MODEL // 4
A_HEADS = 4
A_DQK = GROUP_W // (2 * A_HEADS)
A_DV = GROUP_W // A_HEADS
ROPE_BASE = 10000.0
HY_CH = GROUP_W
HY_ORDER = 2
HY_BANDS = 8
C_HEADS = 4
C_DH = GROUP_W // C_HEADS
NA_KH = 8
NA_KW = 16
NA_QCOLS = 16
POOL_WINDOWS = (2, 4, 8, 16)
POOL_GC = GROUP_W // 4
PROJ_W = 3 * GROUP_W + 3 * HY_CH + 3 * GROUP_W + GROUP_W
N_EXPERTS = 8
TOP_K = 2
Q_BLOCK = 128
EPS = 1e-6
NEG = -1e30

N_PROMPT_TOK = BATCH * SEQ
N_SAMPLE_TOK = DEC_BATCH * DEC_SEQ
N_TOK = N_PROMPT_TOK + N_SAMPLE_TOK
MOD_ROWS = 8
CTX_ROW = DEC_BATCH
ROUTER_PAD = 128

f32 = jnp.float32
bf16 = jnp.bfloat16

VMEM_LIMIT = 56 * 1024 * 1024


def _cparams(n_axes):
    return pltpu.CompilerParams(
        dimension_semantics=("arbitrary",) * n_axes, vmem_limit_bytes=VMEM_LIMIT)


def _mod_row(i, tm):
    n_prompt_blocks = N_PROMPT_TOK // tm
    blocks_per_seq = DEC_SEQ // tm
    return jnp.where(i < n_prompt_blocks, CTX_ROW, (i - n_prompt_blocks) // blocks_per_seq)


def _split3(a):
    a0 = a.astype(bf16)
    r1 = a - a0.astype(f32)
    a1 = r1.astype(bf16)
    a2 = (r1 - a1.astype(f32)).astype(bf16)
    return a0, a1, a2


def _dot_f32(a, b):
    a0, a1, a2 = _split3(a)
    b0, b1, b2 = _split3(b)
    d = functools.partial(jnp.dot, preferred_element_type=f32)
    return ((d(a2, b0) + d(a1, b1) + d(a0, b2)) + (d(a1, b0) + d(a0, b1))) + d(a0, b0)


def _ada_kernel(cond_ref, w_ref, b_ref, o_ref):
    c = cond_ref[...]
    s = c * jax.nn.sigmoid(c)
    o_ref[0] = jnp.dot(s.astype(bf16), w_ref[0].astype(bf16), preferred_element_type=f32) + b_ref[0]


def _ada_mod(cond, ada_w, ada_b):
    tn = 1536
    n6 = 6 * D_MODEL
    return pl.pallas_call(
        _ada_kernel,
        out_shape=jax.ShapeDtypeStruct((DEPTH, MOD_ROWS, n6), f32),
        grid=(DEPTH, n6 // tn),
        in_specs=[
            pl.BlockSpec((MOD_ROWS, D_MODEL), lambda l, j: (0, 0)),
            pl.BlockSpec((1, D_MODEL, tn), lambda l, j: (l, 0, j)),
            pl.BlockSpec((1, 1, tn), lambda l, j: (l, 0, j)),
        ],
        out_specs=pl.BlockSpec((1, MOD_ROWS, tn), lambda l, j: (l, 0, j)),
        compiler_params=_cparams(2),
        name="ada_mod",
    )(cond, ada_w, ada_b.reshape(DEPTH, 1, n6))


def _rope_tables():
    t = np.arange(DEC_SEQ)
    nf = A_DQK // 4
    inv = ROPE_BASE ** (-np.arange(nf, dtype=np.float64) / nf)
    ar = (t // GRID_W)[:, None] * inv
    ac = (t % GRID_W)[:, None] * inv
    cos = np.concatenate([np.cos(ar), np.cos(ar), np.cos(ac), np.cos(ac)], axis=1)
    sin = np.concatenate([-np.sin(ar), np.sin(ar), -np.sin(ac), np.sin(ac)], axis=1)
    reps = GROUP_W // A_DQK
    return (np.tile(cos, (1, reps)).astype(np.float32), np.tile(sin, (1, reps)).astype(np.float32))


def _modulated_norm(x, g, shift, scale):
    ms = jnp.mean(x * x, axis=-1, keepdims=True)
    return (x * lax.rsqrt(ms + EPS) * g) * (1.0 + scale) + shift


def _inproj_kernel(x_ref, mod_ref, g_ref, w_ref, cos_ref, sin_ref,
                   pa_ref, pb_ref, pc_ref, pd_ref, wbf_ref, *, tm):
    i = pl.program_id(0)

    @pl.when(i == 0)
    def _():
        wbf_ref[...] = w_ref[...].astype(bf16)

    row = _mod_row(i, tm)
    shift = mod_ref[pl.ds(row, 1), 0:D_MODEL]
    scale = mod_ref[pl.ds(row, 1), D_MODEL:2 * D_MODEL]
    h = _modulated_norm(x_ref[...], g_ref[...], shift, scale)
    proj = jnp.dot(h.astype(bf16), wbf_ref[...], preferred_element_type=f32)
    w3 = 3 * GROUP_W
    pb_ref[...] = proj[:, w3:2 * w3]
    pc_ref[...] = proj[:, 2 * w3:3 * w3]
    pd_ref[...] = proj[:, 3 * w3:]
    pa_ref[:, 2 * GROUP_W:] = proj[:, 2 * GROUP_W:w3]

    @pl.when(i < N_PROMPT_TOK // tm)
    def _():
        pa_ref[:, :2 * GROUP_W] = proj[:, :2 * GROUP_W]

    @pl.when(i >= N_PROMPT_TOK // tm)
    def _():
        cos = cos_ref[...]
        sin = sin_ref[...]
        lane = lax.broadcasted_iota(jnp.int32, (tm, GROUP_W), 1)
        first = (lane % 16) < 8
        for s in range(2):
            v = proj[:, s * GROUP_W:(s + 1) * GROUP_W]
            partner = jnp.where(first, pltpu.roll(v, GROUP_W - 8, 1), pltpu.roll(v, 8, 1))
            pa_ref[:, s * GROUP_W:(s + 1) * GROUP_W] = v * cos + partner * sin


def _inproj(x, mod_l, g, w_in_l, cos_t, sin_t, tm=512):
    n_prompt_blocks = N_PROMPT_TOK // tm
    blocks_per_seq = DEC_SEQ // tm
    w3 = 3 * GROUP_W

    def rope_idx(i):
        return (jnp.maximum(i - n_prompt_blocks, 0) % blocks_per_seq, 0)

    return pl.pallas_call(
        functools.partial(_inproj_kernel, tm=tm),
        out_shape=(jax.ShapeDtypeStruct((N_TOK, w3), f32), jax.ShapeDtypeStruct((N_TOK, w3), f32),
                   jax.ShapeDtypeStruct((N_TOK, w3), f32), jax.ShapeDtypeStruct((N_TOK, GROUP_W), f32)),
        grid=(N_TOK // tm,),
        in_specs=[
            pl.BlockSpec((tm, D_MODEL), lambda i: (i, 0)),
            pl.BlockSpec((MOD_ROWS, 6 * D_MODEL), lambda i: (0, 0)),
            pl.BlockSpec((1, D_MODEL), lambda i: (0, 0)),
            pl.BlockSpec((D_MODEL, PROJ_W), lambda i: (0, 0)),
            pl.BlockSpec((tm, GROUP_W), rope_idx),
            pl.BlockSpec((tm, GROUP_W), rope_idx),
        ],
        out_specs=(pl.BlockSpec((tm, w3), lambda i: (i, 0)), pl.BlockSpec((tm, w3), lambda i: (i, 0)),
                   pl.BlockSpec((tm, w3), lambda i: (i, 0)), pl.BlockSpec((tm, GROUP_W), lambda i: (i, 0))),
        scratch_shapes=[pltpu.VMEM((D_MODEL, PROJ_W), bf16)],
        compiler_params=_cparams(1),
        name="norm1_inproj",
    )(x, mod_l, g.reshape(1, D_MODEL), w_in_l, cos_t, sin_t)


def _outproj_kernel(*refs, tm, moe):
    if moe:
        (oa_ref, ob_ref, oc_ref, od_ref, x_ref, mod_ref, g_ref, w_ref, r_ref,
         x1_ref, h2_ref, lg_ref, wbf_ref) = refs
    else:
        (oa_ref, ob_ref, oc_ref, od_ref, x_ref, mod_ref, g_ref, w_ref,
         x1_ref, h2_ref, wbf_ref) = refs
    i = pl.program_id(0)

    @pl.when(i == 0)
    def _():
        wbf_ref[...] = w_ref[...].astype(bf16)

    row = _mod_row(i, tm)
    d = D_MODEL
    gate1 = mod_ref[pl.ds(row, 1), 2 * d:3 * d]
    shift2 = mod_ref[pl.ds(row, 1), 3 * d:4 * d]
    scale2 = mod_ref[pl.ds(row, 1), 4 * d:5 * d]
    mixed = jnp.concatenate([oa_ref[...], ob_ref[...], oc_ref[...], od_ref[...]], axis=-1)
    mix = jnp.dot(mixed, wbf_ref[...], preferred_element_type=f32)
    x1 = x_ref[...] + gate1 * mix
    x1_ref[...] = x1
    h = _modulated_norm(x1, g_ref[...], shift2, scale2)
    h2_ref[...] = h.astype(bf16)
    if moe:
        lg_ref[...] = _dot_f32(h, r_ref[...])


def _outproj(oa, ob, oc, od, x, mod_l, g, w_out_l, router_l=None, tm=512):
    moe = router_l is not None
    tok = lambda w: pl.BlockSpec((tm, w), lambda i: (i, 0))
    in_specs = [tok(GROUP_W), tok(GROUP_W), tok(GROUP_W), tok(GROUP_W), tok(D_MODEL),
                pl.BlockSpec((MOD_ROWS, 6 * D_MODEL), lambda i: (0, 0)),
                pl.BlockSpec((1, D_MODEL), lambda i: (0, 0)),
                pl.BlockSpec((D_MODEL, D_MODEL), lambda i: (0, 0))]
    args = [oa, ob, oc, od, x, mod_l, g.reshape(1, D_MODEL), w_out_l]
    out_shape = [jax.ShapeDtypeStruct((N_TOK, D_MODEL), f32), jax.ShapeDtypeStruct((N_TOK, D_MODEL), bf16)]
    out_specs = [tok(D_MODEL), tok(D_MODEL)]
    if moe:
        in_specs.append(pl.BlockSpec((D_MODEL, ROUTER_PAD), lambda i: (0, 0)))
        args.append(jnp.pad(router_l, ((0, 0), (0, ROUTER_PAD - N_EXPERTS))))
        out_shape.append(jax.ShapeDtypeStruct((N_TOK, ROUTER_PAD), f32))
        out_specs.append(tok(ROUTER_PAD))
    return pl.pallas_call(
        functools.partial(_outproj_kernel, tm=tm, moe=moe),
        out_shape=tuple(out_shape),
        grid=(N_TOK // tm,),
        in_specs=in_specs,
        out_specs=tuple(out_specs),
        scratch_shapes=[pltpu.VMEM((D_MODEL, D_MODEL), bf16)],
        compiler_params=_cparams(1),
        name="outproj_norm2",
    )(*args)


def _ffn_kernel(be_ref, nv_ref, x_ref, w1_ref, w3_ref, w2_ref, o_ref):
    i = pl.program_id(0)
    j = pl.program_id(1)

    @pl.when(j == 0)
    def _():
        o_ref[...] = jnp.zeros_like(o_ref)

    @pl.when(i < nv_ref[0])
    def _():
        x = x_ref[...]
        h1 = jnp.dot(x, w1_ref[0].astype(bf16), preferred_element_type=f32)
        h3 = jnp.dot(x, w3_ref[0].astype(bf16), preferred_element_type=f32)
        a = (h1 * jax.nn.sigmoid(h1)) * h3
        o_ref[...] += jnp.dot(a.astype(bf16), w2_ref[0].astype(bf16), preferred_element_type=f32)


def _ffn_grouped(x_rows, w1, w3, w2, blk_expert, n_valid, tm, tf):
    n_rows = x_rows.shape[0]
    ffn = w1.shape[-1]
    nf = ffn // tf
    assert nf * tf == ffn and n_rows % tm == 0

    def wcol(i, j, be, nv):
        return (be[i], 0, jnp.where(i < nv[0], j, nf - 1))

    def wrow(i, j, be, nv):
        return (be[i], jnp.where(i < nv[0], j, nf - 1), 0)

    return pl.pallas_call(
        _ffn_kernel,
        out_shape=jax.ShapeDtypeStruct((n_rows, D_MODEL), f32),
        grid_spec=pltpu.PrefetchScalarGridSpec(
            num_scalar_prefetch=2,
            grid=(n_rows // tm, nf),
            in_specs=[
                pl.BlockSpec((tm, D_MODEL), lambda i, j, be, nv: (i, 0)),
                pl.BlockSpec((1, D_MODEL, tf), wcol),
                pl.BlockSpec((1, D_MODEL, tf), wcol),
                pl.BlockSpec((1, tf, D_MODEL), wrow),
            ],
            out_specs=pl.BlockSpec((tm, D_MODEL), lambda i, j, be, nv: (i, 0)),
        ),
        compiler_params=_cparams(2),
        name="swiglu_grouped",
    )(blk_expert, n_valid, x_rows, w1, w3, w2)


def _residual_kernel(*refs, tm, n_terms, final):
    x1_ref, mod_ref = refs[0], refs[1]
    f_refs = refs[2:2 + n_terms]
    rest = refs[2 + n_terms:]
    i = pl.program_id(0)
    row = _mod_row(i, tm)
    gate2 = mod_ref[pl.ds(row, 1), 5 * D_MODEL:6 * D_MODEL]
    f = f_refs[0][...]
    for r in f_refs[1:]:
        f = f + r[...]
    x2 = x1_ref[...] + gate2 * f
    if final:
        g_ref, o_ref = rest
        ms = jnp.mean(x2 * x2, axis=-1, keepdims=True)
        o_ref[...] = x2 * lax.rsqrt(ms + EPS) * g_ref[...]
    else:
        (o_ref,) = rest
        o_ref[...] = x2


def _residual(x1, mod_l, terms, final_g=None, tm=512):
    final = final_g is not None
    tok = pl.BlockSpec((tm, D_MODEL), lambda i: (i, 0))
    in_specs = [tok, pl.BlockSpec((MOD_ROWS, 6 * D_MODEL), lambda i: (0, 0))] + [tok] * len(terms)
    args = [x1, mod_l] + list(terms)
    if final:
        in_specs.append(pl.BlockSpec((1, D_MODEL), lambda i: (0, 0)))
        args.append(final_g.reshape(1, D_MODEL))
    return pl.pallas_call(
        functools.partial(_residual_kernel, tm=tm, n_terms=len(terms), final=final),
        out_shape=jax.ShapeDtypeStruct((N_TOK, D_MODEL), f32),
        grid=(N_TOK // tm,),
        in_specs=in_specs,
        out_specs=tok,
        compiler_params=_cparams(1),
        name="ffn_residual",
    )(*args)


def _route(logits, tm):
    top_v, top_i = lax.top_k(logits, TOP_K)
    gates = jax.nn.softmax(top_v, axis=-1)
    flat_e = top_i.reshape(-1)
    onehot = (flat_e[:, None] == jnp.arange(N_EXPERTS)[None, :]).astype(jnp.int32)
    csum = jnp.cumsum(onehot, axis=0)
    rank = jnp.take_along_axis(csum, flat_e[:, None], axis=1)[:, 0] - 1
    counts = csum[-1]
    padded = ((counts + tm - 1) // tm) * tm
    pend = jnp.cumsum(padded)
    pstart = pend - padded
    dest = pstart[flat_e] + rank
    n_rows = N_TOK * TOP_K + N_EXPERTS * tm
    tok_of = jnp.arange(N_TOK * TOP_K, dtype=jnp.int32) // TOP_K
    row_token = jnp.zeros((n_rows,), jnp.int32).at[dest].set(tok_of)
    blk_start = jnp.arange(n_rows // tm, dtype=jnp.int32) * tm
    blk_expert = jnp.minimum(jnp.searchsorted(pend, blk_start, side="right"), N_EXPERTS - 1).astype(jnp.int32)
    n_valid = (pend[-1] // tm).astype(jnp.int32).reshape(1)
    blk_expert = jnp.where(blk_start < pend[-1], blk_expert, blk_expert[jnp.maximum(n_valid[0] - 1, 0)])
    return row_token, blk_expert, n_valid, dest.reshape(N_TOK, TOP_K), gates


def _to_blocks(a):
    B, L = a.shape[:2]
    return a.reshape((B, L // Q_BLOCK, Q_BLOCK) + a.shape[2:]).swapaxes(0, 1)


def _from_blocks(a):
    nb, B, qb = a.shape[:3]
    return a.swapaxes(0, 1).reshape((B, nb * qb) + a.shape[3:])


def _j_diff_attention(q, k, v, lam):
    scale = A_DQK ** -0.5

    def block(qb):
        s = jnp.einsum('bqhmd,bkhmd->bhmqk', qb, k).astype(f32) * scale
        p = jax.nn.softmax(s, axis=-1)
        pd = p[:, :, 0] - lam * p[:, :, 1]
        return jnp.einsum('bhqk,bkhe->bqhe', pd.astype(v.dtype), v)

    return _from_blocks(lax.map(block, _to_blocks(q)))


def _j_softmax_attention(q, k, v):
    scale = q.shape[-1] ** -0.5

    def block(qb):
        s = jnp.einsum('bqhd,bkhd->bhqk', qb, k).astype(f32) * scale
        p = jax.nn.softmax(s, axis=-1).astype(v.dtype)
        return jnp.einsum('bhqk,bkhd->bqhd', p, v)

    return _from_blocks(lax.map(block, _to_blocks(q)))


def _j_na_attention(q, k, v, k_ctx, v_ctx, rpb):
    B, L, H, d = q.shape
    R = L // GRID_W
    kh = min(NA_KH, R)
    cb = GRID_W // NA_QCOLS
    band = NA_KW + NA_QCOLS
    rows = jnp.arange(R)
    key_rows = jnp.clip(rows - kh // 2, 0, R - kh)[:, None] + jnp.arange(kh)[None, :]
    qcol = jnp.arange(GRID_W).reshape(cb, NA_QCOLS)
    band_cols = (jnp.clip(jnp.arange(cb) * NA_QCOLS - NA_KW // 2, 0, GRID_W - band)[:, None]
                 + jnp.arange(band)[None, :])
    win_start = jnp.clip(qcol - NA_KW // 2, 0, GRID_W - NA_KW)
    bc = band_cols[:, None, :]
    valid = (bc >= win_start[..., None]) & (bc < win_start[..., None] + NA_KW)
    tok = key_rows[:, None, :, None] * GRID_W + band_cols[None, :, None, :]
    kg = k[:, tok]
    vg = v[:, tok]
    qg = q.reshape(B, R, cb, NA_QCOLS, H, d)
    scale = d ** -0.5
    s_loc = jnp.einsum('brjqhd,brjawhd->bhrjqaw', qg, kg).astype(f32) * scale
    dy = key_rows - rows[:, None] + NA_KH - 1
    dx = jnp.clip(bc - qcol[..., None] + NA_KW - 1, 0, 2 * NA_KW - 2)
    bias = rpb.astype(f32)[:, dy[:, None, None, :, None], dx[None, :, :, None, :]]
    s_loc = jnp.where(valid[:, :, None, :], s_loc + bias, NEG)
    s_ctx = jnp.einsum('brjqhd,bnhd->bhrjqn', qg, k_ctx).astype(f32) * scale
    nloc = kh * band
    s = jnp.concatenate([s_loc.reshape(s_loc.shape[:5] + (nloc,)), s_ctx], axis=-1)
    p = jax.nn.softmax(s, axis=-1).astype(v.dtype)
    p_loc = p[..., :nloc].reshape(s_loc.shape)
    p_ctx = p[..., nloc:]
    o = (jnp.einsum('bhrjqaw,brjawhd->brjqhd', p_loc, vg)
         + jnp.einsum('bhrjqn,bnhd->brjqhd', p_ctx, v_ctx))
    return o.reshape(B, L, H, d)


def _j_short_conv3(u, w):
    up = jnp.pad(u, ((0, 0), (1, 1), (0, 0)))
    return up[:, :-2] * w[0] + up[:, 1:-1] * w[1] + up[:, 2:] * w[2]


def _j_hyena_filters(L, w1, b1, w2, b2, w3, b3, freq, decay):
    t = (jnp.arange(L, dtype=f32) / L)[:, None]
    bands = jnp.arange(1, HY_BANDS + 1, dtype=f32)[None, :]
    ang = 2.0 * math.pi * bands * t
    feat = jnp.concatenate([t, jnp.sin(ang), jnp.cos(ang)], axis=-1)
    fr = freq.astype(f32)
    z = jnp.sin(fr[0] * (feat @ w1.astype(f32) + b1.astype(f32)))
    z = jnp.sin(fr[1] * (z @ w2.astype(f32) + b2.astype(f32)))
    z = (z @ w3.astype(f32) + b3.astype(f32)).reshape(L, HY_ORDER, 2, HY_CH)
    z = z * jnp.exp(-t[:, :, None, None] * jnp.abs(decay.astype(f32)))
    fwd = z[:, :, 0]
    bwd = z[:, :, 1]
    filt = jnp.concatenate([fwd, jnp.zeros((1, HY_ORDER, HY_CH), f32), bwd[:0:-1]], axis=0)
    return filt / jnp.sum(jnp.abs(filt), axis=0, keepdims=True)


def _j_hyena(u, conv_w, w1, b1, w2, b2, w3, b3, freq, decay, dbias):
    L = u.shape[1]
    u = _j_short_conv3(u, conv_w)
    x1, x2, v = jnp.split(u, 3, axis=-1)
    filt = _j_hyena_filters(L, w1, b1, w2, b2, w3, b3, freq, decay)
    filt_f = jnp.fft.rfft(filt, axis=0)

    def longconv(z, o):
        zf = jnp.fft.rfft(z.astype(f32), n=2 * L, axis=1)
        y = jnp.fft.irfft(zf * filt_f[None, :, o, :], n=2 * L, axis=1)[:, :L]
        return y.astype(z.dtype) + dbias[o] * z

    z = x1 * longconv(v, 0)
    return x2 * longconv(z, 1)


def _j_pool_mixer(u, w, scale):
    B, L, _ = u.shape
    uf = u.astype(f32)
    cs = jnp.concatenate([jnp.zeros((B, 1, GROUP_W), f32), jnp.cumsum(uf, axis=1)], axis=1)
    t = jnp.arange(L)
    outs = []
    for g, win in enumerate(POOL_WINDOWS):
        lo = jnp.clip(t - win // 2, 0, L)
        hi = jnp.clip(t - win // 2 + win, 0, L)
        sl = slice(g * POOL_GC, (g + 1) * POOL_GC)
        mean = (cs[:, hi, sl] - cs[:, lo, sl]) / (hi - lo).astype(f32)[None, :, None]
        outs.append(mean - uf[:, :, sl])
    pooled = jnp.stack(outs, axis=2)
    y = jnp.einsum('blgc,gce->blge', pooled, w.astype(f32)).reshape(B, L, GROUP_W)
    return (y * scale.astype(f32)).astype(u.dtype)


def _j_rmsnorm(x, g):
    xf = x.astype(f32)
    y = xf * lax.rsqrt(jnp.mean(xf * xf, axis=-1, keepdims=True) + EPS)
    return (y * g.astype(f32)).astype(x.dtype)


def _j_mixers(pa, pb, pc, pd, B, L, lp, layer, ctx_kv):
    qa, ka, va = jnp.split(pa.reshape(B, L, -1), 3, axis=-1)
    qa = qa.reshape(B, L, A_HEADS, 2, A_DQK)
    ka = ka.reshape(B, L, A_HEADS, 2, A_DQK)
    va = va.reshape(B, L, A_HEADS, A_DV)
    lam_init = 0.8 - 0.6 * math.exp(-0.3 * layer)
    lv = lp['diff_lam'].astype(f32)
    lam = jnp.exp(jnp.sum(lv[0] * lv[1])) - jnp.exp(jnp.sum(lv[2] * lv[3])) + lam_init
    qc, kc, vc = jnp.split(pc.reshape(B, L, -1), 3, axis=-1)
    qc = qc.reshape(B, L, C_HEADS, C_DH)
    kc = kc.reshape(B, L, C_HEADS, C_DH)
    vc = vc.reshape(B, L, C_HEADS, C_DH)
    if ctx_kv is None:
        oa = _j_diff_attention(qa, ka, va, lam)
        oc = _j_softmax_attention(qc, kc, vc)
    else:
        cak, cav, cck, ccv = ctx_kv
        N = cak.shape[1]
        k_all = jnp.concatenate([ka, cak.reshape(B, N, A_HEADS, 2, A_DQK)], axis=1)
        v_all = jnp.concatenate([va, cav], axis=1)
        oa = _j_diff_attention(qa, k_all, v_all, lam)
        oc = _j_na_attention(qc, kc, vc, cck, ccv, lp['na_rpb'])
    oa = _j_rmsnorm(oa, lp['diff_subln_g']) * (1.0 - lam_init)
    ob = _j_hyena(pb.reshape(B, L, -1), *lp['hy'])
    od = _j_pool_mixer(pd.reshape(B, L, -1), lp['pool_w'], lp['pool_scale'])
    flat = lambda a: a.reshape(B * L, GROUP_W).astype(bf16)
    return flat(oa), flat(ob), flat(oc), flat(od)


def kernel(x_prompt, x_sample, cache_diff_k, cache_diff_v, cache_na_k, cache_na_v, c, c_ctx,
           norm1_g, norm2_g, final_g, ada_w, ada_b, w_in, w_out, diff_lam, diff_subln_g,
           hy_conv, hy_w1, hy_b1, hy_w2, hy_b2, hy_w3, hy_b3, hy_freq, hy_decay, hy_dbias,
           na_rpb, pool_w, pool_scale, ffn_w1, ffn_w3, ffn_w2,
           moe_router, moe_w1, moe_w3, moe_w2):
    cond = jnp.concatenate([c, c_ctx[None, :], jnp.zeros((MOD_ROWS - DEC_BATCH - 1, D_MODEL), f32)], axis=0)
    mod = _ada_mod(cond, ada_w, ada_b)
    cos_np, sin_np = _rope_tables()
    cos_t, sin_t = jnp.asarray(cos_np), jnp.asarray(sin_np)

    x = jnp.concatenate([x_prompt.reshape(N_PROMPT_TOK, D_MODEL), x_sample.reshape(N_SAMPLE_TOK, D_MODEL)], axis=0)
    new_kv = [[], [], [], []]
    n_layers = DEPTH
    for l in range(n_layers):
        lp = {
            'diff_lam': diff_lam[l], 'diff_subln_g': diff_subln_g[l],
            'hy': (hy_conv[l], hy_w1[l], hy_b1[l], hy_w2[l], hy_b2[l], hy_w3[l], hy_b3[l],
                   hy_freq[l], hy_decay[l], hy_dbias[l]),
            'na_rpb': na_rpb[l], 'pool_w': pool_w[l], 'pool_scale': pool_scale[l],
        }
        pa, pb, pc, pd = _inproj(x, mod[l], norm1_g[l], w_in[l], cos_t, sin_t)
        P = N_PROMPT_TOK
        new_kv[0].append(pa[:P, GROUP_W:2 * GROUP_W].reshape(BATCH, SEQ, A_HEADS, 2 * A_DQK))
        new_kv[1].append(pa[:P, 2 * GROUP_W:].reshape(BATCH, SEQ, A_HEADS, A_DV))
        new_kv[2].append(pc[:P, GROUP_W:2 * GROUP_W].reshape(BATCH, SEQ, C_HEADS, C_DH))
        new_kv[3].append(pc[:P, 2 * GROUP_W:].reshape(BATCH, SEQ, C_HEADS, C_DH))
        ctx = (cache_diff_k[:, l], cache_diff_v[:, l], cache_na_k[:, l], cache_na_v[:, l])
        mp = _j_mixers(pa[:P], pb[:P], pc[:P], pd[:P], BATCH, SEQ, lp, l, None)
        ms = _j_mixers(pa[P:], pb[P:], pc[P:], pd[P:], DEC_BATCH, DEC_SEQ, lp, l, ctx)
        oa, ob, oc, od = [jnp.concatenate([a, b], axis=0) for a, b in zip(mp, ms)]
        final = final_g if l == n_layers - 1 else None
        if l % 2 == 0:
            x1, h2 = _outproj(oa, ob, oc, od, x, mod[l], norm2_g[l], w_out[l])
            tm = 1024
            y = _ffn_grouped(h2, ffn_w1[l // 2][None], ffn_w3[l // 2][None], ffn_w2[l // 2][None],
                             jnp.zeros((N_TOK // tm,), jnp.int32), jnp.full((1,), N_TOK // tm, jnp.int32),
                             tm=tm, tf=256)
            x = _residual(x1, mod[l], [y], final)
        else:
            x1, h2, logits = _outproj(oa, ob, oc, od, x, mod[l], norm2_g[l], w_out[l], moe_router[l // 2])
            tm = 1024
            row_token, blk_expert, n_valid, dest, gates = _route(logits[:, :N_EXPERTS], tm)
            xs = jnp.take(h2, row_token, axis=0)
            ys = _ffn_grouped(xs, moe_w1[l // 2], moe_w3[l // 2], moe_w2[l // 2], blk_expert, n_valid,
                              tm=tm, tf=512)
            terms = [gates[:, k:k + 1] * jnp.take(ys, dest[:, k], axis=0) for k in range(TOP_K)]
            x = _residual(x1, mod[l], terms, final)

    y_prompt = x[:N_PROMPT_TOK].reshape(BATCH, SEQ, D_MODEL)
    y_sample = x[N_PROMPT_TOK:].reshape(DEC_BATCH, DEC_SEQ, D_MODEL)
    return (y_prompt, y_sample) + tuple(jnp.stack(v, axis=1) for v in new_kv)
```

```python
import functools
import math

import numpy as np
import jax
import jax.numpy as jnp
from jax import lax
from jax.experimental import pallas as pl
from jax.experimental.pallas import tpu as pltpu

D_MODEL = 1024
BATCH = 32
SEQ = 256
DEPTH = 2
DEC_BATCH = 4
DEC_SEQ = 2048
PAST_LEN = 512
GRID_W = 64
GRID_H = DEC_SEQ // GRID_W
GROUP_W = D_MODEL // 4
A_HEADS = 4
A_DQK = GROUP_W // (2 * A_HEADS)
A_DV = GROUP_W // A_HEADS
ROPE_BASE = 10000.0
HY_CH = GROUP_W
HY_ORDER = 2
HY_BANDS = 8
HY_FEAT = 1 + 2 * HY_BANDS
HY_HID = 64
C_HEADS = 4
C_DH = GROUP_W // C_HEADS
NA_KH = 8
NA_KW = 16
POOL_WINDOWS = (2, 4, 8, 16)
POOL_GC = GROUP_W // 4
PROJ_W = 3 * GROUP_W + 3 * HY_CH + 3 * GROUP_W + GROUP_W
N_EXPERTS = 8
TOP_K = 2
EPS = 1e-6
NEG = -1e30

N_PROMPT_TOK = BATCH * SEQ
N_SAMPLE_TOK = DEC_BATCH * DEC_SEQ
N_TOK = N_PROMPT_TOK + N_SAMPLE_TOK
MOD_ROWS = 8
CTX_ROW = DEC_BATCH
ROUTER_PAD = 128
LANES = 128

NA_QROWS = 4
NA_SLAB_ROWS = 12
NA_GROUPS = GRID_H // NA_QROWS

f32 = jnp.float32
bf16 = jnp.bfloat16

VMEM_LIMIT = 56 * 1024 * 1024
_NT = (((1,), (1,)), ((), ()))


def _cparams(n_axes):
    return pltpu.CompilerParams(
        dimension_semantics=("arbitrary",) * n_axes, vmem_limit_bytes=VMEM_LIMIT)


def _mod_row(i, tm):
    n_prompt_blocks = N_PROMPT_TOK // tm
    blocks_per_seq = DEC_SEQ // tm
    return jnp.where(i < n_prompt_blocks, CTX_ROW, (i - n_prompt_blocks) // blocks_per_seq)


def _split3(a):
    a0 = a.astype(bf16)
    r1 = a - a0.astype(f32)
    a1 = r1.astype(bf16)
    a2 = (r1 - a1.astype(f32)).astype(bf16)
    return a0, a1, a2


def _dot_f32(a, b):
    a0, a1, a2 = _split3(a)
    b0, b1, b2 = _split3(b)
    d = functools.partial(jnp.dot, preferred_element_type=f32)
    return ((d(a2, b0) + d(a1, b1) + d(a0, b2)) + (d(a1, b0) + d(a0, b1))) + d(a0, b0)


def _lane_group(shape, width):
    return lax.shift_right_logical(lax.broadcasted_iota(jnp.int32, shape, 1), int(math.log2(width)))


def _shift_rows(x, d):
    n = x.shape[0]
    r = pltpu.roll(x, d % n, 0)
    row = lax.broadcasted_iota(jnp.int32, x.shape, 0)
    keep = (row >= d) if d > 0 else (row < n + d)
    return jnp.where(keep, r, 0.0)


def _ada_kernel(cond_ref, w_ref, b_ref, o_ref):
    c = cond_ref[...]
    s = c * jax.nn.sigmoid(c)
    o_ref[0] = jnp.dot(s.astype(bf16), w_ref[0].astype(bf16), preferred_element_type=f32) + b_ref[0]


def _ada_mod(cond, ada_w, ada_b):
    tn = 1536
    n6 = 6 * D_MODEL
    return pl.pallas_call(
        _ada_kernel,
        out_shape=jax.ShapeDtypeStruct((DEPTH, MOD_ROWS, n6), f32),
        grid=(DEPTH, n6 // tn),
        in_specs=[
            pl.BlockSpec((MOD_ROWS, D_MODEL), lambda l, j: (0, 0)),
            pl.BlockSpec((1, D_MODEL, tn), lambda l, j: (l, 0, j)),
            pl.BlockSpec((1, 1, tn), lambda l, j: (l, 0, j)),
        ],
        out_specs=pl.BlockSpec((1, MOD_ROWS, tn), lambda l, j: (l, 0, j)),
        compiler_params=_cparams(2),
        name="ada_mod",
    )(cond, ada_w, ada_b.reshape(DEPTH, 1, n6))


def _rope_tables():
    t = np.arange(DEC_SEQ)
    nf = A_DQK // 4
    inv = ROPE_BASE ** (-np.arange(nf, dtype=np.float64) / nf)
    ar = (t // GRID_W)[:, None] * inv
    ac = (t % GRID_W)[:, None] * inv
    cos = np.concatenate([np.cos(ar), np.cos(ar), np.cos(ac), np.cos(ac)], axis=1)
    sin = np.concatenate([-np.sin(ar), np.sin(ar), -np.sin(ac), np.sin(ac)], axis=1)
    reps = GROUP_W // A_DQK
    return (np.tile(cos, (1, reps)).astype(np.float32), np.tile(sin, (1, reps)).astype(np.float32))


def _modulated_norm(x, g, shift, scale):
    ms = jnp.mean(x * x, axis=-1, keepdims=True)
    return (x * lax.rsqrt(ms + EPS) * g) * (1.0 + scale) + shift


def _inproj_kernel(x_ref, mod_ref, g_ref, w_ref, cos_ref, sin_ref,
                   pa_ref, pb_ref, pc_ref, pd_ref, wbf_ref, *, tm):
    i = pl.program_id(0)

    @pl.when(i == 0)
    def _():
        wbf_ref[...] = w_ref[...].astype(bf16)

    row = _mod_row(i, tm)
    shift = mod_ref[pl.ds(row, 1), 0:D_MODEL]
    scale = mod_ref[pl.ds(row, 1), D_MODEL:2 * D_MODEL]
    h = _modulated_norm(x_ref[...], g_ref[...], shift, scale)
    proj = jnp.dot(h.astype(bf16), wbf_ref[...], preferred_element_type=f32)
    w3 = 3 * GROUP_W
    pb_ref[...] = proj[:, w3:2 * w3]
    pc_ref[...] = proj[:, 2 * w3:3 * w3]
    pd_ref[...] = proj[:, 3 * w3:]
    pa_ref[:, 2 * GROUP_W:] = proj[:, 2 * GROUP_W:w3]

    @pl.when(i < N_PROMPT_TOK // tm)
    def _():
        pa_ref[:, :2 * GROUP_W] = proj[:, :2 * GROUP_W]

    @pl.when(i >= N_PROMPT_TOK // tm)
    def _():
        cos = cos_ref[...]
        sin = sin_ref[...]
        lane = lax.broadcasted_iota(jnp.int32, (tm, GROUP_W), 1)
        first = (lane % 16) < 8
        for s in range(2):
            v = proj[:, s * GROUP_W:(s + 1) * GROUP_W]
            partner = jnp.where(first, pltpu.roll(v, GROUP_W - 8, 1), pltpu.roll(v, 8, 1))
            pa_ref[:, s * GROUP_W:(s + 1) * GROUP_W] = v * cos + partner * sin


def _inproj(x, mod_l, g, w_in_l, cos_t, sin_t, tm=512):
    n_prompt_blocks = N_PROMPT_TOK // tm
    blocks_per_seq = DEC_SEQ // tm
    w3 = 3 * GROUP_W

    def rope_idx(i):
        return (jnp.maximum(i - n_prompt_blocks, 0) % blocks_per_seq, 0)

    return pl.pallas_call(
        functools.partial(_inproj_kernel, tm=tm),
        out_shape=(jax.ShapeDtypeStruct((N_TOK, w3), f32), jax.ShapeDtypeStruct((N_TOK, w3), f32),
                   jax.ShapeDtypeStruct((N_TOK, w3), f32), jax.ShapeDtypeStruct((N_TOK, GROUP_W), f32)),
        grid=(N_TOK // tm,),
        in_specs=[
            pl.BlockSpec((tm, D_MODEL), lambda i: (i, 0)),
            pl.BlockSpec((MOD_ROWS, 6 * D_MODEL), lambda i: (0, 0)),
            pl.BlockSpec((1, D_MODEL), lambda i: (0, 0)),
            pl.BlockSpec((D_MODEL, PROJ_W), lambda i: (0, 0)),
            pl.BlockSpec((tm, GROUP_W), rope_idx),
            pl.BlockSpec((tm, GROUP_W), rope_idx),
        ],
        out_specs=(pl.BlockSpec((tm, w3), lambda i: (i, 0)), pl.BlockSpec((tm, w3), lambda i: (i, 0)),
                   pl.BlockSpec((tm, w3), lambda i: (i, 0)), pl.BlockSpec((tm, GROUP_W), lambda i: (i, 0))),
        scratch_shapes=[pltpu.VMEM((D_MODEL, PROJ_W), bf16)],
        compiler_params=_cparams(1),
        name="norm1_inproj",
    )(x, mod_l, g.reshape(1, D_MODEL), w_in_l, cos_t, sin_t)


def _attn_kernel(lam_ref, q_ref, k_ref, v_ref, g_ref, o_ref, *, n_heads, diff, scale, out_scale):
    tq = q_ref.shape[0]
    n = k_ref.shape[0]
    q = q_ref[...] * scale
    k = k_ref[...].astype(bf16)
    v = v_ref[...]
    hw = GROUP_W // n_heads
    qgrp = _lane_group((tq, GROUP_W), hw // 2 if diff else hw)
    vhead = _lane_group((n, GROUP_W), hw)
    acc = jnp.zeros((tq, GROUP_W), f32)
    for h in range(n_heads):
        vm = jnp.where(vhead == h, v, 0.0).astype(bf16)
        if diff:
            lam = lam_ref[0]
            pm = []
            for m in range(2):
                qm = jnp.where(qgrp == 2 * h + m, q, 0.0).astype(bf16)
                s = lax.dot_general(qm, k, _NT, preferred_element_type=f32)
                e = jnp.exp(s - jnp.max(s, axis=-1, keepdims=True))
                pm.append(e * (1.0 / jnp.sum(e, axis=-1, keepdims=True)))
            p = (pm[0] - lam * pm[1]).astype(bf16)
            acc = acc + jnp.dot(p, vm, preferred_element_type=f32)
        else:
            qm = jnp.where(qgrp == h, q, 0.0).astype(bf16)
            s = lax.dot_general(qm, k, _NT, preferred_element_type=f32)
            e = jnp.exp(s - jnp.max(s, axis=-1, keepdims=True))
            inv = 1.0 / jnp.sum(e, axis=-1, keepdims=True)
            acc = acc + jnp.dot(e.astype(bf16), vm, preferred_element_type=f32) * inv
    if diff:
        r = lax.shift_right_logical(lax.broadcasted_iota(jnp.int32, (GROUP_W, GROUP_W), 0), 6)
        c = lax.shift_right_logical(lax.broadcasted_iota(jnp.int32, (GROUP_W, GROUP_W), 1), 6)
        bd = jnp.where(r == c, 1.0, 0.0).astype(bf16)
        sq = acc * acc
        hi = sq.astype(bf16)
        lo = (sq - hi.astype(f32)).astype(bf16)
        ms = (jnp.dot(hi, bd, preferred_element_type=f32) + jnp.dot(lo, bd, preferred_element_type=f32)) * (1.0 / A_DV)
        acc = (acc * lax.rsqrt(ms + EPS) * g_ref[...]) * out_scale
    o_ref[...] = acc.astype(o_ref.dtype)


def _attention(q_src, q_col, q_row0, k_src, k_col, k_row0, v_src, v_col, v_row0, *,
               n_seq, seq_len, n_keys, tq, n_heads, diff, scale, lam=None, gain=None, out_scale=1.0):
    qb = seq_len // tq
    if lam is None:
        lam = jnp.zeros((1,), f32)
    if gain is None:
        gain = jnp.ones((1, GROUP_W), f32)
    return pl.pallas_call(
        functools.partial(_attn_kernel, n_heads=n_heads, diff=diff, scale=scale, out_scale=out_scale),
        out_shape=jax.ShapeDtypeStruct((n_seq * seq_len, GROUP_W), bf16),
        grid=(n_seq, qb),
        in_specs=[
            pl.BlockSpec(memory_space=pltpu.SMEM),
            pl.BlockSpec((tq, GROUP_W), lambda b, i: ((q_row0 + b) * qb + i, q_col)),
            pl.BlockSpec((n_keys, GROUP_W), lambda b, i: (k_row0 + b, k_col)),
            pl.BlockSpec((n_keys, GROUP_W), lambda b, i: (v_row0 + b, v_col)),
            pl.BlockSpec((1, GROUP_W), lambda b, i: (0, 0)),
        ],
        out_specs=pl.BlockSpec((tq, GROUP_W), lambda b, i: (b * qb + i, 0)),
        compiler_params=_cparams(2),
        name="diff_attention" if diff else "softmax_attention",
    )(lam, q_src, k_src, v_src, gain)


def _na_group_geometry(g):
    r0 = g * NA_QROWS
    slab0 = min(max(r0 - NA_KH // 2, 0), GRID_H - NA_SLAB_ROWS)
    return r0, slab0


def _na_bias_tables(rpb):
    c = np.arange(GRID_W)
    ws = np.clip(c - NA_KW // 2, 0, GRID_W - NA_KW)
    kc = np.arange(GRID_W)
    col_ok = (kc[None, :] >= ws[:, None]) & (kc[None, :] < ws[:, None] + NA_KW)
    dx = np.clip(kc[None, :] - c[:, None] + NA_KW - 1, 0, 2 * NA_KW - 2)
    tc = jnp.where(col_ok[None, None], rpb.astype(f32)[:, :, dx], NEG)
    neg_blk = jnp.full((C_HEADS, GRID_W, GRID_W), NEG, f32)
    tables = []
    for g in (0, 1, NA_GROUPS - 1):
        r0, slab0 = _na_group_geometry(g)
        rows = []
        for rq in range(NA_QROWS):
            r = r0 + rq
            rs = min(max(r - NA_KH // 2, 0), GRID_H - NA_KH)
            blks = []
            for kl in range(NA_SLAB_ROWS):
                kr = slab0 + kl
                blks.append(tc[:, kr - r + NA_KH - 1] if rs <= kr < rs + NA_KH else neg_blk)
            rows.append(jnp.concatenate(blks, axis=-1))
        tables.append(jnp.concatenate(rows, axis=-2))
    return jnp.stack(tables, axis=0)


def _na_kernel(q_ref, k_ref, v_ref, kx_ref, vx_ref, bias_ref, o_ref):
    g = pl.program_id(1)
    tq = NA_QROWS * GRID_W
    ns = NA_SLAB_ROWS * GRID_W
    slab0 = jnp.clip(g * NA_QROWS - NA_KH // 2, 0, GRID_H - NA_SLAB_ROWS)
    start = pl.multiple_of(slab0 * GRID_W, GRID_W)
    q = q_ref[...] * (C_DH ** -0.5)
    ks = k_ref[pl.ds(start, ns), :].astype(bf16)
    vs = v_ref[pl.ds(start, ns), :]
    kx = kx_ref[...].astype(bf16)
    vx = vx_ref[...]
    qhead = _lane_group((tq, GROUP_W), C_DH)
    vshead = _lane_group((ns, GROUP_W), C_DH)
    vxhead = _lane_group((PAST_LEN, GROUP_W), C_DH)
    acc = jnp.zeros((tq, GROUP_W), f32)
    for h in range(C_HEADS):
        qm = jnp.where(qhead == h, q, 0.0).astype(bf16)
        sl = lax.dot_general(qm, ks, _NT, preferred_element_type=f32)
        b = bias_ref[0, h]
        sl = jnp.where(b > 0.5 * NEG, sl + b, NEG)
        sx = lax.dot_general(qm, kx, _NT, preferred_element_type=f32)
        mx = jnp.maximum(jnp.max(sl, axis=-1, keepdims=True), jnp.max(sx, axis=-1, keepdims=True))
        el = jnp.exp(sl - mx)
        ex = jnp.exp(sx - mx)
        inv = 1.0 / (jnp.sum(el, axis=-1, keepdims=True) + jnp.sum(ex, axis=-1, keepdims=True))
        vsm = jnp.where(vshead == h, vs, 0.0).astype(bf16)
        vxm = jnp.where(vxhead == h, vx, 0.0).astype(bf16)
        o = (jnp.dot(el.astype(bf16), vsm, preferred_element_type=f32)
             + jnp.dot(ex.astype(bf16), vxm, preferred_element_type=f32))
        acc = acc + o * inv
    o_ref[...] = acc.astype(o_ref.dtype)


def _na_attention(pc, kx, vx, bias):
    tq = NA_QROWS * GRID_W
    ns = NA_SLAB_ROWS * GRID_W
    q_blk0 = N_PROMPT_TOK // tq
    s_blk0 = N_PROMPT_TOK // DEC_SEQ

    def bias_idx(b, g):
        return (jnp.where(g == 0, 0, jnp.where(g == NA_GROUPS - 1, 2, 1)), 0, 0, 0)

    return pl.pallas_call(
        _na_kernel,
        out_shape=jax.ShapeDtypeStruct((N_SAMPLE_TOK, GROUP_W), bf16),
        grid=(DEC_BATCH, NA_GROUPS),
        in_specs=[
            pl.BlockSpec((tq, GROUP_W), lambda b, g: (q_blk0 + b * NA_GROUPS + g, 0)),
            pl.BlockSpec((DEC_SEQ, GROUP_W), lambda b, g: (s_blk0 + b, 1)),
            pl.BlockSpec((DEC_SEQ, GROUP_W), lambda b, g: (s_blk0 + b, 2)),
            pl.BlockSpec((PAST_LEN, GROUP_W), lambda b, g: (b, 0)),
            pl.BlockSpec((PAST_LEN, GROUP_W), lambda b, g: (b, 0)),
            pl.BlockSpec((1, C_HEADS, tq, ns), bias_idx),
        ],
        out_specs=pl.BlockSpec((tq, GROUP_W), lambda b, g: (b * NA_GROUPS + g, 0)),
        compiler_params=_cparams(2),
        name="neighbourhood_attention",
    )(pc, pc, pc, kx, vx, bias)


def _dft_matrices(L):
    n = 2 * L
    k = np.arange(L)[:, None]
    s = np.arange(L)[None, :]
    ang = 2.0 * np.pi * ((k * s) % n) / n
    cos, sin = np.cos(ang), np.sin(ang)
    sin[0, :] = (-1.0) ** np.arange(L)
    fwd = np.concatenate([cos, sin], axis=0)
    wk = np.where(np.arange(L) == 0, 1.0, 2.0)[None, :]
    inv = np.concatenate([cos.T * wk, sin.T * wk], axis=1) / n
    inv[:, L] = ((-1.0) ** np.arange(L)) / n
    return fwd, inv


def _hy_filter_kernel(w1_ref, b1_ref, w2_ref, b2_ref, w3_ref, b3_ref, fr_ref, dec_ref, fa_ref, fb_ref,
                      p_ref, q_ref, r_ref, g_ref, nrm_ref, *, L, tk):
    s = pl.program_id(0)

    @pl.when(s == 0)
    def _():
        row = lax.broadcasted_iota(jnp.int32, (L, LANES), 0)
        lane = lax.broadcasted_iota(jnp.int32, (L, LANES), 1)
        t = row.astype(f32) / L
        band = jnp.where(lane <= HY_BANDS, lane, lane - HY_BANDS).astype(f32)
        ang = (2.0 * math.pi * band) * t
        feat = jnp.where(lane == 0, t, jnp.where(lane <= HY_BANDS, jnp.sin(ang),
                                                 jnp.where(lane <= 2 * HY_BANDS, jnp.cos(ang), 0.0)))
        z = jnp.sin(fr_ref[0:1, :] * (_dot_f32(feat, w1_ref[...]) + b1_ref[...]))
        z = jnp.sin(fr_ref[1:2, :] * (_dot_f32(z, w2_ref[...]) + b2_ref[...]))
        z = _dot_f32(z, w3_ref[...]) + b3_ref[...]
        wide = (L, HY_ORDER * 2 * HY_CH)
        tw = lax.broadcasted_iota(jnp.int32, wide, 0).astype(f32) / L
        taps = z * jnp.exp(-tw * jnp.abs(dec_ref[...]))
        bwd = (_lane_group(wide, HY_CH) % 2) == 1
        first = lax.broadcasted_iota(jnp.int32, wide, 0) == 0
        taps = jnp.where(first, jnp.where(bwd, 0.0, taps), taps)
        g_ref[...] = taps.astype(bf16)
        nrm_ref[...] = jnp.sum(jnp.abs(taps), axis=0, keepdims=True)

    ga = jnp.dot(fa_ref[...], g_ref[...], preferred_element_type=f32)
    gb = jnp.dot(fb_ref[...], g_ref[...], preferred_element_type=f32)
    top = (lax.broadcasted_iota(jnp.int32, (tk, HY_CH), 0) + s * tk) == 0
    for o in range(HY_ORDER):
        c0 = o * 2 * HY_CH
        inv = 1.0 / (nrm_ref[:, c0:c0 + HY_CH] + nrm_ref[:, c0 + HY_CH:c0 + 2 * HY_CH])
        hc = (ga[:, c0:c0 + HY_CH] + ga[:, c0 + HY_CH:c0 + 2 * HY_CH]) * inv
        bf_, bb_ = gb[:, c0:c0 + HY_CH], gb[:, c0 + HY_CH:c0 + 2 * HY_CH]
        hs = jnp.where(top, bf_ + bb_, bf_ - bb_) * inv
        oc = slice(o * HY_CH, (o + 1) * HY_CH)
        p_ref[:, oc] = hc
        q_ref[:, oc] = jnp.where(top, 0.0, hs)
        r_ref[:, oc] = jnp.where(top, hs, hc)


def _hy_filter_spectra(L, fwd_bf, w1, b1, w2, b2, w3, b3, freq, decay, tk=256):
    nk = L // tk
    wide = HY_ORDER * 2 * HY_CH
    full = lambda a: pl.BlockSpec(a.shape, lambda s: (0,) * a.ndim)
    w1p = jnp.pad(w1, ((0, LANES - HY_FEAT), (0, 0)))
    args = [w1p, b1.reshape(1, HY_HID), w2, b2.reshape(1, HY_HID), w3, b3.reshape(1, wide), freq,
            decay.reshape(1, wide)]
    out = jax.ShapeDtypeStruct((L, HY_ORDER * HY_CH), f32)
    plane = pl.BlockSpec((tk, HY_ORDER * HY_CH), lambda s: (s, 0))
    return pl.pallas_call(
        functools.partial(_hy_filter_kernel, L=L, tk=tk),
        out_shape=(out, out, out),
        grid=(nk,),
        in_specs=[full(a) for a in args] + [pl.BlockSpec((tk, L), lambda s: (s, 0)),
                                            pl.BlockSpec((tk, L), lambda s: (s + nk, 0))],
        out_specs=(plane, plane, plane),
        scratch_shapes=[pltpu.VMEM((L, wide), bf16), pltpu.VMEM((1, wide), f32)],
        compiler_params=_cparams(1),
        name="hyena_filter_spectra",
    )(*args, fwd_bf, fwd_bf)


def _hy_pre_kernel(u_ref, w_ref, x1_ref, x2_ref, v_ref, vbf_ref):
    u = u_ref[...]
    w = w_ref[...]
    y = _shift_rows(u, 1) * w[0:1, :] + u * w[1:2, :] + _shift_rows(u, -1) * w[2:3, :]
    x1_ref[...] = y[:, :HY_CH]
    x2_ref[...] = y[:, HY_CH:2 * HY_CH]
    v = y[:, 2 * HY_CH:]
    v_ref[...] = v
    vbf_ref[...] = v.astype(bf16)


def _hy_pre(pb, conv_w, seq0, n_seq, L):
    n = n_seq * L
    blk = pl.BlockSpec((L, HY_CH), lambda b: (b, 0))
    o32 = jax.ShapeDtypeStruct((n, HY_CH), f32)
    return pl.pallas_call(
        _hy_pre_kernel,
        out_shape=(o32, o32, o32, jax.ShapeDtypeStruct((n, HY_CH), bf16)),
        grid=(n_seq,),
        in_specs=[pl.BlockSpec((L, 3 * HY_CH), lambda b: (seq0 + b, 0)),
                  pl.BlockSpec((3, 3 * HY_CH), lambda b: (0, 0))],
        out_specs=(blk, blk, blk, blk),
        compiler_params=_cparams(1),
        name="hyena_short_conv",
    )(pb, conv_w)


def _hy_conv_kernel(*refs, n_seq, L, tk, nk, n_out):
    (zbf_ref, z_ref, m_ref, fa_ref, fb_ref, ic_ref, is_ref, p_ref, q_ref, r_ref, db_ref) = refs[:11]
    out_refs = refs[11:11 + n_out]
    acc_ref = refs[11 + n_out]
    s = pl.program_id(0)

    @pl.when(s == 0)
    def _():
        acc_ref[...] = jnp.zeros_like(acc_ref)

    @pl.when(s < nk)
    def _():
        fa, fb, ic, isn = fa_ref[...], fb_ref[...], ic_ref[...], is_ref[...]
        p, q, r = p_ref[...], q_ref[...], r_ref[...]
        for b in range(n_seq):
            rows = slice(b * L, (b + 1) * L)
            zb = zbf_ref[rows, :]
            a = jnp.dot(fa, zb, preferred_element_type=f32)
            bb = jnp.dot(fb, zb, preferred_element_type=f32)
            yc = (a * p - bb * q).astype(bf16)
            ys = (a * q + bb * r).astype(bf16)
            acc_ref[rows, :] += (jnp.dot(ic, yc, preferred_element_type=f32)
                                 + jnp.dot(isn, ys, preferred_element_type=f32))

    @pl.when(s >= nk)
    def _():
        start = pl.multiple_of((s - nk) * L, L)
        y = acc_ref[pl.ds(start, L), :]
        res = m_ref[...] * (y + db_ref[...] * z_ref[...])
        for o_ref in out_refs:
            o_ref[...] = res.astype(o_ref.dtype)


def _hy_longconv(zbf, z, mult, fwd_bf, inv_bf, planes, order, dbias_o, out_dtypes, n_seq, L, tk):
    nk = L // tk
    n = n_seq * L
    kt = lambda s: jnp.minimum(s, nk - 1)
    ep = lambda s: (jnp.maximum(s - nk, 0), 0)
    plane = pl.BlockSpec((tk, HY_CH), lambda s: (kt(s), order))
    return pl.pallas_call(
        functools.partial(_hy_conv_kernel, n_seq=n_seq, L=L, tk=tk, nk=nk, n_out=len(out_dtypes)),
        out_shape=tuple(jax.ShapeDtypeStruct((n, HY_CH), dt) for dt in out_dtypes),
        grid=(nk + n_seq,),
        in_specs=[
            pl.BlockSpec((n, HY_CH), lambda s: (0, 0)),
            pl.BlockSpec((L, HY_CH), ep),
            pl.BlockSpec((L, HY_CH), ep),
            pl.BlockSpec((tk, L), lambda s: (kt(s), 0)),
            pl.BlockSpec((tk, L), lambda s: (kt(s) + nk, 0)),
            pl.BlockSpec((L, tk), lambda s: (0, kt(s))),
            pl.BlockSpec((L, tk), lambda s: (0, kt(s) + nk)),
            plane, plane, plane,
            pl.BlockSpec((1, HY_CH), lambda s: (0, 0)),
        ],
        out_specs=tuple(pl.BlockSpec((L, HY_CH), ep) for _ in out_dtypes),
        scratch_shapes=[pltpu.VMEM((n, HY_CH), f32)],
        compiler_params=_cparams(1),
        name="hyena_longconv",
    )(zbf, z, mult, fwd_bf, fwd_bf, inv_bf, inv_bf, *planes, dbias_o.reshape(1, HY_CH))


def _hyena(pb, seq0, n_seq, L, dft, hy, tk=256):
    conv_w, w1, b1, w2, b2, w3, b3, freq, decay, dbias = hy
    fwd_bf, inv_bf = dft
    planes = _hy_filter_spectra(L, fwd_bf, w1, b1, w2, b2, w3, b3, freq, decay, tk=tk)
    x1, x2, v, vbf = _hy_pre(pb, conv_w, seq0, n_seq, L)
    z, zbf = _hy_longconv(vbf, v, x1, fwd_bf, inv_bf, planes, 0, dbias[0], (f32, bf16), n_seq, L, tk)
    (ob,) = _hy_longconv(zbf, z, x2, fwd_bf, inv_bf, planes, 1, dbias[1], (bf16,), n_seq, L, tk)
    return ob


def _pool_kernel(u_ref, w_ref, sc_ref, o_ref):
    u = u_ref[...]
    L = u.shape[0]
    back = _shift_rows(u, 1)
    fwd = u
    sums = [back + fwd]
    for k in (1, 2, 4):
        back = back + _shift_rows(back, k)
        fwd = fwd + _shift_rows(fwd, -k)
        sums.append(back + fwd)
    a2, a4, a8, a16 = sums
    grp = _lane_group(u.shape, POOL_GC)
    t = lax.broadcasted_iota(jnp.int32, u.shape, 0)
    half = jnp.left_shift(1, grp)
    cnt = jnp.minimum(t + half, L) - jnp.maximum(t - half, 0)
    tot = jnp.where(grp == 0, a2, jnp.where(grp == 1, a4, jnp.where(grp == 2, a8, a16)))
    pooled = tot / cnt.astype(f32) - u
    y = jnp.dot(pooled.astype(bf16), w_ref[...].astype(bf16), preferred_element_type=f32)
    o_ref[...] = (y * sc_ref[...]).astype(o_ref.dtype)


def _pool(pd, w_bd, scale, seq0, n_seq, L):
    return pl.pallas_call(
        _pool_kernel,
        out_shape=jax.ShapeDtypeStruct((n_seq * L, GROUP_W), bf16),
        grid=(n_seq,),
        in_specs=[pl.BlockSpec((L, GROUP_W), lambda b: (seq0 + b, 0)),
                  pl.BlockSpec((GROUP_W, GROUP_W), lambda b: (0, 0)),
                  pl.BlockSpec((1, GROUP_W), lambda b: (0, 0))],
        out_specs=pl.BlockSpec((L, GROUP_W), lambda b: (b, 0)),
        compiler_params=_cparams(1),
        name="pool_mixer",
    )(pd, w_bd, scale.reshape(1, GROUP_W))


def _outproj_kernel(*refs, tm, moe):
    mix_refs = refs[:8]
    if moe:
        x_ref, mod_ref, g_ref, w_ref, r_ref, x1_ref, h2_ref, lg_ref, wbf_ref = refs[8:]
    else:
        x_ref, mod_ref, g_ref, w_ref, x1_ref, h2_ref, wbf_ref = refs[8:]
    i = pl.program_id(0)

    @pl.when(i == 0)
    def _():
        wbf_ref[...] = w_ref[...].astype(bf16)

    row = _mod_row(i, tm)
    d = D_MODEL
    gate1 = mod_ref[pl.ds(row, 1), 2 * d:3 * d]
    shift2 = mod_ref[pl.ds(row, 1), 3 * d:4 * d]
    scale2 = mod_ref[pl.ds(row, 1), 4 * d:5 * d]
    is_prompt = i < N_PROMPT_TOK // tm
    mixed = jnp.concatenate(
        [jnp.where(is_prompt, mix_refs[2 * j][...], mix_refs[2 * j + 1][...]) for j in range(4)], axis=-1)
    mix = jnp.dot(mixed, wbf_ref[...], preferred_element_type=f32)
    x1 = x_ref[...] + gate1 * mix
    x1_ref[...] = x1
    h = _modulated_norm(x1, g_ref[...], shift2, scale2)
    h2_ref[...] = h.astype(bf16)
    if moe:
        lg_ref[...] = _dot_f32(h, r_ref[...])


def _outproj(mixers, x, mod_l, g, w_out_l, router_l=None, tm=512):
    moe = router_l is not None
    npb = N_PROMPT_TOK // tm
    tok = lambda w: pl.BlockSpec((tm, w), lambda i: (i, 0))
    in_specs, args = [], []
    for op, os_ in mixers:
        in_specs.append(pl.BlockSpec((tm, GROUP_W), lambda i: (jnp.minimum(i, npb - 1), 0)))
        in_specs.append(pl.BlockSpec((tm, GROUP_W), lambda i: (jnp.maximum(i - npb, 0), 0)))
        args += [op, os_]
    in_specs += [tok(D_MODEL),
                 pl.BlockSpec((MOD_ROWS, 6 * D_MODEL), lambda i: (0, 0)),
                 pl.BlockSpec((1, D_MODEL), lambda i: (0, 0)),
                 pl.BlockSpec((D_MODEL, D_MODEL), lambda i: (0, 0))]
    args += [x, mod_l, g.reshape(1, D_MODEL), w_out_l]
    out_shape = [jax.ShapeDtypeStruct((N_TOK, D_MODEL), f32), jax.ShapeDtypeStruct((N_TOK, D_MODEL), bf16)]
    out_specs = [tok(D_MODEL), tok(D_MODEL)]
    if moe:
        in_specs.append(pl.BlockSpec((D_MODEL, ROUTER_PAD), lambda i: (0, 0)))
        args.append(jnp.pad(router_l, ((0, 0), (0, ROUTER_PAD - N_EXPERTS))))
        out_shape.append(jax.ShapeDtypeStruct((N_TOK, ROUTER_PAD), f32))
        out_specs.append(tok(ROUTER_PAD))
    return pl.pallas_call(
        functools.partial(_outproj_kernel, tm=tm, moe=moe),
        out_shape=tuple(out_shape),
        grid=(N_TOK // tm,),
        in_specs=in_specs,
        out_specs=tuple(out_specs),
        scratch_shapes=[pltpu.VMEM((D_MODEL, D_MODEL), bf16)],
        compiler_params=_cparams(1),
        name="outproj_norm2",
    )(*args)


def _ffn_kernel(be_ref, nv_ref, x_ref, w1_ref, w3_ref, w2_ref, o_ref):
    i = pl.program_id(0)
    j = pl.program_id(1)

    @pl.when(j == 0)
    def _():
        o_ref[...] = jnp.zeros_like(o_ref)

    @pl.when(i < nv_ref[0])
    def _():
        x = x_ref[...]
        h1 = jnp.dot(x, w1_ref[0].astype(bf16), preferred_element_type=f32)
        h3 = jnp.dot(x, w3_ref[0].astype(bf16), preferred_element_type=f32)
        a = (h1 * jax.nn.sigmoid(h1)) * h3
        o_ref[...] += jnp.dot(a.astype(bf16), w2_ref[0].astype(bf16), preferred_element_type=f32)


def _ffn_grouped(x_rows, w1, w3, w2, blk_expert, n_valid, tm, tf):
    n_rows = x_rows.shape[0]
    ffn = w1.shape[-1]
    nf = ffn // tf
    assert nf * tf == ffn and n_rows % tm == 0

    def wcol(i, j, be, nv):
        return (be[i], 0, jnp.where(i < nv[0], j, nf - 1))

    def wrow(i, j, be, nv):
        return (be[i], jnp.where(i < nv[0], j, nf - 1), 0)

    return pl.pallas_call(
        _ffn_kernel,
        out_shape=jax.ShapeDtypeStruct((n_rows, D_MODEL), f32),
        grid_spec=pltpu.PrefetchScalarGridSpec(
            num_scalar_prefetch=2,
            grid=(n_rows // tm, nf),
            in_specs=[
                pl.BlockSpec((tm, D_MODEL), lambda i, j, be, nv: (i, 0)),
                pl.BlockSpec((1, D_MODEL, tf), wcol),
                pl.BlockSpec((1, D_MODEL, tf), wcol),
                pl.BlockSpec((1, tf, D_MODEL), wrow),
            ],
            out_specs=pl.BlockSpec((tm, D_MODEL), lambda i, j, be, nv: (i, 0)),
        ),
        compiler_params=_cparams(2),
        name="swiglu_grouped",
    )(blk_expert, n_valid, x_rows, w1, w3, w2)


def _residual_kernel(*refs, tm, n_terms, final):
    x1_ref, mod_ref = refs[0], refs[1]
    f_refs = refs[2:2 + n_terms]
    rest = refs[2 + n_terms:]
    i = pl.program_id(0)
    row = _mod_row(i, tm)
    gate2 = mod_ref[pl.ds(row, 1), 5 * D_MODEL:6 * D_MODEL]
    f = f_refs[0][...]
    for r in f_refs[1:]:
        f = f + r[...]
    x2 = x1_ref[...] + gate2 * f
    if final:
        g_ref, o_ref = rest
        ms = jnp.mean(x2 * x2, axis=-1, keepdims=True)
        o_ref[...] = x2 * lax.rsqrt(ms + EPS) * g_ref[...]
    else:
        (o_ref,) = rest
        o_ref[...] = x2


def _residual(x1, mod_l, terms, final_g=None, tm=512):
    final = final_g is not None
    tok = pl.BlockSpec((tm, D_MODEL), lambda i: (i, 0))
    in_specs = [tok, pl.BlockSpec((MOD_ROWS, 6 * D_MODEL), lambda i: (0, 0))] + [tok] * len(terms)
    args = [x1, mod_l] + list(terms)
    if final:
        in_specs.append(pl.BlockSpec((1, D_MODEL), lambda i: (0, 0)))
        args.append(final_g.reshape(1, D_MODEL))
    return pl.pallas_call(
        functools.partial(_residual_kernel, tm=tm, n_terms=len(terms), final=final),
        out_shape=jax.ShapeDtypeStruct((N_TOK, D_MODEL), f32),
        grid=(N_TOK // tm,),
        in_specs=in_specs,
        out_specs=tok,
        compiler_params=_cparams(1),
        name="ffn_residual",
    )(*args)


def _route(logits, tm):
    top_v, top_i = lax.top_k(logits, TOP_K)
    gates = jax.nn.softmax(top_v, axis=-1)
    flat_e = top_i.reshape(-1)
    onehot = (flat_e[:, None] == jnp.arange(N_EXPERTS)[None, :]).astype(jnp.int32)
    csum = jnp.cumsum(onehot, axis=0)
    rank = jnp.take_along_axis(csum, flat_e[:, None], axis=1)[:, 0] - 1
    counts = csum[-1]
    padded = ((counts + tm - 1) // tm) * tm
    pend = jnp.cumsum(padded)
    pstart = pend - padded
    dest = pstart[flat_e] + rank
    n_rows = N_TOK * TOP_K + N_EXPERTS * tm
    tok_of = jnp.arange(N_TOK * TOP_K, dtype=jnp.int32) // TOP_K
    row_token = jnp.zeros((n_rows,), jnp.int32).at[dest].set(tok_of)
    blk_start = jnp.arange(n_rows // tm, dtype=jnp.int32) * tm
    blk_expert = jnp.minimum(jnp.searchsorted(pend, blk_start, side="right"), N_EXPERTS - 1).astype(jnp.int32)
    n_valid = (pend[-1] // tm).astype(jnp.int32).reshape(1)
    blk_expert = jnp.where(blk_start < pend[-1], blk_expert, blk_expert[jnp.maximum(n_valid[0] - 1, 0)])
    return row_token, blk_expert, n_valid, dest.reshape(N_TOK, TOP_K), gates


def _pool_weight(pool_w_l):
    w = jnp.zeros((GROUP_W, GROUP_W), f32)
    for g in range(len(POOL_WINDOWS)):
        w = w.at[g * POOL_GC:(g + 1) * POOL_GC, g * POOL_GC:(g + 1) * POOL_GC].set(pool_w_l[g])
    return w


def kernel(x_prompt, x_sample, cache_diff_k, cache_diff_v, cache_na_k, cache_na_v, c, c_ctx,
           norm1_g, norm2_g, final_g, ada_w, ada_b, w_in, w_out, diff_lam, diff_subln_g,
           hy_conv, hy_w1, hy_b1, hy_w2, hy_b2, hy_w3, hy_b3, hy_freq, hy_decay, hy_dbias,
           na_rpb, pool_w, pool_scale, ffn_w1, ffn_w3, ffn_w2,
           moe_router, moe_w1, moe_w3, moe_w2):
    cond = jnp.concatenate([c, c_ctx[None, :], jnp.zeros((MOD_ROWS - DEC_BATCH - 1, D_MODEL), f32)], axis=0)
    mod = _ada_mod(cond, ada_w, ada_b)
    cos_np, sin_np = _rope_tables()
    cos_t, sin_t = jnp.asarray(cos_np), jnp.asarray(sin_np)
    dft = {L: tuple(jnp.asarray(m, dtype=bf16) for m in _dft_matrices(L)) for L in (SEQ, DEC_SEQ)}

    P = N_PROMPT_TOK
    x = jnp.concatenate([x_prompt.reshape(P, D_MODEL), x_sample.reshape(N_SAMPLE_TOK, D_MODEL)], axis=0)
    new_kv = [[], [], [], []]
    for l in range(DEPTH):
        pa, pb, pc, pd = _inproj(x, mod[l], norm1_g[l], w_in[l], cos_t, sin_t)
        new_kv[0].append(pa[:P, GROUP_W:2 * GROUP_W].reshape(BATCH, SEQ, A_HEADS, 2 * A_DQK))
        new_kv[1].append(pa[:P, 2 * GROUP_W:].reshape(BATCH, SEQ, A_HEADS, A_DV))
        new_kv[2].append(pc[:P, GROUP_W:2 * GROUP_W].reshape(BATCH, SEQ, C_HEADS, C_DH))
        new_kv[3].append(pc[:P, 2 * GROUP_W:].reshape(BATCH, SEQ, C_HEADS, C_DH))

        lam_init = 0.8 - 0.6 * math.exp(-0.3 * l)
        lv = diff_lam[l].astype(f32)
        lam = (jnp.exp(jnp.sum(lv[0] * lv[1])) - jnp.exp(jnp.sum(lv[2] * lv[3])) + lam_init).reshape(1)
        gain = jnp.tile(diff_subln_g[l], A_HEADS).reshape(1, GROUP_W)
        a_kw = dict(n_heads=A_HEADS, diff=True, scale=A_DQK ** -0.5, lam=lam, gain=gain, out_scale=1.0 - lam_init)
        oa_p = _attention(pa, 0, 0, pa, 1, 0, pa, 2, 0, n_seq=BATCH, seq_len=SEQ, n_keys=SEQ, tq=SEQ, **a_kw)
        n_all = DEC_SEQ + PAST_LEN
        k_all = jnp.concatenate([pa[P:, GROUP_W:2 * GROUP_W].reshape(DEC_BATCH, DEC_SEQ, GROUP_W),
                                 cache_diff_k[:, l].reshape(DEC_BATCH, PAST_LEN, GROUP_W)], axis=1)
        v_all = jnp.concatenate([pa[P:, 2 * GROUP_W:].reshape(DEC_BATCH, DEC_SEQ, GROUP_W),
                                 cache_diff_v[:, l].reshape(DEC_BATCH, PAST_LEN, GROUP_W)], axis=1)
        oa_s = _attention(pa, 0, P // DEC_SEQ, k_all.reshape(-1, GROUP_W), 0, 0, v_all.reshape(-1, GROUP_W), 0, 0,
                          n_seq=DEC_BATCH, seq_len=DEC_SEQ, n_keys=n_all, tq=256, **a_kw)

        hy = (hy_conv[l], hy_w1[l], hy_b1[l], hy_w2[l], hy_b2[l], hy_w3[l], hy_b3[l],
              hy_freq[l], hy_decay[l], hy_dbias[l])
        ob_p = _hyena(pb, 0, BATCH, SEQ, dft[SEQ], hy)
        ob_s = _hyena(pb, P // DEC_SEQ, DEC_BATCH, DEC_SEQ, dft[DEC_SEQ], hy)

        oc_p = _attention(pc, 0, 0, pc, 1, 0, pc, 2, 0, n_seq=BATCH, seq_len=SEQ, n_keys=SEQ, tq=SEQ,
                          n_heads=C_HEADS, diff=False, scale=C_DH ** -0.5)
        oc_s = _na_attention(pc, cache_na_k[:, l].reshape(-1, GROUP_W), cache_na_v[:, l].reshape(-1, GROUP_W),
                             _na_bias_tables(na_rpb[l]))

        w_bd = _pool_weight(pool_w[l])
        od_p = _pool(pd, w_bd, pool_scale[l], 0, BATCH, SEQ)
        od_s = _pool(pd, w_bd, pool_scale[l], P // DEC_SEQ, DEC_BATCH, DEC_SEQ)

        mixers = [(oa_p, oa_s), (ob_p, ob_s), (oc_p, oc_s), (od_p, od_s)]
        final = final_g if l == DEPTH - 1 else None
        if l % 2 == 0:
            x1, h2 = _outproj(mixers, x, mod[l], norm2_g[l], w_out[l])
            tm = 1024
            y = _ffn_grouped(h2, ffn_w1[l // 2][None], ffn_w3[l // 2][None], ffn_w2[l // 2][None],
                             jnp.zeros((N_TOK // tm,), jnp.int32), jnp.full((1,), N_TOK // tm, jnp.int32),
                             tm=tm, tf=256)
            x = _residual(x1, mod[l], [y], final)
        else:
            x1, h2, logits = _outproj(mixers, x, mod[l], norm2_g[l], w_out[l], moe_router[l // 2])
            tm = 1024
            row_token, blk_expert, n_valid, dest, gates = _route(logits[:, :N_EXPERTS], tm)
            xs = jnp.take(h2, row_token, axis=0)
            ys = _ffn_grouped(xs, moe_w1[l // 2], moe_w3[l // 2], moe_w2[l // 2], blk_expert, n_valid,
                              tm=tm, tf=512)
            terms = [gates[:, k:k + 1] * jnp.take(ys, dest[:, k], axis=0) for k in range(TOP_K)]
            x = _residual(x1, mod[l], terms, final)

    y_prompt = x[:P].reshape(BATCH, SEQ, D_MODEL)
    y_sample = x[P:].reshape(DEC_BATCH, DEC_SEQ, D_MODEL)
    return (y_prompt, y_sample) + tuple(jnp.stack(v, axis=1) for v in new_kv)
```

```python
import functools
import math

import numpy as np
import jax
import jax.numpy as jnp
from jax import lax
from jax.experimental import pallas as pl
from jax.experimental.pallas import tpu as pltpu

D_MODEL = 1024
BATCH = 32
SEQ = 256
DEPTH = 2
DEC_BATCH = 4
DEC_SEQ = 2048
PAST_LEN = 512
GRID_W = 64
GRID_H = DEC_SEQ // GRID_W
GROUP_W = D_MODEL // 4
A_HEADS = 4
A_DQK = GROUP_W // (2 * A_HEADS)
A_DV = GROUP_W // A_HEADS
ROPE_BASE = 10000.0
HY_CH = GROUP_W
HY_ORDER = 2
HY_BANDS = 8
HY_FEAT = 1 + 2 * HY_BANDS
HY_HID = 64
C_HEADS = 4
C_DH = GROUP_W // C_HEADS
NA_KH = 8
NA_KW = 16
POOL_WINDOWS = (2, 4, 8, 16)
POOL_GC = GROUP_W // 4
PROJ_W = 3 * GROUP_W + 3 * HY_CH + 3 * GROUP_W + GROUP_W
N_EXPERTS = 8
TOP_K = 2
EPS = 1e-6
NEG = -1e30

N_PROMPT_TOK = BATCH * SEQ
N_SAMPLE_TOK = DEC_BATCH * DEC_SEQ
N_TOK = N_PROMPT_TOK + N_SAMPLE_TOK
MOD_ROWS = 8
CTX_ROW = DEC_BATCH
ROUTER_PAD = 128
LANES = 128

NA_QROWS = 4
NA_SLAB_ROWS = 12
NA_GROUPS = GRID_H // NA_QROWS

f32 = jnp.float32
bf16 = jnp.bfloat16

VMEM_LIMIT = 56 * 1024 * 1024
_NT = (((1,), (1,)), ((), ()))


def _cparams(n_axes):
    return pltpu.CompilerParams(
        dimension_semantics=("arbitrary",) * n_axes, vmem_limit_bytes=VMEM_LIMIT)


def _mod_row(i, tm):
    n_prompt_blocks = N_PROMPT_TOK // tm
    blocks_per_seq = DEC_SEQ // tm
    return jnp.where(i < n_prompt_blocks, CTX_ROW, (i - n_prompt_blocks) // blocks_per_seq)


def _split3(a):
    a0 = a.astype(bf16)
    r1 = a - a0.astype(f32)
    a1 = r1.astype(bf16)
    a2 = (r1 - a1.astype(f32)).astype(bf16)
    return a0, a1, a2


def _dot_bf16x3(a, b):
    a0 = a.astype(bf16)
    a1 = (a - a0.astype(f32)).astype(bf16)
    b0 = b.astype(bf16)
    b1 = (b - b0.astype(f32)).astype(bf16)
    d = functools.partial(jnp.dot, preferred_element_type=f32)
    return (d(a1, b0) + d(a0, b1)) + d(a0, b0)


def _dot_f32(a, b):
    a0, a1, a2 = _split3(a)
    b0, b1, b2 = _split3(b)
    d = functools.partial(jnp.dot, preferred_element_type=f32)
    return ((d(a2, b0) + d(a1, b1) + d(a0, b2)) + (d(a1, b0) + d(a0, b1))) + d(a0, b0)


def _lane_group(shape, width):
    return lax.shift_right_logical(lax.broadcasted_iota(jnp.int32, shape, 1), int(math.log2(width)))


def _shift_rows(x, d):
    n = x.shape[0]
    r = pltpu.roll(x, d % n, 0)
    row = lax.broadcasted_iota(jnp.int32, x.shape, 0)
    keep = (row >= d) if d > 0 else (row < n + d)
    return jnp.where(keep, r, 0.0)


def _ada_kernel(cond_ref, w_ref, b_ref, o_ref):
    c = cond_ref[...]
    s = c * jax.nn.sigmoid(c)
    o_ref[0] = jnp.dot(s.astype(bf16), w_ref[0].astype(bf16), preferred_element_type=f32) + b_ref[0]


def _ada_mod(cond, ada_w, ada_b):
    tn = 1536
    n6 = 6 * D_MODEL
    return pl.pallas_call(
        _ada_kernel,
        out_shape=jax.ShapeDtypeStruct((DEPTH, MOD_ROWS, n6), f32),
        grid=(DEPTH, n6 // tn),
        in_specs=[
            pl.BlockSpec((MOD_ROWS, D_MODEL), lambda l, j: (0, 0)),
            pl.BlockSpec((1, D_MODEL, tn), lambda l, j: (l, 0, j)),
            pl.BlockSpec((1, 1, tn), lambda l, j: (l, 0, j)),
        ],
        out_specs=pl.BlockSpec((1, MOD_ROWS, tn), lambda l, j: (l, 0, j)),
        compiler_params=_cparams(2),
        name="ada_mod",
    )(cond, ada_w, ada_b.reshape(DEPTH, 1, n6))


def _rope_tables():
    t = np.arange(DEC_SEQ)
    nf = A_DQK // 4
    inv = ROPE_BASE ** (-np.arange(nf, dtype=np.float64) / nf)
    ar = (t // GRID_W)[:, None] * inv
    ac = (t % GRID_W)[:, None] * inv
    cos = np.concatenate([np.cos(ar), np.cos(ar), np.cos(ac), np.cos(ac)], axis=1)
    sin = np.concatenate([-np.sin(ar), np.sin(ar), -np.sin(ac), np.sin(ac)], axis=1)
    reps = GROUP_W // A_DQK
    return (np.tile(cos, (1, reps)).astype(np.float32), np.tile(sin, (1, reps)).astype(np.float32))


def _modulated_norm(x, g, shift, scale):
    ms = jnp.mean(x * x, axis=-1, keepdims=True)
    return (x * lax.rsqrt(ms + EPS) * g) * (1.0 + scale) + shift


def _inproj_kernel(x_ref, mod_ref, g_ref, w_ref, cos_ref, sin_ref,
                   pa_ref, pb_ref, pc_ref, pd_ref, wbf_ref, *, tm):
    i = pl.program_id(0)

    @pl.when(i == 0)
    def _():
        wbf_ref[...] = w_ref[...].astype(bf16)

    row = _mod_row(i, tm)
    shift = mod_ref[pl.ds(row, 1), 0:D_MODEL]
    scale = mod_ref[pl.ds(row, 1), D_MODEL:2 * D_MODEL]
    h = _modulated_norm(x_ref[...], g_ref[...], shift, scale)
    proj = jnp.dot(h.astype(bf16), wbf_ref[...], preferred_element_type=f32)
    w3 = 3 * GROUP_W
    pb_ref[...] = proj[:, w3:2 * w3]
    pc_ref[...] = proj[:, 2 * w3:3 * w3]
    pd_ref[...] = proj[:, 3 * w3:]
    pa_ref[:, 2 * GROUP_W:] = proj[:, 2 * GROUP_W:w3]

    @pl.when(i < N_PROMPT_TOK // tm)
    def _():
        pa_ref[:, :2 * GROUP_W] = proj[:, :2 * GROUP_W]

    @pl.when(i >= N_PROMPT_TOK // tm)
    def _():
        cos = cos_ref[...]
        sin = sin_ref[...]
        lane = lax.broadcasted_iota(jnp.int32, (tm, GROUP_W), 1)
        first = (lane % 16) < 8
        for s in range(2):
            v = proj[:, s * GROUP_W:(s + 1) * GROUP_W]
            partner = jnp.where(first, pltpu.roll(v, GROUP_W - 8, 1), pltpu.roll(v, 8, 1))
            pa_ref[:, s * GROUP_W:(s + 1) * GROUP_W] = v * cos + partner * sin


def _inproj(x, mod_l, g, w_in_l, cos_t, sin_t, tm=512):
    n_prompt_blocks = N_PROMPT_TOK // tm
    blocks_per_seq = DEC_SEQ // tm
    w3 = 3 * GROUP_W

    def rope_idx(i):
        return (jnp.maximum(i - n_prompt_blocks, 0) % blocks_per_seq, 0)

    return pl.pallas_call(
        functools.partial(_inproj_kernel, tm=tm),
        out_shape=(jax.ShapeDtypeStruct((N_TOK, w3), f32), jax.ShapeDtypeStruct((N_TOK, w3), f32),
                   jax.ShapeDtypeStruct((N_TOK, w3), f32), jax.ShapeDtypeStruct((N_TOK, GROUP_W), f32)),
        grid=(N_TOK // tm,),
        in_specs=[
            pl.BlockSpec((tm, D_MODEL), lambda i: (i, 0)),
            pl.BlockSpec((MOD_ROWS, 6 * D_MODEL), lambda i: (0, 0)),
            pl.BlockSpec((1, D_MODEL), lambda i: (0, 0)),
            pl.BlockSpec((D_MODEL, PROJ_W), lambda i: (0, 0)),
            pl.BlockSpec((tm, GROUP_W), rope_idx),
            pl.BlockSpec((tm, GROUP_W), rope_idx),
        ],
        out_specs=(pl.BlockSpec((tm, w3), lambda i: (i, 0)), pl.BlockSpec((tm, w3), lambda i: (i, 0)),
                   pl.BlockSpec((tm, w3), lambda i: (i, 0)), pl.BlockSpec((tm, GROUP_W), lambda i: (i, 0))),
        scratch_shapes=[pltpu.VMEM((D_MODEL, PROJ_W), bf16)],
        compiler_params=_cparams(1),
        name="norm1_inproj",
    )(x, mod_l, g.reshape(1, D_MODEL), w_in_l, cos_t, sin_t)


def _attn_kernel(lam_ref, q_ref, k_ref, v_ref, g_ref, o_ref, *, n_heads, diff, scale, out_scale):
    tq = q_ref.shape[0]
    n = k_ref.shape[0]
    q = q_ref[...] * scale
    k = k_ref[...].astype(bf16)
    v = v_ref[...]
    hw = GROUP_W // n_heads
    qgrp = _lane_group((tq, GROUP_W), hw // 2 if diff else hw)
    vhead = _lane_group((n, GROUP_W), hw)
    acc = jnp.zeros((tq, GROUP_W), f32)
    for h in range(n_heads):
        vm = jnp.where(vhead == h, v, 0.0).astype(bf16)
        if diff:
            lam = lam_ref[0]
            pm = []
            for m in range(2):
                qm = jnp.where(qgrp == 2 * h + m, q, 0.0).astype(bf16)
                s = lax.dot_general(qm, k, _NT, preferred_element_type=f32)
                e = jnp.exp(s - jnp.max(s, axis=-1, keepdims=True))
                pm.append(e * (1.0 / jnp.sum(e, axis=-1, keepdims=True)))
            p = (pm[0] - lam * pm[1]).astype(bf16)
            acc = acc + jnp.dot(p, vm, preferred_element_type=f32)
        else:
            qm = jnp.where(qgrp == h, q, 0.0).astype(bf16)
            s = lax.dot_general(qm, k, _NT, preferred_element_type=f32)
            e = jnp.exp(s - jnp.max(s, axis=-1, keepdims=True))
            inv = 1.0 / jnp.sum(e, axis=-1, keepdims=True)
            acc = acc + jnp.dot(e.astype(bf16), vm, preferred_element_type=f32) * inv
    if diff:
        r = lax.shift_right_logical(lax.broadcasted_iota(jnp.int32, (GROUP_W, GROUP_W), 0), 6)
        c = lax.shift_right_logical(lax.broadcasted_iota(jnp.int32, (GROUP_W, GROUP_W), 1), 6)
        bd = jnp.where(r == c, 1.0, 0.0).astype(bf16)
        sq = acc * acc
        hi = sq.astype(bf16)
        lo = (sq - hi.astype(f32)).astype(bf16)
        ms = (jnp.dot(hi, bd, preferred_element_type=f32) + jnp.dot(lo, bd, preferred_element_type=f32)) * (1.0 / A_DV)
        acc = (acc * lax.rsqrt(ms + EPS) * g_ref[...]) * out_scale
    o_ref[...] = acc.astype(o_ref.dtype)


def _attention(q_src, q_col, q_row0, k_src, k_col, k_row0, v_src, v_col, v_row0, *,
               n_seq, seq_len, n_keys, tq, n_heads, diff, scale, lam=None, gain=None, out_scale=1.0):
    qb = seq_len // tq
    if lam is None:
        lam = jnp.zeros((1,), f32)
    if gain is None:
        gain = jnp.ones((1, GROUP_W), f32)
    return pl.pallas_call(
        functools.partial(_attn_kernel, n_heads=n_heads, diff=diff, scale=scale, out_scale=out_scale),
        out_shape=jax.ShapeDtypeStruct((n_seq * seq_len, GROUP_W), bf16),
        grid=(n_seq, qb),
        in_specs=[
            pl.BlockSpec(memory_space=pltpu.SMEM),
            pl.BlockSpec((tq, GROUP_W), lambda b, i: ((q_row0 + b) * qb + i, q_col)),
            pl.BlockSpec((n_keys, GROUP_W), lambda b, i: (k_row0 + b, k_col)),
            pl.BlockSpec((n_keys, GROUP_W), lambda b, i: (v_row0 + b, v_col)),
            pl.BlockSpec((1, GROUP_W), lambda b, i: (0, 0)),
        ],
        out_specs=pl.BlockSpec((tq, GROUP_W), lambda b, i: (b * qb + i, 0)),
        compiler_params=_cparams(2),
        name="diff_attention" if diff else "softmax_attention",
    )(lam, q_src, k_src, v_src, gain)


def _na_group_geometry(g):
    r0 = g * NA_QROWS
    slab0 = min(max(r0 - NA_KH // 2, 0), GRID_H - NA_SLAB_ROWS)
    return r0, slab0


def _na_bias_tables(rpb):
    c = np.arange(GRID_W)
    ws = np.clip(c - NA_KW // 2, 0, GRID_W - NA_KW)
    kc = np.arange(GRID_W)
    col_ok = (kc[None, :] >= ws[:, None]) & (kc[None, :] < ws[:, None] + NA_KW)
    dx = np.clip(kc[None, :] - c[:, None] + NA_KW - 1, 0, 2 * NA_KW - 2)
    tc = jnp.where(col_ok[None, None], rpb.astype(f32)[:, :, dx], NEG)
    neg_blk = jnp.full((C_HEADS, GRID_W, GRID_W), NEG, f32)
    tables = []
    for g in (0, 1, NA_GROUPS - 1):
        r0, slab0 = _na_group_geometry(g)
        rows = []
        for rq in range(NA_QROWS):
            r = r0 + rq
            rs = min(max(r - NA_KH // 2, 0), GRID_H - NA_KH)
            blks = []
            for kl in range(NA_SLAB_ROWS):
                kr = slab0 + kl
                blks.append(tc[:, kr - r + NA_KH - 1] if rs <= kr < rs + NA_KH else neg_blk)
            rows.append(jnp.concatenate(blks, axis=-1))
        tables.append(jnp.concatenate(rows, axis=-2))
    return jnp.stack(tables, axis=0)


def _na_kernel(q_ref, k_ref, v_ref, kx_ref, vx_ref, bias_ref, o_ref):
    g = pl.program_id(1)
    tq = NA_QROWS * GRID_W
    ns = NA_SLAB_ROWS * GRID_W
    slab0 = jnp.clip(g * NA_QROWS - NA_KH // 2, 0, GRID_H - NA_SLAB_ROWS)
    start = pl.multiple_of(slab0 * GRID_W, GRID_W)
    q = q_ref[...] * (C_DH ** -0.5)
    ks = k_ref[pl.ds(start, ns), :].astype(bf16)
    vs = v_ref[pl.ds(start, ns), :]
    kx = kx_ref[...].astype(bf16)
    vx = vx_ref[...]
    qhead = _lane_group((tq, GROUP_W), C_DH)
    vshead = _lane_group((ns, GROUP_W), C_DH)
    vxhead = _lane_group((PAST_LEN, GROUP_W), C_DH)
    acc = jnp.zeros((tq, GROUP_W), f32)
    for h in range(C_HEADS):
        qm = jnp.where(qhead == h, q, 0.0).astype(bf16)
        sl = lax.dot_general(qm, ks, _NT, preferred_element_type=f32)
        b = bias_ref[0, h]
        sl = jnp.where(b > 0.5 * NEG, sl + b, NEG)
        sx = lax.dot_general(qm, kx, _NT, preferred_element_type=f32)
        mx = jnp.maximum(jnp.max(sl, axis=-1, keepdims=True), jnp.max(sx, axis=-1, keepdims=True))
        el = jnp.exp(sl - mx)
        ex = jnp.exp(sx - mx)
        inv = 1.0 / (jnp.sum(el, axis=-1, keepdims=True) + jnp.sum(ex, axis=-1, keepdims=True))
        vsm = jnp.where(vshead == h, vs, 0.0).astype(bf16)
        vxm = jnp.where(vxhead == h, vx, 0.0).astype(bf16)
        o = (jnp.dot(el.astype(bf16), vsm, preferred_element_type=f32)
             + jnp.dot(ex.astype(bf16), vxm, preferred_element_type=f32))
        acc = acc + o * inv
    o_ref[...] = acc.astype(o_ref.dtype)


def _na_attention(pc, kx, vx, bias):
    tq = NA_QROWS * GRID_W
    ns = NA_SLAB_ROWS * GRID_W
    q_blk0 = N_PROMPT_TOK // tq
    s_blk0 = N_PROMPT_TOK // DEC_SEQ

    def bias_idx(b, g):
        return (jnp.where(g == 0, 0, jnp.where(g == NA_GROUPS - 1, 2, 1)), 0, 0, 0)

    return pl.pallas_call(
        _na_kernel,
        out_shape=jax.ShapeDtypeStruct((N_SAMPLE_TOK, GROUP_W), bf16),
        grid=(DEC_BATCH, NA_GROUPS),
        in_specs=[
            pl.BlockSpec((tq, GROUP_W), lambda b, g: (q_blk0 + b * NA_GROUPS + g, 0)),
            pl.BlockSpec((DEC_SEQ, GROUP_W), lambda b, g: (s_blk0 + b, 1)),
            pl.BlockSpec((DEC_SEQ, GROUP_W), lambda b, g: (s_blk0 + b, 2)),
            pl.BlockSpec((PAST_LEN, GROUP_W), lambda b, g: (b, 0)),
            pl.BlockSpec((PAST_LEN, GROUP_W), lambda b, g: (b, 0)),
            pl.BlockSpec((1, C_HEADS, tq, ns), bias_idx),
        ],
        out_specs=pl.BlockSpec((tq, GROUP_W), lambda b, g: (b * NA_GROUPS + g, 0)),
        compiler_params=_cparams(2),
        name="neighbourhood_attention",
    )(pc, pc, pc, kx, vx, bias)


def _dft_matrices(L):
    n = 2 * L
    k = np.arange(L)[:, None]
    s = np.arange(L)[None, :]
    ang = 2.0 * np.pi * ((k * s) % n) / n
    cos, sin = np.cos(ang), np.sin(ang)
    sin[0, :] = (-1.0) ** np.arange(L)
    fwd = np.concatenate([cos, sin], axis=0)
    wk = np.where(np.arange(L) == 0, 1.0, 2.0)[None, :]
    inv = np.concatenate([cos.T * wk, sin.T * wk], axis=1) / n
    inv[:, L] = ((-1.0) ** np.arange(L)) / n
    return fwd, inv


def _hy_filter_kernel(w1_ref, b1_ref, w2_ref, b2_ref, w3_ref, b3_ref, fr_ref, dec_ref, fa_ref, fb_ref,
                      p_ref, q_ref, r_ref, g_ref, nrm_ref, *, L, tk):
    s = pl.program_id(0)

    @pl.when(s == 0)
    def _():
        row = lax.broadcasted_iota(jnp.int32, (L, LANES), 0)
        lane = lax.broadcasted_iota(jnp.int32, (L, LANES), 1)
        t = row.astype(f32) / L
        band = jnp.where(lane <= HY_BANDS, lane, lane - HY_BANDS).astype(f32)
        ang = (2.0 * math.pi * band) * t
        feat = jnp.where(lane == 0, t, jnp.where(lane <= HY_BANDS, jnp.sin(ang),
                                                 jnp.where(lane <= 2 * HY_BANDS, jnp.cos(ang), 0.0)))
        z = jnp.sin(fr_ref[0:1, :] * (_dot_f32(feat, w1_ref[...]) + b1_ref[...]))
        z = jnp.sin(fr_ref[1:2, :] * (_dot_f32(z, w2_ref[...]) + b2_ref[...]))
        z = _dot_f32(z, w3_ref[...]) + b3_ref[...]
        wide = (L, HY_ORDER * 2 * HY_CH)
        tw = lax.broadcasted_iota(jnp.int32, wide, 0).astype(f32) / L
        taps = z * jnp.exp(-tw * jnp.abs(dec_ref[...]))
        bwd = (_lane_group(wide, HY_CH) % 2) == 1
        first = lax.broadcasted_iota(jnp.int32, wide, 0) == 0
        taps = jnp.where(first, jnp.where(bwd, 0.0, taps), taps)
        g_ref[...] = taps.astype(bf16)
        nrm_ref[...] = jnp.sum(jnp.abs(taps), axis=0, keepdims=True)

    ga = jnp.dot(fa_ref[...], g_ref[...], preferred_element_type=f32)
    gb = jnp.dot(fb_ref[...], g_ref[...], preferred_element_type=f32)
    top = (lax.broadcasted_iota(jnp.int32, (tk, HY_CH), 0) + s * tk) == 0
    for o in range(HY_ORDER):
        c0 = o * 2 * HY_CH
        inv = 1.0 / (nrm_ref[:, c0:c0 + HY_CH] + nrm_ref[:, c0 + HY_CH:c0 + 2 * HY_CH])
        hc = (ga[:, c0:c0 + HY_CH] + ga[:, c0 + HY_CH:c0 + 2 * HY_CH]) * inv
        bf_, bb_ = gb[:, c0:c0 + HY_CH], gb[:, c0 + HY_CH:c0 + 2 * HY_CH]
        hs = jnp.where(top, bf_ + bb_, bf_ - bb_) * inv
        oc = slice(o * HY_CH, (o + 1) * HY_CH)
        p_ref[:, oc] = hc
        q_ref[:, oc] = jnp.where(top, 0.0, hs)
        r_ref[:, oc] = jnp.where(top, hs, hc)


def _hy_filter_spectra(L, fwd_bf, w1, b1, w2, b2, w3, b3, freq, decay, tk=256):
    nk = L // tk
    wide = HY_ORDER * 2 * HY_CH
    full = lambda a: pl.BlockSpec(a.shape, lambda s: (0,) * a.ndim)
    w1p = jnp.pad(w1, ((0, LANES - HY_FEAT), (0, 0)))
    args = [w1p, b1.reshape(1, HY_HID), w2, b2.reshape(1, HY_HID), w3, b3.reshape(1, wide), freq,
            decay.reshape(1, wide)]
    out = jax.ShapeDtypeStruct((L, HY_ORDER * HY_CH), f32)
    plane = pl.BlockSpec((tk, HY_ORDER * HY_CH), lambda s: (s, 0))
    return pl.pallas_call(
        functools.partial(_hy_filter_kernel, L=L, tk=tk),
        out_shape=(out, out, out),
        grid=(nk,),
        in_specs=[full(a) for a in args] + [pl.BlockSpec((tk, L), lambda s: (s, 0)),
                                            pl.BlockSpec((tk, L), lambda s: (s + nk, 0))],
        out_specs=(plane, plane, plane),
        scratch_shapes=[pltpu.VMEM((L, wide), bf16), pltpu.VMEM((1, wide), f32)],
        compiler_params=_cparams(1),
        name="hyena_filter_spectra",
    )(*args, fwd_bf, fwd_bf)


def _hy_pre_kernel(u_ref, w_ref, x1_ref, x2_ref, v_ref, vbf_ref):
    u = u_ref[...]
    w = w_ref[...]
    y = _shift_rows(u, 1) * w[0:1, :] + u * w[1:2, :] + _shift_rows(u, -1) * w[2:3, :]
    x1_ref[...] = y[:, :HY_CH]
    x2_ref[...] = y[:, HY_CH:2 * HY_CH]
    v = y[:, 2 * HY_CH:]
    v_ref[...] = v
    vbf_ref[...] = v.astype(bf16)


def _hy_pre(pb, conv_w, seq0, n_seq, L):
    n = n_seq * L
    blk = pl.BlockSpec((L, HY_CH), lambda b: (b, 0))
    o32 = jax.ShapeDtypeStruct((n, HY_CH), f32)
    return pl.pallas_call(
        _hy_pre_kernel,
        out_shape=(o32, o32, o32, jax.ShapeDtypeStruct((n, HY_CH), bf16)),
        grid=(n_seq,),
        in_specs=[pl.BlockSpec((L, 3 * HY_CH), lambda b: (seq0 + b, 0)),
                  pl.BlockSpec((3, 3 * HY_CH), lambda b: (0, 0))],
        out_specs=(blk, blk, blk, blk),
        compiler_params=_cparams(1),
        name="hyena_short_conv",
    )(pb, conv_w)


def _hy_conv_kernel(*refs, n_seq, L, tk, nk, n_out):
    (zbf_ref, z_ref, m_ref, fa_ref, fb_ref, ic_ref, is_ref, p_ref, q_ref, r_ref, db_ref) = refs[:11]
    out_refs = refs[11:11 + n_out]
    acc_ref = refs[11 + n_out]
    s = pl.program_id(0)

    @pl.when(s == 0)
    def _():
        acc_ref[...] = jnp.zeros_like(acc_ref)

    @pl.when(s < nk)
    def _():
        fa, fb, ic, isn = fa_ref[...], fb_ref[...], ic_ref[...], is_ref[...]
        p, q, r = p_ref[...], q_ref[...], r_ref[...]
        for b in range(n_seq):
            rows = slice(b * L, (b + 1) * L)
            zb = zbf_ref[rows, :]
            a = jnp.dot(fa, zb, preferred_element_type=f32)
            bb = jnp.dot(fb, zb, preferred_element_type=f32)
            yc = (a * p - bb * q).astype(bf16)
            ys = (a * q + bb * r).astype(bf16)
            acc_ref[rows, :] += (jnp.dot(ic, yc, preferred_element_type=f32)
                                 + jnp.dot(isn, ys, preferred_element_type=f32))

    @pl.when(s >= nk)
    def _():
        start = pl.multiple_of((s - nk) * L, L)
        y = acc_ref[pl.ds(start, L), :]
        res = m_ref[...] * (y + db_ref[...] * z_ref[...])
        for o_ref in out_refs:
            o_ref[...] = res.astype(o_ref.dtype)


def _hy_longconv(zbf, z, mult, fwd_bf, inv_bf, planes, order, dbias_o, out_dtypes, n_seq, L, tk):
    nk = L // tk
    n = n_seq * L
    kt = lambda s: jnp.minimum(s, nk - 1)
    ep = lambda s: (jnp.maximum(s - nk, 0), 0)
    plane = pl.BlockSpec((tk, HY_CH), lambda s: (kt(s), order))
    return pl.pallas_call(
        functools.partial(_hy_conv_kernel, n_seq=n_seq, L=L, tk=tk, nk=nk, n_out=len(out_dtypes)),
        out_shape=tuple(jax.ShapeDtypeStruct((n, HY_CH), dt) for dt in out_dtypes),
        grid=(nk + n_seq,),
        in_specs=[
            pl.BlockSpec((n, HY_CH), lambda s: (0, 0)),
            pl.BlockSpec((L, HY_CH), ep),
            pl.BlockSpec((L, HY_CH), ep),
            pl.BlockSpec((tk, L), lambda s: (kt(s), 0)),
            pl.BlockSpec((tk, L), lambda s: (kt(s) + nk, 0)),
            pl.BlockSpec((L, tk), lambda s: (0, kt(s))),
            pl.BlockSpec((L, tk), lambda s: (0, kt(s) + nk)),
            plane, plane, plane,
            pl.BlockSpec((1, HY_CH), lambda s: (0, 0)),
        ],
        out_specs=tuple(pl.BlockSpec((L, HY_CH), ep) for _ in out_dtypes),
        scratch_shapes=[pltpu.VMEM((n, HY_CH), f32)],
        compiler_params=_cparams(1),
        name="hyena_longconv",
    )(zbf, z, mult, fwd_bf, fwd_bf, inv_bf, inv_bf, *planes, dbias_o.reshape(1, HY_CH))


def _hyena(pb, seq0, n_seq, L, dft, hy, tk=256):
    conv_w, w1, b1, w2, b2, w3, b3, freq, decay, dbias = hy
    fwd_bf, inv_bf = dft
    planes = _hy_filter_spectra(L, fwd_bf, w1, b1, w2, b2, w3, b3, freq, decay, tk=tk)
    x1, x2, v, vbf = _hy_pre(pb, conv_w, seq0, n_seq, L)
    z, zbf = _hy_longconv(vbf, v, x1, fwd_bf, inv_bf, planes, 0, dbias[0], (f32, bf16), n_seq, L, tk)
    (ob,) = _hy_longconv(zbf, z, x2, fwd_bf, inv_bf, planes, 1, dbias[1], (bf16,), n_seq, L, tk)
    return ob


def _pool_kernel(u_ref, w_ref, sc_ref, o_ref):
    u = u_ref[...]
    L = u.shape[0]
    back = _shift_rows(u, 1)
    fwd = u
    sums = [back + fwd]
    for k in (1, 2, 4):
        back = back + _shift_rows(back, k)
        fwd = fwd + _shift_rows(fwd, -k)
        sums.append(back + fwd)
    a2, a4, a8, a16 = sums
    grp = _lane_group(u.shape, POOL_GC)
    t = lax.broadcasted_iota(jnp.int32, u.shape, 0)
    half = jnp.left_shift(1, grp)
    cnt = jnp.minimum(t + half, L) - jnp.maximum(t - half, 0)
    tot = jnp.where(grp == 0, a2, jnp.where(grp == 1, a4, jnp.where(grp == 2, a8, a16)))
    pooled = tot / cnt.astype(f32) - u
    y = jnp.dot(pooled.astype(bf16), w_ref[...].astype(bf16), preferred_element_type=f32)
    o_ref[...] = (y * sc_ref[...]).astype(o_ref.dtype)


def _pool(pd, w_bd, scale, seq0, n_seq, L):
    return pl.pallas_call(
        _pool_kernel,
        out_shape=jax.ShapeDtypeStruct((n_seq * L, GROUP_W), bf16),
        grid=(n_seq,),
        in_specs=[pl.BlockSpec((L, GROUP_W), lambda b: (seq0 + b, 0)),
                  pl.BlockSpec((GROUP_W, GROUP_W), lambda b: (0, 0)),
                  pl.BlockSpec((1, GROUP_W), lambda b: (0, 0))],
        out_specs=pl.BlockSpec((L, GROUP_W), lambda b: (b, 0)),
        compiler_params=_cparams(1),
        name="pool_mixer",
    )(pd, w_bd, scale.reshape(1, GROUP_W))


def _outproj_kernel(*refs, tm, moe):
    mix_refs = refs[:8]
    if moe:
        x_ref, mod_ref, g_ref, w_ref, r_ref, x1_ref, h2_ref, lg_ref, wbf_ref = refs[8:]
    else:
        x_ref, mod_ref, g_ref, w_ref, x1_ref, h2_ref, wbf_ref = refs[8:]
    i = pl.program_id(0)

    @pl.when(i == 0)
    def _():
        wbf_ref[...] = w_ref[...].astype(bf16)

    row = _mod_row(i, tm)
    d = D_MODEL
    gate1 = mod_ref[pl.ds(row, 1), 2 * d:3 * d]
    shift2 = mod_ref[pl.ds(row, 1), 3 * d:4 * d]
    scale2 = mod_ref[pl.ds(row, 1), 4 * d:5 * d]
    is_prompt = i < N_PROMPT_TOK // tm
    mixed = jnp.concatenate(
        [jnp.where(is_prompt, mix_refs[2 * j][...], mix_refs[2 * j + 1][...]) for j in range(4)], axis=-1)
    mix = jnp.dot(mixed, wbf_ref[...], preferred_element_type=f32)
    x1 = x_ref[...] + gate1 * mix
    x1_ref[...] = x1
    h = _modulated_norm(x1, g_ref[...], shift2, scale2)
    h2_ref[...] = h.astype(h2_ref.dtype)
    if moe:
        lg_ref[...] = _dot_bf16x3(h, r_ref[...])


def _outproj(mixers, x, mod_l, g, w_out_l, router_l=None, tm=512):
    moe = router_l is not None
    npb = N_PROMPT_TOK // tm
    tok = lambda w: pl.BlockSpec((tm, w), lambda i: (i, 0))
    in_specs, args = [], []
    for op, os_ in mixers:
        in_specs.append(pl.BlockSpec((tm, GROUP_W), lambda i: (jnp.minimum(i, npb - 1), 0)))
        in_specs.append(pl.BlockSpec((tm, GROUP_W), lambda i: (jnp.maximum(i - npb, 0), 0)))
        args += [op, os_]
    in_specs += [tok(D_MODEL),
                 pl.BlockSpec((MOD_ROWS, 6 * D_MODEL), lambda i: (0, 0)),
                 pl.BlockSpec((1, D_MODEL), lambda i: (0, 0)),
                 pl.BlockSpec((D_MODEL, D_MODEL), lambda i: (0, 0))]
    args += [x, mod_l, g.reshape(1, D_MODEL), w_out_l]
    out_shape = [jax.ShapeDtypeStruct((N_TOK, D_MODEL), f32),
                 jax.ShapeDtypeStruct((N_TOK, D_MODEL), f32 if moe else bf16)]
    out_specs = [tok(D_MODEL), tok(D_MODEL)]
    if moe:
        in_specs.append(pl.BlockSpec((D_MODEL, ROUTER_PAD), lambda i: (0, 0)))
        args.append(jnp.pad(router_l, ((0, 0), (0, ROUTER_PAD - N_EXPERTS))))
        out_shape.append(jax.ShapeDtypeStruct((N_TOK, ROUTER_PAD), f32))
        out_specs.append(tok(ROUTER_PAD))
    return pl.pallas_call(
        functools.partial(_outproj_kernel, tm=tm, moe=moe),
        out_shape=tuple(out_shape),
        grid=(N_TOK // tm,),
        in_specs=in_specs,
        out_specs=tuple(out_specs),
        scratch_shapes=[pltpu.VMEM((D_MODEL, D_MODEL), bf16)],
        compiler_params=_cparams(1),
        name="outproj_norm2",
    )(*args)


def _ffn_kernel(*refs, tm, nf, residual):
    if residual:
        be_ref, nv_ref, x_ref, w1_ref, w3_ref, w2_ref, x1_ref, mod_ref, o_ref, xbf_ref = refs
    else:
        be_ref, nv_ref, x_ref, w1_ref, w3_ref, w2_ref, o_ref, xbf_ref = refs
    i = pl.program_id(0)
    j = pl.program_id(1)

    @pl.when(j == 0)
    def _():
        o_ref[...] = jnp.zeros_like(o_ref)
        xbf_ref[...] = x_ref[...].astype(bf16)

    @pl.when(i < nv_ref[0])
    def _():
        x = xbf_ref[...]
        h1 = jnp.dot(x, w1_ref[0].astype(bf16), preferred_element_type=f32)
        h3 = jnp.dot(x, w3_ref[0].astype(bf16), preferred_element_type=f32)
        a = (h1 * jax.nn.sigmoid(h1)) * h3
        o_ref[...] += jnp.dot(a.astype(bf16), w2_ref[0].astype(bf16), preferred_element_type=f32)

    if residual:
        @pl.when(j == nf - 1)
        def _():
            gate2 = mod_ref[pl.ds(_mod_row(i, tm), 1), 5 * D_MODEL:6 * D_MODEL]
            o_ref[...] = x1_ref[...] + gate2 * o_ref[...]


def _ffn_grouped(x_rows, w1, w3, w2, blk_expert, n_valid, tm, tf, x1=None, mod_l=None):
    n_rows = x_rows.shape[0]
    ffn = w1.shape[-1]
    nf = ffn // tf
    assert nf * tf == ffn and n_rows % tm == 0
    residual = x1 is not None

    def wcol(i, j, be, nv):
        return (be[i], 0, jnp.where(i < nv[0], j, nf - 1))

    def wrow(i, j, be, nv):
        return (be[i], jnp.where(i < nv[0], j, nf - 1), 0)

    rows = pl.BlockSpec((tm, D_MODEL), lambda i, j, be, nv: (i, 0))
    in_specs = [rows, pl.BlockSpec((1, D_MODEL, tf), wcol), pl.BlockSpec((1, D_MODEL, tf), wcol),
                pl.BlockSpec((1, tf, D_MODEL), wrow)]
    args = [x_rows, w1, w3, w2]
    if residual:
        in_specs += [rows, pl.BlockSpec((MOD_ROWS, 6 * D_MODEL), lambda i, j, be, nv: (0, 0))]
        args += [x1, mod_l]
    return pl.pallas_call(
        functools.partial(_ffn_kernel, tm=tm, nf=nf, residual=residual),
        out_shape=jax.ShapeDtypeStruct((n_rows, D_MODEL), f32),
        grid_spec=pltpu.PrefetchScalarGridSpec(
            num_scalar_prefetch=2,
            grid=(n_rows // tm, nf),
            in_specs=in_specs,
            out_specs=rows,
            scratch_shapes=[pltpu.VMEM((tm, D_MODEL), bf16)],
        ),
        compiler_params=_cparams(2),
        name="swiglu_grouped",
    )(blk_expert, n_valid, *args)


def _gather_kernel(idx_ref, src_ref, o_ref, buf_ref, sem, *, rb):
    def row_copy(r, src_row):
        return pltpu.make_async_copy(src_ref.at[pl.ds(src_row, 1), :], buf_ref.at[pl.ds(r, 1), :], sem)

    def issue(r, carry):
        row_copy(r, idx_ref[0, 0, r]).start()
        return carry

    def drain(r, carry):
        row_copy(r, 0).wait()
        return carry

    lax.fori_loop(0, rb, issue, 0, unroll=8)
    lax.fori_loop(0, rb, drain, 0, unroll=8)
    o_ref[...] = buf_ref[...]


def _row_gather(src, idx, rb=512):
    n = idx.shape[0]
    w = src.shape[1]
    assert n % rb == 0
    return pl.pallas_call(
        functools.partial(_gather_kernel, rb=rb),
        out_shape=jax.ShapeDtypeStruct((n, w), src.dtype),
        grid=(n // rb,),
        in_specs=[pl.BlockSpec((1, 1, rb), lambda i: (i, 0, 0), memory_space=pltpu.SMEM),
                  pl.BlockSpec(memory_space=pl.ANY)],
        out_specs=pl.BlockSpec((rb, w), lambda i: (i, 0)),
        scratch_shapes=[pltpu.VMEM((rb, w), src.dtype), pltpu.SemaphoreType.DMA],
        compiler_params=_cparams(1),
        name="row_gather",
    )(idx.reshape(n // rb, 1, rb), src)


def _combine_kernel(*refs, tm, final):
    if final:
        x1_ref, mod_ref, y2_ref, gt_ref, g_ref, o_ref = refs
    else:
        x1_ref, mod_ref, y2_ref, gt_ref, o_ref = refs
    i = pl.program_id(0)
    gate2 = mod_ref[pl.ds(_mod_row(i, tm), 1), 5 * D_MODEL:6 * D_MODEL]
    gt = gt_ref[...]
    f = gt[:, 0:1] * y2_ref[:, :D_MODEL] + gt[:, 1:2] * y2_ref[:, D_MODEL:]
    x2 = x1_ref[...] + gate2 * f
    if final:
        ms = jnp.mean(x2 * x2, axis=-1, keepdims=True)
        x2 = x2 * lax.rsqrt(ms + EPS) * g_ref[...]
    o_ref[...] = x2


def _combine(x1, mod_l, y2, gates, final_g=None, tm=512):
    final = final_g is not None
    tok = pl.BlockSpec((tm, D_MODEL), lambda i: (i, 0))
    in_specs = [tok, pl.BlockSpec((MOD_ROWS, 6 * D_MODEL), lambda i: (0, 0)),
                pl.BlockSpec((tm, TOP_K * D_MODEL), lambda i: (i, 0)),
                pl.BlockSpec((tm, TOP_K), lambda i: (i, 0))]
    args = [x1, mod_l, y2, gates]
    if final:
        in_specs.append(pl.BlockSpec((1, D_MODEL), lambda i: (0, 0)))
        args.append(final_g.reshape(1, D_MODEL))
    return pl.pallas_call(
        functools.partial(_combine_kernel, tm=tm, final=final),
        out_shape=jax.ShapeDtypeStruct((N_TOK, D_MODEL), f32),
        grid=(N_TOK // tm,),
        in_specs=in_specs,
        out_specs=tok,
        compiler_params=_cparams(1),
        name="expert_combine",
    )(*args)


def _route(logits, tm):
    eid = jnp.arange(N_EXPERTS, dtype=jnp.int32)[None, :]
    v0 = jnp.max(logits, axis=-1, keepdims=True)
    i0 = jnp.min(jnp.where(logits == v0, eid, N_EXPERTS), axis=-1, keepdims=True)
    rest = jnp.where(eid == i0, -jnp.inf, logits)
    v1 = jnp.max(rest, axis=-1, keepdims=True)
    i1 = jnp.min(jnp.where(rest == v1, eid, N_EXPERTS), axis=-1, keepdims=True)
    gates = jax.nn.softmax(jnp.concatenate([v0, v1], axis=-1), axis=-1)
    flat_e = jnp.concatenate([i0, i1], axis=-1).reshape(-1)
    onehot = (flat_e[:, None] == jnp.arange(N_EXPERTS)[None, :]).astype(jnp.int32)
    csum = jnp.cumsum(onehot, axis=0)
    rank = jnp.take_along_axis(csum, flat_e[:, None], axis=1)[:, 0] - 1
    counts = csum[-1]
    padded = ((counts + tm - 1) // tm) * tm
    pend = jnp.cumsum(padded)
    pstart = pend - padded
    dest = pstart[flat_e] + rank
    n_rows = N_TOK * TOP_K + N_EXPERTS * tm
    tok_of = jnp.arange(N_TOK * TOP_K, dtype=jnp.int32) // TOP_K
    row_token = jnp.zeros((n_rows,), jnp.int32).at[dest].set(tok_of)
    blk_start = jnp.arange(n_rows // tm, dtype=jnp.int32) * tm
    blk_expert = jnp.minimum(jnp.searchsorted(pend, blk_start, side="right"), N_EXPERTS - 1).astype(jnp.int32)
    n_valid = (pend[-1] // tm).astype(jnp.int32).reshape(1)
    blk_expert = jnp.where(blk_start < pend[-1], blk_expert, blk_expert[jnp.maximum(n_valid[0] - 1, 0)])
    return row_token, blk_expert, n_valid, dest.reshape(N_TOK, TOP_K), gates


def _pool_weight(pool_w_l):
    w = jnp.zeros((GROUP_W, GROUP_W), f32)
    for g in range(len(POOL_WINDOWS)):
        w = w.at[g * POOL_GC:(g + 1) * POOL_GC, g * POOL_GC:(g + 1) * POOL_GC].set(pool_w_l[g])
    return w


def kernel(x_prompt, x_sample, cache_diff_k, cache_diff_v, cache_na_k, cache_na_v, c, c_ctx,
           norm1_g, norm2_g, final_g, ada_w, ada_b, w_in, w_out, diff_lam, diff_subln_g,
           hy_conv, hy_w1, hy_b1, hy_w2, hy_b2, hy_w3, hy_b3, hy_freq, hy_decay, hy_dbias,
           na_rpb, pool_w, pool_scale, ffn_w1, ffn_w3, ffn_w2,
           moe_router, moe_w1, moe_w3, moe_w2):
    cond = jnp.concatenate([c, c_ctx[None, :], jnp.zeros((MOD_ROWS - DEC_BATCH - 1, D_MODEL), f32)], axis=0)
    mod = _ada_mod(cond, ada_w, ada_b)
    cos_np, sin_np = _rope_tables()
    cos_t, sin_t = jnp.asarray(cos_np), jnp.asarray(sin_np)
    dft = {L: tuple(jnp.asarray(m, dtype=bf16) for m in _dft_matrices(L)) for L in (SEQ, DEC_SEQ)}

    P = N_PROMPT_TOK
    x = jnp.concatenate([x_prompt.reshape(P, D_MODEL), x_sample.reshape(N_SAMPLE_TOK, D_MODEL)], axis=0)
    new_kv = [[], [], [], []]
    for l in range(DEPTH):
        pa, pb, pc, pd = _inproj(x, mod[l], norm1_g[l], w_in[l], cos_t, sin_t)
        new_kv[0].append(pa[:P, GROUP_W:2 * GROUP_W].reshape(BATCH, SEQ, A_HEADS, 2 * A_DQK))
        new_kv[1].append(pa[:P, 2 * GROUP_W:].reshape(BATCH, SEQ, A_HEADS, A_DV))
        new_kv[2].append(pc[:P, GROUP_W:2 * GROUP_W].reshape(BATCH, SEQ, C_HEADS, C_DH))
        new_kv[3].append(pc[:P, 2 * GROUP_W:].reshape(BATCH, SEQ, C_HEADS, C_DH))

        lam_init = 0.8 - 0.6 * math.exp(-0.3 * l)
        lv = diff_lam[l].astype(f32)
        lam = (jnp.exp(jnp.sum(lv[0] * lv[1])) - jnp.exp(jnp.sum(lv[2] * lv[3])) + lam_init).reshape(1)
        gain = jnp.tile(diff_subln_g[l], A_HEADS).reshape(1, GROUP_W)
        a_kw = dict(n_heads=A_HEADS, diff=True, scale=A_DQK ** -0.5, lam=lam, gain=gain, out_scale=1.0 - lam_init)
        oa_p = _attention(pa, 0, 0, pa, 1, 0, pa, 2, 0, n_seq=BATCH, seq_len=SEQ, n_keys=SEQ, tq=SEQ, **a_kw)
        n_all = DEC_SEQ + PAST_LEN
        k_all = jnp.concatenate([pa[P:, GROUP_W:2 * GROUP_W].reshape(DEC_BATCH, DEC_SEQ, GROUP_W),
                                 cache_diff_k[:, l].reshape(DEC_BATCH, PAST_LEN, GROUP_W)], axis=1)
        v_all = jnp.concatenate([pa[P:, 2 * GROUP_W:].reshape(DEC_BATCH, DEC_SEQ, GROUP_W),
                                 cache_diff_v[:, l].reshape(DEC_BATCH, PAST_LEN, GROUP_W)], axis=1)
        oa_s = _attention(pa, 0, P // DEC_SEQ, k_all.reshape(-1, GROUP_W), 0, 0, v_all.reshape(-1, GROUP_W), 0, 0,
                          n_seq=DEC_BATCH, seq_len=DEC_SEQ, n_keys=n_all, tq=256, **a_kw)

        hy = (hy_conv[l], hy_w1[l], hy_b1[l], hy_w2[l], hy_b2[l], hy_w3[l], hy_b3[l],
              hy_freq[l], hy_decay[l], hy_dbias[l])
        ob_p = _hyena(pb, 0, BATCH, SEQ, dft[SEQ], hy)
        ob_s = _hyena(pb, P // DEC_SEQ, DEC_BATCH, DEC_SEQ, dft[DEC_SEQ], hy)

        oc_p = _attention(pc, 0, 0, pc, 1, 0, pc, 2, 0, n_seq=BATCH, seq_len=SEQ, n_keys=SEQ, tq=SEQ,
                          n_heads=C_HEADS, diff=False, scale=C_DH ** -0.5)
        oc_s = _na_attention(pc, cache_na_k[:, l].reshape(-1, GROUP_W), cache_na_v[:, l].reshape(-1, GROUP_W),
                             _na_bias_tables(na_rpb[l]))

        w_bd = _pool_weight(pool_w[l])
        od_p = _pool(pd, w_bd, pool_scale[l], 0, BATCH, SEQ)
        od_s = _pool(pd, w_bd, pool_scale[l], P // DEC_SEQ, DEC_BATCH, DEC_SEQ)

        mixers = [(oa_p, oa_s), (ob_p, ob_s), (oc_p, oc_s), (od_p, od_s)]
        tm = 1024
        if l % 2 == 0:
            assert l != DEPTH - 1
            x1, h2 = _outproj(mixers, x, mod[l], norm2_g[l], w_out[l])
            x = _ffn_grouped(h2, ffn_w1[l // 2][None], ffn_w3[l // 2][None], ffn_w2[l // 2][None],
                             jnp.zeros((N_TOK // tm,), jnp.int32), jnp.full((1,), N_TOK // tm, jnp.int32),
                             tm=tm, tf=256, x1=x1, mod_l=mod[l])
        else:
            x1, h2, logits = _outproj(mixers, x, mod[l], norm2_g[l], w_out[l], moe_router[l // 2])
            row_token, blk_expert, n_valid, dest, gates = _route(logits[:, :N_EXPERTS], tm)
            xs = _row_gather(h2, row_token)
            ys = _ffn_grouped(xs, moe_w1[l // 2], moe_w3[l // 2], moe_w2[l // 2], blk_expert, n_valid,
                              tm=tm, tf=512)
            y2 = _row_gather(ys, dest.reshape(-1)).reshape(N_TOK, TOP_K * D_MODEL)
            x = _combine(x1, mod[l], y2, gates, final_g if l == DEPTH - 1 else None)

    y_prompt = x[:P].reshape(BATCH, SEQ, D_MODEL)
    y_sample = x[P:].reshape(DEC_BATCH, DEC_SEQ, D_MODEL)
    return (y_prompt, y_sample) + tuple(jnp.stack(v, axis=1) for v in new_kv)
```

```python
import functools
import math

import numpy as np
import jax
import jax.numpy as jnp
from jax import lax
from jax.experimental import pallas as pl
from jax.experimental.pallas import tpu as pltpu

D_MODEL = 1024
BATCH = 32
SEQ = 256
DEPTH = 2
DEC_BATCH = 4
DEC_SEQ = 2048
PAST_LEN = 512
GRID_W = 64
GRID_H = DEC_SEQ // GRID_W
GROUP_W = D_MODEL // 4
A_HEADS = 4
A_DQK = GROUP_W // (2 * A_HEADS)
A_DV = GROUP_W // A_HEADS
ROPE_BASE = 10000.0
HY_CH = GROUP_W
HY_ORDER = 2
HY_BANDS = 8
HY_FEAT = 1 + 2 * HY_BANDS
HY_HID = 64
C_HEADS = 4
C_DH = GROUP_W // C_HEADS
NA_KH = 8
NA_KW = 16
POOL_WINDOWS = (2, 4, 8, 16)
POOL_GC = GROUP_W // 4
PROJ_W = 3 * GROUP_W + 3 * HY_CH + 3 * GROUP_W + GROUP_W
N_EXPERTS = 8
TOP_K = 2
EPS = 1e-6
NEG = -1e30
LOG2E = math.log2(math.e)

N_PROMPT_TOK = BATCH * SEQ
N_SAMPLE_TOK = DEC_BATCH * DEC_SEQ
N_TOK = N_PROMPT_TOK + N_SAMPLE_TOK
MOD_ROWS = 8
CTX_ROW = DEC_BATCH
ROUTER_PAD = 128
LANES = 128

NA_QROWS = 4
NA_SLAB_ROWS = 12
NA_GROUPS = GRID_H // NA_QROWS

f32 = jnp.float32
bf16 = jnp.bfloat16

VMEM_LIMIT = 56 * 1024 * 1024
_NT = (((1,), (1,)), ((), ()))


def _cparams(n_axes):
    return pltpu.CompilerParams(
        dimension_semantics=("arbitrary",) * n_axes, vmem_limit_bytes=VMEM_LIMIT)


def _mod_row(i, tm):
    n_prompt_blocks = N_PROMPT_TOK // tm
    blocks_per_seq = DEC_SEQ // tm
    return jnp.where(i < n_prompt_blocks, CTX_ROW, (i - n_prompt_blocks) // blocks_per_seq)


def _split3(a):
    a0 = a.astype(bf16)
    r1 = a - a0.astype(f32)
    a1 = r1.astype(bf16)
    a2 = (r1 - a1.astype(f32)).astype(bf16)
    return a0, a1, a2


def _dot_bf16x3(a, b):
    a0 = a.astype(bf16)
    a1 = (a - a0.astype(f32)).astype(bf16)
    b0 = b.astype(bf16)
    b1 = (b - b0.astype(f32)).astype(bf16)
    d = functools.partial(jnp.dot, preferred_element_type=f32)
    return (d(a1, b0) + d(a0, b1)) + d(a0, b0)


def _dot_f32(a, b):
    a0, a1, a2 = _split3(a)
    b0, b1, b2 = _split3(b)
    d = functools.partial(jnp.dot, preferred_element_type=f32)
    return ((d(a2, b0) + d(a1, b1) + d(a0, b2)) + (d(a1, b0) + d(a0, b1))) + d(a0, b0)


def _lane_group(shape, width):
    return lax.shift_right_logical(lax.broadcasted_iota(jnp.int32, shape, 1), int(math.log2(width)))


def _shift_rows(x, d):
    n = x.shape[0]
    r = pltpu.roll(x, d % n, 0)
    row = lax.broadcasted_iota(jnp.int32, x.shape, 0)
    keep = (row >= d) if d > 0 else (row < n + d)
    return jnp.where(keep, r, 0.0)


def _ada_kernel(cond_ref, w_ref, b_ref, o_ref):
    c = cond_ref[...]
    s = c * jax.nn.sigmoid(c)
    o_ref[0] = jnp.dot(s.astype(bf16), w_ref[0].astype(bf16), preferred_element_type=f32) + b_ref[0]


def _ada_mod(cond, ada_w, ada_b):
    tn = 1536
    n6 = 6 * D_MODEL
    return pl.pallas_call(
        _ada_kernel,
        out_shape=jax.ShapeDtypeStruct((DEPTH, MOD_ROWS, n6), f32),
        grid=(DEPTH, n6 // tn),
        in_specs=[
            pl.BlockSpec((MOD_ROWS, D_MODEL), lambda l, j: (0, 0)),
            pl.BlockSpec((1, D_MODEL, tn), lambda l, j: (l, 0, j)),
            pl.BlockSpec((1, 1, tn), lambda l, j: (l, 0, j)),
        ],
        out_specs=pl.BlockSpec((1, MOD_ROWS, tn), lambda l, j: (l, 0, j)),
        compiler_params=_cparams(2),
        name="ada_mod",
    )(cond, ada_w, ada_b.reshape(DEPTH, 1, n6))


def _rope_tables():
    t = np.arange(DEC_SEQ)
    nf = A_DQK // 4
    inv = ROPE_BASE ** (-np.arange(nf, dtype=np.float64) / nf)
    ar = (t // GRID_W)[:, None] * inv
    ac = (t % GRID_W)[:, None] * inv
    cos = np.concatenate([np.cos(ar), np.cos(ar), np.cos(ac), np.cos(ac)], axis=1)
    sin = np.concatenate([-np.sin(ar), np.sin(ar), -np.sin(ac), np.sin(ac)], axis=1)
    reps = GROUP_W // A_DQK
    return (np.tile(cos, (1, reps)).astype(np.float32), np.tile(sin, (1, reps)).astype(np.float32))


def _modulated_norm(x, g, shift, scale):
    ms = jnp.mean(x * x, axis=-1, keepdims=True)
    return (x * lax.rsqrt(ms + EPS) * g) * (1.0 + scale) + shift


def _inproj_kernel(*refs, tm, split_x, n_alias):
    n_x = 2 if split_x else 1
    x_refs = refs[:n_x]
    mod_ref, g_ref, w_ref, cos_ref, sin_ref = refs[n_x:n_x + 5]
    outs = refs[n_x + 5 + n_alias:]
    pa_ref, pb_ref, pc_ref, pd_ref, kak_ref, kav_ref, kck_ref, kcv_ref, wbf_ref = outs
    i = pl.program_id(0)
    is_prompt = i < N_PROMPT_TOK // tm

    @pl.when(i == 0)
    def _():
        wbf_ref[...] = w_ref[...].astype(bf16)

    row = _mod_row(i, tm)
    shift = mod_ref[pl.ds(row, 1), 0:D_MODEL]
    scale = mod_ref[pl.ds(row, 1), D_MODEL:2 * D_MODEL]
    x = jnp.where(is_prompt, x_refs[0][...], x_refs[1][...]) if split_x else x_refs[0][...]
    h = _modulated_norm(x, g_ref[...], shift, scale)
    proj = jnp.dot(h.astype(bf16), wbf_ref[...], preferred_element_type=f32)
    w3 = 3 * GROUP_W
    pb_ref[...] = proj[:, w3:2 * w3]
    pc_ref[...] = proj[:, 2 * w3:3 * w3]
    pd_ref[...] = proj[:, 3 * w3:]
    pa_ref[:, 2 * GROUP_W:] = proj[:, 2 * GROUP_W:w3]

    @pl.when(is_prompt)
    def _():
        pa_ref[:, :2 * GROUP_W] = proj[:, :2 * GROUP_W]
        kv_shape = (tm // SEQ, 1, SEQ, GROUP_W)
        cols = (GROUP_W, 2 * GROUP_W, 2 * w3 + GROUP_W, 2 * w3 + 2 * GROUP_W)
        for ref, c0 in zip((kak_ref, kav_ref, kck_ref, kcv_ref), cols):
            ref[:, 0:1] = proj[:, c0:c0 + GROUP_W].reshape(kv_shape)
            if ref.shape[1] > 1:
                ref[:, 1:] = jnp.zeros((tm // SEQ, ref.shape[1] - 1, SEQ, GROUP_W), f32)

    @pl.when(i >= N_PROMPT_TOK // tm)
    def _():
        cos = cos_ref[...]
        sin = sin_ref[...]
        lane = lax.broadcasted_iota(jnp.int32, (tm, GROUP_W), 1)
        first = (lane % 16) < 8
        for s in range(2):
            v = proj[:, s * GROUP_W:(s + 1) * GROUP_W]
            partner = jnp.where(first, pltpu.roll(v, GROUP_W - 8, 1), pltpu.roll(v, 8, 1))
            pa_ref[:, s * GROUP_W:(s + 1) * GROUP_W] = v * cos + partner * sin


def _inproj(x, layer, kv_prev, mod_l, g, w_in_l, cos_t, sin_t, tm=512):
    npb = N_PROMPT_TOK // tm
    blocks_per_seq = DEC_SEQ // tm
    w3 = 3 * GROUP_W
    split_x = isinstance(x, tuple)

    def rope_idx(i):
        return (jnp.maximum(i - npb, 0) % blocks_per_seq, 0)

    if split_x:
        x_specs = [pl.BlockSpec((tm, D_MODEL), lambda i: (jnp.minimum(i, npb - 1), 0)),
                   pl.BlockSpec((tm, D_MODEL), lambda i: (jnp.maximum(i - npb, 0), 0))]
        x_args = list(x)
    else:
        x_specs = [pl.BlockSpec((tm, D_MODEL), lambda i: (i, 0))]
        x_args = [x]
    kv_args = list(kv_prev) if kv_prev is not None else []
    n_in = len(x_args) + 5
    kv_shape = jax.ShapeDtypeStruct((BATCH, DEPTH, SEQ, GROUP_W), f32)
    kv_layers = DEPTH if layer == 0 else 1
    kv_spec = pl.BlockSpec((tm // SEQ, kv_layers, SEQ, GROUP_W), lambda i: (jnp.minimum(i, npb - 1), layer, 0, 0))
    tok = lambda w: pl.BlockSpec((tm, w), lambda i: (i, 0))
    return pl.pallas_call(
        functools.partial(_inproj_kernel, tm=tm, split_x=split_x, n_alias=len(kv_args)),
        out_shape=(jax.ShapeDtypeStruct((N_TOK, w3), f32), jax.ShapeDtypeStruct((N_TOK, w3), f32),
                   jax.ShapeDtypeStruct((N_TOK, w3), f32), jax.ShapeDtypeStruct((N_TOK, GROUP_W), f32),
                   kv_shape, kv_shape, kv_shape, kv_shape),
        grid=(N_TOK // tm,),
        in_specs=x_specs + [
            pl.BlockSpec((MOD_ROWS, 6 * D_MODEL), lambda i: (0, 0)),
            pl.BlockSpec((1, D_MODEL), lambda i: (0, 0)),
            pl.BlockSpec((D_MODEL, PROJ_W), lambda i: (0, 0)),
            pl.BlockSpec((tm, GROUP_W), rope_idx),
            pl.BlockSpec((tm, GROUP_W), rope_idx),
        ] + [pl.BlockSpec(memory_space=pl.ANY)] * len(kv_args),
        out_specs=(tok(w3), tok(w3), tok(w3), tok(GROUP_W), kv_spec, kv_spec, kv_spec, kv_spec),
        scratch_shapes=[pltpu.VMEM((D_MODEL, PROJ_W), bf16)],
        input_output_aliases={n_in + j: 4 + j for j in range(len(kv_args))},
        compiler_params=_cparams(1),
        name="norm1_inproj",
    )(*x_args, mod_l, g.reshape(1, D_MODEL), w_in_l, cos_t, sin_t, *kv_args)


def _values_with_ones(v, vhead, vlane, h, hw):
    ones_col = ((h + 1) * hw) % GROUP_W
    vm = jnp.where(vhead == h, v, jnp.where(vlane == ones_col, 1.0, 0.0))
    return vm.astype(bf16), ones_col


def _attn_kernel(lam_ref, q_ref, k_ref, v_ref, g_ref, o_ref, *, n_heads, diff, scale, out_scale):
    tq = q_ref.shape[0]
    n = k_ref.shape[0]
    q = q_ref[...] * (scale * LOG2E)
    k = k_ref[...].astype(bf16)
    v = v_ref[...]
    hw = GROUP_W // n_heads
    n_maps = 2 if diff else 1
    qgrp = _lane_group((tq, GROUP_W), hw // n_maps)
    qhead = _lane_group((tq, GROUP_W), hw)
    vhead = _lane_group((n, GROUP_W), hw)
    vlane = lax.broadcasted_iota(jnp.int32, (n, GROUP_W), 1)
    acc = jnp.zeros((tq, GROUP_W), f32)
    for h in range(n_heads):
        vm, ones_col = _values_with_ones(v, vhead, vlane, h, hw)
        maps = []
        for m in range(n_maps):
            qm = jnp.where(qgrp == n_maps * h + m, q, 0.0).astype(bf16)
            s = lax.dot_general(qm, k, _NT, preferred_element_type=f32)
            e = jnp.exp2(s - jnp.max(s, axis=-1, keepdims=True)).astype(bf16)
            o = jnp.dot(e, vm, preferred_element_type=f32)
            maps.append(o * (1.0 / o[:, ones_col:ones_col + 1]))
        oh = maps[0] - lam_ref[0] * maps[1] if diff else maps[0]
        acc = acc + jnp.where(qhead == h, oh, 0.0)
    if diff:
        r = lax.shift_right_logical(lax.broadcasted_iota(jnp.int32, (GROUP_W, GROUP_W), 0), 6)
        c = lax.shift_right_logical(lax.broadcasted_iota(jnp.int32, (GROUP_W, GROUP_W), 1), 6)
        bd = jnp.where(r == c, 1.0, 0.0).astype(bf16)
        sq = acc * acc
        hi = sq.astype(bf16)
        lo = (sq - hi.astype(f32)).astype(bf16)
        ms = (jnp.dot(hi, bd, preferred_element_type=f32) + jnp.dot(lo, bd, preferred_element_type=f32)) * (1.0 / A_DV)
        acc = (acc * lax.rsqrt(ms + EPS) * g_ref[...]) * out_scale
    o_ref[...] = acc.astype(o_ref.dtype)


def _attention(q_src, q_col, q_row0, k_src, k_col, k_row0, v_src, v_col, v_row0, *,
               n_seq, seq_len, n_keys, tq, n_heads, diff, scale, lam=None, gain=None, out_scale=1.0):
    qb = seq_len // tq
    if lam is None:
        lam = jnp.zeros((1,), f32)
    if gain is None:
        gain = jnp.ones((1, GROUP_W), f32)
    return pl.pallas_call(
        functools.partial(_attn_kernel, n_heads=n_heads, diff=diff, scale=scale, out_scale=out_scale),
        out_shape=jax.ShapeDtypeStruct((n_seq * seq_len, GROUP_W), bf16),
        grid=(n_seq, qb),
        in_specs=[
            pl.BlockSpec(memory_space=pltpu.SMEM),
            pl.BlockSpec((tq, GROUP_W), lambda b, i: ((q_row0 + b) * qb + i, q_col)),
            pl.BlockSpec((n_keys, GROUP_W), lambda b, i: (k_row0 + b, k_col)),
            pl.BlockSpec((n_keys, GROUP_W), lambda b, i: (v_row0 + b, v_col)),
            pl.BlockSpec((1, GROUP_W), lambda b, i: (0, 0)),
        ],
        out_specs=pl.BlockSpec((tq, GROUP_W), lambda b, i: (b * qb + i, 0)),
        compiler_params=_cparams(2),
        name="diff_attention" if diff else "softmax_attention",
    )(lam, q_src, k_src, v_src, gain)


def _na_group_geometry(g):
    r0 = g * NA_QROWS
    slab0 = min(max(r0 - NA_KH // 2, 0), GRID_H - NA_SLAB_ROWS)
    return r0, slab0


def _na_bias_tables(rpb):
    c = np.arange(GRID_W)
    ws = np.clip(c - NA_KW // 2, 0, GRID_W - NA_KW)
    kc = np.arange(GRID_W)
    col_ok = (kc[None, :] >= ws[:, None]) & (kc[None, :] < ws[:, None] + NA_KW)
    pad = GRID_W - NA_KW
    rp = jnp.pad(rpb.astype(f32) * LOG2E, ((0, 0), (0, 0), (pad, pad)))
    tc = jnp.stack([rp[:, :, GRID_W - 1 - ci:2 * GRID_W - 1 - ci] for ci in range(GRID_W)], axis=2)
    tc = jnp.where(col_ok[None, None], tc, NEG)
    neg_blk = jnp.full((C_HEADS, GRID_W, GRID_W), NEG, f32)
    tables = []
    for g in (0, 1, NA_GROUPS - 1):
        r0, slab0 = _na_group_geometry(g)
        rows = []
        for rq in range(NA_QROWS):
            r = r0 + rq
            rs = min(max(r - NA_KH // 2, 0), GRID_H - NA_KH)
            blks = []
            for kl in range(NA_SLAB_ROWS):
                kr = slab0 + kl
                blks.append(tc[:, kr - r + NA_KH - 1] if rs <= kr < rs + NA_KH else neg_blk)
            rows.append(jnp.concatenate(blks, axis=-1))
        tables.append(jnp.concatenate(rows, axis=-2))
    return jnp.stack(tables, axis=0)


def _na_kernel(q_ref, k_ref, v_ref, kx_ref, vx_ref, bias_ref, o_ref):
    g = pl.program_id(1)
    tq = NA_QROWS * GRID_W
    ns = NA_SLAB_ROWS * GRID_W
    slab0 = jnp.clip(g * NA_QROWS - NA_KH // 2, 0, GRID_H - NA_SLAB_ROWS)
    start = pl.multiple_of(slab0 * GRID_W, GRID_W)
    q = q_ref[...] * (C_DH ** -0.5 * LOG2E)
    ks = k_ref[pl.ds(start, ns), :].astype(bf16)
    vs = v_ref[pl.ds(start, ns), :]
    kx = kx_ref[...].astype(bf16)
    vx = vx_ref[...]
    qhead = _lane_group((tq, GROUP_W), C_DH)
    vshead = _lane_group((ns, GROUP_W), C_DH)
    vslane = lax.broadcasted_iota(jnp.int32, (ns, GROUP_W), 1)
    vxhead = _lane_group((PAST_LEN, GROUP_W), C_DH)
    vxlane = lax.broadcasted_iota(jnp.int32, (PAST_LEN, GROUP_W), 1)
    acc = jnp.zeros((tq, GROUP_W), f32)
    for h in range(C_HEADS):
        qm = jnp.where(qhead == h, q, 0.0).astype(bf16)
        sl = lax.dot_general(qm, ks, _NT, preferred_element_type=f32)
        b = bias_ref[0, h]
        sl = jnp.where(b > 0.5 * NEG, sl + b, NEG)
        sx = lax.dot_general(qm, kx, _NT, preferred_element_type=f32)
        mx = jnp.maximum(jnp.max(sl, axis=-1, keepdims=True), jnp.max(sx, axis=-1, keepdims=True))
        el = jnp.exp2(sl - mx).astype(bf16)
        ex = jnp.exp2(sx - mx).astype(bf16)
        vsm, ones_col = _values_with_ones(vs, vshead, vslane, h, C_DH)
        vxm, _ = _values_with_ones(vx, vxhead, vxlane, h, C_DH)
        o = jnp.dot(el, vsm, preferred_element_type=f32) + jnp.dot(ex, vxm, preferred_element_type=f32)
        acc = acc + jnp.where(qhead == h, o * (1.0 / o[:, ones_col:ones_col + 1]), 0.0)
    o_ref[...] = acc.astype(o_ref.dtype)


def _na_attention(pc, kx, vx, bias):
    tq = NA_QROWS * GRID_W
    ns = NA_SLAB_ROWS * GRID_W
    q_blk0 = N_PROMPT_TOK // tq
    s_blk0 = N_PROMPT_TOK // DEC_SEQ

    def bias_idx(b, g):
        return (jnp.where(g == 0, 0, jnp.where(g == NA_GROUPS - 1, 2, 1)), 0, 0, 0)

    return pl.pallas_call(
        _na_kernel,
        out_shape=jax.ShapeDtypeStruct((N_SAMPLE_TOK, GROUP_W), bf16),
        grid=(DEC_BATCH, NA_GROUPS),
        in_specs=[
            pl.BlockSpec((tq, GROUP_W), lambda b, g: (q_blk0 + b * NA_GROUPS + g, 0)),
            pl.BlockSpec((DEC_SEQ, GROUP_W), lambda b, g: (s_blk0 + b, 1)),
            pl.BlockSpec((DEC_SEQ, GROUP_W), lambda b, g: (s_blk0 + b, 2)),
            pl.BlockSpec((PAST_LEN, GROUP_W), lambda b, g: (b, 0)),
            pl.BlockSpec((PAST_LEN, GROUP_W), lambda b, g: (b, 0)),
            pl.BlockSpec((1, C_HEADS, tq, ns), bias_idx),
        ],
        out_specs=pl.BlockSpec((tq, GROUP_W), lambda b, g: (b * NA_GROUPS + g, 0)),
        compiler_params=_cparams(2),
        name="neighbourhood_attention",
    )(pc, pc, pc, kx, vx, bias)


def _dft_matrices(L):
    n = 2 * L
    k = np.arange(L)[:, None]
    s = np.arange(L)[None, :]
    ang = 2.0 * np.pi * ((k * s) % n) / n
    cos, sin = np.cos(ang), np.sin(ang)
    sin[0, :] = (-1.0) ** np.arange(L)
    fwd = np.concatenate([cos, sin], axis=0)
    wk = np.where(np.arange(L) == 0, 1.0, 2.0)[None, :]
    inv = np.concatenate([cos.T * wk, sin.T * wk], axis=1) / n
    inv[:, L] = ((-1.0) ** np.arange(L)) / n
    return fwd, inv


def _hy_filter_kernel(w1_ref, b1_ref, w2_ref, b2_ref, w3_ref, b3_ref, fr_ref, dec_ref, fa_ref, fb_ref,
                      p_ref, q_ref, r_ref, g_ref, nrm_ref, *, L, tk):
    s = pl.program_id(0)

    @pl.when(s == 0)
    def _():
        row = lax.broadcasted_iota(jnp.int32, (L, LANES), 0)
        lane = lax.broadcasted_iota(jnp.int32, (L, LANES), 1)
        t = row.astype(f32) / L
        band = jnp.where(lane <= HY_BANDS, lane, lane - HY_BANDS).astype(f32)
        ang = (2.0 * math.pi * band) * t
        feat = jnp.where(lane == 0, t, jnp.where(lane <= HY_BANDS, jnp.sin(ang),
                                                 jnp.where(lane <= 2 * HY_BANDS, jnp.cos(ang), 0.0)))
        z = jnp.sin(fr_ref[0:1, :] * (_dot_f32(feat, w1_ref[...]) + b1_ref[...]))
        z = jnp.sin(fr_ref[1:2, :] * (_dot_f32(z, w2_ref[...]) + b2_ref[...]))
        z = _dot_f32(z, w3_ref[...]) + b3_ref[...]
        wide = (L, HY_ORDER * 2 * HY_CH)
        tw = lax.broadcasted_iota(jnp.int32, wide, 0).astype(f32) / L
        taps = z * jnp.exp(-tw * jnp.abs(dec_ref[...]))
        bwd = (_lane_group(wide, HY_CH) % 2) == 1
        first = lax.broadcasted_iota(jnp.int32, wide, 0) == 0
        taps = jnp.where(first, jnp.where(bwd, 0.0, taps), taps)
        g_ref[...] = taps.astype(bf16)
        nrm_ref[...] = jnp.sum(jnp.abs(taps), axis=0, keepdims=True)

    ga = jnp.dot(fa_ref[...], g_ref[...], preferred_element_type=f32)
    gb = jnp.dot(fb_ref[...], g_ref[...], preferred_element_type=f32)
    top = (lax.broadcasted_iota(jnp.int32, (tk, HY_CH), 0) + s * tk) == 0
    for o in range(HY_ORDER):
        c0 = o * 2 * HY_CH
        inv = 1.0 / (nrm_ref[:, c0:c0 + HY_CH] + nrm_ref[:, c0 + HY_CH:c0 + 2 * HY_CH])
        hc = (ga[:, c0:c0 + HY_CH] + ga[:, c0 + HY_CH:c0 + 2 * HY_CH]) * inv
        bf_, bb_ = gb[:, c0:c0 + HY_CH], gb[:, c0 + HY_CH:c0 + 2 * HY_CH]
        hs = jnp.where(top, bf_ + bb_, bf_ - bb_) * inv
        oc = slice(o * HY_CH, (o + 1) * HY_CH)
        p_ref[:, oc] = hc
        q_ref[:, oc] = jnp.where(top, 0.0, hs)
        r_ref[:, oc] = jnp.where(top, hs, hc)


def _hy_filter_spectra(L, fwd_bf, w1, b1, w2, b2, w3, b3, freq, decay, tk=256):
    nk = L // tk
    wide = HY_ORDER * 2 * HY_CH
    full = lambda a: pl.BlockSpec(a.shape, lambda s: (0,) * a.ndim)
    w1p = jnp.pad(w1, ((0, LANES - HY_FEAT), (0, 0)))
    args = [w1p, b1.reshape(1, HY_HID), w2, b2.reshape(1, HY_HID), w3, b3.reshape(1, wide), freq,
            decay.reshape(1, wide)]
    out = jax.ShapeDtypeStruct((L, HY_ORDER * HY_CH), f32)
    plane = pl.BlockSpec((tk, HY_ORDER * HY_CH), lambda s: (s, 0))
    return pl.pallas_call(
        functools.partial(_hy_filter_kernel, L=L, tk=tk),
        out_shape=(out, out, out),
        grid=(nk,),
        in_specs=[full(a) for a in args] + [pl.BlockSpec((tk, L), lambda s: (s, 0)),
                                            pl.BlockSpec((tk, L), lambda s: (s + nk, 0))],
        out_specs=(plane, plane, plane),
        scratch_shapes=[pltpu.VMEM((L, wide), bf16), pltpu.VMEM((1, wide), f32)],
        compiler_params=_cparams(1),
        name="hyena_filter_spectra",
    )(*args, fwd_bf, fwd_bf)


def _hy_pre_kernel(u_ref, w_ref, x1_ref, x2_ref, v_ref, vbf_ref):
    u = u_ref[...]
    w = w_ref[...]
    y = _shift_rows(u, 1) * w[0:1, :] + u * w[1:2, :] + _shift_rows(u, -1) * w[2:3, :]
    x1_ref[...] = y[:, :HY_CH]
    x2_ref[...] = y[:, HY_CH:2 * HY_CH]
    v = y[:, 2 * HY_CH:]
    v_ref[...] = v
    vbf_ref[...] = v.astype(bf16)


def _hy_pre(pb, conv_w, seq0, n_seq, L):
    n = n_seq * L
    blk = pl.BlockSpec((L, HY_CH), lambda b: (b, 0))
    o32 = jax.ShapeDtypeStruct((n, HY_CH), f32)
    return pl.pallas_call(
        _hy_pre_kernel,
        out_shape=(o32, o32, o32, jax.ShapeDtypeStruct((n, HY_CH), bf16)),
        grid=(n_seq,),
        in_specs=[pl.BlockSpec((L, 3 * HY_CH), lambda b: (seq0 + b, 0)),
                  pl.BlockSpec((3, 3 * HY_CH), lambda b: (0, 0))],
        out_specs=(blk, blk, blk, blk),
        compiler_params=_cparams(1),
        name="hyena_short_conv",
    )(pb, conv_w)


def _hy_conv_kernel(*refs, n_seq, L, tk, nk, n_out):
    (zbf_ref, z_ref, m_ref, fa_ref, fb_ref, ic_ref, is_ref, p_ref, q_ref, r_ref, db_ref) = refs[:11]
    out_refs = refs[11:11 + n_out]
    acc_ref = refs[11 + n_out]
    s = pl.program_id(0)

    @pl.when(s == 0)
    def _():
        acc_ref[...] = jnp.zeros_like(acc_ref)

    @pl.when(s < nk)
    def _():
        fa, fb, ic, isn = fa_ref[...], fb_ref[...], ic_ref[...], is_ref[...]
        p, q, r = p_ref[...], q_ref[...], r_ref[...]
        for b in range(n_seq):
            rows = slice(b * L, (b + 1) * L)
            zb = zbf_ref[rows, :]
            a = jnp.dot(fa, zb, preferred_element_type=f32)
            bb = jnp.dot(fb, zb, preferred_element_type=f32)
            yc = (a * p - bb * q).astype(bf16)
            ys = (a * q + bb * r).astype(bf16)
            acc_ref[rows, :] += (jnp.dot(ic, yc, preferred_element_type=f32)
                                 + jnp.dot(isn, ys, preferred_element_type=f32))

    @pl.when(s >= nk)
    def _():
        start = pl.multiple_of((s - nk) * L, L)
        y = acc_ref[pl.ds(start, L), :]
        res = m_ref[...] * (y + db_ref[...] * z_ref[...])
        for o_ref in out_refs:
            o_ref[...] = res.astype(o_ref.dtype)


def _hy_longconv(zbf, z, mult, fwd_bf, inv_bf, planes, order, dbias_o, out_dtypes, n_seq, L, tk):
    nk = L // tk
    n = n_seq * L
    kt = lambda s: jnp.minimum(s, nk - 1)
    ep = lambda s: (jnp.maximum(s - nk, 0), 0)
    plane = pl.BlockSpec((tk, HY_CH), lambda s: (kt(s), order))
    return pl.pallas_call(
        functools.partial(_hy_conv_kernel, n_seq=n_seq, L=L, tk=tk, nk=nk, n_out=len(out_dtypes)),
        out_shape=tuple(jax.ShapeDtypeStruct((n, HY_CH), dt) for dt in out_dtypes),
        grid=(nk + n_seq,),
        in_specs=[
            pl.BlockSpec((n, HY_CH), lambda s: (0, 0)),
            pl.BlockSpec((L, HY_CH), ep),
            pl.BlockSpec((L, HY_CH), ep),
            pl.BlockSpec((tk, L), lambda s: (kt(s), 0)),
            pl.BlockSpec((tk, L), lambda s: (kt(s) + nk, 0)),
            pl.BlockSpec((L, tk), lambda s: (0, kt(s))),
            pl.BlockSpec((L, tk), lambda s: (0, kt(s) + nk)),
            plane, plane, plane,
            pl.BlockSpec((1, HY_CH), lambda s: (0, 0)),
        ],
        out_specs=tuple(pl.BlockSpec((L, HY_CH), ep) for _ in out_dtypes),
        scratch_shapes=[pltpu.VMEM((n, HY_CH), f32)],
        compiler_params=_cparams(1),
        name="hyena_longconv",
    )(zbf, z, mult, fwd_bf, fwd_bf, inv_bf, inv_bf, *planes, dbias_o.reshape(1, HY_CH))


def _hyena(pb, seq0, n_seq, L, dft, hy, tk=256):
    conv_w, w1, b1, w2, b2, w3, b3, freq, decay, dbias = hy
    fwd_bf, inv_bf = dft
    planes = _hy_filter_spectra(L, fwd_bf, w1, b1, w2, b2, w3, b3, freq, decay, tk=tk)
    x1, x2, v, vbf = _hy_pre(pb, conv_w, seq0, n_seq, L)
    z, zbf = _hy_longconv(vbf, v, x1, fwd_bf, inv_bf, planes, 0, dbias[0], (f32, bf16), n_seq, L, tk)
    (ob,) = _hy_longconv(zbf, z, x2, fwd_bf, inv_bf, planes, 1, dbias[1], (bf16,), n_seq, L, tk)
    return ob


def _pool_kernel(u_ref, w_ref, sc_ref, o_ref):
    u = u_ref[...]
    L = u.shape[0]
    back = _shift_rows(u, 1)
    fwd = u
    sums = [back + fwd]
    for k in (1, 2, 4):
        back = back + _shift_rows(back, k)
        fwd = fwd + _shift_rows(fwd, -k)
        sums.append(back + fwd)
    a2, a4, a8, a16 = sums
    grp = _lane_group(u.shape, POOL_GC)
    t = lax.broadcasted_iota(jnp.int32, u.shape, 0)
    half = jnp.left_shift(1, grp)
    cnt = jnp.minimum(t + half, L) - jnp.maximum(t - half, 0)
    tot = jnp.where(grp == 0, a2, jnp.where(grp == 1, a4, jnp.where(grp == 2, a8, a16)))
    pooled = tot / cnt.astype(f32) - u
    y = jnp.dot(pooled.astype(bf16), w_ref[...].astype(bf16), preferred_element_type=f32)
    o_ref[...] = (y * sc_ref[...]).astype(o_ref.dtype)


def _pool(pd, w_bd, scale, seq0, n_seq, L):
    return pl.pallas_call(
        _pool_kernel,
        out_shape=jax.ShapeDtypeStruct((n_seq * L, GROUP_W), bf16),
        grid=(n_seq,),
        in_specs=[pl.BlockSpec((L, GROUP_W), lambda b: (seq0 + b, 0)),
                  pl.BlockSpec((GROUP_W, GROUP_W), lambda b: (0, 0)),
                  pl.BlockSpec((1, GROUP_W), lambda b: (0, 0))],
        out_specs=pl.BlockSpec((L, GROUP_W), lambda b: (b, 0)),
        compiler_params=_cparams(1),
        name="pool_mixer",
    )(pd, w_bd, scale.reshape(1, GROUP_W))


def _outproj_kernel(*refs, tm, moe, split_x):
    mix_refs = refs[:8]
    n_x = 2 if split_x else 1
    x_refs = refs[8:8 + n_x]
    if moe:
        mod_ref, g_ref, w_ref, r_ref, x1_ref, h2_ref, lg_ref, wbf_ref = refs[8 + n_x:]
    else:
        mod_ref, g_ref, w_ref, x1_ref, h2_ref, wbf_ref = refs[8 + n_x:]
    i = pl.program_id(0)

    @pl.when(i == 0)
    def _():
        wbf_ref[...] = w_ref[...].astype(bf16)

    row = _mod_row(i, tm)
    d = D_MODEL
    gate1 = mod_ref[pl.ds(row, 1), 2 * d:3 * d]
    shift2 = mod_ref[pl.ds(row, 1), 3 * d:4 * d]
    scale2 = mod_ref[pl.ds(row, 1), 4 * d:5 * d]
    is_prompt = i < N_PROMPT_TOK // tm
    mixed = jnp.concatenate(
        [jnp.where(is_prompt, mix_refs[2 * j][...], mix_refs[2 * j + 1][...]) for j in range(4)], axis=-1)
    mix = jnp.dot(mixed, wbf_ref[...], preferred_element_type=f32)
    x = jnp.where(is_prompt, x_refs[0][...], x_refs[1][...]) if split_x else x_refs[0][...]
    x1 = x + gate1 * mix
    x1_ref[...] = x1
    h = _modulated_norm(x1, g_ref[...], shift2, scale2)
    h2_ref[...] = h.astype(h2_ref.dtype)
    if moe:
        lg_ref[...] = _dot_bf16x3(h, r_ref[...])


def _outproj(mixers, x, mod_l, g, w_out_l, router_l=None, tm=512):
    moe = router_l is not None
    npb = N_PROMPT_TOK // tm
    tok = lambda w: pl.BlockSpec((tm, w), lambda i: (i, 0))
    in_specs, args = [], []
    for op, os_ in mixers:
        in_specs.append(pl.BlockSpec((tm, GROUP_W), lambda i: (jnp.minimum(i, npb - 1), 0)))
        in_specs.append(pl.BlockSpec((tm, GROUP_W), lambda i: (jnp.maximum(i - npb, 0), 0)))
        args += [op, os_]
    split_x = isinstance(x, tuple)
    if split_x:
        in_specs += [pl.BlockSpec((tm, D_MODEL), lambda i: (jnp.minimum(i, npb - 1), 0)),
                     pl.BlockSpec((tm, D_MODEL), lambda i: (jnp.maximum(i - npb, 0), 0))]
        args += list(x)
    else:
        in_specs.append(tok(D_MODEL))
        args.append(x)
    in_specs += [pl.BlockSpec((MOD_ROWS, 6 * D_MODEL), lambda i: (0, 0)),
                 pl.BlockSpec((1, D_MODEL), lambda i: (0, 0)),
                 pl.BlockSpec((D_MODEL, D_MODEL), lambda i: (0, 0))]
    args += [mod_l, g.reshape(1, D_MODEL), w_out_l]
    out_shape = [jax.ShapeDtypeStruct((N_TOK, D_MODEL), f32),
                 jax.ShapeDtypeStruct((N_TOK, D_MODEL), f32 if moe else bf16)]
    out_specs = [tok(D_MODEL), tok(D_MODEL)]
    if moe:
        in_specs.append(pl.BlockSpec((D_MODEL, ROUTER_PAD), lambda i: (0, 0)))
        args.append(jnp.pad(router_l, ((0, 0), (0, ROUTER_PAD - N_EXPERTS))))
        out_shape.append(jax.ShapeDtypeStruct((N_TOK, ROUTER_PAD), f32))
        out_specs.append(tok(ROUTER_PAD))
    return pl.pallas_call(
        functools.partial(_outproj_kernel, tm=tm, moe=moe, split_x=split_x),
        out_shape=tuple(out_shape),
        grid=(N_TOK // tm,),
        in_specs=in_specs,
        out_specs=tuple(out_specs),
        scratch_shapes=[pltpu.VMEM((D_MODEL, D_MODEL), bf16)],
        compiler_params=_cparams(1),
        name="outproj_norm2",
    )(*args)


def _ffn_kernel(*refs, tm, nf, residual):
    if residual:
        be_ref, nv_ref, x_ref, w1_ref, w3_ref, w2_ref, x1_ref, mod_ref, o_ref, xbf_ref = refs
    else:
        be_ref, nv_ref, x_ref, w1_ref, w3_ref, w2_ref, o_ref, xbf_ref = refs
    i = pl.program_id(0)
    j = pl.program_id(1)

    @pl.when(j == 0)
    def _():
        o_ref[...] = jnp.zeros_like(o_ref)
        xbf_ref[...] = x_ref[...].astype(bf16)

    @pl.when(i < nv_ref[0])
    def _():
        x = xbf_ref[...]
        h1 = jnp.dot(x, w1_ref[0].astype(bf16), preferred_element_type=f32)
        h3 = jnp.dot(x, w3_ref[0].astype(bf16), preferred_element_type=f32)
        a = (h1 * jax.nn.sigmoid(h1)) * h3
        o_ref[...] += jnp.dot(a.astype(bf16), w2_ref[0].astype(bf16), preferred_element_type=f32)

    if residual:
        @pl.when(j == nf - 1)
        def _():
            gate2 = mod_ref[pl.ds(_mod_row(i, tm), 1), 5 * D_MODEL:6 * D_MODEL]
            o_ref[...] = x1_ref[...] + gate2 * o_ref[...]


def _ffn_grouped(x_rows, w1, w3, w2, blk_expert, n_valid, tm, tf, x1=None, mod_l=None):
    n_rows = x_rows.shape[0]
    ffn = w1.shape[-1]
    nf = ffn // tf
    assert nf * tf == ffn and n_rows % tm == 0
    residual = x1 is not None

    def wcol(i, j, be, nv):
        return (be[i], 0, jnp.where(i < nv[0], j, nf - 1))

    def wrow(i, j, be, nv):
        return (be[i], jnp.where(i < nv[0], j, nf - 1), 0)

    rows = pl.BlockSpec((tm, D_MODEL), lambda i, j, be, nv: (i, 0))
    in_specs = [rows, pl.BlockSpec((1, D_MODEL, tf), wcol), pl.BlockSpec((1, D_MODEL, tf), wcol),
                pl.BlockSpec((1, tf, D_MODEL), wrow)]
    args = [x_rows, w1, w3, w2]
    if residual:
        in_specs += [rows, pl.BlockSpec((MOD_ROWS, 6 * D_MODEL), lambda i, j, be, nv: (0, 0))]
        args += [x1, mod_l]
    return pl.pallas_call(
        functools.partial(_ffn_kernel, tm=tm, nf=nf, residual=residual),
        out_shape=jax.ShapeDtypeStruct((n_rows, D_MODEL), f32),
        grid_spec=pltpu.PrefetchScalarGridSpec(
            num_scalar_prefetch=2,
            grid=(n_rows // tm, nf),
            in_specs=in_specs,
            out_specs=rows,
            scratch_shapes=[pltpu.VMEM((tm, D_MODEL), bf16)],
        ),
        compiler_params=_cparams(2),
        name="swiglu_grouped",
    )(blk_expert, n_valid, *args)


def _gather_kernel(idx_ref, src_ref, o_ref, buf_ref, sem, *, rb, pack):
    w = src_ref.shape[1]

    def row_copy(rr, k, src_row):
        dst = buf_ref.at[pl.ds(rr, 1), pl.ds(k * w, w)]
        return pltpu.make_async_copy(src_ref.at[pl.ds(src_row, 1), :], dst, sem)

    def issue(rr, carry):
        for k in range(pack):
            row_copy(rr, k, idx_ref[0, 0, rr * pack + k]).start()
        return carry

    def drain(rr, carry):
        for k in range(pack):
            row_copy(rr, k, 0).wait()
        return carry

    lax.fori_loop(0, rb // pack, issue, 0, unroll=8)
    lax.fori_loop(0, rb // pack, drain, 0, unroll=8)
    o_ref[...] = buf_ref[...]


def _row_gather(src, idx, rb=512, pack=1):
    n = idx.shape[0]
    w = src.shape[1]
    assert n % rb == 0 and rb % pack == 0
    return pl.pallas_call(
        functools.partial(_gather_kernel, rb=rb, pack=pack),
        out_shape=jax.ShapeDtypeStruct((n // pack, pack * w), src.dtype),
        grid=(n // rb,),
        in_specs=[pl.BlockSpec((1, 1, rb), lambda i: (i, 0, 0), memory_space=pltpu.SMEM),
                  pl.BlockSpec(memory_space=pl.ANY)],
        out_specs=pl.BlockSpec((rb // pack, pack * w), lambda i: (i, 0)),
        scratch_shapes=[pltpu.VMEM((rb // pack, pack * w), src.dtype), pltpu.SemaphoreType.DMA],
        compiler_params=_cparams(1),
        name="row_gather",
    )(idx.reshape(n // rb, 1, rb), src)


def _combine_kernel(*refs, tm, final):
    if final:
        x1_ref, mod_ref, y2_ref, gt_ref, g_ref, op_ref, os_ref = refs
    else:
        x1_ref, mod_ref, y2_ref, gt_ref, o_ref = refs
    i = pl.program_id(0)
    gate2 = mod_ref[pl.ds(_mod_row(i, tm), 1), 5 * D_MODEL:6 * D_MODEL]
    gt = gt_ref[...]
    f = gt[:, 0:1] * y2_ref[:, :D_MODEL] + gt[:, 1:2] * y2_ref[:, D_MODEL:]
    x2 = x1_ref[...] + gate2 * f
    if final:
        ms = jnp.mean(x2 * x2, axis=-1, keepdims=True)
        y = x2 * lax.rsqrt(ms + EPS) * g_ref[...]

        @pl.when(i < N_PROMPT_TOK // tm)
        def _():
            op_ref[...] = y

        @pl.when(i >= N_PROMPT_TOK // tm)
        def _():
            os_ref[...] = y
    else:
        o_ref[...] = x2


def _combine(x1, mod_l, y2, gates, final_g=None, tm=512):
    final = final_g is not None
    npb = N_PROMPT_TOK // tm
    tok = pl.BlockSpec((tm, D_MODEL), lambda i: (i, 0))
    in_specs = [tok, pl.BlockSpec((MOD_ROWS, 6 * D_MODEL), lambda i: (0, 0)),
                pl.BlockSpec((tm, TOP_K * D_MODEL), lambda i: (i, 0)),
                pl.BlockSpec((tm, TOP_K), lambda i: (i, 0))]
    args = [x1, mod_l, y2, gates]
    if final:
        in_specs.append(pl.BlockSpec((1, D_MODEL), lambda i: (0, 0)))
        args.append(final_g.reshape(1, D_MODEL))
        out_shape = (jax.ShapeDtypeStruct((N_PROMPT_TOK, D_MODEL), f32),
                     jax.ShapeDtypeStruct((N_SAMPLE_TOK, D_MODEL), f32))
        out_specs = (pl.BlockSpec((tm, D_MODEL), lambda i: (jnp.minimum(i, npb - 1), 0)),
                     pl.BlockSpec((tm, D_MODEL), lambda i: (jnp.maximum(i - npb, 0), 0)))
    else:
        out_shape = jax.ShapeDtypeStruct((N_TOK, D_MODEL), f32)
        out_specs = tok
    return pl.pallas_call(
        functools.partial(_combine_kernel, tm=tm, final=final),
        out_shape=out_shape,
        grid=(N_TOK // tm,),
        in_specs=in_specs,
        out_specs=out_specs,
        compiler_params=_cparams(1),
        name="expert_combine",
    )(*args)


def _route(logits, tm):
    eid = jnp.arange(N_EXPERTS, dtype=jnp.int32)[None, :]
    v0 = jnp.max(logits, axis=-1, keepdims=True)
    i0 = jnp.min(jnp.where(logits == v0, eid, N_EXPERTS), axis=-1, keepdims=True)
    rest = jnp.where(eid == i0, -jnp.inf, logits)
    v1 = jnp.max(rest, axis=-1, keepdims=True)
    i1 = jnp.min(jnp.where(rest == v1, eid, N_EXPERTS), axis=-1, keepdims=True)
    gates = jax.nn.softmax(jnp.concatenate([v0, v1], axis=-1), axis=-1)
    flat_e = jnp.concatenate([i0, i1], axis=-1).reshape(-1)
    onehot = (flat_e[:, None] == jnp.arange(N_EXPERTS)[None, :]).astype(jnp.int32)
    csum = jnp.cumsum(onehot, axis=0)
    rank = jnp.take_along_axis(csum, flat_e[:, None], axis=1)[:, 0] - 1
    counts = csum[-1]
    padded = ((counts + tm - 1) // tm) * tm
    pend = jnp.cumsum(padded)
    pstart = pend - padded
    dest = pstart[flat_e] + rank
    n_rows = N_TOK * TOP_K + N_EXPERTS * tm
    tok_of = jnp.arange(N_TOK * TOP_K, dtype=jnp.int32) // TOP_K
    row_token = jnp.zeros((n_rows,), jnp.int32).at[dest].set(tok_of)
    blk_start = jnp.arange(n_rows // tm, dtype=jnp.int32) * tm
    blk_expert = jnp.minimum(jnp.sum((blk_start[:, None] >= pend[None, :]).astype(jnp.int32), axis=1),
                             N_EXPERTS - 1)
    n_valid = (pend[-1] // tm).astype(jnp.int32).reshape(1)
    blk_expert = jnp.where(blk_start < pend[-1], blk_expert, blk_expert[jnp.maximum(n_valid[0] - 1, 0)])
    return row_token, blk_expert, n_valid, dest.reshape(N_TOK, TOP_K), gates


def _pool_weight(pool_w_l):
    w = jnp.zeros((GROUP_W, GROUP_W), f32)
    for g in range(len(POOL_WINDOWS)):
        w = w.at[g * POOL_GC:(g + 1) * POOL_GC, g * POOL_GC:(g + 1) * POOL_GC].set(pool_w_l[g])
    return w


def kernel(x_prompt, x_sample, cache_diff_k, cache_diff_v, cache_na_k, cache_na_v, c, c_ctx,
           norm1_g, norm2_g, final_g, ada_w, ada_b, w_in, w_out, diff_lam, diff_subln_g,
           hy_conv, hy_w1, hy_b1, hy_w2, hy_b2, hy_w3, hy_b3, hy_freq, hy_decay, hy_dbias,
           na_rpb, pool_w, pool_scale, ffn_w1, ffn_w3, ffn_w2,
           moe_router, moe_w1, moe_w3, moe_w2):
    cond = jnp.concatenate([c, c_ctx[None, :], jnp.zeros((MOD_ROWS - DEC_BATCH - 1, D_MODEL), f32)], axis=0)
    mod = _ada_mod(cond, ada_w, ada_b)
    cos_np, sin_np = _rope_tables()
    cos_t, sin_t = jnp.asarray(cos_np), jnp.asarray(sin_np)
    dft = {L: tuple(jnp.asarray(m, dtype=bf16) for m in _dft_matrices(L)) for L in (SEQ, DEC_SEQ)}

    P = N_PROMPT_TOK
    x = (x_prompt.reshape(P, D_MODEL), x_sample.reshape(N_SAMPLE_TOK, D_MODEL))
    new_kv = None
    for l in range(DEPTH):
        pa, pb, pc, pd, *new_kv = _inproj(x, l, new_kv, mod[l], norm1_g[l], w_in[l], cos_t, sin_t)

        lam_init = 0.8 - 0.6 * math.exp(-0.3 * l)
        lv = diff_lam[l].astype(f32)
        lam = (jnp.exp(jnp.sum(lv[0] * lv[1])) - jnp.exp(jnp.sum(lv[2] * lv[3])) + lam_init).reshape(1)
        gain = jnp.tile(diff_subln_g[l], A_HEADS).reshape(1, GROUP_W)
        a_kw = dict(n_heads=A_HEADS, diff=True, scale=A_DQK ** -0.5, lam=lam, gain=gain, out_scale=1.0 - lam_init)
        oa_p = _attention(pa, 0, 0, pa, 1, 0, pa, 2, 0, n_seq=BATCH, seq_len=SEQ, n_keys=SEQ, tq=SEQ, **a_kw)
        n_all = DEC_SEQ + PAST_LEN
        k_all = jnp.concatenate([pa[P:, GROUP_W:2 * GROUP_W].reshape(DEC_BATCH, DEC_SEQ, GROUP_W),
                                 cache_diff_k[:, l].reshape(DEC_BATCH, PAST_LEN, GROUP_W)], axis=1)
        v_all = jnp.concatenate([pa[P:, 2 * GROUP_W:].reshape(DEC_BATCH, DEC_SEQ, GROUP_W),
                                 cache_diff_v[:, l].reshape(DEC_BATCH, PAST_LEN, GROUP_W)], axis=1)
        oa_s = _attention(pa, 0, P // DEC_SEQ, k_all.reshape(-1, GROUP_W), 0, 0, v_all.reshape(-1, GROUP_W), 0, 0,
                          n_seq=DEC_BATCH, seq_len=DEC_SEQ, n_keys=n_all, tq=256, **a_kw)

        hy = (hy_conv[l], hy_w1[l], hy_b1[l], hy_w2[l], hy_b2[l], hy_w3[l], hy_b3[l],
              hy_freq[l], hy_decay[l], hy_dbias[l])
        ob_p = _hyena(pb, 0, BATCH, SEQ, dft[SEQ], hy)
        ob_s = _hyena(pb, P // DEC_SEQ, DEC_BATCH, DEC_SEQ, dft[DEC_SEQ], hy)

        oc_p = _attention(pc, 0, 0, pc, 1, 0, pc, 2, 0, n_seq=BATCH, seq_len=SEQ, n_keys=SEQ, tq=SEQ,
                          n_heads=C_HEADS, diff=False, scale=C_DH ** -0.5)
        oc_s = _na_attention(pc, cache_na_k[:, l].reshape(-1, GROUP_W), cache_na_v[:, l].reshape(-1, GROUP_W),
                             _na_bias_tables(na_rpb[l]))

        w_bd = _pool_weight(pool_w[l])
        od_p = _pool(pd, w_bd, pool_scale[l], 0, BATCH, SEQ)
        od_s = _pool(pd, w_bd, pool_scale[l], P // DEC_SEQ, DEC_BATCH, DEC_SEQ)

        mixers = [(oa_p, oa_s), (ob_p, ob_s), (oc_p, oc_s), (od_p, od_s)]
        tm = 1024
        if l % 2 == 0:
            assert l != DEPTH - 1
            x1, h2 = _outproj(mixers, x, mod[l], norm2_g[l], w_out[l])
            x = _ffn_grouped(h2, ffn_w1[l // 2][None], ffn_w3[l // 2][None], ffn_w2[l // 2][None],
                             jnp.zeros((N_TOK // tm,), jnp.int32), jnp.full((1,), N_TOK // tm, jnp.int32),
                             tm=tm, tf=256, x1=x1, mod_l=mod[l])
        else:
            x1, h2, logits = _outproj(mixers, x, mod[l], norm2_g[l], w_out[l], moe_router[l // 2])
            row_token, blk_expert, n_valid, dest, gates = _route(logits[:, :N_EXPERTS], tm)
            xs = _row_gather(h2, row_token)
            ys = _ffn_grouped(xs, moe_w1[l // 2], moe_w3[l // 2], moe_w2[l // 2], blk_expert, n_valid,
                              tm=tm, tf=512)
            y2 = _row_gather(ys, dest.reshape(-1), pack=TOP_K)
            x = _combine(x1, mod[l], y2, gates, final_g if l == DEPTH - 1 else None)

    assert isinstance(x, tuple)
    y_prompt = x[0].reshape(BATCH, SEQ, D_MODEL)
    y_sample = x[1].reshape(DEC_BATCH, DEC_SEQ, D_MODEL)
    kak, kav, kck, kcv = new_kv
    return (y_prompt, y_sample,
            kak.reshape(BATCH, DEPTH, SEQ, A_HEADS, 2 * A_DQK), kav.reshape(BATCH, DEPTH, SEQ, A_HEADS, A_DV),
            kck.reshape(BATCH, DEPTH, SEQ, C_HEADS, C_DH), kcv.reshape(BATCH, DEPTH, SEQ, C_HEADS, C_DH))
```

```python
import functools
import math

import numpy as np
import jax
import jax.numpy as jnp
from jax import lax
from jax.experimental import pallas as pl
from jax.experimental.pallas import tpu as pltpu

D_MODEL = 1024
BATCH = 32
SEQ = 256
DEPTH = 2
DEC_BATCH = 4
DEC_SEQ = 2048
PAST_LEN = 512
GRID_W = 64
GRID_H = DEC_SEQ // GRID_W
GROUP_W = D_MODEL // 4
A_HEADS = 4
A_DQK = GROUP_W // (2 * A_HEADS)
A_DV = GROUP_W // A_HEADS
ROPE_BASE = 10000.0
HY_CH = GROUP_W
HY_ORDER = 2
HY_BANDS = 8
HY_FEAT = 1 + 2 * HY_BANDS
HY_HID = 64
C_HEADS = 4
C_DH = GROUP_W // C_HEADS
NA_KH = 8
NA_KW = 16
POOL_WINDOWS = (2, 4, 8, 16)
POOL_GC = GROUP_W // 4
PROJ_W = 3 * GROUP_W + 3 * HY_CH + 3 * GROUP_W + GROUP_W
N_EXPERTS = 8
TOP_K = 2
EPS = 1e-6
NEG = -1e30
LOG2E = math.log2(math.e)

N_PROMPT_TOK = BATCH * SEQ
N_SAMPLE_TOK = DEC_BATCH * DEC_SEQ
N_TOK = N_PROMPT_TOK + N_SAMPLE_TOK
MOD_ROWS = 8
CTX_ROW = DEC_BATCH
ROUTER_PAD = 128
LANES = 128

NA_QROWS = 4
NA_SLAB_ROWS = 12
NA_GROUPS = GRID_H // NA_QROWS

f32 = jnp.float32
bf16 = jnp.bfloat16

VMEM_LIMIT = 56 * 1024 * 1024
_NT = (((1,), (1,)), ((), ()))


def _cparams(n_axes):
    return pltpu.CompilerParams(
        dimension_semantics=("arbitrary",) * n_axes, vmem_limit_bytes=VMEM_LIMIT)


def _mod_row(i, tm):
    n_prompt_blocks = N_PROMPT_TOK // tm
    blocks_per_seq = DEC_SEQ // tm
    return jnp.where(i < n_prompt_blocks, CTX_ROW, (i - n_prompt_blocks) // blocks_per_seq)


def _split3(a):
    a0 = a.astype(bf16)
    r1 = a - a0.astype(f32)
    a1 = r1.astype(bf16)
    a2 = (r1 - a1.astype(f32)).astype(bf16)
    return a0, a1, a2


def _dot_bf16x3(a, b):
    a0 = a.astype(bf16)
    a1 = (a - a0.astype(f32)).astype(bf16)
    b0 = b.astype(bf16)
    b1 = (b - b0.astype(f32)).astype(bf16)
    d = functools.partial(jnp.dot, preferred_element_type=f32)
    return (d(a1, b0) + d(a0, b1)) + d(a0, b0)


def _dot_f32(a, b):
    a0, a1, a2 = _split3(a)
    b0, b1, b2 = _split3(b)
    d = functools.partial(jnp.dot, preferred_element_type=f32)
    return ((d(a2, b0) + d(a1, b1) + d(a0, b2)) + (d(a1, b0) + d(a0, b1))) + d(a0, b0)


def _lane_group(shape, width):
    return lax.shift_right_logical(lax.broadcasted_iota(jnp.int32, shape, 1), int(math.log2(width)))


def _shift_rows(x, d):
    n = x.shape[0]
    r = pltpu.roll(x, d % n, 0)
    row = lax.broadcasted_iota(jnp.int32, x.shape, 0)
    keep = (row >= d) if d > 0 else (row < n + d)
    return jnp.where(keep, r, 0.0)


def _ada_kernel(cond_ref, w_ref, b_ref, o_ref):
    c = cond_ref[...]
    s = c * jax.nn.sigmoid(c)
    o_ref[0] = jnp.dot(s.astype(bf16), w_ref[0].astype(bf16), preferred_element_type=f32) + b_ref[0]


def _ada_mod(cond, ada_w, ada_b):
    tn = 1536
    n6 = 6 * D_MODEL
    return pl.pallas_call(
        _ada_kernel,
        out_shape=jax.ShapeDtypeStruct((DEPTH, MOD_ROWS, n6), f32),
        grid=(DEPTH, n6 // tn),
        in_specs=[
            pl.BlockSpec((MOD_ROWS, D_MODEL), lambda l, j: (0, 0)),
            pl.BlockSpec((1, D_MODEL, tn), lambda l, j: (l, 0, j)),
            pl.BlockSpec((1, 1, tn), lambda l, j: (l, 0, j)),
        ],
        out_specs=pl.BlockSpec((1, MOD_ROWS, tn), lambda l, j: (l, 0, j)),
        compiler_params=_cparams(2),
        name="ada_mod",
    )(cond, ada_w, ada_b.reshape(DEPTH, 1, n6))


def _rope_tables():
    t = np.arange(DEC_SEQ)
    nf = A_DQK // 4
    inv = ROPE_BASE ** (-np.arange(nf, dtype=np.float64) / nf)
    ar = (t // GRID_W)[:, None] * inv
    ac = (t % GRID_W)[:, None] * inv
    cos = np.concatenate([np.cos(ar), np.cos(ar), np.cos(ac), np.cos(ac)], axis=1)
    sin = np.concatenate([-np.sin(ar), np.sin(ar), -np.sin(ac), np.sin(ac)], axis=1)
    reps = GROUP_W // A_DQK
    return (np.tile(cos, (1, reps)).astype(np.float32), np.tile(sin, (1, reps)).astype(np.float32))


def _modulated_norm(x, g, shift, scale):
    ms = jnp.mean(x * x, axis=-1, keepdims=True)
    return (x * lax.rsqrt(ms + EPS) * g) * (1.0 + scale) + shift


def _inproj_kernel(*refs, tm, split_x, n_alias):
    n_x = 2 if split_x else 1
    x_refs = refs[:n_x]
    mod_ref, g_ref, w_ref, cos_ref, sin_ref = refs[n_x:n_x + 5]
    outs = refs[n_x + 5 + n_alias:]
    pa_ref, pb_ref, pc_ref, pd_ref, kak_ref, kav_ref, kck_ref, kcv_ref, wbf_ref = outs
    i = pl.program_id(0)
    is_prompt = i < N_PROMPT_TOK // tm

    @pl.when(i == 0)
    def _():
        wbf_ref[...] = w_ref[...].astype(bf16)

    row = _mod_row(i, tm)
    shift = mod_ref[pl.ds(row, 1), 0:D_MODEL]
    scale = mod_ref[pl.ds(row, 1), D_MODEL:2 * D_MODEL]
    x = jnp.where(is_prompt, x_refs[0][...], x_refs[1][...]) if split_x else x_refs[0][...]
    h = _modulated_norm(x, g_ref[...], shift, scale)
    proj = jnp.dot(h.astype(bf16), wbf_ref[...], preferred_element_type=f32)
    w3 = 3 * GROUP_W
    pb_ref[...] = proj[:, w3:2 * w3]
    pc_ref[...] = proj[:, 2 * w3:3 * w3]
    pd_ref[...] = proj[:, 3 * w3:]
    pa_ref[:, 2 * GROUP_W:] = proj[:, 2 * GROUP_W:w3]

    @pl.when(is_prompt)
    def _():
        pa_ref[:, :2 * GROUP_W] = proj[:, :2 * GROUP_W]
        kv_shape = (tm // SEQ, 1, SEQ, GROUP_W)
        cols = (GROUP_W, 2 * GROUP_W, 2 * w3 + GROUP_W, 2 * w3 + 2 * GROUP_W)
        for ref, c0 in zip((kak_ref, kav_ref, kck_ref, kcv_ref), cols):
            ref[:, 0:1] = proj[:, c0:c0 + GROUP_W].reshape(kv_shape)
            if ref.shape[1] > 1:
                ref[:, 1:] = jnp.zeros((tm // SEQ, ref.shape[1] - 1, SEQ, GROUP_W), f32)

    @pl.when(i >= N_PROMPT_TOK // tm)
    def _():
        cos = cos_ref[...]
        sin = sin_ref[...]
        lane = lax.broadcasted_iota(jnp.int32, (tm, GROUP_W), 1)
        first = (lane % 16) < 8
        for s in range(2):
            v = proj[:, s * GROUP_W:(s + 1) * GROUP_W]
            partner = jnp.where(first, pltpu.roll(v, GROUP_W - 8, 1), pltpu.roll(v, 8, 1))
            pa_ref[:, s * GROUP_W:(s + 1) * GROUP_W] = v * cos + partner * sin


def _inproj(x, layer, kv_prev, mod_l, g, w_in_l, cos_t, sin_t, tm=512):
    npb = N_PROMPT_TOK // tm
    blocks_per_seq = DEC_SEQ // tm
    w3 = 3 * GROUP_W
    split_x = isinstance(x, tuple)

    def rope_idx(i):
        return (jnp.maximum(i - npb, 0) % blocks_per_seq, 0)

    if split_x:
        x_specs = [pl.BlockSpec((tm, D_MODEL), lambda i: (jnp.minimum(i, npb - 1), 0)),
                   pl.BlockSpec((tm, D_MODEL), lambda i: (jnp.maximum(i - npb, 0), 0))]
        x_args = list(x)
    else:
        x_specs = [pl.BlockSpec((tm, D_MODEL), lambda i: (i, 0))]
        x_args = [x]
    kv_args = list(kv_prev) if kv_prev is not None else []
    n_in = len(x_args) + 5
    kv_shape = jax.ShapeDtypeStruct((BATCH, DEPTH, SEQ, GROUP_W), f32)
    kv_layers = DEPTH if layer == 0 else 1
    kv_spec = pl.BlockSpec((tm // SEQ, kv_layers, SEQ, GROUP_W), lambda i: (jnp.minimum(i, npb - 1), layer, 0, 0))
    tok = lambda w: pl.BlockSpec((tm, w), lambda i: (i, 0))
    return pl.pallas_call(
        functools.partial(_inproj_kernel, tm=tm, split_x=split_x, n_alias=len(kv_args)),
        out_shape=(jax.ShapeDtypeStruct((N_TOK, w3), f32), jax.ShapeDtypeStruct((N_TOK, w3), f32),
                   jax.ShapeDtypeStruct((N_TOK, w3), f32), jax.ShapeDtypeStruct((N_TOK, GROUP_W), f32),
                   kv_shape, kv_shape, kv_shape, kv_shape),
        grid=(N_TOK // tm,),
        in_specs=x_specs + [
            pl.BlockSpec((MOD_ROWS, 6 * D_MODEL), lambda i: (0, 0)),
            pl.BlockSpec((1, D_MODEL), lambda i: (0, 0)),
            pl.BlockSpec((D_MODEL, PROJ_W), lambda i: (0, 0)),
            pl.BlockSpec((tm, GROUP_W), rope_idx),
            pl.BlockSpec((tm, GROUP_W), rope_idx),
        ] + [pl.BlockSpec(memory_space=pl.ANY)] * len(kv_args),
        out_specs=(tok(w3), tok(w3), tok(w3), tok(GROUP_W), kv_spec, kv_spec, kv_spec, kv_spec),
        scratch_shapes=[pltpu.VMEM((D_MODEL, PROJ_W), bf16)],
        input_output_aliases={n_in + j: 4 + j for j in range(len(kv_args))},
        compiler_params=_cparams(1),
        name="norm1_inproj",
    )(*x_args, mod_l, g.reshape(1, D_MODEL), w_in_l, cos_t, sin_t, *kv_args)


def _values_with_ones(v, vhead, vlane, h, hw):
    ones_col = ((h + 1) * hw) % GROUP_W
    vm = jnp.where(vhead == h, v, jnp.where(vlane == ones_col, 1.0, 0.0))
    return vm.astype(bf16), ones_col


def _attn_kernel(lam_ref, q_ref, k_ref, v_ref, g_ref, o_ref, *, n_heads, diff, scale, out_scale):
    tq = q_ref.shape[0]
    n = k_ref.shape[0]
    q = q_ref[...] * (scale * LOG2E)
    k = k_ref[...].astype(bf16)
    v = v_ref[...]
    hw = GROUP_W // n_heads
    n_maps = 2 if diff else 1
    qgrp = _lane_group((tq, GROUP_W), hw // n_maps)
    qhead = _lane_group((tq, GROUP_W), hw)
    vhead = _lane_group((n, GROUP_W), hw)
    vlane = lax.broadcasted_iota(jnp.int32, (n, GROUP_W), 1)
    acc = jnp.zeros((tq, GROUP_W), f32)
    for h in range(n_heads):
        vm, ones_col = _values_with_ones(v, vhead, vlane, h, hw)
        maps = []
        for m in range(n_maps):
            qm = jnp.where(qgrp == n_maps * h + m, q, 0.0).astype(bf16)
            s = lax.dot_general(qm, k, _NT, preferred_element_type=f32)
            e = jnp.exp2(s - jnp.max(s, axis=-1, keepdims=True)).astype(bf16)
            o = jnp.dot(e, vm, preferred_element_type=f32)
            maps.append(o * (1.0 / o[:, ones_col:ones_col + 1]))
        oh = maps[0] - lam_ref[0] * maps[1] if diff else maps[0]
        acc = acc + jnp.where(qhead == h, oh, 0.0)
    if diff:
        r = lax.shift_right_logical(lax.broadcasted_iota(jnp.int32, (GROUP_W, GROUP_W), 0), 6)
        c = lax.shift_right_logical(lax.broadcasted_iota(jnp.int32, (GROUP_W, GROUP_W), 1), 6)
        bd = jnp.where(r == c, 1.0, 0.0).astype(bf16)
        sq = acc * acc
        hi = sq.astype(bf16)
        lo = (sq - hi.astype(f32)).astype(bf16)
        ms = (jnp.dot(hi, bd, preferred_element_type=f32) + jnp.dot(lo, bd, preferred_element_type=f32)) * (1.0 / A_DV)
        acc = (acc * lax.rsqrt(ms + EPS) * g_ref[...]) * out_scale
    o_ref[...] = acc.astype(o_ref.dtype)


def _attention(q_src, q_col, q_row0, k_src, k_col, k_row0, v_src, v_col, v_row0, *,
               n_seq, seq_len, n_keys, tq, n_heads, diff, scale, lam=None, gain=None, out_scale=1.0):
    qb = seq_len // tq
    if lam is None:
        lam = jnp.zeros((1,), f32)
    if gain is None:
        gain = jnp.ones((1, GROUP_W), f32)
    return pl.pallas_call(
        functools.partial(_attn_kernel, n_heads=n_heads, diff=diff, scale=scale, out_scale=out_scale),
        out_shape=jax.ShapeDtypeStruct((n_seq * seq_len, GROUP_W), bf16),
        grid=(n_seq, qb),
        in_specs=[
            pl.BlockSpec(memory_space=pltpu.SMEM),
            pl.BlockSpec((tq, GROUP_W), lambda b, i: ((q_row0 + b) * qb + i, q_col)),
            pl.BlockSpec((n_keys, GROUP_W), lambda b, i: (k_row0 + b, k_col)),
            pl.BlockSpec((n_keys, GROUP_W), lambda b, i: (v_row0 + b, v_col)),
            pl.BlockSpec((1, GROUP_W), lambda b, i: (0, 0)),
        ],
        out_specs=pl.BlockSpec((tq, GROUP_W), lambda b, i: (b * qb + i, 0)),
        compiler_params=_cparams(2),
        name="diff_attention" if diff else "softmax_attention",
    )(lam, q_src, k_src, v_src, gain)


def _na_group_geometry(g):
    r0 = g * NA_QROWS
    slab0 = min(max(r0 - NA_KH // 2, 0), GRID_H - NA_SLAB_ROWS)
    return r0, slab0


def _na_bias_tables(rpb):
    c = np.arange(GRID_W)
    ws = np.clip(c - NA_KW // 2, 0, GRID_W - NA_KW)
    kc = np.arange(GRID_W)
    col_ok = (kc[None, :] >= ws[:, None]) & (kc[None, :] < ws[:, None] + NA_KW)
    pad = GRID_W - NA_KW
    rp = jnp.pad(rpb.astype(f32) * LOG2E, ((0, 0), (0, 0), (pad, pad)))
    tc = jnp.stack([rp[:, :, GRID_W - 1 - ci:2 * GRID_W - 1 - ci] for ci in range(GRID_W)], axis=2)
    tc = jnp.where(col_ok[None, None], tc, NEG)
    neg_blk = jnp.full((C_HEADS, GRID_W, GRID_W), NEG, f32)
    tables = []
    for g in (0, 1, NA_GROUPS - 1):
        r0, slab0 = _na_group_geometry(g)
        rows = []
        for rq in range(NA_QROWS):
            r = r0 + rq
            rs = min(max(r - NA_KH // 2, 0), GRID_H - NA_KH)
            blks = []
            for kl in range(NA_SLAB_ROWS):
                kr = slab0 + kl
                blks.append(tc[:, kr - r + NA_KH - 1] if rs <= kr < rs + NA_KH else neg_blk)
            rows.append(jnp.concatenate(blks, axis=-1))
        tables.append(jnp.concatenate(rows, axis=-2))
    return jnp.stack(tables, axis=0)


def _na_kernel(q_ref, k_ref, v_ref, kx_ref, vx_ref, bias_ref, o_ref):
    g = pl.program_id(1)
    tq = NA_QROWS * GRID_W
    ns = NA_SLAB_ROWS * GRID_W
    slab0 = jnp.clip(g * NA_QROWS - NA_KH // 2, 0, GRID_H - NA_SLAB_ROWS)
    start = pl.multiple_of(slab0 * GRID_W, GRID_W)
    q = q_ref[...] * (C_DH ** -0.5 * LOG2E)
    ks = k_ref[pl.ds(start, ns), :].astype(bf16)
    vs = v_ref[pl.ds(start, ns), :]
    kx = kx_ref[...].astype(bf16)
    vx = vx_ref[...]
    qhead = _lane_group((tq, GROUP_W), C_DH)
    vshead = _lane_group((ns, GROUP_W), C_DH)
    vslane = lax.broadcasted_iota(jnp.int32, (ns, GROUP_W), 1)
    vxhead = _lane_group((PAST_LEN, GROUP_W), C_DH)
    vxlane = lax.broadcasted_iota(jnp.int32, (PAST_LEN, GROUP_W), 1)
    acc = jnp.zeros((tq, GROUP_W), f32)
    for h in range(C_HEADS):
        qm = jnp.where(qhead == h, q, 0.0).astype(bf16)
        sl = lax.dot_general(qm, ks, _NT, preferred_element_type=f32)
        b = bias_ref[0, h]
        sl = jnp.where(b > 0.5 * NEG, sl + b, NEG)
        sx = lax.dot_general(qm, kx, _NT, preferred_element_type=f32)
        mx = jnp.maximum(jnp.max(sl, axis=-1, keepdims=True), jnp.max(sx, axis=-1, keepdims=True))
        el = jnp.exp2(sl - mx).astype(bf16)
        ex = jnp.exp2(sx - mx).astype(bf16)
        vsm, ones_col = _values_with_ones(vs, vshead, vslane, h, C_DH)
        vxm, _ = _values_with_ones(vx, vxhead, vxlane, h, C_DH)
        o = jnp.dot(el, vsm, preferred_element_type=f32) + jnp.dot(ex, vxm, preferred_element_type=f32)
        acc = acc + jnp.where(qhead == h, o * (1.0 / o[:, ones_col:ones_col + 1]), 0.0)
    o_ref[...] = acc.astype(o_ref.dtype)


def _na_attention(pc, kx, vx, bias):
    tq = NA_QROWS * GRID_W
    ns = NA_SLAB_ROWS * GRID_W
    q_blk0 = N_PROMPT_TOK // tq
    s_blk0 = N_PROMPT_TOK // DEC_SEQ

    def bias_idx(b, g):
        return (jnp.where(g == 0, 0, jnp.where(g == NA_GROUPS - 1, 2, 1)), 0, 0, 0)

    return pl.pallas_call(
        _na_kernel,
        out_shape=jax.ShapeDtypeStruct((N_SAMPLE_TOK, GROUP_W), bf16),
        grid=(DEC_BATCH, NA_GROUPS),
        in_specs=[
            pl.BlockSpec((tq, GROUP_W), lambda b, g: (q_blk0 + b * NA_GROUPS + g, 0)),
            pl.BlockSpec((DEC_SEQ, GROUP_W), lambda b, g: (s_blk0 + b, 1)),
            pl.BlockSpec((DEC_SEQ, GROUP_W), lambda b, g: (s_blk0 + b, 2)),
            pl.BlockSpec((PAST_LEN, GROUP_W), lambda b, g: (b, 0)),
            pl.BlockSpec((PAST_LEN, GROUP_W), lambda b, g: (b, 0)),
            pl.BlockSpec((1, C_HEADS, tq, ns), bias_idx),
        ],
        out_specs=pl.BlockSpec((tq, GROUP_W), lambda b, g: (b * NA_GROUPS + g, 0)),
        compiler_params=_cparams(2),
        name="neighbourhood_attention",
    )(pc, pc, pc, kx, vx, bias)


def _dft_matrices(L):
    n = 2 * L
    k = np.arange(L)[:, None]
    s = np.arange(L)[None, :]
    ang = 2.0 * np.pi * ((k * s) % n) / n
    cos, sin = np.cos(ang), np.sin(ang)
    sin[0, :] = (-1.0) ** np.arange(L)
    fwd = np.concatenate([cos, sin], axis=0)
    wk = np.where(np.arange(L) == 0, 1.0, 2.0)[None, :]
    inv = np.concatenate([cos.T * wk, sin.T * wk], axis=1) / n
    inv[:, L] = ((-1.0) ** np.arange(L)) / n
    return fwd, inv


def _hy_filter_kernel(w1_ref, b1_ref, w2_ref, b2_ref, w3_ref, b3_ref, fr_ref, dec_ref, fa_ref, fb_ref,
                      p_ref, q_ref, r_ref, g_ref, nrm_ref, *, L, tk):
    s = pl.program_id(0)

    @pl.when(s == 0)
    def _():
        row = lax.broadcasted_iota(jnp.int32, (L, LANES), 0)
        lane = lax.broadcasted_iota(jnp.int32, (L, LANES), 1)
        t = row.astype(f32) / L
        band = jnp.where(lane <= HY_BANDS, lane, lane - HY_BANDS).astype(f32)
        ang = (2.0 * math.pi * band) * t
        feat = jnp.where(lane == 0, t, jnp.where(lane <= HY_BANDS, jnp.sin(ang),
                                                 jnp.where(lane <= 2 * HY_BANDS, jnp.cos(ang), 0.0)))
        z = jnp.sin(fr_ref[0:1, :] * (_dot_f32(feat, w1_ref[...]) + b1_ref[...]))
        z = jnp.sin(fr_ref[1:2, :] * (_dot_f32(z, w2_ref[...]) + b2_ref[...]))
        z = _dot_f32(z, w3_ref[...]) + b3_ref[...]
        wide = (L, HY_ORDER * 2 * HY_CH)
        tw = lax.broadcasted_iota(jnp.int32, wide, 0).astype(f32) / L
        taps = z * jnp.exp(-tw * jnp.abs(dec_ref[...]))
        bwd = (_lane_group(wide, HY_CH) % 2) == 1
        first = lax.broadcasted_iota(jnp.int32, wide, 0) == 0
        taps = jnp.where(first, jnp.where(bwd, 0.0, taps), taps)
        g_ref[...] = taps.astype(bf16)
        nrm_ref[...] = jnp.sum(jnp.abs(taps), axis=0, keepdims=True)

    ga = jnp.dot(fa_ref[...], g_ref[...], preferred_element_type=f32)
    gb = jnp.dot(fb_ref[...], g_ref[...], preferred_element_type=f32)
    top = (lax.broadcasted_iota(jnp.int32, (tk, HY_CH), 0) + s * tk) == 0
    for o in range(HY_ORDER):
        c0 = o * 2 * HY_CH
        inv = 1.0 / (nrm_ref[:, c0:c0 + HY_CH] + nrm_ref[:, c0 + HY_CH:c0 + 2 * HY_CH])
        hc = (ga[:, c0:c0 + HY_CH] + ga[:, c0 + HY_CH:c0 + 2 * HY_CH]) * inv
        bf_, bb_ = gb[:, c0:c0 + HY_CH], gb[:, c0 + HY_CH:c0 + 2 * HY_CH]
        hs = jnp.where(top, bf_ + bb_, bf_ - bb_) * inv
        oc = slice(o * HY_CH, (o + 1) * HY_CH)
        p_ref[:, oc] = hc
        q_ref[:, oc] = jnp.where(top, 0.0, hs)
        r_ref[:, oc] = jnp.where(top, hs, hc)


def _hy_filter_spectra(L, fwd_bf, w1, b1, w2, b2, w3, b3, freq, decay, tk=256):
    nk = L // tk
    wide = HY_ORDER * 2 * HY_CH
    full = lambda a: pl.BlockSpec(a.shape, lambda s: (0,) * a.ndim)
    w1p = jnp.pad(w1, ((0, LANES - HY_FEAT), (0, 0)))
    args = [w1p, b1.reshape(1, HY_HID), w2, b2.reshape(1, HY_HID), w3, b3.reshape(1, wide), freq,
            decay.reshape(1, wide)]
    out = jax.ShapeDtypeStruct((L, HY_ORDER * HY_CH), f32)
    plane = pl.BlockSpec((tk, HY_ORDER * HY_CH), lambda s: (s, 0))
    return pl.pallas_call(
        functools.partial(_hy_filter_kernel, L=L, tk=tk),
        out_shape=(out, out, out),
        grid=(nk,),
        in_specs=[full(a) for a in args] + [pl.BlockSpec((tk, L), lambda s: (s, 0)),
                                            pl.BlockSpec((tk, L), lambda s: (s + nk, 0))],
        out_specs=(plane, plane, plane),
        scratch_shapes=[pltpu.VMEM((L, wide), bf16), pltpu.VMEM((1, wide), f32)],
        compiler_params=_cparams(1),
        name="hyena_filter_spectra",
    )(*args, fwd_bf, fwd_bf)


def _hy_pre_kernel(u_ref, w_ref, x1_ref, x2_ref, v_ref, vbf_ref):
    u = u_ref[...]
    w = w_ref[...]
    y = _shift_rows(u, 1) * w[0:1, :] + u * w[1:2, :] + _shift_rows(u, -1) * w[2:3, :]
    x1_ref[...] = y[:, :HY_CH]
    x2_ref[...] = y[:, HY_CH:2 * HY_CH]
    v = y[:, 2 * HY_CH:]
    v_ref[...] = v
    vbf_ref[...] = v.astype(bf16)


def _hy_pre(pb, conv_w, seq0, n_seq, L):
    n = n_seq * L
    blk = pl.BlockSpec((L, HY_CH), lambda b: (b, 0))
    o32 = jax.ShapeDtypeStruct((n, HY_CH), f32)
    return pl.pallas_call(
        _hy_pre_kernel,
        out_shape=(o32, o32, o32, jax.ShapeDtypeStruct((n, HY_CH), bf16)),
        grid=(n_seq,),
        in_specs=[pl.BlockSpec((L, 3 * HY_CH), lambda b: (seq0 + b, 0)),
                  pl.BlockSpec((3, 3 * HY_CH), lambda b: (0, 0))],
        out_specs=(blk, blk, blk, blk),
        compiler_params=_cparams(1),
        name="hyena_short_conv",
    )(pb, conv_w)


def _hy_conv_kernel(*refs, n_seq, L, tk, nk, n_out):
    (zbf_ref, z_ref, m_ref, fa_ref, fb_ref, ic_ref, is_ref, p_ref, q_ref, r_ref, db_ref) = refs[:11]
    out_refs = refs[11:11 + n_out]
    acc_ref = refs[11 + n_out]
    s = pl.program_id(0)

    @pl.when(s == 0)
    def _():
        acc_ref[...] = jnp.zeros_like(acc_ref)

    @pl.when(s < nk)
    def _():
        fa, fb, ic, isn = fa_ref[...], fb_ref[...], ic_ref[...], is_ref[...]
        p, q, r = p_ref[...], q_ref[...], r_ref[...]
        for b in range(n_seq):
            rows = slice(b * L, (b + 1) * L)
            zb = zbf_ref[rows, :]
            a = jnp.dot(fa, zb, preferred_element_type=f32)
            bb = jnp.dot(fb, zb, preferred_element_type=f32)
            yc = (a * p - bb * q).astype(bf16)
            ys = (a * q + bb * r).astype(bf16)
            acc_ref[rows, :] += (jnp.dot(ic, yc, preferred_element_type=f32)
                                 + jnp.dot(isn, ys, preferred_element_type=f32))

    @pl.when(s >= nk)
    def _():
        start = pl.multiple_of((s - nk) * L, L)
        y = acc_ref[pl.ds(start, L), :]
        res = m_ref[...] * (y + db_ref[...] * z_ref[...])
        for o_ref in out_refs:
            o_ref[...] = res.astype(o_ref.dtype)


def _hy_longconv(zbf, z, mult, fwd_bf, inv_bf, planes, order, dbias_o, out_dtypes, n_seq, L, tk):
    nk = L // tk
    n = n_seq * L
    kt = lambda s: jnp.minimum(s, nk - 1)
    ep = lambda s: (jnp.maximum(s - nk, 0), 0)
    plane = pl.BlockSpec((tk, HY_CH), lambda s: (kt(s), order))
    return pl.pallas_call(
        functools.partial(_hy_conv_kernel, n_seq=n_seq, L=L, tk=tk, nk=nk, n_out=len(out_dtypes)),
        out_shape=tuple(jax.ShapeDtypeStruct((n, HY_CH), dt) for dt in out_dtypes),
        grid=(nk + n_seq,),
        in_specs=[
            pl.BlockSpec((n, HY_CH), lambda s: (0, 0)),
            pl.BlockSpec((L, HY_CH), ep),
            pl.BlockSpec((L, HY_CH), ep),
            pl.BlockSpec((tk, L), lambda s: (kt(s), 0)),
            pl.BlockSpec((tk, L), lambda s: (kt(s) + nk, 0)),
            pl.BlockSpec((L, tk), lambda s: (0, kt(s))),
            pl.BlockSpec((L, tk), lambda s: (0, kt(s) + nk)),
            plane, plane, plane,
            pl.BlockSpec((1, HY_CH), lambda s: (0, 0)),
        ],
        out_specs=tuple(pl.BlockSpec((L, HY_CH), ep) for _ in out_dtypes),
        scratch_shapes=[pltpu.VMEM((n, HY_CH), f32)],
        compiler_params=_cparams(1),
        name="hyena_longconv",
    )(zbf, z, mult, fwd_bf, fwd_bf, inv_bf, inv_bf, *planes, dbias_o.reshape(1, HY_CH))


def _hyena(pb, seq0, n_seq, L, dft, hy, tk=256):
    conv_w, w1, b1, w2, b2, w3, b3, freq, decay, dbias = hy
    fwd_bf, inv_bf = dft
    planes = _hy_filter_spectra(L, fwd_bf, w1, b1, w2, b2, w3, b3, freq, decay, tk=tk)
    x1, x2, v, vbf = _hy_pre(pb, conv_w, seq0, n_seq, L)
    z, zbf = _hy_longconv(vbf, v, x1, fwd_bf, inv_bf, planes, 0, dbias[0], (f32, bf16), n_seq, L, tk)
    (ob,) = _hy_longconv(zbf, z, x2, fwd_bf, inv_bf, planes, 1, dbias[1], (bf16,), n_seq, L, tk)
    return ob


def _pool_kernel(u_ref, w_ref, sc_ref, o_ref):
    u = u_ref[...]
    L = u.shape[0]
    back = _shift_rows(u, 1)
    fwd = u
    sums = [back + fwd]
    for k in (1, 2, 4):
        back = back + _shift_rows(back, k)
        fwd = fwd + _shift_rows(fwd, -k)
        sums.append(back + fwd)
    a2, a4, a8, a16 = sums
    grp = _lane_group(u.shape, POOL_GC)
    t = lax.broadcasted_iota(jnp.int32, u.shape, 0)
    half = jnp.left_shift(1, grp)
    cnt = jnp.minimum(t + half, L) - jnp.maximum(t - half, 0)
    tot = jnp.where(grp == 0, a2, jnp.where(grp == 1, a4, jnp.where(grp == 2, a8, a16)))
    pooled = tot / cnt.astype(f32) - u
    y = jnp.dot(pooled.astype(bf16), w_ref[...].astype(bf16), preferred_element_type=f32)
    o_ref[...] = (y * sc_ref[...]).astype(o_ref.dtype)


def _pool(pd, w_bd, scale, seq0, n_seq, L):
    return pl.pallas_call(
        _pool_kernel,
        out_shape=jax.ShapeDtypeStruct((n_seq * L, GROUP_W), bf16),
        grid=(n_seq,),
        in_specs=[pl.BlockSpec((L, GROUP_W), lambda b: (seq0 + b, 0)),
                  pl.BlockSpec((GROUP_W, GROUP_W), lambda b: (0, 0)),
                  pl.BlockSpec((1, GROUP_W), lambda b: (0, 0))],
        out_specs=pl.BlockSpec((L, GROUP_W), lambda b: (b, 0)),
        compiler_params=_cparams(1),
        name="pool_mixer",
    )(pd, w_bd, scale.reshape(1, GROUP_W))


def _outproj_kernel(*refs, tm, moe, split_x):
    mix_refs = refs[:8]
    n_x = 2 if split_x else 1
    x_refs = refs[8:8 + n_x]
    if moe:
        mod_ref, g_ref, w_ref, r_ref, x1_ref, h2_ref, lg_ref, wbf_ref = refs[8 + n_x:]
    else:
        mod_ref, g_ref, w_ref, x1_ref, h2_ref, wbf_ref = refs[8 + n_x:]
    i = pl.program_id(0)

    @pl.when(i == 0)
    def _():
        wbf_ref[...] = w_ref[...].astype(bf16)

    row = _mod_row(i, tm)
    d = D_MODEL
    gate1 = mod_ref[pl.ds(row, 1), 2 * d:3 * d]
    shift2 = mod_ref[pl.ds(row, 1), 3 * d:4 * d]
    scale2 = mod_ref[pl.ds(row, 1), 4 * d:5 * d]
    is_prompt = i < N_PROMPT_TOK // tm
    mixed = jnp.concatenate(
        [jnp.where(is_prompt, mix_refs[2 * j][...], mix_refs[2 * j + 1][...]) for j in range(4)], axis=-1)
    mix = jnp.dot(mixed, wbf_ref[...], preferred_element_type=f32)
    x = jnp.where(is_prompt, x_refs[0][...], x_refs[1][...]) if split_x else x_refs[0][...]
    x1 = x + gate1 * mix
    x1_ref[...] = x1
    h = _modulated_norm(x1, g_ref[...], shift2, scale2)
    h2_ref[...] = h.astype(h2_ref.dtype)
    if moe:
        lg_ref[...] = _dot_bf16x3(h, r_ref[...])


def _outproj(mixers, x, mod_l, g, w_out_l, router_l=None, tm=512):
    moe = router_l is not None
    npb = N_PROMPT_TOK // tm
    tok = lambda w: pl.BlockSpec((tm, w), lambda i: (i, 0))
    in_specs, args = [], []
    for op, os_ in mixers:
        in_specs.append(pl.BlockSpec((tm, GROUP_W), lambda i: (jnp.minimum(i, npb - 1), 0)))
        in_specs.append(pl.BlockSpec((tm, GROUP_W), lambda i: (jnp.maximum(i - npb, 0), 0)))
        args += [op, os_]
    split_x = isinstance(x, tuple)
    if split_x:
        in_specs += [pl.BlockSpec((tm, D_MODEL), lambda i: (jnp.minimum(i, npb - 1), 0)),
                     pl.BlockSpec((tm, D_MODEL), lambda i: (jnp.maximum(i - npb, 0), 0))]
        args += list(x)
    else:
        in_specs.append(tok(D_MODEL))
        args.append(x)
    in_specs += [pl.BlockSpec((MOD_ROWS, 6 * D_MODEL), lambda i: (0, 0)),
                 pl.BlockSpec((1, D_MODEL), lambda i: (0, 0)),
                 pl.BlockSpec((D_MODEL, D_MODEL), lambda i: (0, 0))]
    args += [mod_l, g.reshape(1, D_MODEL), w_out_l]
    out_shape = [jax.ShapeDtypeStruct((N_TOK, D_MODEL), f32),
                 jax.ShapeDtypeStruct((N_TOK, D_MODEL), f32 if moe else bf16)]
    out_specs = [tok(D_MODEL), tok(D_MODEL)]
    if moe:
        in_specs.append(pl.BlockSpec((D_MODEL, ROUTER_PAD), lambda i: (0, 0)))
        args.append(jnp.pad(router_l, ((0, 0), (0, ROUTER_PAD - N_EXPERTS))))
        out_shape.append(jax.ShapeDtypeStruct((N_TOK, ROUTER_PAD), f32))
        out_specs.append(tok(ROUTER_PAD))
    return pl.pallas_call(
        functools.partial(_outproj_kernel, tm=tm, moe=moe, split_x=split_x),
        out_shape=tuple(out_shape),
        grid=(N_TOK // tm,),
        in_specs=in_specs,
        out_specs=tuple(out_specs),
        scratch_shapes=[pltpu.VMEM((D_MODEL, D_MODEL), bf16)],
        compiler_params=_cparams(1),
        name="outproj_norm2",
    )(*args)


def _ffn_kernel(*refs, tm, nf, residual, gather):
    if gather:
        be_ref, nv_ref, idx_ref, idx_next_ref, src_ref, w1_ref, w3_ref, w2_ref, o_ref, xbf_ref, xbuf_ref, sem = refs
    elif residual:
        be_ref, nv_ref, x_ref, w1_ref, w3_ref, w2_ref, x1_ref, mod_ref, o_ref, xbf_ref = refs
    else:
        be_ref, nv_ref, x_ref, w1_ref, w3_ref, w2_ref, o_ref, xbf_ref = refs
    i = pl.program_id(0)
    j = pl.program_id(1)

    @pl.when(j == 0)
    def _():
        o_ref[...] = jnp.zeros_like(o_ref)
        if not gather:
            xbf_ref[...] = x_ref[...].astype(bf16)

    if gather:
        slot = i % 2

        @pl.when((j == 0) & (i == 0))
        def _():
            _start_rows(idx_ref, src_ref, xbuf_ref, 0, sem, tm, 1)

        @pl.when((j == 0) & (i < nv_ref[0]))
        def _():
            _wait_rows(src_ref, xbuf_ref, slot, sem, tm, 1)

        @pl.when((j == 0) & (i + 1 < nv_ref[0]))
        def _():
            _start_rows(idx_next_ref, src_ref, xbuf_ref, 1 - slot, sem, tm, 1)

        @pl.when((j == 0) & (i < nv_ref[0]))
        def _():
            xbf_ref[...] = xbuf_ref[slot].astype(bf16)

    @pl.when(i < nv_ref[0])
    def _():
        x = xbf_ref[...]
        h1 = jnp.dot(x, w1_ref[0].astype(bf16), preferred_element_type=f32)
        h3 = jnp.dot(x, w3_ref[0].astype(bf16), preferred_element_type=f32)
        a = (h1 * jax.nn.sigmoid(h1)) * h3
        o_ref[...] += jnp.dot(a.astype(bf16), w2_ref[0].astype(bf16), preferred_element_type=f32)

    if residual:
        @pl.when(j == nf - 1)
        def _():
            gate2 = mod_ref[pl.ds(_mod_row(i, tm), 1), 5 * D_MODEL:6 * D_MODEL]
            o_ref[...] = x1_ref[...] + gate2 * o_ref[...]


def _ffn_grouped(x_rows, w1, w3, w2, blk_expert, n_valid, tm, tf, x1=None, mod_l=None, row_src=None):
    n_rows = x_rows.shape[0]
    ffn = w1.shape[-1]
    nf = ffn // tf
    nblk = n_rows // tm
    assert nf * tf == ffn and nblk * tm == n_rows
    residual = x1 is not None
    gather = row_src is not None

    def wcol(i, j, be, nv):
        return (be[i], 0, jnp.where(i < nv[0], j, nf - 1))

    def wrow(i, j, be, nv):
        return (be[i], jnp.where(i < nv[0], j, nf - 1), 0)

    rows = pl.BlockSpec((tm, D_MODEL), lambda i, j, be, nv: (i, 0))
    w_specs = [pl.BlockSpec((1, D_MODEL, tf), wcol), pl.BlockSpec((1, D_MODEL, tf), wcol),
               pl.BlockSpec((1, tf, D_MODEL), wrow)]
    scratch = [pltpu.VMEM((tm, D_MODEL), bf16)]
    if gather:
        idx3 = x_rows.reshape(nblk, 1, tm)
        in_specs = [pl.BlockSpec((1, 1, tm), lambda i, j, be, nv: (i, 0, 0), memory_space=pltpu.SMEM),
                    pl.BlockSpec((1, 1, tm), lambda i, j, be, nv: (jnp.minimum(i + 1, nblk - 1), 0, 0),
                                 memory_space=pltpu.SMEM),
                    pl.BlockSpec(memory_space=pl.ANY)] + w_specs
        args = [idx3, idx3, row_src, w1, w3, w2]
        scratch += [pltpu.VMEM((2, tm, D_MODEL), row_src.dtype), pltpu.SemaphoreType.DMA((2,))]
    else:
        in_specs = [rows] + w_specs
        args = [x_rows, w1, w3, w2]
    if residual:
        in_specs += [rows, pl.BlockSpec((MOD_ROWS, 6 * D_MODEL), lambda i, j, be, nv: (0, 0))]
        args += [x1, mod_l]
    return pl.pallas_call(
        functools.partial(_ffn_kernel, tm=tm, nf=nf, residual=residual, gather=gather),
        out_shape=jax.ShapeDtypeStruct((n_rows, D_MODEL), f32),
        grid_spec=pltpu.PrefetchScalarGridSpec(
            num_scalar_prefetch=2,
            grid=(nblk, nf),
            in_specs=in_specs,
            out_specs=rows,
            scratch_shapes=scratch,
        ),
        compiler_params=_cparams(2),
        name="swiglu_grouped",
    )(blk_expert, n_valid, *args)


def _row_copy(src_ref, src_row, buf_ref, slot, rr, k, sem):
    w = src_ref.shape[1]
    return pltpu.make_async_copy(src_ref.at[pl.ds(src_row, 1), :],
                                 buf_ref.at[slot, pl.ds(rr, 1), pl.ds(k * w, w)], sem.at[slot])


def _start_rows(idx_ref, src_ref, buf_ref, slot, sem, n, pack):
    def body(rr, carry):
        for k in range(pack):
            _row_copy(src_ref, idx_ref[0, 0, rr * pack + k], buf_ref, slot, rr, k, sem).start()
        return carry

    lax.fori_loop(0, n // pack, body, 0, unroll=8)


def _wait_rows(src_ref, buf_ref, slot, sem, n, pack):
    def body(rr, carry):
        for k in range(pack):
            _row_copy(src_ref, 0, buf_ref, slot, rr, k, sem).wait()
        return carry

    lax.fori_loop(0, n // pack, body, 0, unroll=8)


def _combine_kernel(*refs, tm, final):
    if final:
        idx_ref, idx_next_ref, ys_ref, x1_ref, mod_ref, gt_ref, g_ref, op_ref, os_ref, ybuf_ref, sem = refs
    else:
        idx_ref, idx_next_ref, ys_ref, x1_ref, mod_ref, gt_ref, o_ref, ybuf_ref, sem = refs
    i = pl.program_id(0)
    n_steps = pl.num_programs(0)
    slot = i % 2
    n_rows = tm * TOP_K

    @pl.when(i == 0)
    def _():
        _start_rows(idx_ref, ys_ref, ybuf_ref, 0, sem, n_rows, TOP_K)

    _wait_rows(ys_ref, ybuf_ref, slot, sem, n_rows, TOP_K)

    @pl.when(i + 1 < n_steps)
    def _():
        _start_rows(idx_next_ref, ys_ref, ybuf_ref, 1 - slot, sem, n_rows, TOP_K)

    gate2 = mod_ref[pl.ds(_mod_row(i, tm), 1), 5 * D_MODEL:6 * D_MODEL]
    gt = gt_ref[...]
    f = gt[:, 0:1] * ybuf_ref[slot, :, :D_MODEL] + gt[:, 1:2] * ybuf_ref[slot, :, D_MODEL:]
    x2 = x1_ref[...] + gate2 * f
    if final:
        ms = jnp.mean(x2 * x2, axis=-1, keepdims=True)
        y = x2 * lax.rsqrt(ms + EPS) * g_ref[...]

        @pl.when(i < N_PROMPT_TOK // tm)
        def _():
            op_ref[...] = y

        @pl.when(i >= N_PROMPT_TOK // tm)
        def _():
            os_ref[...] = y
    else:
        o_ref[...] = x2


def _combine(x1, mod_l, ys, slots, gates, final_g=None, tm=512):
    final = final_g is not None
    npb = N_PROMPT_TOK // tm
    nblk = N_TOK // tm
    tok = pl.BlockSpec((tm, D_MODEL), lambda i: (i, 0))
    idx3 = slots.reshape(nblk, 1, tm * TOP_K)
    in_specs = [pl.BlockSpec((1, 1, tm * TOP_K), lambda i: (i, 0, 0), memory_space=pltpu.SMEM),
                pl.BlockSpec((1, 1, tm * TOP_K), lambda i: (jnp.minimum(i + 1, nblk - 1), 0, 0),
                             memory_space=pltpu.SMEM),
                pl.BlockSpec(memory_space=pl.ANY),
                tok, pl.BlockSpec((MOD_ROWS, 6 * D_MODEL), lambda i: (0, 0)),
                pl.BlockSpec((tm, TOP_K), lambda i: (i, 0))]
    args = [idx3, idx3, ys, x1, mod_l, gates]
    if final:
        in_specs.append(pl.BlockSpec((1, D_MODEL), lambda i: (0, 0)))
        args.append(final_g.reshape(1, D_MODEL))
        out_shape = (jax.ShapeDtypeStruct((N_PROMPT_TOK, D_MODEL), f32),
                     jax.ShapeDtypeStruct((N_SAMPLE_TOK, D_MODEL), f32))
        out_specs = (pl.BlockSpec((tm, D_MODEL), lambda i: (jnp.minimum(i, npb - 1), 0)),
                     pl.BlockSpec((tm, D_MODEL), lambda i: (jnp.maximum(i - npb, 0), 0)))
    else:
        out_shape = jax.ShapeDtypeStruct((N_TOK, D_MODEL), f32)
        out_specs = tok
    return pl.pallas_call(
        functools.partial(_combine_kernel, tm=tm, final=final),
        out_shape=out_shape,
        grid=(N_TOK // tm,),
        in_specs=in_specs,
        out_specs=out_specs,
        scratch_shapes=[pltpu.VMEM((2, tm, TOP_K * D_MODEL), ys.dtype), pltpu.SemaphoreType.DMA((2,))],
        compiler_params=_cparams(1),
        name="expert_combine",
    )(*args)


def _route(logits, tm):
    eid = jnp.arange(N_EXPERTS, dtype=jnp.int32)[None, :]
    v0 = jnp.max(logits, axis=-1, keepdims=True)
    i0 = jnp.min(jnp.where(logits == v0, eid, N_EXPERTS), axis=-1, keepdims=True)
    rest = jnp.where(eid == i0, -jnp.inf, logits)
    v1 = jnp.max(rest, axis=-1, keepdims=True)
    i1 = jnp.min(jnp.where(rest == v1, eid, N_EXPERTS), axis=-1, keepdims=True)
    gates = jax.nn.softmax(jnp.concatenate([v0, v1], axis=-1), axis=-1)
    flat_e = jnp.concatenate([i0, i1], axis=-1).reshape(-1)
    onehot = (flat_e[:, None] == jnp.arange(N_EXPERTS)[None, :]).astype(jnp.int32)
    csum = jnp.cumsum(onehot, axis=0)
    rank = jnp.take_along_axis(csum, flat_e[:, None], axis=1)[:, 0] - 1
    counts = csum[-1]
    padded = ((counts + tm - 1) // tm) * tm
    pend = jnp.cumsum(padded)
    pstart = pend - padded
    dest = pstart[flat_e] + rank
    n_rows = N_TOK * TOP_K + N_EXPERTS * tm
    tok_of = jnp.arange(N_TOK * TOP_K, dtype=jnp.int32) // TOP_K
    row_token = jnp.zeros((n_rows,), jnp.int32).at[dest].set(tok_of)
    blk_start = jnp.arange(n_rows // tm, dtype=jnp.int32) * tm
    blk_expert = jnp.minimum(jnp.sum((blk_start[:, None] >= pend[None, :]).astype(jnp.int32), axis=1),
                             N_EXPERTS - 1)
    n_valid = (pend[-1] // tm).astype(jnp.int32).reshape(1)
    blk_expert = jnp.where(blk_start < pend[-1], blk_expert, blk_expert[jnp.maximum(n_valid[0] - 1, 0)])
    return row_token, blk_expert, n_valid, dest.reshape(N_TOK, TOP_K), gates


def _pool_weight(pool_w_l):
    w = jnp.zeros((GROUP_W, GROUP_W), f32)
    for g in range(len(POOL_WINDOWS)):
        w = w.at[g * POOL_GC:(g + 1) * POOL_GC, g * POOL_GC:(g + 1) * POOL_GC].set(pool_w_l[g])
    return w


def kernel(x_prompt, x_sample, cache_diff_k, cache_diff_v, cache_na_k, cache_na_v, c, c_ctx,
           norm1_g, norm2_g, final_g, ada_w, ada_b, w_in, w_out, diff_lam, diff_subln_g,
           hy_conv, hy_w1, hy_b1, hy_w2, hy_b2, hy_w3, hy_b3, hy_freq, hy_decay, hy_dbias,
           na_rpb, pool_w, pool_scale, ffn_w1, ffn_w3, ffn_w2,
           moe_router, moe_w1, moe_w3, moe_w2):
    cond = jnp.concatenate([c, c_ctx[None, :], jnp.zeros((MOD_ROWS - DEC_BATCH - 1, D_MODEL), f32)], axis=0)
    mod = _ada_mod(cond, ada_w, ada_b)
    cos_np, sin_np = _rope_tables()
    cos_t, sin_t = jnp.asarray(cos_np), jnp.asarray(sin_np)
    dft = {L: tuple(jnp.asarray(m, dtype=bf16) for m in _dft_matrices(L)) for L in (SEQ, DEC_SEQ)}

    P = N_PROMPT_TOK
    x = (x_prompt.reshape(P, D_MODEL), x_sample.reshape(N_SAMPLE_TOK, D_MODEL))
    new_kv = None
    for l in range(DEPTH):
        pa, pb, pc, pd, *new_kv = _inproj(x, l, new_kv, mod[l], norm1_g[l], w_in[l], cos_t, sin_t)

        lam_init = 0.8 - 0.6 * math.exp(-0.3 * l)
        lv = diff_lam[l].astype(f32)
        lam = (jnp.exp(jnp.sum(lv[0] * lv[1])) - jnp.exp(jnp.sum(lv[2] * lv[3])) + lam_init).reshape(1)
        gain = jnp.tile(diff_subln_g[l], A_HEADS).reshape(1, GROUP_W)
        a_kw = dict(n_heads=A_HEADS, diff=True, scale=A_DQK ** -0.5, lam=lam, gain=gain, out_scale=1.0 - lam_init)
        oa_p = _attention(pa, 0, 0, pa, 1, 0, pa, 2, 0, n_seq=BATCH, seq_len=SEQ, n_keys=SEQ, tq=SEQ, **a_kw)
        n_all = DEC_SEQ + PAST_LEN
        k_all = jnp.concatenate([pa[P:, GROUP_W:2 * GROUP_W].reshape(DEC_BATCH, DEC_SEQ, GROUP_W),
                                 cache_diff_k[:, l].reshape(DEC_BATCH, PAST_LEN, GROUP_W)], axis=1)
        v_all = jnp.concatenate([pa[P:, 2 * GROUP_W:].reshape(DEC_BATCH, DEC_SEQ, GROUP_W),
                                 cache_diff_v[:, l].reshape(DEC_BATCH, PAST_LEN, GROUP_W)], axis=1)
        oa_s = _attention(pa, 0, P // DEC_SEQ, k_all.reshape(-1, GROUP_W), 0, 0, v_all.reshape(-1, GROUP_W), 0, 0,
                          n_seq=DEC_BATCH, seq_len=DEC_SEQ, n_keys=n_all, tq=256, **a_kw)

        hy = (hy_conv[l], hy_w1[l], hy_b1[l], hy_w2[l], hy_b2[l], hy_w3[l], hy_b3[l],
              hy_freq[l], hy_decay[l], hy_dbias[l])
        ob_p = _hyena(pb, 0, BATCH, SEQ, dft[SEQ], hy)
        ob_s = _hyena(pb, P // DEC_SEQ, DEC_BATCH, DEC_SEQ, dft[DEC_SEQ], hy)

        oc_p = _attention(pc, 0, 0, pc, 1, 0, pc, 2, 0, n_seq=BATCH, seq_len=SEQ, n_keys=SEQ, tq=SEQ,
                          n_heads=C_HEADS, diff=False, scale=C_DH ** -0.5)
        oc_s = _na_attention(pc, cache_na_k[:, l].reshape(-1, GROUP_W), cache_na_v[:, l].reshape(-1, GROUP_W),
                             _na_bias_tables(na_rpb[l]))

        w_bd = _pool_weight(pool_w[l])
        od_p = _pool(pd, w_bd, pool_scale[l], 0, BATCH, SEQ)
        od_s = _pool(pd, w_bd, pool_scale[l], P // DEC_SEQ, DEC_BATCH, DEC_SEQ)

        mixers = [(oa_p, oa_s), (ob_p, ob_s), (oc_p, oc_s), (od_p, od_s)]
        tm = 1024
        if l % 2 == 0:
            assert l != DEPTH - 1
            x1, h2 = _outproj(mixers, x, mod[l], norm2_g[l], w_out[l])
            x = _ffn_grouped(h2, ffn_w1[l // 2][None], ffn_w3[l // 2][None], ffn_w2[l // 2][None],
                             jnp.zeros((N_TOK // tm,), jnp.int32), jnp.full((1,), N_TOK // tm, jnp.int32),
                             tm=tm, tf=256, x1=x1, mod_l=mod[l])
        else:
            x1, h2, logits = _outproj(mixers, x, mod[l], norm2_g[l], w_out[l], moe_router[l // 2])
            row_token, blk_expert, n_valid, dest, gates = _route(logits[:, :N_EXPERTS], tm)
            ys = _ffn_grouped(row_token, moe_w1[l // 2], moe_w3[l // 2], moe_w2[l // 2], blk_expert, n_valid,
                              tm=tm, tf=512, row_src=h2)
            x = _combine(x1, mod[l], ys, dest, gates, final_g if l == DEPTH - 1 else None)

    assert isinstance(x, tuple)
    y_prompt = x[0].reshape(BATCH, SEQ, D_MODEL)
    y_sample = x[1].reshape(DEC_BATCH, DEC_SEQ, D_MODEL)
    kak, kav, kck, kcv = new_kv
    return (y_prompt, y_sample,
            kak.reshape(BATCH, DEPTH, SEQ, A_HEADS, 2 * A_DQK), kav.reshape(BATCH, DEPTH, SEQ, A_HEADS, A_DV),
            kck.reshape(BATCH, DEPTH, SEQ, C_HEADS, C_DH), kcv.reshape(BATCH, DEPTH, SEQ, C_HEADS, C_DH))
```

```python
import functools
import math

import numpy as np
import jax
import jax.numpy as jnp
from jax import lax
from jax.experimental import pallas as pl
from jax.experimental.pallas import tpu as pltpu

D_MODEL = 1024
BATCH = 32
SEQ = 256
DEPTH = 2
DEC_BATCH = 4
DEC_SEQ = 2048
PAST_LEN = 512
GRID_W = 64
GRID_H = DEC_SEQ // GRID_W
GROUP_W = D_MODEL // 4
A_HEADS = 4
A_DQK = GROUP_W // (2 * A_HEADS)
A_DV = GROUP_W // A_HEADS
ROPE_BASE = 10000.0
HY_CH = GROUP_W
HY_ORDER = 2
HY_BANDS = 8
HY_FEAT = 1 + 2 * HY_BANDS
HY_HID = 64
C_HEADS = 4
C_DH = GROUP_W // C_HEADS
NA_KH = 8
NA_KW = 16
POOL_WINDOWS = (2, 4, 8, 16)
POOL_GC = GROUP_W // 4
PROJ_W = 3 * GROUP_W + 3 * HY_CH + 3 * GROUP_W + GROUP_W
N_EXPERTS = 8
TOP_K = 2
EPS = 1e-6
NEG = -1e30
LOG2E = math.log2(math.e)

N_PROMPT_TOK = BATCH * SEQ
N_SAMPLE_TOK = DEC_BATCH * DEC_SEQ
N_TOK = N_PROMPT_TOK + N_SAMPLE_TOK
MOD_ROWS = 8
CTX_ROW = DEC_BATCH
ROUTER_PAD = 128
LANES = 128

NA_QROWS = 4
NA_SLAB_ROWS = 12
NA_GROUPS = GRID_H // NA_QROWS

f32 = jnp.float32
bf16 = jnp.bfloat16

VMEM_LIMIT = 56 * 1024 * 1024
N_DMA_QUEUES = 2
HY_SPECTRUM_TILE = 512
HY_EPILOGUE_ROWS = 2048
_NT = (((1,), (1,)), ((), ()))


def _cparams(n_axes):
    return pltpu.CompilerParams(
        dimension_semantics=("arbitrary",) * n_axes, vmem_limit_bytes=VMEM_LIMIT)


def _mod_row(i, tm):
    n_prompt_blocks = N_PROMPT_TOK // tm
    blocks_per_seq = DEC_SEQ // tm
    return jnp.where(i < n_prompt_blocks, CTX_ROW, (i - n_prompt_blocks) // blocks_per_seq)


def _split3(a):
    a0 = a.astype(bf16)
    r1 = a - a0.astype(f32)
    a1 = r1.astype(bf16)
    a2 = (r1 - a1.astype(f32)).astype(bf16)
    return a0, a1, a2


def _dot_bf16x3(a, b):
    a0 = a.astype(bf16)
    a1 = (a - a0.astype(f32)).astype(bf16)
    b0 = b.astype(bf16)
    b1 = (b - b0.astype(f32)).astype(bf16)
    d = functools.partial(jnp.dot, preferred_element_type=f32)
    return (d(a1, b0) + d(a0, b1)) + d(a0, b0)


def _dot_f32(a, b):
    a0, a1, a2 = _split3(a)
    b0, b1, b2 = _split3(b)
    d = functools.partial(jnp.dot, preferred_element_type=f32)
    return ((d(a2, b0) + d(a1, b1) + d(a0, b2)) + (d(a1, b0) + d(a0, b1))) + d(a0, b0)


def _lane_group(shape, width):
    return lax.shift_right_logical(lax.broadcasted_iota(jnp.int32, shape, 1), int(math.log2(width)))


def _shift_rows(x, d):
    n = x.shape[0]
    r = pltpu.roll(x, d % n, 0)
    row = lax.broadcasted_iota(jnp.int32, x.shape, 0)
    keep = (row >= d) if d > 0 else (row < n + d)
    return jnp.where(keep, r, 0.0)


def _ada_kernel(cond_ref, w_ref, b_ref, o_ref):
    c = cond_ref[...]
    s = c * jax.nn.sigmoid(c)
    o_ref[0] = jnp.dot(s.astype(bf16), w_ref[0].astype(bf16), preferred_element_type=f32) + b_ref[0]


def _ada_mod(cond, ada_w, ada_b):
    tn = 1536
    n6 = 6 * D_MODEL
    return pl.pallas_call(
        _ada_kernel,
        out_shape=jax.ShapeDtypeStruct((DEPTH, MOD_ROWS, n6), f32),
        grid=(DEPTH, n6 // tn),
        in_specs=[
            pl.BlockSpec((MOD_ROWS, D_MODEL), lambda l, j: (0, 0)),
            pl.BlockSpec((1, D_MODEL, tn), lambda l, j: (l, 0, j)),
            pl.BlockSpec((1, 1, tn), lambda l, j: (l, 0, j)),
        ],
        out_specs=pl.BlockSpec((1, MOD_ROWS, tn), lambda l, j: (l, 0, j)),
        compiler_params=_cparams(2),
        name="ada_mod",
    )(cond, ada_w, ada_b.reshape(DEPTH, 1, n6))


def _rope_tables():
    t = np.arange(DEC_SEQ)
    nf = A_DQK // 4
    inv = ROPE_BASE ** (-np.arange(nf, dtype=np.float64) / nf)
    ar = (t // GRID_W)[:, None] * inv
    ac = (t % GRID_W)[:, None] * inv
    cos = np.concatenate([np.cos(ar), np.cos(ar), np.cos(ac), np.cos(ac)], axis=1)
    sin = np.concatenate([-np.sin(ar), np.sin(ar), -np.sin(ac), np.sin(ac)], axis=1)
    reps = GROUP_W // A_DQK
    return (np.tile(cos, (1, reps)).astype(np.float32), np.tile(sin, (1, reps)).astype(np.float32))


def _modulated_norm(x, g, shift, scale):
    ms = jnp.mean(x * x, axis=-1, keepdims=True)
    return (x * lax.rsqrt(ms + EPS) * g) * (1.0 + scale) + shift


def _inproj_kernel(*refs, tm, split_x, n_alias):
    n_x = 2 if split_x else 1
    x_refs = refs[:n_x]
    mod_ref, g_ref, w_ref, cos_ref, sin_ref = refs[n_x:n_x + 5]
    outs = refs[n_x + 5 + n_alias:]
    pa_ref, pb_ref, pc_ref, pd_ref, kak_ref, kav_ref, kck_ref, kcv_ref, wbf_ref = outs
    i = pl.program_id(0)
    is_prompt = i < N_PROMPT_TOK // tm

    @pl.when(i == 0)
    def _():
        wbf_ref[...] = w_ref[...].astype(bf16)

    row = _mod_row(i, tm)
    shift = mod_ref[pl.ds(row, 1), 0:D_MODEL]
    scale = mod_ref[pl.ds(row, 1), D_MODEL:2 * D_MODEL]
    x = jnp.where(is_prompt, x_refs[0][...], x_refs[1][...]) if split_x else x_refs[0][...]
    h = _modulated_norm(x, g_ref[...], shift, scale)
    proj = jnp.dot(h.astype(bf16), wbf_ref[...], preferred_element_type=f32)
    w3 = 3 * GROUP_W
    pb_ref[...] = proj[:, w3:2 * w3]
    pc_ref[...] = proj[:, 2 * w3:3 * w3]
    pd_ref[...] = proj[:, 3 * w3:]
    pa_ref[:, 2 * GROUP_W:] = proj[:, 2 * GROUP_W:w3]

    @pl.when(is_prompt)
    def _():
        pa_ref[:, :2 * GROUP_W] = proj[:, :2 * GROUP_W]
        kv_shape = (tm // SEQ, 1, SEQ, GROUP_W)
        cols = (GROUP_W, 2 * GROUP_W, 2 * w3 + GROUP_W, 2 * w3 + 2 * GROUP_W)
        for ref, c0 in zip((kak_ref, kav_ref, kck_ref, kcv_ref), cols):
            ref[:, 0:1] = proj[:, c0:c0 + GROUP_W].reshape(kv_shape)
            if ref.shape[1] > 1:
                ref[:, 1:] = jnp.zeros((tm // SEQ, ref.shape[1] - 1, SEQ, GROUP_W), f32)

    @pl.when(i >= N_PROMPT_TOK // tm)
    def _():
        cos = cos_ref[...]
        sin = sin_ref[...]
        lane = lax.broadcasted_iota(jnp.int32, (tm, GROUP_W), 1)
        first = (lane % 16) < 8
        for s in range(2):
            v = proj[:, s * GROUP_W:(s + 1) * GROUP_W]
            partner = jnp.where(first, pltpu.roll(v, GROUP_W - 8, 1), pltpu.roll(v, 8, 1))
            pa_ref[:, s * GROUP_W:(s + 1) * GROUP_W] = v * cos + partner * sin


def _inproj(x, layer, kv_prev, mod_l, g, w_in_l, cos_t, sin_t, tm=512):
    npb = N_PROMPT_TOK // tm
    blocks_per_seq = DEC_SEQ // tm
    w3 = 3 * GROUP_W
    split_x = isinstance(x, tuple)

    def rope_idx(i):
        return (jnp.maximum(i - npb, 0) % blocks_per_seq, 0)

    if split_x:
        x_specs = [pl.BlockSpec((tm, D_MODEL), lambda i: (jnp.minimum(i, npb - 1), 0)),
                   pl.BlockSpec((tm, D_MODEL), lambda i: (jnp.maximum(i - npb, 0), 0))]
        x_args = list(x)
    else:
        x_specs = [pl.BlockSpec((tm, D_MODEL), lambda i: (i, 0))]
        x_args = [x]
    kv_args = list(kv_prev) if kv_prev is not None else []
    n_in = len(x_args) + 5
    kv_shape = jax.ShapeDtypeStruct((BATCH, DEPTH, SEQ, GROUP_W), f32)
    kv_layers = DEPTH if layer == 0 else 1
    kv_spec = pl.BlockSpec((tm // SEQ, kv_layers, SEQ, GROUP_W), lambda i: (jnp.minimum(i, npb - 1), layer, 0, 0))
    tok = lambda w: pl.BlockSpec((tm, w), lambda i: (i, 0))
    return pl.pallas_call(
        functools.partial(_inproj_kernel, tm=tm, split_x=split_x, n_alias=len(kv_args)),
        out_shape=(jax.ShapeDtypeStruct((N_TOK, w3), f32), jax.ShapeDtypeStruct((N_TOK, w3), f32),
                   jax.ShapeDtypeStruct((N_TOK, w3), f32), jax.ShapeDtypeStruct((N_TOK, GROUP_W), f32),
                   kv_shape, kv_shape, kv_shape, kv_shape),
        grid=(N_TOK // tm,),
        in_specs=x_specs + [
            pl.BlockSpec((MOD_ROWS, 6 * D_MODEL), lambda i: (0, 0)),
            pl.BlockSpec((1, D_MODEL), lambda i: (0, 0)),
            pl.BlockSpec((D_MODEL, PROJ_W), lambda i: (0, 0)),
            pl.BlockSpec((tm, GROUP_W), rope_idx),
            pl.BlockSpec((tm, GROUP_W), rope_idx),
        ] + [pl.BlockSpec(memory_space=pl.ANY)] * len(kv_args),
        out_specs=(tok(w3), tok(w3), tok(w3), tok(GROUP_W), kv_spec, kv_spec, kv_spec, kv_spec),
        scratch_shapes=[pltpu.VMEM((D_MODEL, PROJ_W), bf16)],
        input_output_aliases={n_in + j: 4 + j for j in range(len(kv_args))},
        compiler_params=_cparams(1),
        name="norm1_inproj",
    )(*x_args, mod_l, g.reshape(1, D_MODEL), w_in_l, cos_t, sin_t, *kv_args)


def _values_with_ones(v, vhead, vlane, h, hw):
    ones_col = ((h + 1) * hw) % GROUP_W
    vm = jnp.where(vhead == h, v, jnp.where(vlane == ones_col, 1.0, 0.0))
    return vm.astype(bf16), ones_col


def _attn_kernel(lam_ref, q_ref, k_ref, v_ref, g_ref, o_ref, *, n_heads, diff, scale, out_scale):
    tq = q_ref.shape[0]
    n = k_ref.shape[0]
    q = q_ref[...] * (scale * LOG2E)
    k = k_ref[...].astype(bf16)
    v = v_ref[...]
    hw = GROUP_W // n_heads
    n_maps = 2 if diff else 1
    qgrp = _lane_group((tq, GROUP_W), hw // n_maps)
    qhead = _lane_group((tq, GROUP_W), hw)
    vhead = _lane_group((n, GROUP_W), hw)
    vlane = lax.broadcasted_iota(jnp.int32, (n, GROUP_W), 1)
    acc = jnp.zeros((tq, GROUP_W), f32)
    for h in range(n_heads):
        vm, ones_col = _values_with_ones(v, vhead, vlane, h, hw)
        maps = []
        for m in range(n_maps):
            qm = jnp.where(qgrp == n_maps * h + m, q, 0.0).astype(bf16)
            s = lax.dot_general(qm, k, _NT, preferred_element_type=f32)
            e = jnp.exp2(s - jnp.max(s, axis=-1, keepdims=True)).astype(bf16)
            o = jnp.dot(e, vm, preferred_element_type=f32)
            maps.append(o * (1.0 / o[:, ones_col:ones_col + 1]))
        oh = maps[0] - lam_ref[0] * maps[1] if diff else maps[0]
        acc = acc + jnp.where(qhead == h, oh, 0.0)
    if diff:
        r = lax.shift_right_logical(lax.broadcasted_iota(jnp.int32, (GROUP_W, GROUP_W), 0), 6)
        c = lax.shift_right_logical(lax.broadcasted_iota(jnp.int32, (GROUP_W, GROUP_W), 1), 6)
        bd = jnp.where(r == c, 1.0, 0.0).astype(bf16)
        sq = acc * acc
        hi = sq.astype(bf16)
        lo = (sq - hi.astype(f32)).astype(bf16)
        ms = (jnp.dot(hi, bd, preferred_element_type=f32) + jnp.dot(lo, bd, preferred_element_type=f32)) * (1.0 / A_DV)
        acc = (acc * lax.rsqrt(ms + EPS) * g_ref[...]) * out_scale
    o_ref[...] = acc.astype(o_ref.dtype)


def _attention(q_src, q_col, q_row0, k_src, k_col, k_row0, v_src, v_col, v_row0, *,
               n_seq, seq_len, n_keys, tq, n_heads, diff, scale, lam=None, gain=None, out_scale=1.0):
    qb = seq_len // tq
    if lam is None:
        lam = jnp.zeros((1,), f32)
    if gain is None:
        gain = jnp.ones((1, GROUP_W), f32)
    return pl.pallas_call(
        functools.partial(_attn_kernel, n_heads=n_heads, diff=diff, scale=scale, out_scale=out_scale),
        out_shape=jax.ShapeDtypeStruct((n_seq * seq_len, GROUP_W), bf16),
        grid=(n_seq, qb),
        in_specs=[
            pl.BlockSpec(memory_space=pltpu.SMEM),
            pl.BlockSpec((tq, GROUP_W), lambda b, i: ((q_row0 + b) * qb + i, q_col)),
            pl.BlockSpec((n_keys, GROUP_W), lambda b, i: (k_row0 + b, k_col)),
            pl.BlockSpec((n_keys, GROUP_W), lambda b, i: (v_row0 + b, v_col)),
            pl.BlockSpec((1, GROUP_W), lambda b, i: (0, 0)),
        ],
        out_specs=pl.BlockSpec((tq, GROUP_W), lambda b, i: (b * qb + i, 0)),
        compiler_params=_cparams(2),
        name="diff_attention" if diff else "softmax_attention",
    )(lam, q_src, k_src, v_src, gain)


def _na_group_geometry(g):
    r0 = g * NA_QROWS
    slab0 = min(max(r0 - NA_KH // 2, 0), GRID_H - NA_SLAB_ROWS)
    return r0, slab0


def _na_bias_tables(rpb):
    c = np.arange(GRID_W)
    ws = np.clip(c - NA_KW // 2, 0, GRID_W - NA_KW)
    kc = np.arange(GRID_W)
    col_ok = (kc[None, :] >= ws[:, None]) & (kc[None, :] < ws[:, None] + NA_KW)
    pad = GRID_W - NA_KW
    rp = jnp.pad(rpb.astype(f32) * LOG2E, ((0, 0), (0, 0), (pad, pad)))
    tc = jnp.stack([rp[:, :, GRID_W - 1 - ci:2 * GRID_W - 1 - ci] for ci in range(GRID_W)], axis=2)
    tc = jnp.where(col_ok[None, None], tc, NEG)
    neg_blk = jnp.full((C_HEADS, GRID_W, GRID_W), NEG, f32)
    tables = []
    for g in (0, 1, NA_GROUPS - 1):
        r0, slab0 = _na_group_geometry(g)
        rows = []
        for rq in range(NA_QROWS):
            r = r0 + rq
            rs = min(max(r - NA_KH // 2, 0), GRID_H - NA_KH)
            blks = []
            for kl in range(NA_SLAB_ROWS):
                kr = slab0 + kl
                blks.append(tc[:, kr - r + NA_KH - 1] if rs <= kr < rs + NA_KH else neg_blk)
            rows.append(jnp.concatenate(blks, axis=-1))
        tables.append(jnp.concatenate(rows, axis=-2))
    return jnp.stack(tables, axis=0)


def _na_kernel(q_ref, k_ref, v_ref, kx_ref, vx_ref, bias_ref, o_ref):
    g = pl.program_id(1)
    tq = NA_QROWS * GRID_W
    ns = NA_SLAB_ROWS * GRID_W
    slab0 = jnp.clip(g * NA_QROWS - NA_KH // 2, 0, GRID_H - NA_SLAB_ROWS)
    start = pl.multiple_of(slab0 * GRID_W, GRID_W)
    q = q_ref[...] * (C_DH ** -0.5 * LOG2E)
    ks = k_ref[pl.ds(start, ns), :].astype(bf16)
    vs = v_ref[pl.ds(start, ns), :]
    kx = kx_ref[...].astype(bf16)
    vx = vx_ref[...]
    qhead = _lane_group((tq, GROUP_W), C_DH)
    vshead = _lane_group((ns, GROUP_W), C_DH)
    vslane = lax.broadcasted_iota(jnp.int32, (ns, GROUP_W), 1)
    vxhead = _lane_group((PAST_LEN, GROUP_W), C_DH)
    vxlane = lax.broadcasted_iota(jnp.int32, (PAST_LEN, GROUP_W), 1)
    acc = jnp.zeros((tq, GROUP_W), f32)
    for h in range(C_HEADS):
        qm = jnp.where(qhead == h, q, 0.0).astype(bf16)
        sl = lax.dot_general(qm, ks, _NT, preferred_element_type=f32)
        b = bias_ref[0, h]
        sl = jnp.where(b > 0.5 * NEG, sl + b, NEG)
        sx = lax.dot_general(qm, kx, _NT, preferred_element_type=f32)
        mx = jnp.maximum(jnp.max(sl, axis=-1, keepdims=True), jnp.max(sx, axis=-1, keepdims=True))
        el = jnp.exp2(sl - mx).astype(bf16)
        ex = jnp.exp2(sx - mx).astype(bf16)
        vsm, ones_col = _values_with_ones(vs, vshead, vslane, h, C_DH)
        vxm, _ = _values_with_ones(vx, vxhead, vxlane, h, C_DH)
        o = jnp.dot(el, vsm, preferred_element_type=f32) + jnp.dot(ex, vxm, preferred_element_type=f32)
        acc = acc + jnp.where(qhead == h, o * (1.0 / o[:, ones_col:ones_col + 1]), 0.0)
    o_ref[...] = acc.astype(o_ref.dtype)


def _na_attention(pc, kx, vx, bias):
    tq = NA_QROWS * GRID_W
    ns = NA_SLAB_ROWS * GRID_W
    q_blk0 = N_PROMPT_TOK // tq
    s_blk0 = N_PROMPT_TOK // DEC_SEQ

    def bias_idx(b, g):
        return (jnp.where(g == 0, 0, jnp.where(g == NA_GROUPS - 1, 2, 1)), 0, 0, 0)

    return pl.pallas_call(
        _na_kernel,
        out_shape=jax.ShapeDtypeStruct((N_SAMPLE_TOK, GROUP_W), bf16),
        grid=(DEC_BATCH, NA_GROUPS),
        in_specs=[
            pl.BlockSpec((tq, GROUP_W), lambda b, g: (q_blk0 + b * NA_GROUPS + g, 0)),
            pl.BlockSpec((DEC_SEQ, GROUP_W), lambda b, g: (s_blk0 + b, 1)),
            pl.BlockSpec((DEC_SEQ, GROUP_W), lambda b, g: (s_blk0 + b, 2)),
            pl.BlockSpec((PAST_LEN, GROUP_W), lambda b, g: (b, 0)),
            pl.BlockSpec((PAST_LEN, GROUP_W), lambda b, g: (b, 0)),
            pl.BlockSpec((1, C_HEADS, tq, ns), bias_idx),
        ],
        out_specs=pl.BlockSpec((tq, GROUP_W), lambda b, g: (b * NA_GROUPS + g, 0)),
        compiler_params=_cparams(2),
        name="neighbourhood_attention",
    )(pc, pc, pc, kx, vx, bias)


def _dft_matrices(L):
    n = 2 * L
    k = np.arange(L)[:, None]
    s = np.arange(L)[None, :]
    ang = 2.0 * np.pi * ((k * s) % n) / n
    cos, sin = np.cos(ang), np.sin(ang)
    sin[0, :] = (-1.0) ** np.arange(L)
    fwd = np.concatenate([cos, sin], axis=0)
    wk = np.where(np.arange(L) == 0, 1.0, 2.0)[None, :]
    inv = np.concatenate([cos.T * wk, sin.T * wk], axis=1) / n
    inv[:, L] = ((-1.0) ** np.arange(L)) / n
    return fwd, inv


def _hy_filter_kernel(w1_ref, b1_ref, w2_ref, b2_ref, w3_ref, b3_ref, fr_ref, dec_ref, fa_ref, fb_ref,
                      p_ref, q_ref, r_ref, g_ref, nrm_ref, *, L, tk):
    s = pl.program_id(0)

    @pl.when(s == 0)
    def _():
        row = lax.broadcasted_iota(jnp.int32, (L, LANES), 0)
        lane = lax.broadcasted_iota(jnp.int32, (L, LANES), 1)
        t = row.astype(f32) / L
        band = jnp.where(lane <= HY_BANDS, lane, lane - HY_BANDS).astype(f32)
        ang = (2.0 * math.pi * band) * t
        feat = jnp.where(lane == 0, t, jnp.where(lane <= HY_BANDS, jnp.sin(ang),
                                                 jnp.where(lane <= 2 * HY_BANDS, jnp.cos(ang), 0.0)))
        z = jnp.sin(fr_ref[0:1, :] * (_dot_f32(feat, w1_ref[...]) + b1_ref[...]))
        z = jnp.sin(fr_ref[1:2, :] * (_dot_f32(z, w2_ref[...]) + b2_ref[...]))
        z = _dot_f32(z, w3_ref[...]) + b3_ref[...]
        wide = (L, HY_ORDER * 2 * HY_CH)
        tw = lax.broadcasted_iota(jnp.int32, wide, 0).astype(f32) / L
        taps = z * jnp.exp(-tw * jnp.abs(dec_ref[...]))
        bwd = (_lane_group(wide, HY_CH) % 2) == 1
        first = lax.broadcasted_iota(jnp.int32, wide, 0) == 0
        taps = jnp.where(first, jnp.where(bwd, 0.0, taps), taps)
        g_ref[...] = taps.astype(bf16)
        nrm_ref[...] = jnp.sum(jnp.abs(taps), axis=0, keepdims=True)

    ga = jnp.dot(fa_ref[...], g_ref[...], preferred_element_type=f32)
    gb = jnp.dot(fb_ref[...], g_ref[...], preferred_element_type=f32)
    top = (lax.broadcasted_iota(jnp.int32, (tk, HY_CH), 0) + s * tk) == 0
    for o in range(HY_ORDER):
        c0 = o * 2 * HY_CH
        inv = 1.0 / (nrm_ref[:, c0:c0 + HY_CH] + nrm_ref[:, c0 + HY_CH:c0 + 2 * HY_CH])
        hc = (ga[:, c0:c0 + HY_CH] + ga[:, c0 + HY_CH:c0 + 2 * HY_CH]) * inv
        bf_, bb_ = gb[:, c0:c0 + HY_CH], gb[:, c0 + HY_CH:c0 + 2 * HY_CH]
        hs = jnp.where(top, bf_ + bb_, bf_ - bb_) * inv
        oc = slice(o * HY_CH, (o + 1) * HY_CH)
        p_ref[:, oc] = hc
        q_ref[:, oc] = jnp.where(top, 0.0, hs)
        r_ref[:, oc] = jnp.where(top, hs, hc)


def _hy_filter_spectra(L, fwd_bf, w1, b1, w2, b2, w3, b3, freq, decay, tk=256):
    nk = L // tk
    wide = HY_ORDER * 2 * HY_CH
    full = lambda a: pl.BlockSpec(a.shape, lambda s: (0,) * a.ndim)
    w1p = jnp.pad(w1, ((0, LANES - HY_FEAT), (0, 0)))
    args = [w1p, b1.reshape(1, HY_HID), w2, b2.reshape(1, HY_HID), w3, b3.reshape(1, wide), freq,
            decay.reshape(1, wide)]
    out = jax.ShapeDtypeStruct((L, HY_ORDER * HY_CH), f32)
    plane = pl.BlockSpec((tk, HY_ORDER * HY_CH), lambda s: (s, 0))
    return pl.pallas_call(
        functools.partial(_hy_filter_kernel, L=L, tk=tk),
        out_shape=(out, out, out),
        grid=(nk,),
        in_specs=[full(a) for a in args] + [pl.BlockSpec((tk, L), lambda s: (s, 0)),
                                            pl.BlockSpec((tk, L), lambda s: (s + nk, 0))],
        out_specs=(plane, plane, plane),
        scratch_shapes=[pltpu.VMEM((L, wide), bf16), pltpu.VMEM((1, wide), f32)],
        compiler_params=_cparams(1),
        name="hyena_filter_spectra",
    )(*args, fwd_bf, fwd_bf)


def _hy_pre_kernel(u_ref, w_ref, x1_ref, x2_ref, v_ref, vbf_ref):
    u = u_ref[...]
    w = w_ref[...]
    y = _shift_rows(u, 1) * w[0:1, :] + u * w[1:2, :] + _shift_rows(u, -1) * w[2:3, :]
    x1_ref[...] = y[:, :HY_CH]
    x2_ref[...] = y[:, HY_CH:2 * HY_CH]
    v = y[:, 2 * HY_CH:]
    v_ref[...] = v
    vbf_ref[...] = v.astype(bf16)


def _hy_pre(pb, conv_w, seq0, n_seq, L):
    n = n_seq * L
    blk = pl.BlockSpec((L, HY_CH), lambda b: (b, 0))
    o32 = jax.ShapeDtypeStruct((n, HY_CH), f32)
    return pl.pallas_call(
        _hy_pre_kernel,
        out_shape=(o32, o32, o32, jax.ShapeDtypeStruct((n, HY_CH), bf16)),
        grid=(n_seq,),
        in_specs=[pl.BlockSpec((L, 3 * HY_CH), lambda b: (seq0 + b, 0)),
                  pl.BlockSpec((3, 3 * HY_CH), lambda b: (0, 0))],
        out_specs=(blk, blk, blk, blk),
        compiler_params=_cparams(1),
        name="hyena_short_conv",
    )(pb, conv_w)


def _hy_conv_kernel(*refs, n_seq, L, nk, te, n_out):
    (zbf_ref, z_ref, m_ref, fa_ref, fb_ref, ic_ref, is_ref, p_ref, q_ref, r_ref, db_ref) = refs[:11]
    out_refs = refs[11:11 + n_out]
    acc_ref = refs[11 + n_out]
    s = pl.program_id(0)

    @pl.when(s == 0)
    def _():
        acc_ref[...] = jnp.zeros_like(acc_ref)

    @pl.when(s < nk)
    def _():
        fa, fb, ic, isn = fa_ref[...], fb_ref[...], ic_ref[...], is_ref[...]
        p, q, r = p_ref[...], q_ref[...], r_ref[...]
        for b in range(n_seq):
            rows = slice(b * L, (b + 1) * L)
            zb = zbf_ref[rows, :]
            a = jnp.dot(fa, zb, preferred_element_type=f32)
            bb = jnp.dot(fb, zb, preferred_element_type=f32)
            yc = (a * p - bb * q).astype(bf16)
            ys = (a * q + bb * r).astype(bf16)
            acc_ref[rows, :] += (jnp.dot(ic, yc, preferred_element_type=f32)
                                 + jnp.dot(isn, ys, preferred_element_type=f32))

    @pl.when(s >= nk)
    def _():
        start = pl.multiple_of((s - nk) * te, te)
        y = acc_ref[pl.ds(start, te), :]
        res = m_ref[...] * (y + db_ref[...] * z_ref[...])
        for o_ref in out_refs:
            o_ref[...] = res.astype(o_ref.dtype)


def _hy_longconv(zbf, z, mult, fwd_bf, inv_bf, planes, order, dbias_o, out_dtypes, n_seq, L, tk):
    nk = L // tk
    n = n_seq * L
    te = max(L, HY_EPILOGUE_ROWS)
    assert n % te == 0
    kt = lambda s: jnp.minimum(s, nk - 1)
    ep = lambda s: (jnp.maximum(s - nk, 0), 0)
    plane = pl.BlockSpec((tk, HY_CH), lambda s: (kt(s), order))
    return pl.pallas_call(
        functools.partial(_hy_conv_kernel, n_seq=n_seq, L=L, nk=nk, te=te, n_out=len(out_dtypes)),
        out_shape=tuple(jax.ShapeDtypeStruct((n, HY_CH), dt) for dt in out_dtypes),
        grid=(nk + n // te,),
        in_specs=[
            pl.BlockSpec((n, HY_CH), lambda s: (0, 0)),
            pl.BlockSpec((te, HY_CH), ep),
            pl.BlockSpec((te, HY_CH), ep),
            pl.BlockSpec((tk, L), lambda s: (kt(s), 0)),
            pl.BlockSpec((tk, L), lambda s: (kt(s) + nk, 0)),
            pl.BlockSpec((L, tk), lambda s: (0, kt(s))),
            pl.BlockSpec((L, tk), lambda s: (0, kt(s) + nk)),
            plane, plane, plane,
            pl.BlockSpec((1, HY_CH), lambda s: (0, 0)),
        ],
        out_specs=tuple(pl.BlockSpec((te, HY_CH), ep) for _ in out_dtypes),
        scratch_shapes=[pltpu.VMEM((n, HY_CH), f32)],
        compiler_params=_cparams(1),
        name="hyena_longconv",
    )(zbf, z, mult, fwd_bf, fwd_bf, inv_bf, inv_bf, *planes, dbias_o.reshape(1, HY_CH))


def _hyena(pb, seq0, n_seq, L, dft, hy):
    conv_w, w1, b1, w2, b2, w3, b3, freq, decay, dbias = hy
    fwd_bf, inv_bf = dft
    tk = min(L, HY_SPECTRUM_TILE)
    planes = _hy_filter_spectra(L, fwd_bf, w1, b1, w2, b2, w3, b3, freq, decay, tk=tk)
    x1, x2, v, vbf = _hy_pre(pb, conv_w, seq0, n_seq, L)
    z, zbf = _hy_longconv(vbf, v, x1, fwd_bf, inv_bf, planes, 0, dbias[0], (f32, bf16), n_seq, L, tk)
    (ob,) = _hy_longconv(zbf, z, x2, fwd_bf, inv_bf, planes, 1, dbias[1], (bf16,), n_seq, L, tk)
    return ob


def _pool_kernel(u_ref, w_ref, sc_ref, o_ref):
    u = u_ref[...]
    L = u.shape[0]
    back = _shift_rows(u, 1)
    fwd = u
    sums = [back + fwd]
    for k in (1, 2, 4):
        back = back + _shift_rows(back, k)
        fwd = fwd + _shift_rows(fwd, -k)
        sums.append(back + fwd)
    a2, a4, a8, a16 = sums
    grp = _lane_group(u.shape, POOL_GC)
    t = lax.broadcasted_iota(jnp.int32, u.shape, 0)
    half = jnp.left_shift(1, grp)
    cnt = jnp.minimum(t + half, L) - jnp.maximum(t - half, 0)
    tot = jnp.where(grp == 0, a2, jnp.where(grp == 1, a4, jnp.where(grp == 2, a8, a16)))
    pooled = tot / cnt.astype(f32) - u
    y = jnp.dot(pooled.astype(bf16), w_ref[...].astype(bf16), preferred_element_type=f32)
    o_ref[...] = (y * sc_ref[...]).astype(o_ref.dtype)


def _pool(pd, w_bd, scale, seq0, n_seq, L):
    return pl.pallas_call(
        _pool_kernel,
        out_shape=jax.ShapeDtypeStruct((n_seq * L, GROUP_W), bf16),
        grid=(n_seq,),
        in_specs=[pl.BlockSpec((L, GROUP_W), lambda b: (seq0 + b, 0)),
                  pl.BlockSpec((GROUP_W, GROUP_W), lambda b: (0, 0)),
                  pl.BlockSpec((1, GROUP_W), lambda b: (0, 0))],
        out_specs=pl.BlockSpec((L, GROUP_W), lambda b: (b, 0)),
        compiler_params=_cparams(1),
        name="pool_mixer",
    )(pd, w_bd, scale.reshape(1, GROUP_W))


def _outproj_kernel(*refs, tm, moe, split_x):
    mix_refs = refs[:8]
    n_x = 2 if split_x else 1
    x_refs = refs[8:8 + n_x]
    if moe:
        mod_ref, g_ref, w_ref, r_ref, x1_ref, h2_ref, lg_ref, wbf_ref = refs[8 + n_x:]
    else:
        mod_ref, g_ref, w_ref, x1_ref, h2_ref, wbf_ref = refs[8 + n_x:]
    i = pl.program_id(0)

    @pl.when(i == 0)
    def _():
        wbf_ref[...] = w_ref[...].astype(bf16)

    row = _mod_row(i, tm)
    d = D_MODEL
    gate1 = mod_ref[pl.ds(row, 1), 2 * d:3 * d]
    shift2 = mod_ref[pl.ds(row, 1), 3 * d:4 * d]
    scale2 = mod_ref[pl.ds(row, 1), 4 * d:5 * d]
    is_prompt = i < N_PROMPT_TOK // tm
    mixed = jnp.concatenate(
        [jnp.where(is_prompt, mix_refs[2 * j][...], mix_refs[2 * j + 1][...]) for j in range(4)], axis=-1)
    mix = jnp.dot(mixed, wbf_ref[...], preferred_element_type=f32)
    x = jnp.where(is_prompt, x_refs[0][...], x_refs[1][...]) if split_x else x_refs[0][...]
    x1 = x + gate1 * mix
    x1_ref[...] = x1
    h = _modulated_norm(x1, g_ref[...], shift2, scale2)
    h2_ref[...] = h.astype(h2_ref.dtype)
    if moe:
        lg_ref[...] = _dot_bf16x3(h, r_ref[...])


def _outproj(mixers, x, mod_l, g, w_out_l, router_l=None, tm=512):
    moe = router_l is not None
    npb = N_PROMPT_TOK // tm
    tok = lambda w: pl.BlockSpec((tm, w), lambda i: (i, 0))
    in_specs, args = [], []
    for op, os_ in mixers:
        in_specs.append(pl.BlockSpec((tm, GROUP_W), lambda i: (jnp.minimum(i, npb - 1), 0)))
        in_specs.append(pl.BlockSpec((tm, GROUP_W), lambda i: (jnp.maximum(i - npb, 0), 0)))
        args += [op, os_]
    split_x = isinstance(x, tuple)
    if split_x:
        in_specs += [pl.BlockSpec((tm, D_MODEL), lambda i: (jnp.minimum(i, npb - 1), 0)),
                     pl.BlockSpec((tm, D_MODEL), lambda i: (jnp.maximum(i - npb, 0), 0))]
        args += list(x)
    else:
        in_specs.append(tok(D_MODEL))
        args.append(x)
    in_specs += [pl.BlockSpec((MOD_ROWS, 6 * D_MODEL), lambda i: (0, 0)),
                 pl.BlockSpec((1, D_MODEL), lambda i: (0, 0)),
                 pl.BlockSpec((D_MODEL, D_MODEL), lambda i: (0, 0))]
    args += [mod_l, g.reshape(1, D_MODEL), w_out_l]
    out_shape = [jax.ShapeDtypeStruct((N_TOK, D_MODEL), f32),
                 jax.ShapeDtypeStruct((N_TOK, D_MODEL), f32 if moe else bf16)]
    out_specs = [tok(D_MODEL), tok(D_MODEL)]
    if moe:
        in_specs.append(pl.BlockSpec((D_MODEL, ROUTER_PAD), lambda i: (0, 0)))
        args.append(jnp.pad(router_l, ((0, 0), (0, ROUTER_PAD - N_EXPERTS))))
        out_shape.append(jax.ShapeDtypeStruct((N_TOK, ROUTER_PAD), f32))
        out_specs.append(tok(ROUTER_PAD))
    return pl.pallas_call(
        functools.partial(_outproj_kernel, tm=tm, moe=moe, split_x=split_x),
        out_shape=tuple(out_shape),
        grid=(N_TOK // tm,),
        in_specs=in_specs,
        out_specs=tuple(out_specs),
        scratch_shapes=[pltpu.VMEM((D_MODEL, D_MODEL), bf16)],
        compiler_params=_cparams(1),
        name="outproj_norm2",
    )(*args)


def _ffn_kernel(be_ref, nv_ref, idx_ref, idx_next_ref, src_ref, w1_ref, w3_ref, w2_ref, o_ref,
                xbf_ref, xbuf_ref, sem, *, tm):
    i = pl.program_id(0)
    j = pl.program_id(1)
    live = i < nv_ref[0]
    slot = i % 2

    @pl.when(j == 0)
    def _():
        o_ref[...] = jnp.zeros_like(o_ref)

    @pl.when((j == 0) & (i == 0))
    def _():
        _start_rows(idx_ref, src_ref, xbuf_ref, 0, sem, tm, 1)

    @pl.when((j == 0) & live)
    def _():
        _wait_rows(src_ref, xbuf_ref, slot, sem, tm, 1)

    @pl.when((j == 0) & (i + 1 < nv_ref[0]))
    def _():
        _start_rows(idx_next_ref, src_ref, xbuf_ref, 1 - slot, sem, tm, 1)

    @pl.when((j == 0) & live)
    def _():
        xbf_ref[...] = xbuf_ref[slot].astype(bf16)

    @pl.when(live)
    def _():
        x = xbf_ref[...]
        h1 = jnp.dot(x, w1_ref[0].astype(bf16), preferred_element_type=f32)
        h3 = jnp.dot(x, w3_ref[0].astype(bf16), preferred_element_type=f32)
        a = (h1 * jax.nn.sigmoid(h1)) * h3
        o_ref[...] += jnp.dot(a.astype(bf16), w2_ref[0].astype(bf16), preferred_element_type=f32)


def _ffn_routed(row_src, row_token, w1, w3, w2, blk_expert, n_valid, tm, tf):
    n_rows = row_token.shape[0]
    ffn = w1.shape[-1]
    nf = ffn // tf
    nblk = n_rows // tm
    assert nf * tf == ffn and nblk * tm == n_rows

    def wcol(i, j, be, nv):
        return (be[i], 0, jnp.where(i < nv[0], j, nf - 1))

    def wrow(i, j, be, nv):
        return (be[i], jnp.where(i < nv[0], j, nf - 1), 0)

    idx3 = row_token.reshape(nblk, 1, tm)
    return pl.pallas_call(
        functools.partial(_ffn_kernel, tm=tm),
        out_shape=jax.ShapeDtypeStruct((n_rows, D_MODEL), f32),
        grid_spec=pltpu.PrefetchScalarGridSpec(
            num_scalar_prefetch=2,
            grid=(nblk, nf),
            in_specs=[
                pl.BlockSpec((1, 1, tm), lambda i, j, be, nv: (i, 0, 0), memory_space=pltpu.SMEM),
                pl.BlockSpec((1, 1, tm), lambda i, j, be, nv: (jnp.minimum(i + 1, nblk - 1), 0, 0),
                             memory_space=pltpu.SMEM),
                pl.BlockSpec(memory_space=pl.ANY),
                pl.BlockSpec((1, D_MODEL, tf), wcol),
                pl.BlockSpec((1, D_MODEL, tf), wcol),
                pl.BlockSpec((1, tf, D_MODEL), wrow),
            ],
            out_specs=pl.BlockSpec((tm, D_MODEL), lambda i, j, be, nv: (i, 0)),
            scratch_shapes=[pltpu.VMEM((tm, D_MODEL), bf16), pltpu.VMEM((2, tm, D_MODEL), row_src.dtype),
                            pltpu.SemaphoreType.DMA((2,))],
        ),
        compiler_params=_cparams(2),
        name="swiglu_routed",
    )(blk_expert, n_valid, idx3, idx3, row_src, w1, w3, w2)


def _cast_kernel(a_ref, o_ref):
    o_ref[...] = a_ref[...].astype(o_ref.dtype)


def _to_bf16(a, block_rows):
    rows, cols = a.shape
    assert rows % block_rows == 0
    return pl.pallas_call(
        _cast_kernel,
        out_shape=jax.ShapeDtypeStruct(a.shape, bf16),
        grid=(rows // block_rows,),
        in_specs=[pl.BlockSpec((block_rows, cols), lambda i: (i, 0))],
        out_specs=pl.BlockSpec((block_rows, cols), lambda i: (i, 0)),
        compiler_params=_cparams(1),
        name="cast_bf16",
    )(a)


def _ffn_dense_kernel(x_ref, w1_ref, w3_ref, w2_ref, x1_ref, mod_ref, o_ref, *, tm, chunks):
    i = pl.program_id(0)
    x = x_ref[...]
    acc = jnp.zeros((tm, D_MODEL), f32)
    c0 = 0
    for width in chunks:
        h1 = jnp.dot(x, w1_ref[:, c0:c0 + width], preferred_element_type=f32)
        h3 = jnp.dot(x, w3_ref[:, c0:c0 + width], preferred_element_type=f32)
        a = (h1 * jax.nn.sigmoid(h1)) * h3
        acc = acc + jnp.dot(a.astype(bf16), w2_ref[c0:c0 + width, :], preferred_element_type=f32)
        c0 += width
    gate2 = mod_ref[pl.ds(_mod_row(i, tm), 1), 5 * D_MODEL:6 * D_MODEL]
    o_ref[...] = x1_ref[...] + gate2 * acc


def _ffn_dense(h2, w1, w3, w2, x1, mod_l, tm=512, chunk=512):
    ffn = w1.shape[1]
    chunks = [chunk] * (ffn // chunk) + ([ffn % chunk] if ffn % chunk else [])
    w1b, w3b, w2b = _to_bf16(w1, 256), _to_bf16(w3, 256), _to_bf16(w2, 256)
    tok = lambda dt: pl.BlockSpec((tm, D_MODEL), lambda i: (i, 0))
    full = lambda a: pl.BlockSpec(a.shape, lambda i: (0, 0))
    return pl.pallas_call(
        functools.partial(_ffn_dense_kernel, tm=tm, chunks=tuple(chunks)),
        out_shape=jax.ShapeDtypeStruct((N_TOK, D_MODEL), f32),
        grid=(N_TOK // tm,),
        in_specs=[tok(bf16), full(w1b), full(w3b), full(w2b), tok(f32),
                  pl.BlockSpec((MOD_ROWS, 6 * D_MODEL), lambda i: (0, 0))],
        out_specs=tok(f32),
        compiler_params=_cparams(1),
        name="swiglu_dense",
    )(h2, w1b, w3b, w2b, x1, mod_l)


def _row_copy(src_ref, src_row, buf_ref, slot, rr, k, sem):
    w = src_ref.shape[1]
    return pltpu.make_async_copy(src_ref.at[pl.ds(src_row, 1), :],
                                 buf_ref.at[slot, pl.ds(rr, 1), pl.ds(k * w, w)], sem.at[slot])


def _start_rows(idx_ref, src_ref, buf_ref, slot, sem, n, pack):
    per_iter = max(pack, N_DMA_QUEUES)

    def body(it, carry):
        for u in range(per_iter):
            rr = it * (per_iter // pack) + u // pack
            _row_copy(src_ref, idx_ref[0, 0, it * per_iter + u], buf_ref, slot, rr, u % pack, sem).start(
                priority=u % N_DMA_QUEUES)
        return carry

    lax.fori_loop(0, n // per_iter, body, 0, unroll=8)


def _wait_rows(src_ref, buf_ref, slot, sem, n, pack):
    def body(rr, carry):
        for k in range(pack):
            _row_copy(src_ref, 0, buf_ref, slot, rr, k, sem).wait()
        return carry

    lax.fori_loop(0, n // pack, body, 0, unroll=8)


def _combine_kernel(*refs, tm, final):
    if final:
        idx_ref, idx_next_ref, ys_ref, x1_ref, mod_ref, gt_ref, g_ref, op_ref, os_ref, ybuf_ref, sem = refs
    else:
        idx_ref, idx_next_ref, ys_ref, x1_ref, mod_ref, gt_ref, o_ref, ybuf_ref, sem = refs
    i = pl.program_id(0)
    n_steps = pl.num_programs(0)
    slot = i % 2
    n_rows = tm * TOP_K

    @pl.when(i == 0)
    def _():
        _start_rows(idx_ref, ys_ref, ybuf_ref, 0, sem, n_rows, TOP_K)

    _wait_rows(ys_ref, ybuf_ref, slot, sem, n_rows, TOP_K)

    @pl.when(i + 1 < n_steps)
    def _():
        _start_rows(idx_next_ref, ys_ref, ybuf_ref, 1 - slot, sem, n_rows, TOP_K)

    gate2 = mod_ref[pl.ds(_mod_row(i, tm), 1), 5 * D_MODEL:6 * D_MODEL]
    gt = gt_ref[...]
    f = gt[:, 0:1] * ybuf_ref[slot, :, :D_MODEL] + gt[:, 1:2] * ybuf_ref[slot, :, D_MODEL:]
    x2 = x1_ref[...] + gate2 * f
    if final:
        ms = jnp.mean(x2 * x2, axis=-1, keepdims=True)
        y = x2 * lax.rsqrt(ms + EPS) * g_ref[...]

        @pl.when(i < N_PROMPT_TOK // tm)
        def _():
            op_ref[...] = y

        @pl.when(i >= N_PROMPT_TOK // tm)
        def _():
            os_ref[...] = y
    else:
        o_ref[...] = x2


def _combine(x1, mod_l, ys, slots, gates, final_g=None, tm=512):
    final = final_g is not None
    npb = N_PROMPT_TOK // tm
    nblk = N_TOK // tm
    tok = pl.BlockSpec((tm, D_MODEL), lambda i: (i, 0))
    idx3 = slots.reshape(nblk, 1, tm * TOP_K)
    in_specs = [pl.BlockSpec((1, 1, tm * TOP_K), lambda i: (i, 0, 0), memory_space=pltpu.SMEM),
                pl.BlockSpec((1, 1, tm * TOP_K), lambda i: (jnp.minimum(i + 1, nblk - 1), 0, 0),
                             memory_space=pltpu.SMEM),
                pl.BlockSpec(memory_space=pl.ANY),
                tok, pl.BlockSpec((MOD_ROWS, 6 * D_MODEL), lambda i: (0, 0)),
                pl.BlockSpec((tm, TOP_K), lambda i: (i, 0))]
    args = [idx3, idx3, ys, x1, mod_l, gates]
    if final:
        in_specs.append(pl.BlockSpec((1, D_MODEL), lambda i: (0, 0)))
        args.append(final_g.reshape(1, D_MODEL))
        out_shape = (jax.ShapeDtypeStruct((N_PROMPT_TOK, D_MODEL), f32),
                     jax.ShapeDtypeStruct((N_SAMPLE_TOK, D_MODEL), f32))
        out_specs = (pl.BlockSpec((tm, D_MODEL), lambda i: (jnp.minimum(i, npb - 1), 0)),
                     pl.BlockSpec((tm, D_MODEL), lambda i: (jnp.maximum(i - npb, 0), 0)))
    else:
        out_shape = jax.ShapeDtypeStruct((N_TOK, D_MODEL), f32)
        out_specs = tok
    return pl.pallas_call(
        functools.partial(_combine_kernel, tm=tm, final=final),
        out_shape=out_shape,
        grid=(N_TOK // tm,),
        in_specs=in_specs,
        out_specs=out_specs,
        scratch_shapes=[pltpu.VMEM((2, tm, TOP_K * D_MODEL), ys.dtype), pltpu.SemaphoreType.DMA((2,))],
        compiler_params=_cparams(1),
        name="expert_combine",
    )(*args)


def _route(logits, tm):
    eid = jnp.arange(N_EXPERTS, dtype=jnp.int32)[None, :]
    v0 = jnp.max(logits, axis=-1, keepdims=True)
    i0 = jnp.min(jnp.where(logits == v0, eid, N_EXPERTS), axis=-1, keepdims=True)
    rest = jnp.where(eid == i0, -jnp.inf, logits)
    v1 = jnp.max(rest, axis=-1, keepdims=True)
    i1 = jnp.min(jnp.where(rest == v1, eid, N_EXPERTS), axis=-1, keepdims=True)
    gates = jax.nn.softmax(jnp.concatenate([v0, v1], axis=-1), axis=-1)
    flat_e = jnp.concatenate([i0, i1], axis=-1).reshape(-1)
    onehot = (flat_e[:, None] == jnp.arange(N_EXPERTS)[None, :]).astype(jnp.int32)
    csum = jnp.cumsum(onehot, axis=0)
    rank = jnp.take_along_axis(csum, flat_e[:, None], axis=1)[:, 0] - 1
    counts = csum[-1]
    padded = ((counts + tm - 1) // tm) * tm
    pend = jnp.cumsum(padded)
    pstart = pend - padded
    dest = pstart[flat_e] + rank
    n_rows = N_TOK * TOP_K + N_EXPERTS * tm
    tok_of = jnp.arange(N_TOK * TOP_K, dtype=jnp.int32) // TOP_K
    row_token = jnp.zeros((n_rows,), jnp.int32).at[dest].set(tok_of, unique_indices=True)
    blk_start = jnp.arange(n_rows // tm, dtype=jnp.int32) * tm
    blk_expert = jnp.minimum(jnp.sum((blk_start[:, None] >= pend[None, :]).astype(jnp.int32), axis=1),
                             N_EXPERTS - 1)
    n_valid = (pend[-1] // tm).astype(jnp.int32).reshape(1)
    blk_expert = jnp.where(blk_start < pend[-1], blk_expert, blk_expert[jnp.maximum(n_valid[0] - 1, 0)])
    return row_token, blk_expert, n_valid, dest.reshape(N_TOK, TOP_K), gates


def _pool_weight(pool_w_l):
    w = jnp.zeros((GROUP_W, GROUP_W), f32)
    for g in range(len(POOL_WINDOWS)):
        w = w.at[g * POOL_GC:(g + 1) * POOL_GC, g * POOL_GC:(g + 1) * POOL_GC].set(pool_w_l[g])
    return w


def kernel(x_prompt, x_sample, cache_diff_k, cache_diff_v, cache_na_k, cache_na_v, c, c_ctx,
           norm1_g, norm2_g, final_g, ada_w, ada_b, w_in, w_out, diff_lam, diff_subln_g,
           hy_conv, hy_w1, hy_b1, hy_w2, hy_b2, hy_w3, hy_b3, hy_freq, hy_decay, hy_dbias,
           na_rpb, pool_w, pool_scale, ffn_w1, ffn_w3, ffn_w2,
           moe_router, moe_w1, moe_w3, moe_w2):
    cond = jnp.concatenate([c, c_ctx[None, :], jnp.zeros((MOD_ROWS - DEC_BATCH - 1, D_MODEL), f32)], axis=0)
    mod = _ada_mod(cond, ada_w, ada_b)
    cos_np, sin_np = _rope_tables()
    cos_t, sin_t = jnp.asarray(cos_np), jnp.asarray(sin_np)
    dft = {L: tuple(jnp.asarray(m, dtype=bf16) for m in _dft_matrices(L)) for L in (SEQ, DEC_SEQ)}

    P = N_PROMPT_TOK
    x = (x_prompt.reshape(P, D_MODEL), x_sample.reshape(N_SAMPLE_TOK, D_MODEL))
    new_kv = None
    for l in range(DEPTH):
        pa, pb, pc, pd, *new_kv = _inproj(x, l, new_kv, mod[l], norm1_g[l], w_in[l], cos_t, sin_t)

        lam_init = 0.8 - 0.6 * math.exp(-0.3 * l)
        lv = diff_lam[l].astype(f32)
        lam = (jnp.exp(jnp.sum(lv[0] * lv[1])) - jnp.exp(jnp.sum(lv[2] * lv[3])) + lam_init).reshape(1)
        gain = jnp.tile(diff_subln_g[l], A_HEADS).reshape(1, GROUP_W)
        a_kw = dict(n_heads=A_HEADS, diff=True, scale=A_DQK ** -0.5, lam=lam, gain=gain, out_scale=1.0 - lam_init)
        oa_p = _attention(pa, 0, 0, pa, 1, 0, pa, 2, 0, n_seq=BATCH, seq_len=SEQ, n_keys=SEQ, tq=SEQ, **a_kw)
        n_all = DEC_SEQ + PAST_LEN
        k_all = jnp.concatenate([pa[P:, GROUP_W:2 * GROUP_W].reshape(DEC_BATCH, DEC_SEQ, GROUP_W),
                                 cache_diff_k[:, l].reshape(DEC_BATCH, PAST_LEN, GROUP_W)], axis=1)
        v_all = jnp.concatenate([pa[P:, 2 * GROUP_W:].reshape(DEC_BATCH, DEC_SEQ, GROUP_W),
                                 cache_diff_v[:, l].reshape(DEC_BATCH, PAST_LEN, GROUP_W)], axis=1)
        oa_s = _attention(pa, 0, P // DEC_SEQ, k_all.reshape(-1, GROUP_W), 0, 0, v_all.reshape(-1, GROUP_W), 0, 0,
                          n_seq=DEC_BATCH, seq_len=DEC_SEQ, n_keys=n_all, tq=256, **a_kw)

        hy = (hy_conv[l], hy_w1[l], hy_b1[l], hy_w2[l], hy_b2[l], hy_w3[l], hy_b3[l],
              hy_freq[l], hy_decay[l], hy_dbias[l])
        ob_p = _hyena(pb, 0, BATCH, SEQ, dft[SEQ], hy)
        ob_s = _hyena(pb, P // DEC_SEQ, DEC_BATCH, DEC_SEQ, dft[DEC_SEQ], hy)

        oc_p = _attention(pc, 0, 0, pc, 1, 0, pc, 2, 0, n_seq=BATCH, seq_len=SEQ, n_keys=SEQ, tq=SEQ,
                          n_heads=C_HEADS, diff=False, scale=C_DH ** -0.5)
        oc_s = _na_attention(pc, cache_na_k[:, l].reshape(-1, GROUP_W), cache_na_v[:, l].reshape(-1, GROUP_W),
                             _na_bias_tables(na_rpb[l]))

        w_bd = _pool_weight(pool_w[l])
        od_p = _pool(pd, w_bd, pool_scale[l], 0, BATCH, SEQ)
        od_s = _pool(pd, w_bd, pool_scale[l], P // DEC_SEQ, DEC_BATCH, DEC_SEQ)

        mixers = [(oa_p, oa_s), (ob_p, ob_s), (oc_p, oc_s), (od_p, od_s)]
        tm = 1024
        if l % 2 == 0:
            assert l != DEPTH - 1
            x1, h2 = _outproj(mixers, x, mod[l], norm2_g[l], w_out[l])
            x = _ffn_dense(h2, ffn_w1[l // 2], ffn_w3[l // 2], ffn_w2[l // 2], x1, mod[l])
        else:
            x1, h2, logits = _outproj(mixers, x, mod[l], norm2_g[l], w_out[l], moe_router[l // 2])
            row_token, blk_expert, n_valid, dest, gates = _route(logits[:, :N_EXPERTS], tm)
            ys = _ffn_routed(h2, row_token, moe_w1[l // 2], moe_w3[l // 2], moe_w2[l // 2], blk_expert, n_valid,
                             tm=tm, tf=512)
            x = _combine(x1, mod[l], ys, dest, gates, final_g if l == DEPTH - 1 else None)

    assert isinstance(x, tuple)
    y_prompt = x[0].reshape(BATCH, SEQ, D_MODEL)
    y_sample = x[1].reshape(DEC_BATCH, DEC_SEQ, D_MODEL)
    kak, kav, kck, kcv = new_kv
    return (y_prompt, y_sample,
            kak.reshape(BATCH, DEPTH, SEQ, A_HEADS, 2 * A_DQK), kav.reshape(BATCH, DEPTH, SEQ, A_HEADS, A_DV),
            kck.reshape(BATCH, DEPTH, SEQ, C_HEADS, C_DH), kcv.reshape(BATCH, DEPTH, SEQ, C_HEADS, C_DH))
```

```python
import functools
import math

import numpy as np
import jax
import jax.numpy as jnp
from jax import lax
from jax.experimental import pallas as pl
from jax.experimental.pallas import tpu as pltpu

D_MODEL = 1024
BATCH = 32
SEQ = 256
DEPTH = 2
DEC_BATCH = 4
DEC_SEQ = 2048
PAST_LEN = 512
GRID_W = 64
GRID_H = DEC_SEQ // GRID_W
GROUP_W = D_MODEL // 4
A_HEADS = 4
A_DQK = GROUP_W // (2 * A_HEADS)
A_DV = GROUP_W // A_HEADS
ROPE_BASE = 10000.0
HY_CH = GROUP_W
HY_ORDER = 2
HY_BANDS = 8
HY_FEAT = 1 + 2 * HY_BANDS
HY_HID = 64
C_HEADS = 4
C_DH = GROUP_W // C_HEADS
NA_KH = 8
NA_KW = 16
POOL_WINDOWS = (2, 4, 8, 16)
POOL_GC = GROUP_W // 4
PROJ_W = 3 * GROUP_W + 3 * HY_CH + 3 * GROUP_W + GROUP_W
N_EXPERTS = 8
TOP_K = 2
EPS = 1e-6
NEG = -1e30
LOG2E = math.log2(math.e)

N_PROMPT_TOK = BATCH * SEQ
N_SAMPLE_TOK = DEC_BATCH * DEC_SEQ
N_TOK = N_PROMPT_TOK + N_SAMPLE_TOK
MOD_ROWS = 8
CTX_ROW = DEC_BATCH
ROUTER_PAD = 128
LANES = 128

NA_QROWS = 4
NA_SLAB_ROWS = 12
NA_GROUPS = GRID_H // NA_QROWS

f32 = jnp.float32
bf16 = jnp.bfloat16

VMEM_LIMIT = 56 * 1024 * 1024
ROW_DMA_PRIORITY = 1
HY_SPECTRUM_TILE = 512
HY_EPILOGUE_ROWS = 2048
_NT = (((1,), (1,)), ((), ()))


def _cparams(n_axes):
    return pltpu.CompilerParams(
        dimension_semantics=("arbitrary",) * n_axes, vmem_limit_bytes=VMEM_LIMIT)


def _mod_row(i, tm):
    n_prompt_blocks = N_PROMPT_TOK // tm
    blocks_per_seq = DEC_SEQ // tm
    return jnp.where(i < n_prompt_blocks, CTX_ROW, (i - n_prompt_blocks) // blocks_per_seq)


def _split3(a):
    a0 = a.astype(bf16)
    r1 = a - a0.astype(f32)
    a1 = r1.astype(bf16)
    a2 = (r1 - a1.astype(f32)).astype(bf16)
    return a0, a1, a2


def _dot_bf16x3(a, b):
    a0 = a.astype(bf16)
    a1 = (a - a0.astype(f32)).astype(bf16)
    b0 = b.astype(bf16)
    b1 = (b - b0.astype(f32)).astype(bf16)
    d = functools.partial(jnp.dot, preferred_element_type=f32)
    return (d(a1, b0) + d(a0, b1)) + d(a0, b0)


def _dot_f32(a, b):
    a0, a1, a2 = _split3(a)
    b0, b1, b2 = _split3(b)
    d = functools.partial(jnp.dot, preferred_element_type=f32)
    return ((d(a2, b0) + d(a1, b1) + d(a0, b2)) + (d(a1, b0) + d(a0, b1))) + d(a0, b0)


def _lane_group(shape, width):
    return lax.shift_right_logical(lax.broadcasted_iota(jnp.int32, shape, 1), int(math.log2(width)))


def _shift_rows(x, d):
    n = x.shape[0]
    r = pltpu.roll(x, d % n, 0)
    row = lax.broadcasted_iota(jnp.int32, x.shape, 0)
    keep = (row >= d) if d > 0 else (row < n + d)
    return jnp.where(keep, r, 0.0)


def _ada_kernel(cond_ref, w_ref, b_ref, o_ref):
    c = cond_ref[...]
    s = c * jax.nn.sigmoid(c)
    o_ref[0] = jnp.dot(s.astype(bf16), w_ref[0].astype(bf16), preferred_element_type=f32) + b_ref[0]


def _ada_mod(cond, ada_w, ada_b):
    tn = 1536
    n6 = 6 * D_MODEL
    return pl.pallas_call(
        _ada_kernel,
        out_shape=jax.ShapeDtypeStruct((DEPTH, MOD_ROWS, n6), f32),
        grid=(DEPTH, n6 // tn),
        in_specs=[
            pl.BlockSpec((MOD_ROWS, D_MODEL), lambda l, j: (0, 0)),
            pl.BlockSpec((1, D_MODEL, tn), lambda l, j: (l, 0, j)),
            pl.BlockSpec((1, 1, tn), lambda l, j: (l, 0, j)),
        ],
        out_specs=pl.BlockSpec((1, MOD_ROWS, tn), lambda l, j: (l, 0, j)),
        compiler_params=_cparams(2),
        name="ada_mod",
    )(cond, ada_w, ada_b.reshape(DEPTH, 1, n6))


def _rope_tables():
    t = np.arange(DEC_SEQ)
    nf = A_DQK // 4
    inv = ROPE_BASE ** (-np.arange(nf, dtype=np.float64) / nf)
    ar = (t // GRID_W)[:, None] * inv
    ac = (t % GRID_W)[:, None] * inv
    cos = np.concatenate([np.cos(ar), np.cos(ar), np.cos(ac), np.cos(ac)], axis=1)
    sin = np.concatenate([-np.sin(ar), np.sin(ar), -np.sin(ac), np.sin(ac)], axis=1)
    reps = GROUP_W // A_DQK
    return (np.tile(cos, (1, reps)).astype(np.float32), np.tile(sin, (1, reps)).astype(np.float32))


def _modulated_norm(x, g, shift, scale):
    ms = jnp.mean(x * x, axis=-1, keepdims=True)
    return (x * lax.rsqrt(ms + EPS) * g) * (1.0 + scale) + shift


def _inproj_kernel(*refs, tm, split_x, n_alias):
    n_x = 2 if split_x else 1
    x_refs = refs[:n_x]
    mod_ref, g_ref, w_ref, cos_ref, sin_ref = refs[n_x:n_x + 5]
    outs = refs[n_x + 5 + n_alias:]
    pa_ref, pb_ref, pc_ref, pd_ref, kak_ref, kav_ref, kck_ref, kcv_ref, wbf_ref = outs
    i = pl.program_id(0)
    is_prompt = i < N_PROMPT_TOK // tm

    @pl.when(i == 0)
    def _():
        wbf_ref[...] = w_ref[0].astype(bf16)

    row = _mod_row(i, tm)
    shift = mod_ref[pl.ds(row, 1), 0:D_MODEL]
    scale = mod_ref[pl.ds(row, 1), D_MODEL:2 * D_MODEL]
    x = jnp.where(is_prompt, x_refs[0][...], x_refs[1][...]) if split_x else x_refs[0][...]
    h = _modulated_norm(x, g_ref[...], shift, scale)
    proj = jnp.dot(h.astype(bf16), wbf_ref[...], preferred_element_type=f32)
    w3 = 3 * GROUP_W
    pb_ref[...] = proj[:, w3:2 * w3]
    pc_ref[...] = proj[:, 2 * w3:3 * w3]
    pd_ref[...] = proj[:, 3 * w3:]
    pa_ref[:, 2 * GROUP_W:] = proj[:, 2 * GROUP_W:w3]

    @pl.when(is_prompt)
    def _():
        pa_ref[:, :2 * GROUP_W] = proj[:, :2 * GROUP_W]
        kv_shape = (tm // SEQ, 1, SEQ, GROUP_W)
        cols = (GROUP_W, 2 * GROUP_W, 2 * w3 + GROUP_W, 2 * w3 + 2 * GROUP_W)
        for ref, c0 in zip((kak_ref, kav_ref, kck_ref, kcv_ref), cols):
            ref[:, 0:1] = proj[:, c0:c0 + GROUP_W].reshape(kv_shape)
            if ref.shape[1] > 1:
                ref[:, 1:] = jnp.zeros((tm // SEQ, ref.shape[1] - 1, SEQ, GROUP_W), f32)

    @pl.when(i >= N_PROMPT_TOK // tm)
    def _():
        cos = cos_ref[...]
        sin = sin_ref[...]
        lane = lax.broadcasted_iota(jnp.int32, (tm, GROUP_W), 1)
        first = (lane % 16) < 8
        for s in range(2):
            v = proj[:, s * GROUP_W:(s + 1) * GROUP_W]
            partner = jnp.where(first, pltpu.roll(v, GROUP_W - 8, 1), pltpu.roll(v, 8, 1))
            pa_ref[:, s * GROUP_W:(s + 1) * GROUP_W] = v * cos + partner * sin


def _inproj(x, layer, kv_prev, mod_l, g, w_in, cos_t, sin_t, tm=512):
    npb = N_PROMPT_TOK // tm
    blocks_per_seq = DEC_SEQ // tm
    w3 = 3 * GROUP_W
    split_x = isinstance(x, tuple)

    def rope_idx(i):
        return (jnp.maximum(i - npb, 0) % blocks_per_seq, 0)

    if split_x:
        x_specs = [pl.BlockSpec((tm, D_MODEL), lambda i: (jnp.minimum(i, npb - 1), 0)),
                   pl.BlockSpec((tm, D_MODEL), lambda i: (jnp.maximum(i - npb, 0), 0))]
        x_args = list(x)
    else:
        x_specs = [pl.BlockSpec((tm, D_MODEL), lambda i: (i, 0))]
        x_args = [x]
    kv_args = list(kv_prev) if kv_prev is not None else []
    n_in = len(x_args) + 5
    kv_shape = jax.ShapeDtypeStruct((BATCH, DEPTH, SEQ, GROUP_W), f32)
    kv_layers = DEPTH if layer == 0 else 1
    kv_spec = pl.BlockSpec((tm // SEQ, kv_layers, SEQ, GROUP_W), lambda i: (jnp.minimum(i, npb - 1), layer, 0, 0))
    tok = lambda w: pl.BlockSpec((tm, w), lambda i: (i, 0))
    return pl.pallas_call(
        functools.partial(_inproj_kernel, tm=tm, split_x=split_x, n_alias=len(kv_args)),
        out_shape=(jax.ShapeDtypeStruct((N_TOK, w3), f32), jax.ShapeDtypeStruct((N_TOK, w3), f32),
                   jax.ShapeDtypeStruct((N_TOK, w3), f32), jax.ShapeDtypeStruct((N_TOK, GROUP_W), f32),
                   kv_shape, kv_shape, kv_shape, kv_shape),
        grid=(N_TOK // tm,),
        in_specs=x_specs + [
            pl.BlockSpec((MOD_ROWS, 6 * D_MODEL), lambda i: (0, 0)),
            pl.BlockSpec((1, D_MODEL), lambda i: (0, 0)),
            pl.BlockSpec((1, D_MODEL, PROJ_W), lambda i: (layer, 0, 0)),
            pl.BlockSpec((tm, GROUP_W), rope_idx),
            pl.BlockSpec((tm, GROUP_W), rope_idx),
        ] + [pl.BlockSpec(memory_space=pl.ANY)] * len(kv_args),
        out_specs=(tok(w3), tok(w3), tok(w3), tok(GROUP_W), kv_spec, kv_spec, kv_spec, kv_spec),
        scratch_shapes=[pltpu.VMEM((D_MODEL, PROJ_W), bf16)],
        input_output_aliases={n_in + j: 4 + j for j in range(len(kv_args))},
        compiler_params=_cparams(1),
        name="norm1_inproj",
    )(*x_args, mod_l, g.reshape(1, D_MODEL), w_in, cos_t, sin_t, *kv_args)


def _values_with_ones(v, vhead, vlane, h, hw):
    ones_col = ((h + 1) * hw) % GROUP_W
    vm = jnp.where(vhead == h, v, jnp.where(vlane == ones_col, 1.0, 0.0))
    return vm.astype(bf16), ones_col


def _attn_kernel(*refs, n_heads, diff, scale, out_scale, cached):
    if cached:
        lam_ref, q_ref, k_ref, v_ref, kx_ref, vx_ref, g_ref, o_ref = refs
        k = jnp.concatenate([k_ref[...], kx_ref[...]], axis=0).astype(bf16)
        v = jnp.concatenate([v_ref[...], vx_ref[...]], axis=0)
    else:
        lam_ref, q_ref, k_ref, v_ref, g_ref, o_ref = refs
        k = k_ref[...].astype(bf16)
        v = v_ref[...]
    tq = q_ref.shape[0]
    n = k.shape[0]
    q = q_ref[...] * (scale * LOG2E)
    hw = GROUP_W // n_heads
    n_maps = 2 if diff else 1
    qgrp = _lane_group((tq, GROUP_W), hw // n_maps)
    qhead = _lane_group((tq, GROUP_W), hw)
    vhead = _lane_group((n, GROUP_W), hw)
    vlane = lax.broadcasted_iota(jnp.int32, (n, GROUP_W), 1)
    acc = jnp.zeros((tq, GROUP_W), f32)
    for h in range(n_heads):
        vm, ones_col = _values_with_ones(v, vhead, vlane, h, hw)
        maps = []
        for m in range(n_maps):
            qm = jnp.where(qgrp == n_maps * h + m, q, 0.0).astype(bf16)
            s = lax.dot_general(qm, k, _NT, preferred_element_type=f32)
            e = jnp.exp2(s - jnp.max(s, axis=-1, keepdims=True)).astype(bf16)
            o = jnp.dot(e, vm, preferred_element_type=f32)
            maps.append(o * (1.0 / o[:, ones_col:ones_col + 1]))
        oh = maps[0] - lam_ref[0] * maps[1] if diff else maps[0]
        acc = acc + jnp.where(qhead == h, oh, 0.0)
    if diff:
        r = lax.shift_right_logical(lax.broadcasted_iota(jnp.int32, (GROUP_W, GROUP_W), 0), 6)
        c = lax.shift_right_logical(lax.broadcasted_iota(jnp.int32, (GROUP_W, GROUP_W), 1), 6)
        bd = jnp.where(r == c, 1.0, 0.0).astype(bf16)
        sq = acc * acc
        hi = sq.astype(bf16)
        lo = (sq - hi.astype(f32)).astype(bf16)
        ms = (jnp.dot(hi, bd, preferred_element_type=f32) + jnp.dot(lo, bd, preferred_element_type=f32)) * (1.0 / A_DV)
        acc = (acc * lax.rsqrt(ms + EPS) * g_ref[...]) * out_scale
    o_ref[...] = acc.astype(o_ref.dtype)


def _attention(q_src, q_col, q_row0, k_src, k_col, k_row0, v_src, v_col, v_row0, *,
               n_seq, seq_len, n_keys, tq, n_heads, diff, scale, lam=None, gain=None, out_scale=1.0,
               cache_kv=None):
    qb = seq_len // tq
    if lam is None:
        lam = jnp.zeros((1,), f32)
    if gain is None:
        gain = jnp.ones((1, GROUP_W), f32)
    in_specs = [
        pl.BlockSpec(memory_space=pltpu.SMEM),
        pl.BlockSpec((tq, GROUP_W), lambda b, i: ((q_row0 + b) * qb + i, q_col)),
        pl.BlockSpec((n_keys, GROUP_W), lambda b, i: (k_row0 + b, k_col)),
        pl.BlockSpec((n_keys, GROUP_W), lambda b, i: (v_row0 + b, v_col)),
    ]
    args = [lam, q_src, k_src, v_src]
    if cache_kv is not None:
        n_cached = cache_kv[0].shape[0] // n_seq
        in_specs += [pl.BlockSpec((n_cached, GROUP_W), lambda b, i: (b, 0))] * 2
        args += list(cache_kv)
    in_specs.append(pl.BlockSpec((1, GROUP_W), lambda b, i: (0, 0)))
    args.append(gain)
    return pl.pallas_call(
        functools.partial(_attn_kernel, n_heads=n_heads, diff=diff, scale=scale, out_scale=out_scale,
                          cached=cache_kv is not None),
        out_shape=jax.ShapeDtypeStruct((n_seq * seq_len, GROUP_W), bf16),
        grid=(n_seq, qb),
        in_specs=in_specs,
        out_specs=pl.BlockSpec((tq, GROUP_W), lambda b, i: (b * qb + i, 0)),
        compiler_params=_cparams(2),
        name="diff_attention" if diff else "softmax_attention",
    )(*args)


def _na_group_geometry(g):
    r0 = g * NA_QROWS
    slab0 = min(max(r0 - NA_KH // 2, 0), GRID_H - NA_SLAB_ROWS)
    return r0, slab0


def _na_bias_tables(rpb):
    c = np.arange(GRID_W)
    ws = np.clip(c - NA_KW // 2, 0, GRID_W - NA_KW)
    kc = np.arange(GRID_W)
    col_ok = (kc[None, :] >= ws[:, None]) & (kc[None, :] < ws[:, None] + NA_KW)
    pad = GRID_W - NA_KW
    rp = jnp.pad(rpb.astype(f32) * LOG2E, ((0, 0), (0, 0), (pad, pad)))
    tc = jnp.stack([rp[:, :, GRID_W - 1 - ci:2 * GRID_W - 1 - ci] for ci in range(GRID_W)], axis=2)
    tc = jnp.where(col_ok[None, None], tc, NEG)
    neg_blk = jnp.full((C_HEADS, GRID_W, GRID_W), NEG, f32)
    tables = []
    for g in (0, 1, NA_GROUPS - 1):
        r0, slab0 = _na_group_geometry(g)
        rows = []
        for rq in range(NA_QROWS):
            r = r0 + rq
            rs = min(max(r - NA_KH // 2, 0), GRID_H - NA_KH)
            blks = []
            for kl in range(NA_SLAB_ROWS):
                kr = slab0 + kl
                blks.append(tc[:, kr - r + NA_KH - 1] if rs <= kr < rs + NA_KH else neg_blk)
            rows.append(jnp.concatenate(blks, axis=-1))
        tables.append(jnp.concatenate(rows, axis=-2))
    return jnp.stack(tables, axis=0)


def _na_kernel(q_ref, k_ref, v_ref, kx_ref, vx_ref, bias_ref, o_ref):
    g = pl.program_id(1)
    tq = NA_QROWS * GRID_W
    ns = NA_SLAB_ROWS * GRID_W
    slab0 = jnp.clip(g * NA_QROWS - NA_KH // 2, 0, GRID_H - NA_SLAB_ROWS)
    start = pl.multiple_of(slab0 * GRID_W, GRID_W)
    q = q_ref[...] * (C_DH ** -0.5 * LOG2E)
    ks = k_ref[pl.ds(start, ns), :].astype(bf16)
    vs = v_ref[pl.ds(start, ns), :]
    kx = kx_ref[...].astype(bf16)
    vx = vx_ref[...]
    qhead = _lane_group((tq, GROUP_W), C_DH)
    vshead = _lane_group((ns, GROUP_W), C_DH)
    vslane = lax.broadcasted_iota(jnp.int32, (ns, GROUP_W), 1)
    vxhead = _lane_group((PAST_LEN, GROUP_W), C_DH)
    vxlane = lax.broadcasted_iota(jnp.int32, (PAST_LEN, GROUP_W), 1)
    acc = jnp.zeros((tq, GROUP_W), f32)
    for h in range(C_HEADS):
        qm = jnp.where(qhead == h, q, 0.0).astype(bf16)
        sl = lax.dot_general(qm, ks, _NT, preferred_element_type=f32)
        b = bias_ref[0, h]
        sl = jnp.where(b > 0.5 * NEG, sl + b, NEG)
        sx = lax.dot_general(qm, kx, _NT, preferred_element_type=f32)
        mx = jnp.maximum(jnp.max(sl, axis=-1, keepdims=True), jnp.max(sx, axis=-1, keepdims=True))
        el = jnp.exp2(sl - mx).astype(bf16)
        ex = jnp.exp2(sx - mx).astype(bf16)
        vsm, ones_col = _values_with_ones(vs, vshead, vslane, h, C_DH)
        vxm, _ = _values_with_ones(vx, vxhead, vxlane, h, C_DH)
        o = jnp.dot(el, vsm, preferred_element_type=f32) + jnp.dot(ex, vxm, preferred_element_type=f32)
        acc = acc + jnp.where(qhead == h, o * (1.0 / o[:, ones_col:ones_col + 1]), 0.0)
    o_ref[...] = acc.astype(o_ref.dtype)


def _na_attention(pc, kx, vx, bias):
    tq = NA_QROWS * GRID_W
    ns = NA_SLAB_ROWS * GRID_W
    q_blk0 = N_PROMPT_TOK // tq
    s_blk0 = N_PROMPT_TOK // DEC_SEQ

    def bias_idx(b, g):
        return (jnp.where(g == 0, 0, jnp.where(g == NA_GROUPS - 1, 2, 1)), 0, 0, 0)

    return pl.pallas_call(
        _na_kernel,
        out_shape=jax.ShapeDtypeStruct((N_SAMPLE_TOK, GROUP_W), bf16),
        grid=(DEC_BATCH, NA_GROUPS),
        in_specs=[
            pl.BlockSpec((tq, GROUP_W), lambda b, g: (q_blk0 + b * NA_GROUPS + g, 0)),
            pl.BlockSpec((DEC_SEQ, GROUP_W), lambda b, g: (s_blk0 + b, 1)),
            pl.BlockSpec((DEC_SEQ, GROUP_W), lambda b, g: (s_blk0 + b, 2)),
            pl.BlockSpec((PAST_LEN, GROUP_W), lambda b, g: (b, 0)),
            pl.BlockSpec((PAST_LEN, GROUP_W), lambda b, g: (b, 0)),
            pl.BlockSpec((1, C_HEADS, tq, ns), bias_idx),
        ],
        out_specs=pl.BlockSpec((tq, GROUP_W), lambda b, g: (b * NA_GROUPS + g, 0)),
        compiler_params=_cparams(2),
        name="neighbourhood_attention",
    )(pc, pc, pc, kx, vx, bias)


def _dft_matrices(L):
    n = 2 * L
    k = np.arange(L)[:, None]
    s = np.arange(L)[None, :]
    ang = 2.0 * np.pi * ((k * s) % n) / n
    cos, sin = np.cos(ang), np.sin(ang)
    sin[0, :] = (-1.0) ** np.arange(L)
    fwd = np.concatenate([cos, sin], axis=0)
    wk = np.where(np.arange(L) == 0, 1.0, 2.0)[None, :]
    inv = np.concatenate([cos.T * wk, sin.T * wk], axis=1) / n
    inv[:, L] = ((-1.0) ** np.arange(L)) / n
    return fwd, inv


def _hy_filter_kernel(w1_ref, b1_ref, w2_ref, b2_ref, w3_ref, b3_ref, fr_ref, dec_ref, fa_ref, fb_ref,
                      p_ref, q_ref, r_ref, g_ref, nrm_ref, *, L, tk):
    s = pl.program_id(0)

    @pl.when(s == 0)
    def _():
        row = lax.broadcasted_iota(jnp.int32, (L, LANES), 0)
        lane = lax.broadcasted_iota(jnp.int32, (L, LANES), 1)
        t = row.astype(f32) / L
        band = jnp.where(lane <= HY_BANDS, lane, lane - HY_BANDS).astype(f32)
        ang = (2.0 * math.pi * band) * t
        feat = jnp.where(lane == 0, t, jnp.where(lane <= HY_BANDS, jnp.sin(ang),
                                                 jnp.where(lane <= 2 * HY_BANDS, jnp.cos(ang), 0.0)))
        z = jnp.sin(fr_ref[0:1, :] * (_dot_f32(feat, w1_ref[...]) + b1_ref[...]))
        z = jnp.sin(fr_ref[1:2, :] * (_dot_f32(z, w2_ref[...]) + b2_ref[...]))
        z = _dot_f32(z, w3_ref[...]) + b3_ref[...]
        wide = (L, HY_ORDER * 2 * HY_CH)
        tw = lax.broadcasted_iota(jnp.int32, wide, 0).astype(f32) / L
        taps = z * jnp.exp(-tw * jnp.abs(dec_ref[...]))
        bwd = (_lane_group(wide, HY_CH) % 2) == 1
        first = lax.broadcasted_iota(jnp.int32, wide, 0) == 0
        taps = jnp.where(first, jnp.where(bwd, 0.0, taps), taps)
        g_ref[...] = taps.astype(bf16)
        nrm_ref[...] = jnp.sum(jnp.abs(taps), axis=0, keepdims=True)

    ga = jnp.dot(fa_ref[...], g_ref[...], preferred_element_type=f32)
    gb = jnp.dot(fb_ref[...], g_ref[...], preferred_element_type=f32)
    top = (lax.broadcasted_iota(jnp.int32, (tk, HY_CH), 0) + s * tk) == 0
    for o in range(HY_ORDER):
        c0 = o * 2 * HY_CH
        inv = 1.0 / (nrm_ref[:, c0:c0 + HY_CH] + nrm_ref[:, c0 + HY_CH:c0 + 2 * HY_CH])
        hc = (ga[:, c0:c0 + HY_CH] + ga[:, c0 + HY_CH:c0 + 2 * HY_CH]) * inv
        bf_, bb_ = gb[:, c0:c0 + HY_CH], gb[:, c0 + HY_CH:c0 + 2 * HY_CH]
        hs = jnp.where(top, bf_ + bb_, bf_ - bb_) * inv
        oc = slice(o * HY_CH, (o + 1) * HY_CH)
        p_ref[:, oc] = hc
        q_ref[:, oc] = jnp.where(top, 0.0, hs)
        r_ref[:, oc] = jnp.where(top, hs, hc)


def _hy_filter_spectra(L, fwd_bf, w1, b1, w2, b2, w3, b3, freq, decay, tk=256):
    nk = L // tk
    wide = HY_ORDER * 2 * HY_CH
    full = lambda a: pl.BlockSpec(a.shape, lambda s: (0,) * a.ndim)
    w1p = jnp.pad(w1, ((0, LANES - HY_FEAT), (0, 0)))
    args = [w1p, b1.reshape(1, HY_HID), w2, b2.reshape(1, HY_HID), w3, b3.reshape(1, wide), freq,
            decay.reshape(1, wide)]
    out = jax.ShapeDtypeStruct((L, HY_ORDER * HY_CH), f32)
    plane = pl.BlockSpec((tk, HY_ORDER * HY_CH), lambda s: (s, 0))
    return pl.pallas_call(
        functools.partial(_hy_filter_kernel, L=L, tk=tk),
        out_shape=(out, out, out),
        grid=(nk,),
        in_specs=[full(a) for a in args] + [pl.BlockSpec((tk, L), lambda s: (s, 0)),
                                            pl.BlockSpec((tk, L), lambda s: (s + nk, 0))],
        out_specs=(plane, plane, plane),
        scratch_shapes=[pltpu.VMEM((L, wide), bf16), pltpu.VMEM((1, wide), f32)],
        compiler_params=_cparams(1),
        name="hyena_filter_spectra",
    )(*args, fwd_bf, fwd_bf)


def _hy_pre_kernel(u_ref, w_ref, x1_ref, x2_ref, v_ref, vbf_ref):
    u = u_ref[...]
    w = w_ref[...]
    y = _shift_rows(u, 1) * w[0:1, :] + u * w[1:2, :] + _shift_rows(u, -1) * w[2:3, :]
    x1_ref[...] = y[:, :HY_CH]
    x2_ref[...] = y[:, HY_CH:2 * HY_CH]
    v = y[:, 2 * HY_CH:]
    v_ref[...] = v
    vbf_ref[...] = v.astype(bf16)


def _hy_pre(pb, conv_w, seq0, n_seq, L):
    n = n_seq * L
    blk = pl.BlockSpec((L, HY_CH), lambda b: (b, 0))
    o32 = jax.ShapeDtypeStruct((n, HY_CH), f32)
    return pl.pallas_call(
        _hy_pre_kernel,
        out_shape=(o32, o32, o32, jax.ShapeDtypeStruct((n, HY_CH), bf16)),
        grid=(n_seq,),
        in_specs=[pl.BlockSpec((L, 3 * HY_CH), lambda b: (seq0 + b, 0)),
                  pl.BlockSpec((3, 3 * HY_CH), lambda b: (0, 0))],
        out_specs=(blk, blk, blk, blk),
        compiler_params=_cparams(1),
        name="hyena_short_conv",
    )(pb, conv_w)


def _hy_conv_kernel(*refs, n_seq, L, nk, te, n_out):
    (zbf_ref, z_ref, m_ref, fa_ref, fb_ref, ic_ref, is_ref, p_ref, q_ref, r_ref, db_ref) = refs[:11]
    out_refs = refs[11:11 + n_out]
    acc_ref = refs[11 + n_out]
    s = pl.program_id(0)

    @pl.when(s == 0)
    def _():
        acc_ref[...] = jnp.zeros_like(acc_ref)

    @pl.when(s < nk)
    def _():
        fa, fb, ic, isn = fa_ref[...], fb_ref[...], ic_ref[...], is_ref[...]
        p, q, r = p_ref[...], q_ref[...], r_ref[...]
        for b in range(n_seq):
            rows = slice(b * L, (b + 1) * L)
            zb = zbf_ref[rows, :]
            a = jnp.dot(fa, zb, preferred_element_type=f32)
            bb = jnp.dot(fb, zb, preferred_element_type=f32)
            yc = (a * p - bb * q).astype(bf16)
            ys = (a * q + bb * r).astype(bf16)
            acc_ref[rows, :] += (jnp.dot(ic, yc, preferred_element_type=f32)
                                 + jnp.dot(isn, ys, preferred_element_type=f32))

    @pl.when(s >= nk)
    def _():
        start = pl.multiple_of((s - nk) * te, te)
        y = acc_ref[pl.ds(start, te), :]
        res = m_ref[...] * (y + db_ref[...] * z_ref[...])
        for o_ref in out_refs:
            o_ref[...] = res.astype(o_ref.dtype)


def _hy_longconv(zbf, z, mult, fwd_bf, inv_bf, planes, order, dbias_o, out_dtypes, n_seq, L, tk):
    nk = L // tk
    n = n_seq * L
    te = max(L, HY_EPILOGUE_ROWS)
    assert n % te == 0
    kt = lambda s: jnp.minimum(s, nk - 1)
    ep = lambda s: (jnp.maximum(s - nk, 0), 0)
    plane = pl.BlockSpec((tk, HY_CH), lambda s: (kt(s), order))
    return pl.pallas_call(
        functools.partial(_hy_conv_kernel, n_seq=n_seq, L=L, nk=nk, te=te, n_out=len(out_dtypes)),
        out_shape=tuple(jax.ShapeDtypeStruct((n, HY_CH), dt) for dt in out_dtypes),
        grid=(nk + n // te,),
        in_specs=[
            pl.BlockSpec((n, HY_CH), lambda s: (0, 0)),
            pl.BlockSpec((te, HY_CH), ep),
            pl.BlockSpec((te, HY_CH), ep),
            pl.BlockSpec((tk, L), lambda s: (kt(s), 0)),
            pl.BlockSpec((tk, L), lambda s: (kt(s) + nk, 0)),
            pl.BlockSpec((L, tk), lambda s: (0, kt(s))),
            pl.BlockSpec((L, tk), lambda s: (0, kt(s) + nk)),
            plane, plane, plane,
            pl.BlockSpec((1, HY_CH), lambda s: (0, 0)),
        ],
        out_specs=tuple(pl.BlockSpec((te, HY_CH), ep) for _ in out_dtypes),
        scratch_shapes=[pltpu.VMEM((n, HY_CH), f32)],
        compiler_params=_cparams(1),
        name="hyena_longconv",
    )(zbf, z, mult, fwd_bf, fwd_bf, inv_bf, inv_bf, *planes, dbias_o.reshape(1, HY_CH))


def _hyena(pb, seq0, n_seq, L, dft, hy):
    conv_w, w1, b1, w2, b2, w3, b3, freq, decay, dbias = hy
    fwd_bf, inv_bf = dft
    tk = min(L, HY_SPECTRUM_TILE)
    planes = _hy_filter_spectra(L, fwd_bf, w1, b1, w2, b2, w3, b3, freq, decay, tk=tk)
    x1, x2, v, vbf = _hy_pre(pb, conv_w, seq0, n_seq, L)
    z, zbf = _hy_longconv(vbf, v, x1, fwd_bf, inv_bf, planes, 0, dbias[0], (f32, bf16), n_seq, L, tk)
    (ob,) = _hy_longconv(zbf, z, x2, fwd_bf, inv_bf, planes, 1, dbias[1], (bf16,), n_seq, L, tk)
    return ob


def _pool_kernel(u_ref, w_ref, sc_ref, o_ref):
    u = u_ref[...]
    L = u.shape[0]
    back = _shift_rows(u, 1)
    fwd = u
    sums = [back + fwd]
    for k in (1, 2, 4):
        back = back + _shift_rows(back, k)
        fwd = fwd + _shift_rows(fwd, -k)
        sums.append(back + fwd)
    a2, a4, a8, a16 = sums
    grp = _lane_group(u.shape, POOL_GC)
    t = lax.broadcasted_iota(jnp.int32, u.shape, 0)
    half = jnp.left_shift(1, grp)
    cnt = jnp.minimum(t + half, L) - jnp.maximum(t - half, 0)
    tot = jnp.where(grp == 0, a2, jnp.where(grp == 1, a4, jnp.where(grp == 2, a8, a16)))
    pooled = tot / cnt.astype(f32) - u
    y = jnp.dot(pooled.astype(bf16), w_ref[...].astype(bf16), preferred_element_type=f32)
    o_ref[...] = (y * sc_ref[...]).astype(o_ref.dtype)


def _pool(pd, w_bd, scale, seq0, n_seq, L):
    return pl.pallas_call(
        _pool_kernel,
        out_shape=jax.ShapeDtypeStruct((n_seq * L, GROUP_W), bf16),
        grid=(n_seq,),
        in_specs=[pl.BlockSpec((L, GROUP_W), lambda b: (seq0 + b, 0)),
                  pl.BlockSpec((GROUP_W, GROUP_W), lambda b: (0, 0)),
                  pl.BlockSpec((1, GROUP_W), lambda b: (0, 0))],
        out_specs=pl.BlockSpec((L, GROUP_W), lambda b: (b, 0)),
        compiler_params=_cparams(1),
        name="pool_mixer",
    )(pd, w_bd, scale.reshape(1, GROUP_W))


def _outproj_kernel(*refs, tm, moe, split_x):
    mix_refs = refs[:8]
    n_x = 2 if split_x else 1
    x_refs = refs[8:8 + n_x]
    if moe:
        mod_ref, g_ref, w_ref, r_ref, x1_ref, h2_ref, lg_ref, wbf_ref = refs[8 + n_x:]
    else:
        mod_ref, g_ref, w_ref, x1_ref, h2_ref, wbf_ref = refs[8 + n_x:]
    i = pl.program_id(0)

    @pl.when(i == 0)
    def _():
        wbf_ref[...] = w_ref[0].astype(bf16)

    row = _mod_row(i, tm)
    d = D_MODEL
    gate1 = mod_ref[pl.ds(row, 1), 2 * d:3 * d]
    shift2 = mod_ref[pl.ds(row, 1), 3 * d:4 * d]
    scale2 = mod_ref[pl.ds(row, 1), 4 * d:5 * d]
    is_prompt = i < N_PROMPT_TOK // tm
    mixed = jnp.concatenate(
        [jnp.where(is_prompt, mix_refs[2 * j][...], mix_refs[2 * j + 1][...]) for j in range(4)], axis=-1)
    mix = jnp.dot(mixed, wbf_ref[...], preferred_element_type=f32)
    x = jnp.where(is_prompt, x_refs[0][...], x_refs[1][...]) if split_x else x_refs[0][...]
    x1 = x + gate1 * mix
    x1_ref[...] = x1
    h = _modulated_norm(x1, g_ref[...], shift2, scale2)
    h2_ref[...] = h.astype(h2_ref.dtype)
    if moe:
        lg_ref[...] = _dot_bf16x3(h, r_ref[...])


def _outproj(mixers, x, layer, mod_l, g, w_out, router_l=None, tm=512):
    moe = router_l is not None
    npb = N_PROMPT_TOK // tm
    tok = lambda w: pl.BlockSpec((tm, w), lambda i: (i, 0))
    in_specs, args = [], []
    for op, os_ in mixers:
        in_specs.append(pl.BlockSpec((tm, GROUP_W), lambda i: (jnp.minimum(i, npb - 1), 0)))
        in_specs.append(pl.BlockSpec((tm, GROUP_W), lambda i: (jnp.maximum(i - npb, 0), 0)))
        args += [op, os_]
    split_x = isinstance(x, tuple)
    if split_x:
        in_specs += [pl.BlockSpec((tm, D_MODEL), lambda i: (jnp.minimum(i, npb - 1), 0)),
                     pl.BlockSpec((tm, D_MODEL), lambda i: (jnp.maximum(i - npb, 0), 0))]
        args += list(x)
    else:
        in_specs.append(tok(D_MODEL))
        args.append(x)
    in_specs += [pl.BlockSpec((MOD_ROWS, 6 * D_MODEL), lambda i: (0, 0)),
                 pl.BlockSpec((1, D_MODEL), lambda i: (0, 0)),
                 pl.BlockSpec((1, D_MODEL, D_MODEL), lambda i: (layer, 0, 0))]
    args += [mod_l, g.reshape(1, D_MODEL), w_out]
    out_shape = [jax.ShapeDtypeStruct((N_TOK, D_MODEL), f32),
                 jax.ShapeDtypeStruct((N_TOK, D_MODEL), f32 if moe else bf16)]
    out_specs = [tok(D_MODEL), tok(D_MODEL)]
    if moe:
        in_specs.append(pl.BlockSpec((D_MODEL, ROUTER_PAD), lambda i: (0, 0)))
        args.append(jnp.pad(router_l, ((0, 0), (0, ROUTER_PAD - N_EXPERTS))))
        out_shape.append(jax.ShapeDtypeStruct((N_TOK, ROUTER_PAD), f32))
        out_specs.append(tok(ROUTER_PAD))
    return pl.pallas_call(
        functools.partial(_outproj_kernel, tm=tm, moe=moe, split_x=split_x),
        out_shape=tuple(out_shape),
        grid=(N_TOK // tm,),
        in_specs=in_specs,
        out_specs=tuple(out_specs),
        scratch_shapes=[pltpu.VMEM((D_MODEL, D_MODEL), bf16)],
        compiler_params=_cparams(1),
        name="outproj_norm2",
    )(*args)


def _ffn_kernel(be_ref, nv_ref, idx_ref, idx_next_ref, src_ref, w1_ref, w3_ref, w2_ref, o_ref,
                xbf_ref, xbuf_ref, sem, *, tm, nf):
    i = pl.program_id(0)
    j = pl.program_id(1)
    live = i < nv_ref[0]
    slot = i % 2

    @pl.when(j == 0)
    def _():
        o_ref[...] = jnp.zeros_like(o_ref)

    @pl.when((j == 0) & (i == 0))
    def _():
        _start_rows(idx_ref, src_ref, xbuf_ref, 0, sem, tm, 1)

    @pl.when((j == 0) & live)
    def _():
        _wait_rows(src_ref, xbuf_ref, slot, sem, tm, 1)

    @pl.when((j == 0) & live)
    def _():
        xbf_ref[...] = xbuf_ref[slot].astype(bf16)

    @pl.when(i + 1 < nv_ref[0])
    def _():
        share = tm // nf
        count = jnp.where(j == nf - 1, tm - share * (nf - 1), share)
        _start_rows(idx_next_ref, src_ref, xbuf_ref, 1 - slot, sem, count, 1, first=j * share)

    @pl.when(live)
    def _():
        x = xbf_ref[...]
        h1 = jnp.dot(x, w1_ref[0].astype(bf16), preferred_element_type=f32)
        h3 = jnp.dot(x, w3_ref[0].astype(bf16), preferred_element_type=f32)
        a = (h1 * jax.nn.sigmoid(h1)) * h3
        o_ref[...] += jnp.dot(a.astype(bf16), w2_ref[0].astype(bf16), preferred_element_type=f32)


def _ffn_routed(row_src, row_token, w1, w3, w2, blk_expert, n_valid, tm, tf):
    n_rows = row_token.shape[0]
    ffn = w1.shape[-1]
    nf = ffn // tf
    nblk = n_rows // tm
    assert nf * tf == ffn and nblk * tm == n_rows

    def wcol(i, j, be, nv):
        return (be[i], 0, jnp.where(i < nv[0], j, nf - 1))

    def wrow(i, j, be, nv):
        return (be[i], jnp.where(i < nv[0], j, nf - 1), 0)

    idx3 = row_token.reshape(nblk, 1, tm)
    return pl.pallas_call(
        functools.partial(_ffn_kernel, tm=tm, nf=nf),
        out_shape=jax.ShapeDtypeStruct((n_rows, D_MODEL), f32),
        grid_spec=pltpu.PrefetchScalarGridSpec(
            num_scalar_prefetch=2,
            grid=(nblk, nf),
            in_specs=[
                pl.BlockSpec((1, 1, tm), lambda i, j, be, nv: (i, 0, 0), memory_space=pltpu.SMEM),
                pl.BlockSpec((1, 1, tm), lambda i, j, be, nv: (jnp.minimum(i + 1, nblk - 1), 0, 0),
                             memory_space=pltpu.SMEM),
                pl.BlockSpec(memory_space=pl.ANY),
                pl.BlockSpec((1, D_MODEL, tf), wcol),
                pl.BlockSpec((1, D_MODEL, tf), wcol),
                pl.BlockSpec((1, tf, D_MODEL), wrow),
            ],
            out_specs=pl.BlockSpec((tm, D_MODEL), lambda i, j, be, nv: (i, 0)),
            scratch_shapes=[pltpu.VMEM((tm, D_MODEL), bf16), pltpu.VMEM((2, tm, D_MODEL), row_src.dtype),
                            pltpu.SemaphoreType.DMA((2,))],
        ),
        compiler_params=_cparams(2),
        name="swiglu_routed",
    )(blk_expert, n_valid, idx3, idx3, row_src, w1, w3, w2)


def _cast_kernel(a_ref, o_ref):
    o_ref[...] = a_ref[...].astype(o_ref.dtype)


def _to_bf16(a, block_rows):
    rows, cols = a.shape
    assert rows % block_rows == 0
    return pl.pallas_call(
        _cast_kernel,
        out_shape=jax.ShapeDtypeStruct(a.shape, bf16),
        grid=(rows // block_rows,),
        in_specs=[pl.BlockSpec((block_rows, cols), lambda i: (i, 0))],
        out_specs=pl.BlockSpec((block_rows, cols), lambda i: (i, 0)),
        compiler_params=_cparams(1),
        name="cast_bf16",
    )(a)


def _ffn_dense_kernel(x_ref, w1_ref, w3_ref, w2_ref, x1_ref, mod_ref, o_ref, *, tm, chunks):
    i = pl.program_id(0)
    x = x_ref[...]
    acc = jnp.zeros((tm, D_MODEL), f32)
    c0 = 0
    for width in chunks:
        h1 = jnp.dot(x, w1_ref[:, c0:c0 + width], preferred_element_type=f32)
        h3 = jnp.dot(x, w3_ref[:, c0:c0 + width], preferred_element_type=f32)
        a = (h1 * jax.nn.sigmoid(h1)) * h3
        acc = acc + jnp.dot(a.astype(bf16), w2_ref[c0:c0 + width, :], preferred_element_type=f32)
        c0 += width
    gate2 = mod_ref[pl.ds(_mod_row(i, tm), 1), 5 * D_MODEL:6 * D_MODEL]
    o_ref[...] = x1_ref[...] + gate2 * acc


def _ffn_dense(h2, w1, w3, w2, x1, mod_l, tm=512, chunk=512):
    ffn = w1.shape[1]
    chunks = [chunk] * (ffn // chunk) + ([ffn % chunk] if ffn % chunk else [])
    w1b, w3b, w2b = _to_bf16(w1, 256), _to_bf16(w3, 256), _to_bf16(w2, 256)
    tok = lambda dt: pl.BlockSpec((tm, D_MODEL), lambda i: (i, 0))
    full = lambda a: pl.BlockSpec(a.shape, lambda i: (0, 0))
    return pl.pallas_call(
        functools.partial(_ffn_dense_kernel, tm=tm, chunks=tuple(chunks)),
        out_shape=jax.ShapeDtypeStruct((N_TOK, D_MODEL), f32),
        grid=(N_TOK // tm,),
        in_specs=[tok(bf16), full(w1b), full(w3b), full(w2b), tok(f32),
                  pl.BlockSpec((MOD_ROWS, 6 * D_MODEL), lambda i: (0, 0))],
        out_specs=tok(f32),
        compiler_params=_cparams(1),
        name="swiglu_dense",
    )(h2, w1b, w3b, w2b, x1, mod_l)


def _row_copy(src_ref, src_row, buf_ref, slot, rr, k, sem):
    w = src_ref.shape[1]
    return pltpu.make_async_copy(src_ref.at[pl.ds(src_row, 1), :],
                                 buf_ref.at[slot, pl.ds(rr, 1), pl.ds(k * w, w)], sem.at[slot])


def _start_rows(idx_ref, src_ref, buf_ref, slot, sem, n, pack, first=0):
    def body(rr, carry):
        for k in range(pack):
            _row_copy(src_ref, idx_ref[0, 0, rr * pack + k], buf_ref, slot, rr, k, sem).start(
                priority=ROW_DMA_PRIORITY)
        return carry

    groups = n if pack == 1 else n // pack
    if isinstance(first, int) and isinstance(groups, int):
        lax.fori_loop(first, first + groups, body, 0, unroll=8)
    else:
        lax.fori_loop(first, first + groups, body, 0)


def _wait_rows(src_ref, buf_ref, slot, sem, n, pack):
    def body(rr, carry):
        for k in range(pack):
            _row_copy(src_ref, 0, buf_ref, slot, rr, k, sem).wait()
        return carry

    lax.fori_loop(0, n // pack, body, 0, unroll=8)


def _combine_kernel(*refs, tm, final):
    if final:
        idx_ref, idx_next_ref, ys_ref, x1_ref, mod_ref, gt_ref, g_ref, op_ref, os_ref, ybuf_ref, sem = refs
    else:
        idx_ref, idx_next_ref, ys_ref, x1_ref, mod_ref, gt_ref, o_ref, ybuf_ref, sem = refs
    i = pl.program_id(0)
    n_steps = pl.num_programs(0)
    slot = i % 2
    n_rows = tm * TOP_K

    @pl.when(i == 0)
    def _():
        _start_rows(idx_ref, ys_ref, ybuf_ref, 0, sem, n_rows, TOP_K)

    _wait_rows(ys_ref, ybuf_ref, slot, sem, n_rows, TOP_K)

    @pl.when(i + 1 < n_steps)
    def _():
        _start_rows(idx_next_ref, ys_ref, ybuf_ref, 1 - slot, sem, n_rows, TOP_K)

    gate2 = mod_ref[pl.ds(_mod_row(i, tm), 1), 5 * D_MODEL:6 * D_MODEL]
    gt = gt_ref[...]
    f = gt[:, 0:1] * ybuf_ref[slot, :, :D_MODEL] + gt[:, 1:2] * ybuf_ref[slot, :, D_MODEL:]
    x2 = x1_ref[...] + gate2 * f
    if final:
        ms = jnp.mean(x2 * x2, axis=-1, keepdims=True)
        y = x2 * lax.rsqrt(ms + EPS) * g_ref[...]

        @pl.when(i < N_PROMPT_TOK // tm)
        def _():
            op_ref[...] = y

        @pl.when(i >= N_PROMPT_TOK // tm)
        def _():
            os_ref[...] = y
    else:
        o_ref[...] = x2


def _combine(x1, mod_l, ys, slots, gates, final_g=None, tm=512):
    final = final_g is not None
    npb = N_PROMPT_TOK // tm
    nblk = N_TOK // tm
    tok = pl.BlockSpec((tm, D_MODEL), lambda i: (i, 0))
    idx3 = slots.reshape(nblk, 1, tm * TOP_K)
    in_specs = [pl.BlockSpec((1, 1, tm * TOP_K), lambda i: (i, 0, 0), memory_space=pltpu.SMEM),
                pl.BlockSpec((1, 1, tm * TOP_K), lambda i: (jnp.minimum(i + 1, nblk - 1), 0, 0),
                             memory_space=pltpu.SMEM),
                pl.BlockSpec(memory_space=pl.ANY),
                tok, pl.BlockSpec((MOD_ROWS, 6 * D_MODEL), lambda i: (0, 0)),
                pl.BlockSpec((tm, TOP_K), lambda i: (i, 0))]
    args = [idx3, idx3, ys, x1, mod_l, gates]
    if final:
        in_specs.append(pl.BlockSpec((1, D_MODEL), lambda i: (0, 0)))
        args.append(final_g.reshape(1, D_MODEL))
        out_shape = (jax.ShapeDtypeStruct((N_PROMPT_TOK, D_MODEL), f32),
                     jax.ShapeDtypeStruct((N_SAMPLE_TOK, D_MODEL), f32))
        out_specs = (pl.BlockSpec((tm, D_MODEL), lambda i: (jnp.minimum(i, npb - 1), 0)),
                     pl.BlockSpec((tm, D_MODEL), lambda i: (jnp.maximum(i - npb, 0), 0)))
    else:
        out_shape = jax.ShapeDtypeStruct((N_TOK, D_MODEL), f32)
        out_specs = tok
    return pl.pallas_call(
        functools.partial(_combine_kernel, tm=tm, final=final),
        out_shape=out_shape,
        grid=(N_TOK // tm,),
        in_specs=in_specs,
        out_specs=out_specs,
        scratch_shapes=[pltpu.VMEM((2, tm, TOP_K * D_MODEL), ys.dtype), pltpu.SemaphoreType.DMA((2,))],
        compiler_params=_cparams(1),
        name="expert_combine",
    )(*args)


def _route(logits, tm):
    eid = jnp.arange(N_EXPERTS, dtype=jnp.int32)[None, :]
    v0 = jnp.max(logits, axis=-1, keepdims=True)
    i0 = jnp.min(jnp.where(logits == v0, eid, N_EXPERTS), axis=-1, keepdims=True)
    rest = jnp.where(eid == i0, -jnp.inf, logits)
    v1 = jnp.max(rest, axis=-1, keepdims=True)
    i1 = jnp.min(jnp.where(rest == v1, eid, N_EXPERTS), axis=-1, keepdims=True)
    gates = jax.nn.softmax(jnp.concatenate([v0, v1], axis=-1), axis=-1)
    flat_e = jnp.concatenate([i0, i1], axis=-1).reshape(-1)
    onehot = (flat_e[:, None] == jnp.arange(N_EXPERTS)[None, :]).astype(jnp.int32)
    csum = jnp.cumsum(onehot, axis=0)
    rank = jnp.take_along_axis(csum, flat_e[:, None], axis=1)[:, 0] - 1
    counts = csum[-1]
    padded = ((counts + tm - 1) // tm) * tm
    pend = jnp.cumsum(padded)
    pstart = pend - padded
    dest = pstart[flat_e] + rank
    n_rows = N_TOK * TOP_K + N_EXPERTS * tm
    tok_of = jnp.arange(N_TOK * TOP_K, dtype=jnp.int32) // TOP_K
    row_token = jnp.zeros((n_rows,), jnp.int32).at[dest].set(tok_of, unique_indices=True)
    blk_start = jnp.arange(n_rows // tm, dtype=jnp.int32) * tm
    blk_expert = jnp.minimum(jnp.sum((blk_start[:, None] >= pend[None, :]).astype(jnp.int32), axis=1),
                             N_EXPERTS - 1)
    n_valid = (pend[-1] // tm).astype(jnp.int32).reshape(1)
    blk_expert = jnp.where(blk_start < pend[-1], blk_expert, blk_expert[jnp.maximum(n_valid[0] - 1, 0)])
    return row_token, blk_expert, n_valid, dest.reshape(N_TOK, TOP_K), gates


def _pool_weight(pool_w_l):
    w = jnp.zeros((GROUP_W, GROUP_W), f32)
    for g in range(len(POOL_WINDOWS)):
        w = w.at[g * POOL_GC:(g + 1) * POOL_GC, g * POOL_GC:(g + 1) * POOL_GC].set(pool_w_l[g])
    return w


def kernel(x_prompt, x_sample, cache_diff_k, cache_diff_v, cache_na_k, cache_na_v, c, c_ctx,
           norm1_g, norm2_g, final_g, ada_w, ada_b, w_in, w_out, diff_lam, diff_subln_g,
           hy_conv, hy_w1, hy_b1, hy_w2, hy_b2, hy_w3, hy_b3, hy_freq, hy_decay, hy_dbias,
           na_rpb, pool_w, pool_scale, ffn_w1, ffn_w3, ffn_w2,
           moe_router, moe_w1, moe_w3, moe_w2):
    cond = jnp.concatenate([c, c_ctx[None, :], jnp.zeros((MOD_ROWS - DEC_BATCH - 1, D_MODEL), f32)], axis=0)
    mod = _ada_mod(cond, ada_w, ada_b)
    cos_np, sin_np = _rope_tables()
    cos_t, sin_t = jnp.asarray(cos_np), jnp.asarray(sin_np)
    dft = {L: tuple(jnp.asarray(m, dtype=bf16) for m in _dft_matrices(L)) for L in (SEQ, DEC_SEQ)}

    P = N_PROMPT_TOK
    x = (x_prompt.reshape(P, D_MODEL), x_sample.reshape(N_SAMPLE_TOK, D_MODEL))
    new_kv = None
    for l in range(DEPTH):
        pa, pb, pc, pd, *new_kv = _inproj(x, l, new_kv, mod[l], norm1_g[l], w_in, cos_t, sin_t)

        lam_init = 0.8 - 0.6 * math.exp(-0.3 * l)
        lv = diff_lam[l].astype(f32)
        lam = (jnp.exp(jnp.sum(lv[0] * lv[1])) - jnp.exp(jnp.sum(lv[2] * lv[3])) + lam_init).reshape(1)
        gain = jnp.tile(diff_subln_g[l], A_HEADS).reshape(1, GROUP_W)
        a_kw = dict(n_heads=A_HEADS, diff=True, scale=A_DQK ** -0.5, lam=lam, gain=gain, out_scale=1.0 - lam_init)
        oa_p = _attention(pa, 0, 0, pa, 1, 0, pa, 2, 0, n_seq=BATCH, seq_len=SEQ, n_keys=SEQ, tq=SEQ, **a_kw)
        s0 = P // DEC_SEQ
        cache_a = (cache_diff_k[:, l].reshape(-1, GROUP_W), cache_diff_v[:, l].reshape(-1, GROUP_W))
        oa_s = _attention(pa, 0, s0, pa, 1, s0, pa, 2, s0, n_seq=DEC_BATCH, seq_len=DEC_SEQ, n_keys=DEC_SEQ,
                          tq=256, cache_kv=cache_a, **a_kw)

        hy = (hy_conv[l], hy_w1[l], hy_b1[l], hy_w2[l], hy_b2[l], hy_w3[l], hy_b3[l],
              hy_freq[l], hy_decay[l], hy_dbias[l])
        ob_p = _hyena(pb, 0, BATCH, SEQ, dft[SEQ], hy)
        ob_s = _hyena(pb, P // DEC_SEQ, DEC_BATCH, DEC_SEQ, dft[DEC_SEQ], hy)

        oc_p = _attention(pc, 0, 0, pc, 1, 0, pc, 2, 0, n_seq=BATCH, seq_len=SEQ, n_keys=SEQ, tq=SEQ,
                          n_heads=C_HEADS, diff=False, scale=C_DH ** -0.5)
        oc_s = _na_attention(pc, cache_na_k[:, l].reshape(-1, GROUP_W), cache_na_v[:, l].reshape(-1, GROUP_W),
                             _na_bias_tables(na_rpb[l]))

        w_bd = _pool_weight(pool_w[l])
        od_p = _pool(pd, w_bd, pool_scale[l], 0, BATCH, SEQ)
        od_s = _pool(pd, w_bd, pool_scale[l], P // DEC_SEQ, DEC_BATCH, DEC_SEQ)

        mixers = [(oa_p, oa_s), (ob_p, ob_s), (oc_p, oc_s), (od_p, od_s)]
        tm = 1024
        if l % 2 == 0:
            assert l != DEPTH - 1
            x1, h2 = _outproj(mixers, x, l, mod[l], norm2_g[l], w_out)
            x = _ffn_dense(h2, ffn_w1[l // 2], ffn_w3[l // 2], ffn_w2[l // 2], x1, mod[l])
        else:
            x1, h2, logits = _outproj(mixers, x, l, mod[l], norm2_g[l], w_out, moe_router[l // 2])
            row_token, blk_expert, n_valid, dest, gates = _route(logits[:, :N_EXPERTS], tm)
            ys = _ffn_routed(h2, row_token, moe_w1[l // 2], moe_w3[l // 2], moe_w2[l // 2], blk_expert, n_valid,
                             tm=tm, tf=512)
            x = _combine(x1, mod[l], ys, dest, gates, final_g if l == DEPTH - 1 else None)

    assert isinstance(x, tuple)
    y_prompt = x[0].reshape(BATCH, SEQ, D_MODEL)
    y_sample = x[1].reshape(DEC_BATCH, DEC_SEQ, D_MODEL)
    kak, kav, kck, kcv = new_kv
    return (y_prompt, y_sample,
            kak.reshape(BATCH, DEPTH, SEQ, A_HEADS, 2 * A_DQK), kav.reshape(BATCH, DEPTH, SEQ, A_HEADS, A_DV),
            kck.reshape(BATCH, DEPTH, SEQ, C_HEADS, C_DH), kcv.reshape(BATCH, DEPTH, SEQ, C_HEADS, C_DH))
```

```python
import functools
import math

import numpy as np
import jax
import jax.numpy as jnp
from jax import lax
from jax.experimental import pallas as pl
from jax.experimental.pallas import tpu as pltpu

D_MODEL = 1024
BATCH = 32
SEQ = 256
DEPTH = 2
DEC_BATCH = 4
DEC_SEQ = 2048
PAST_LEN = 512
GRID_W = 64
GRID_H = DEC_SEQ // GRID_W
GROUP_W = D_MODEL // 4
A_HEADS = 4
A_DQK = GROUP_W // (2 * A_HEADS)
A_DV = GROUP_W // A_HEADS
ROPE_BASE = 10000.0
HY_CH = GROUP_W
HY_ORDER = 2
HY_BANDS = 8
HY_FEAT = 1 + 2 * HY_BANDS
HY_HID = 64
C_HEADS = 4
C_DH = GROUP_W // C_HEADS
NA_KH = 8
NA_KW = 16
POOL_WINDOWS = (2, 4, 8, 16)
POOL_GC = GROUP_W // 4
PROJ_W = 3 * GROUP_W + 3 * HY_CH + 3 * GROUP_W + GROUP_W
N_EXPERTS = 8
TOP_K = 2
EPS = 1e-6
NEG = -1e30
LOG2E = math.log2(math.e)

N_PROMPT_TOK = BATCH * SEQ
N_SAMPLE_TOK = DEC_BATCH * DEC_SEQ
N_TOK = N_PROMPT_TOK + N_SAMPLE_TOK
MOD_ROWS = 8
CTX_ROW = DEC_BATCH
ROUTER_PAD = 128
LANES = 128

NA_QROWS = 4
NA_SLAB_ROWS = 12
NA_GROUPS = GRID_H // NA_QROWS

f32 = jnp.float32
bf16 = jnp.bfloat16

VMEM_LIMIT = 56 * 1024 * 1024
ROW_DMA_PRIORITY = 1
HY_SPECTRUM_TILE = 512
HY_EPILOGUE_ROWS = 2048
_NT = (((1,), (1,)), ((), ()))


def _cparams(n_axes):
    return pltpu.CompilerParams(
        dimension_semantics=("arbitrary",) * n_axes, vmem_limit_bytes=VMEM_LIMIT)


def _mod_row(i, tm):
    n_prompt_blocks = N_PROMPT_TOK // tm
    blocks_per_seq = DEC_SEQ // tm
    return jnp.where(i < n_prompt_blocks, CTX_ROW, (i - n_prompt_blocks) // blocks_per_seq)


def _split3(a):
    a0 = a.astype(bf16)
    r1 = a - a0.astype(f32)
    a1 = r1.astype(bf16)
    a2 = (r1 - a1.astype(f32)).astype(bf16)
    return a0, a1, a2


def _dot_bf16x3(a, b):
    a0 = a.astype(bf16)
    a1 = (a - a0.astype(f32)).astype(bf16)
    b0 = b.astype(bf16)
    b1 = (b - b0.astype(f32)).astype(bf16)
    d = functools.partial(jnp.dot, preferred_element_type=f32)
    return (d(a1, b0) + d(a0, b1)) + d(a0, b0)


def _dot_f32(a, b):
    a0, a1, a2 = _split3(a)
    b0, b1, b2 = _split3(b)
    d = functools.partial(jnp.dot, preferred_element_type=f32)
    return ((d(a2, b0) + d(a1, b1) + d(a0, b2)) + (d(a1, b0) + d(a0, b1))) + d(a0, b0)


def _lane_group(shape, width):
    return lax.shift_right_logical(lax.broadcasted_iota(jnp.int32, shape, 1), int(math.log2(width)))


def _shift_rows(x, d):
    n = x.shape[0]
    r = pltpu.roll(x, d % n, 0)
    row = lax.broadcasted_iota(jnp.int32, x.shape, 0)
    keep = (row >= d) if d > 0 else (row < n + d)
    return jnp.where(keep, r, 0.0)


def _ada_kernel(cond_ref, w_ref, b_ref, o_ref):
    c = cond_ref[...]
    s = c * jax.nn.sigmoid(c)
    o_ref[0] = jnp.dot(s.astype(bf16), w_ref[0].astype(bf16), preferred_element_type=f32) + b_ref[0]


def _ada_mod(cond, ada_w, ada_b):
    tn = 1536
    n6 = 6 * D_MODEL
    return pl.pallas_call(
        _ada_kernel,
        out_shape=jax.ShapeDtypeStruct((DEPTH, MOD_ROWS, n6), f32),
        grid=(DEPTH, n6 // tn),
        in_specs=[
            pl.BlockSpec((MOD_ROWS, D_MODEL), lambda l, j: (0, 0)),
            pl.BlockSpec((1, D_MODEL, tn), lambda l, j: (l, 0, j)),
            pl.BlockSpec((1, 1, tn), lambda l, j: (l, 0, j)),
        ],
        out_specs=pl.BlockSpec((1, MOD_ROWS, tn), lambda l, j: (l, 0, j)),
        compiler_params=_cparams(2),
        name="ada_mod",
    )(cond, ada_w, ada_b.reshape(DEPTH, 1, n6))


def _rope_tables():
    t = np.arange(DEC_SEQ)
    nf = A_DQK // 4
    inv = ROPE_BASE ** (-np.arange(nf, dtype=np.float64) / nf)
    ar = (t // GRID_W)[:, None] * inv
    ac = (t % GRID_W)[:, None] * inv
    cos = np.concatenate([np.cos(ar), np.cos(ar), np.cos(ac), np.cos(ac)], axis=1)
    sin = np.concatenate([-np.sin(ar), np.sin(ar), -np.sin(ac), np.sin(ac)], axis=1)
    reps = GROUP_W // A_DQK
    return (np.tile(cos, (1, reps)).astype(np.float32), np.tile(sin, (1, reps)).astype(np.float32))


def _modulated_norm(x, g, shift, scale):
    ms = jnp.mean(x * x, axis=-1, keepdims=True)
    return (x * lax.rsqrt(ms + EPS) * g) * (1.0 + scale) + shift


def _inproj_kernel(*refs, tm, split_x, n_alias):
    n_x = 2 if split_x else 1
    x_refs = refs[:n_x]
    mod_ref, g_ref, w_ref, cos_ref, sin_ref = refs[n_x:n_x + 5]
    outs = refs[n_x + 5 + n_alias:]
    pa_ref, pb_ref, pc_ref, pd_ref, kak_ref, kav_ref, kck_ref, kcv_ref, wbf_ref = outs
    i = pl.program_id(0)
    is_prompt = i < N_PROMPT_TOK // tm

    @pl.when(i == 0)
    def _():
        wbf_ref[...] = w_ref[0].astype(bf16)

    row = _mod_row(i, tm)
    shift = mod_ref[pl.ds(row, 1), 0:D_MODEL]
    scale = mod_ref[pl.ds(row, 1), D_MODEL:2 * D_MODEL]
    x = jnp.where(is_prompt, x_refs[0][...], x_refs[1][...]) if split_x else x_refs[0][...]
    h = _modulated_norm(x, g_ref[...], shift, scale)
    proj = jnp.dot(h.astype(bf16), wbf_ref[...], preferred_element_type=f32)
    w3 = 3 * GROUP_W
    pb_ref[...] = proj[:, w3:2 * w3]
    pc_ref[...] = proj[:, 2 * w3:3 * w3]
    pd_ref[...] = proj[:, 3 * w3:]
    pa_ref[:, 2 * GROUP_W:] = proj[:, 2 * GROUP_W:w3]

    @pl.when(is_prompt)
    def _():
        pa_ref[:, :2 * GROUP_W] = proj[:, :2 * GROUP_W]
        kv_shape = (tm // SEQ, 1, SEQ, GROUP_W)
        cols = (GROUP_W, 2 * GROUP_W, 2 * w3 + GROUP_W, 2 * w3 + 2 * GROUP_W)
        for ref, c0 in zip((kak_ref, kav_ref, kck_ref, kcv_ref), cols):
            ref[:, 0:1] = proj[:, c0:c0 + GROUP_W].reshape(kv_shape)
            if ref.shape[1] > 1:
                ref[:, 1:] = jnp.zeros((tm // SEQ, ref.shape[1] - 1, SEQ, GROUP_W), f32)

    @pl.when(i >= N_PROMPT_TOK // tm)
    def _():
        cos = cos_ref[...]
        sin = sin_ref[...]
        lane = lax.broadcasted_iota(jnp.int32, (tm, GROUP_W), 1)
        first = (lane % 16) < 8
        for s in range(2):
            v = proj[:, s * GROUP_W:(s + 1) * GROUP_W]
            partner = jnp.where(first, pltpu.roll(v, GROUP_W - 8, 1), pltpu.roll(v, 8, 1))
            pa_ref[:, s * GROUP_W:(s + 1) * GROUP_W] = v * cos + partner * sin


def _inproj(x, layer, kv_prev, mod_l, g, w_in, cos_t, sin_t, tm=512):
    npb = N_PROMPT_TOK // tm
    blocks_per_seq = DEC_SEQ // tm
    w3 = 3 * GROUP_W
    split_x = isinstance(x, tuple)

    def rope_idx(i):
        return (jnp.maximum(i - npb, 0) % blocks_per_seq, 0)

    if split_x:
        x_specs = [pl.BlockSpec((tm, D_MODEL), lambda i: (jnp.minimum(i, npb - 1), 0)),
                   pl.BlockSpec((tm, D_MODEL), lambda i: (jnp.maximum(i - npb, 0), 0))]
        x_args = list(x)
    else:
        x_specs = [pl.BlockSpec((tm, D_MODEL), lambda i: (i, 0))]
        x_args = [x]
    kv_args = list(kv_prev) if kv_prev is not None else []
    n_in = len(x_args) + 5
    kv_shape = jax.ShapeDtypeStruct((BATCH, DEPTH, SEQ, GROUP_W), f32)
    kv_layers = DEPTH if layer == 0 else 1
    kv_spec = pl.BlockSpec((tm // SEQ, kv_layers, SEQ, GROUP_W), lambda i: (jnp.minimum(i, npb - 1), layer, 0, 0))
    tok = lambda w: pl.BlockSpec((tm, w), lambda i: (i, 0))
    return pl.pallas_call(
        functools.partial(_inproj_kernel, tm=tm, split_x=split_x, n_alias=len(kv_args)),
        out_shape=(jax.ShapeDtypeStruct((N_TOK, w3), f32), jax.ShapeDtypeStruct((N_TOK, w3), f32),
                   jax.ShapeDtypeStruct((N_TOK, w3), f32), jax.ShapeDtypeStruct((N_TOK, GROUP_W), f32),
                   kv_shape, kv_shape, kv_shape, kv_shape),
        grid=(N_TOK // tm,),
        in_specs=x_specs + [
            pl.BlockSpec((MOD_ROWS, 6 * D_MODEL), lambda i: (0, 0)),
            pl.BlockSpec((1, D_MODEL), lambda i: (0, 0)),
            pl.BlockSpec((1, D_MODEL, PROJ_W), lambda i: (layer, 0, 0)),
            pl.BlockSpec((tm, GROUP_W), rope_idx),
            pl.BlockSpec((tm, GROUP_W), rope_idx),
        ] + [pl.BlockSpec(memory_space=pl.ANY)] * len(kv_args),
        out_specs=(tok(w3), tok(w3), tok(w3), tok(GROUP_W), kv_spec, kv_spec, kv_spec, kv_spec),
        scratch_shapes=[pltpu.VMEM((D_MODEL, PROJ_W), bf16)],
        input_output_aliases={n_in + j: 4 + j for j in range(len(kv_args))},
        compiler_params=_cparams(1),
        name="norm1_inproj",
    )(*x_args, mod_l, g.reshape(1, D_MODEL), w_in, cos_t, sin_t, *kv_args)


def _values_with_ones(v, vhead, vlane, h, hw):
    ones_col = ((h + 1) * hw) % GROUP_W
    vm = jnp.where(vhead == h, v, jnp.where(vlane == ones_col, 1.0, 0.0))
    return vm.astype(bf16), ones_col


def _attn_kernel(*refs, n_heads, diff, scale, out_scale, cached):
    if cached:
        lam_ref, q_ref, k_ref, v_ref, kx_ref, vx_ref, g_ref, o_ref = refs
        k = jnp.concatenate([k_ref[...], kx_ref[...]], axis=0).astype(bf16)
        v = jnp.concatenate([v_ref[...], vx_ref[...]], axis=0)
    else:
        lam_ref, q_ref, k_ref, v_ref, g_ref, o_ref = refs
        k = k_ref[...].astype(bf16)
        v = v_ref[...]
    tq = q_ref.shape[0]
    n = k.shape[0]
    q = q_ref[...] * (scale * LOG2E)
    hw = GROUP_W // n_heads
    n_maps = 2 if diff else 1
    qgrp = _lane_group((tq, GROUP_W), hw // n_maps)
    qhead = _lane_group((tq, GROUP_W), hw)
    vhead = _lane_group((n, GROUP_W), hw)
    vlane = lax.broadcasted_iota(jnp.int32, (n, GROUP_W), 1)
    acc = jnp.zeros((tq, GROUP_W), f32)
    for h in range(n_heads):
        vm, ones_col = _values_with_ones(v, vhead, vlane, h, hw)
        maps = []
        for m in range(n_maps):
            qm = jnp.where(qgrp == n_maps * h + m, q, 0.0).astype(bf16)
            s = lax.dot_general(qm, k, _NT, preferred_element_type=f32)
            e = jnp.exp2(s - jnp.max(s, axis=-1, keepdims=True)).astype(bf16)
            o = jnp.dot(e, vm, preferred_element_type=f32)
            maps.append(o * (1.0 / o[:, ones_col:ones_col + 1]))
        oh = maps[0] - lam_ref[0] * maps[1] if diff else maps[0]
        acc = acc + jnp.where(qhead == h, oh, 0.0)
    if diff:
        r = lax.shift_right_logical(lax.broadcasted_iota(jnp.int32, (GROUP_W, GROUP_W), 0), 6)
        c = lax.shift_right_logical(lax.broadcasted_iota(jnp.int32, (GROUP_W, GROUP_W), 1), 6)
        bd = jnp.where(r == c, 1.0, 0.0).astype(bf16)
        sq = acc * acc
        hi = sq.astype(bf16)
        lo = (sq - hi.astype(f32)).astype(bf16)
        ms = (jnp.dot(hi, bd, preferred_element_type=f32) + jnp.dot(lo, bd, preferred_element_type=f32)) * (1.0 / A_DV)
        acc = (acc * lax.rsqrt(ms + EPS) * g_ref[...]) * out_scale
    o_ref[...] = acc.astype(o_ref.dtype)


def _attention(q_src, q_col, q_row0, k_src, k_col, k_row0, v_src, v_col, v_row0, *,
               n_seq, seq_len, n_keys, tq, n_heads, diff, scale, lam=None, gain=None, out_scale=1.0,
               cache_kv=None):
    qb = seq_len // tq
    if lam is None:
        lam = jnp.zeros((1,), f32)
    if gain is None:
        gain = jnp.ones((1, GROUP_W), f32)
    in_specs = [
        pl.BlockSpec(memory_space=pltpu.SMEM),
        pl.BlockSpec((tq, GROUP_W), lambda b, i: ((q_row0 + b) * qb + i, q_col)),
        pl.BlockSpec((n_keys, GROUP_W), lambda b, i: (k_row0 + b, k_col)),
        pl.BlockSpec((n_keys, GROUP_W), lambda b, i: (v_row0 + b, v_col)),
    ]
    args = [lam, q_src, k_src, v_src]
    if cache_kv is not None:
        n_cached = cache_kv[0].shape[0] // n_seq
        in_specs += [pl.BlockSpec((n_cached, GROUP_W), lambda b, i: (b, 0))] * 2
        args += list(cache_kv)
    in_specs.append(pl.BlockSpec((1, GROUP_W), lambda b, i: (0, 0)))
    args.append(gain)
    return pl.pallas_call(
        functools.partial(_attn_kernel, n_heads=n_heads, diff=diff, scale=scale, out_scale=out_scale,
                          cached=cache_kv is not None),
        out_shape=jax.ShapeDtypeStruct((n_seq * seq_len, GROUP_W), bf16),
        grid=(n_seq, qb),
        in_specs=in_specs,
        out_specs=pl.BlockSpec((tq, GROUP_W), lambda b, i: (b * qb + i, 0)),
        compiler_params=_cparams(2),
        name="diff_attention" if diff else "softmax_attention",
    )(*args)


def _na_group_geometry(g):
    r0 = g * NA_QROWS
    slab0 = min(max(r0 - NA_KH // 2, 0), GRID_H - NA_SLAB_ROWS)
    return r0, slab0


def _na_bias_tables(rpb):
    c = np.arange(GRID_W)
    ws = np.clip(c - NA_KW // 2, 0, GRID_W - NA_KW)
    kc = np.arange(GRID_W)
    col_ok = (kc[None, :] >= ws[:, None]) & (kc[None, :] < ws[:, None] + NA_KW)
    pad = GRID_W - NA_KW
    rp = jnp.pad(rpb.astype(f32) * LOG2E, ((0, 0), (0, 0), (pad, pad)))
    tc = jnp.stack([rp[:, :, GRID_W - 1 - ci:2 * GRID_W - 1 - ci] for ci in range(GRID_W)], axis=2)
    tc = jnp.where(col_ok[None, None], tc, NEG)
    neg_blk = jnp.full((C_HEADS, GRID_W, GRID_W), NEG, f32)
    tables = []
    for g in (0, 1, NA_GROUPS - 1):
        r0, slab0 = _na_group_geometry(g)
        rows = []
        for rq in range(NA_QROWS):
            r = r0 + rq
            rs = min(max(r - NA_KH // 2, 0), GRID_H - NA_KH)
            blks = []
            for kl in range(NA_SLAB_ROWS):
                kr = slab0 + kl
                blks.append(tc[:, kr - r + NA_KH - 1] if rs <= kr < rs + NA_KH else neg_blk)
            rows.append(jnp.concatenate(blks, axis=-1))
        tables.append(jnp.concatenate(rows, axis=-2))
    return jnp.stack(tables, axis=0)


def _na_kernel(q_ref, k_ref, v_ref, kx_ref, vx_ref, bias_ref, o_ref):
    g = pl.program_id(1)
    tq = NA_QROWS * GRID_W
    ns = NA_SLAB_ROWS * GRID_W
    slab0 = jnp.clip(g * NA_QROWS - NA_KH // 2, 0, GRID_H - NA_SLAB_ROWS)
    start = pl.multiple_of(slab0 * GRID_W, GRID_W)
    q = q_ref[...] * (C_DH ** -0.5 * LOG2E)
    ks = k_ref[pl.ds(start, ns), :].astype(bf16)
    vs = v_ref[pl.ds(start, ns), :]
    kx = kx_ref[...].astype(bf16)
    vx = vx_ref[...]
    qhead = _lane_group((tq, GROUP_W), C_DH)
    vshead = _lane_group((ns, GROUP_W), C_DH)
    vslane = lax.broadcasted_iota(jnp.int32, (ns, GROUP_W), 1)
    vxhead = _lane_group((PAST_LEN, GROUP_W), C_DH)
    vxlane = lax.broadcasted_iota(jnp.int32, (PAST_LEN, GROUP_W), 1)
    acc = jnp.zeros((tq, GROUP_W), f32)
    for h in range(C_HEADS):
        qm = jnp.where(qhead == h, q, 0.0).astype(bf16)
        sl = lax.dot_general(qm, ks, _NT, preferred_element_type=f32)
        b = bias_ref[0, h]
        sl = jnp.where(b > 0.5 * NEG, sl + b, NEG)
        sx = lax.dot_general(qm, kx, _NT, preferred_element_type=f32)
        mx = jnp.maximum(jnp.max(sl, axis=-1, keepdims=True), jnp.max(sx, axis=-1, keepdims=True))
        el = jnp.exp2(sl - mx).astype(bf16)
        ex = jnp.exp2(sx - mx).astype(bf16)
        vsm, ones_col = _values_with_ones(vs, vshead, vslane, h, C_DH)
        vxm, _ = _values_with_ones(vx, vxhead, vxlane, h, C_DH)
        o = jnp.dot(el, vsm, preferred_element_type=f32) + jnp.dot(ex, vxm, preferred_element_type=f32)
        acc = acc + jnp.where(qhead == h, o * (1.0 / o[:, ones_col:ones_col + 1]), 0.0)
    o_ref[...] = acc.astype(o_ref.dtype)


def _na_attention(pc, kx, vx, bias):
    tq = NA_QROWS * GRID_W
    ns = NA_SLAB_ROWS * GRID_W
    q_blk0 = N_PROMPT_TOK // tq
    s_blk0 = N_PROMPT_TOK // DEC_SEQ

    def bias_idx(b, g):
        return (jnp.where(g == 0, 0, jnp.where(g == NA_GROUPS - 1, 2, 1)), 0, 0, 0)

    return pl.pallas_call(
        _na_kernel,
        out_shape=jax.ShapeDtypeStruct((N_SAMPLE_TOK, GROUP_W), bf16),
        grid=(DEC_BATCH, NA_GROUPS),
        in_specs=[
            pl.BlockSpec((tq, GROUP_W), lambda b, g: (q_blk0 + b * NA_GROUPS + g, 0)),
            pl.BlockSpec((DEC_SEQ, GROUP_W), lambda b, g: (s_blk0 + b, 1)),
            pl.BlockSpec((DEC_SEQ, GROUP_W), lambda b, g: (s_blk0 + b, 2)),
            pl.BlockSpec((PAST_LEN, GROUP_W), lambda b, g: (b, 0)),
            pl.BlockSpec((PAST_LEN, GROUP_W), lambda b, g: (b, 0)),
            pl.BlockSpec((1, C_HEADS, tq, ns), bias_idx),
        ],
        out_specs=pl.BlockSpec((tq, GROUP_W), lambda b, g: (b * NA_GROUPS + g, 0)),
        compiler_params=_cparams(2),
        name="neighbourhood_attention",
    )(pc, pc, pc, kx, vx, bias)


def _dft_matrices(L):
    n = 2 * L
    k = np.arange(L)[:, None]
    s = np.arange(L)[None, :]
    ang = 2.0 * np.pi * ((k * s) % n) / n
    cos, sin = np.cos(ang), np.sin(ang)
    sin[0, :] = (-1.0) ** np.arange(L)
    fwd = np.concatenate([cos, sin], axis=0)
    wk = np.where(np.arange(L) == 0, 1.0, 2.0)[None, :]
    inv = np.concatenate([cos.T * wk, sin.T * wk], axis=1) / n
    inv[:, L] = ((-1.0) ** np.arange(L)) / n
    return fwd, inv


def _hy_filter_kernel(w1_ref, b1_ref, w2_ref, b2_ref, w3_ref, b3_ref, fr_ref, dec_ref, fa_ref, fb_ref,
                      p_ref, q_ref, r_ref, g_ref, nrm_ref, *, L, tk):
    s = pl.program_id(0)

    @pl.when(s == 0)
    def _():
        row = lax.broadcasted_iota(jnp.int32, (L, LANES), 0)
        lane = lax.broadcasted_iota(jnp.int32, (L, LANES), 1)
        t = row.astype(f32) / L
        band = jnp.where(lane <= HY_BANDS, lane, lane - HY_BANDS).astype(f32)
        ang = (2.0 * math.pi * band) * t
        feat = jnp.where(lane == 0, t, jnp.where(lane <= HY_BANDS, jnp.sin(ang),
                                                 jnp.where(lane <= 2 * HY_BANDS, jnp.cos(ang), 0.0)))
        z = jnp.sin(fr_ref[0:1, :] * (_dot_f32(feat, w1_ref[...]) + b1_ref[...]))
        z = jnp.sin(fr_ref[1:2, :] * (_dot_f32(z, w2_ref[...]) + b2_ref[...]))
        z = _dot_f32(z, w3_ref[...]) + b3_ref[...]
        wide = (L, HY_ORDER * 2 * HY_CH)
        tw = lax.broadcasted_iota(jnp.int32, wide, 0).astype(f32) / L
        taps = z * jnp.exp(-tw * jnp.abs(dec_ref[...]))
        bwd = (_lane_group(wide, HY_CH) % 2) == 1
        first = lax.broadcasted_iota(jnp.int32, wide, 0) == 0
        taps = jnp.where(first, jnp.where(bwd, 0.0, taps), taps)
        g_ref[...] = taps.astype(bf16)
        nrm_ref[...] = jnp.sum(jnp.abs(taps), axis=0, keepdims=True)

    ga = jnp.dot(fa_ref[...], g_ref[...], preferred_element_type=f32)
    gb = jnp.dot(fb_ref[...], g_ref[...], preferred_element_type=f32)
    top = (lax.broadcasted_iota(jnp.int32, (tk, HY_CH), 0) + s * tk) == 0
    for o in range(HY_ORDER):
        c0 = o * 2 * HY_CH
        inv = 1.0 / (nrm_ref[:, c0:c0 + HY_CH] + nrm_ref[:, c0 + HY_CH:c0 + 2 * HY_CH])
        hc = (ga[:, c0:c0 + HY_CH] + ga[:, c0 + HY_CH:c0 + 2 * HY_CH]) * inv
        bf_, bb_ = gb[:, c0:c0 + HY_CH], gb[:, c0 + HY_CH:c0 + 2 * HY_CH]
        hs = jnp.where(top, bf_ + bb_, bf_ - bb_) * inv
        oc = slice(o * HY_CH, (o + 1) * HY_CH)
        p_ref[:, oc] = hc
        q_ref[:, oc] = jnp.where(top, 0.0, hs)
        r_ref[:, oc] = jnp.where(top, hs, hc)


def _hy_filter_spectra(L, fwd_bf, w1, b1, w2, b2, w3, b3, freq, decay, tk=256):
    nk = L // tk
    wide = HY_ORDER * 2 * HY_CH
    full = lambda a: pl.BlockSpec(a.shape, lambda s: (0,) * a.ndim)
    w1p = jnp.pad(w1, ((0, LANES - HY_FEAT), (0, 0)))
    args = [w1p, b1.reshape(1, HY_HID), w2, b2.reshape(1, HY_HID), w3, b3.reshape(1, wide), freq,
            decay.reshape(1, wide)]
    out = jax.ShapeDtypeStruct((L, HY_ORDER * HY_CH), f32)
    plane = pl.BlockSpec((tk, HY_ORDER * HY_CH), lambda s: (s, 0))
    return pl.pallas_call(
        functools.partial(_hy_filter_kernel, L=L, tk=tk),
        out_shape=(out, out, out),
        grid=(nk,),
        in_specs=[full(a) for a in args] + [pl.BlockSpec((tk, L), lambda s: (s, 0)),
                                            pl.BlockSpec((tk, L), lambda s: (s + nk, 0))],
        out_specs=(plane, plane, plane),
        scratch_shapes=[pltpu.VMEM((L, wide), bf16), pltpu.VMEM((1, wide), f32)],
        compiler_params=_cparams(1),
        name="hyena_filter_spectra",
    )(*args, fwd_bf, fwd_bf)


def _hy_pre_kernel(u_ref, w_ref, x1_ref, x2_ref, v_ref, vbf_ref):
    u = u_ref[...]
    w = w_ref[...]
    y = _shift_rows(u, 1) * w[0:1, :] + u * w[1:2, :] + _shift_rows(u, -1) * w[2:3, :]
    x1_ref[...] = y[:, :HY_CH]
    x2_ref[...] = y[:, HY_CH:2 * HY_CH]
    v = y[:, 2 * HY_CH:]
    v_ref[...] = v
    vbf_ref[...] = v.astype(bf16)


def _hy_pre(pb, conv_w, seq0, n_seq, L):
    n = n_seq * L
    blk = pl.BlockSpec((L, HY_CH), lambda b: (b, 0))
    o32 = jax.ShapeDtypeStruct((n, HY_CH), f32)
    return pl.pallas_call(
        _hy_pre_kernel,
        out_shape=(o32, o32, o32, jax.ShapeDtypeStruct((n, HY_CH), bf16)),
        grid=(n_seq,),
        in_specs=[pl.BlockSpec((L, 3 * HY_CH), lambda b: (seq0 + b, 0)),
                  pl.BlockSpec((3, 3 * HY_CH), lambda b: (0, 0))],
        out_specs=(blk, blk, blk, blk),
        compiler_params=_cparams(1),
        name="hyena_short_conv",
    )(pb, conv_w)


def _hy_conv_kernel(*refs, n_seq, L, nk, te, n_out):
    (zbf_ref, z_ref, m_ref, fa_ref, fb_ref, ic_ref, is_ref, p_ref, q_ref, r_ref, db_ref) = refs[:11]
    out_refs = refs[11:11 + n_out]
    acc_ref = refs[11 + n_out]
    s = pl.program_id(0)

    @pl.when(s == 0)
    def _():
        acc_ref[...] = jnp.zeros_like(acc_ref)

    @pl.when(s < nk)
    def _():
        fa, fb, ic, isn = fa_ref[...], fb_ref[...], ic_ref[...], is_ref[...]
        p, q, r = p_ref[...], q_ref[...], r_ref[...]
        for b in range(n_seq):
            rows = slice(b * L, (b + 1) * L)
            zb = zbf_ref[rows, :]
            a = jnp.dot(fa, zb, preferred_element_type=f32)
            bb = jnp.dot(fb, zb, preferred_element_type=f32)
            yc = (a * p - bb * q).astype(bf16)
            ys = (a * q + bb * r).astype(bf16)
            acc_ref[rows, :] += (jnp.dot(ic, yc, preferred_element_type=f32)
                                 + jnp.dot(isn, ys, preferred_element_type=f32))

    @pl.when(s >= nk)
    def _():
        start = pl.multiple_of((s - nk) * te, te)
        y = acc_ref[pl.ds(start, te), :]
        res = m_ref[...] * (y + db_ref[...] * z_ref[...])
        for o_ref in out_refs:
            o_ref[...] = res.astype(o_ref.dtype)


def _hy_longconv(zbf, z, mult, fwd_bf, inv_bf, planes, order, dbias_o, out_dtypes, n_seq, L, tk):
    nk = L // tk
    n = n_seq * L
    te = max(L, HY_EPILOGUE_ROWS)
    assert n % te == 0
    kt = lambda s: jnp.minimum(s, nk - 1)
    ep = lambda s: (jnp.maximum(s - nk, 0), 0)
    plane = pl.BlockSpec((tk, HY_CH), lambda s: (kt(s), order))
    return pl.pallas_call(
        functools.partial(_hy_conv_kernel, n_seq=n_seq, L=L, nk=nk, te=te, n_out=len(out_dtypes)),
        out_shape=tuple(jax.ShapeDtypeStruct((n, HY_CH), dt) for dt in out_dtypes),
        grid=(nk + n // te,),
        in_specs=[
            pl.BlockSpec((n, HY_CH), lambda s: (0, 0)),
            pl.BlockSpec((te, HY_CH), ep),
            pl.BlockSpec((te, HY_CH), ep),
            pl.BlockSpec((tk, L), lambda s: (kt(s), 0)),
            pl.BlockSpec((tk, L), lambda s: (kt(s) + nk, 0)),
            pl.BlockSpec((L, tk), lambda s: (0, kt(s))),
            pl.BlockSpec((L, tk), lambda s: (0, kt(s) + nk)),
            plane, plane, plane,
            pl.BlockSpec((1, HY_CH), lambda s: (0, 0)),
        ],
        out_specs=tuple(pl.BlockSpec((te, HY_CH), ep) for _ in out_dtypes),
        scratch_shapes=[pltpu.VMEM((n, HY_CH), f32)],
        compiler_params=_cparams(1),
        name="hyena_longconv",
    )(zbf, z, mult, fwd_bf, fwd_bf, inv_bf, inv_bf, *planes, dbias_o.reshape(1, HY_CH))


def _hyena(pb, seq0, n_seq, L, dft, hy):
    conv_w, w1, b1, w2, b2, w3, b3, freq, decay, dbias = hy
    fwd_bf, inv_bf = dft
    tk = min(L, HY_SPECTRUM_TILE)
    planes = _hy_filter_spectra(L, fwd_bf, w1, b1, w2, b2, w3, b3, freq, decay, tk=tk)
    x1, x2, v, vbf = _hy_pre(pb, conv_w, seq0, n_seq, L)
    z, zbf = _hy_longconv(vbf, v, x1, fwd_bf, inv_bf, planes, 0, dbias[0], (f32, bf16), n_seq, L, tk)
    (ob,) = _hy_longconv(zbf, z, x2, fwd_bf, inv_bf, planes, 1, dbias[1], (bf16,), n_seq, L, tk)
    return ob


def _pool_kernel(u_ref, w_ref, sc_ref, o_ref):
    u = u_ref[...]
    L = u.shape[0]
    back = _shift_rows(u, 1)
    fwd = u
    sums = [back + fwd]
    for k in (1, 2, 4):
        back = back + _shift_rows(back, k)
        fwd = fwd + _shift_rows(fwd, -k)
        sums.append(back + fwd)
    a2, a4, a8, a16 = sums
    grp = _lane_group(u.shape, POOL_GC)
    t = lax.broadcasted_iota(jnp.int32, u.shape, 0)
    half = jnp.left_shift(1, grp)
    cnt = jnp.minimum(t + half, L) - jnp.maximum(t - half, 0)
    tot = jnp.where(grp == 0, a2, jnp.where(grp == 1, a4, jnp.where(grp == 2, a8, a16)))
    pooled = tot / cnt.astype(f32) - u
    y = jnp.dot(pooled.astype(bf16), w_ref[...].astype(bf16), preferred_element_type=f32)
    o_ref[...] = (y * sc_ref[...]).astype(o_ref.dtype)


def _pool(pd, w_bd, scale, seq0, n_seq, L):
    return pl.pallas_call(
        _pool_kernel,
        out_shape=jax.ShapeDtypeStruct((n_seq * L, GROUP_W), bf16),
        grid=(n_seq,),
        in_specs=[pl.BlockSpec((L, GROUP_W), lambda b: (seq0 + b, 0)),
                  pl.BlockSpec((GROUP_W, GROUP_W), lambda b: (0, 0)),
                  pl.BlockSpec((1, GROUP_W), lambda b: (0, 0))],
        out_specs=pl.BlockSpec((L, GROUP_W), lambda b: (b, 0)),
        compiler_params=_cparams(1),
        name="pool_mixer",
    )(pd, w_bd, scale.reshape(1, GROUP_W))


def _outproj_kernel(*refs, tm, moe, split_x):
    mix_refs = refs[:8]
    n_x = 2 if split_x else 1
    x_refs = refs[8:8 + n_x]
    if moe:
        mod_ref, g_ref, w_ref, r_ref, x1_ref, h2_ref, lg_ref, wbf_ref = refs[8 + n_x:]
    else:
        mod_ref, g_ref, w_ref, x1_ref, h2_ref, wbf_ref = refs[8 + n_x:]
    i = pl.program_id(0)

    @pl.when(i == 0)
    def _():
        wbf_ref[...] = w_ref[0].astype(bf16)

    row = _mod_row(i, tm)
    d = D_MODEL
    gate1 = mod_ref[pl.ds(row, 1), 2 * d:3 * d]
    shift2 = mod_ref[pl.ds(row, 1), 3 * d:4 * d]
    scale2 = mod_ref[pl.ds(row, 1), 4 * d:5 * d]
    is_prompt = i < N_PROMPT_TOK // tm
    mixed = jnp.concatenate(
        [jnp.where(is_prompt, mix_refs[2 * j][...], mix_refs[2 * j + 1][...]) for j in range(4)], axis=-1)
    mix = jnp.dot(mixed, wbf_ref[...], preferred_element_type=f32)
    x = jnp.where(is_prompt, x_refs[0][...], x_refs[1][...]) if split_x else x_refs[0][...]
    x1 = x + gate1 * mix
    x1_ref[...] = x1
    h = _modulated_norm(x1, g_ref[...], shift2, scale2)
    h2_ref[...] = h.astype(h2_ref.dtype)
    if moe:
        lg_ref[...] = _dot_bf16x3(h, r_ref[...])


def _outproj(mixers, x, layer, mod_l, g, w_out, router_l=None, tm=512):
    moe = router_l is not None
    npb = N_PROMPT_TOK // tm
    tok = lambda w: pl.BlockSpec((tm, w), lambda i: (i, 0))
    in_specs, args = [], []
    for op, os_ in mixers:
        in_specs.append(pl.BlockSpec((tm, GROUP_W), lambda i: (jnp.minimum(i, npb - 1), 0)))
        in_specs.append(pl.BlockSpec((tm, GROUP_W), lambda i: (jnp.maximum(i - npb, 0), 0)))
        args += [op, os_]
    split_x = isinstance(x, tuple)
    if split_x:
        in_specs += [pl.BlockSpec((tm, D_MODEL), lambda i: (jnp.minimum(i, npb - 1), 0)),
                     pl.BlockSpec((tm, D_MODEL), lambda i: (jnp.maximum(i - npb, 0), 0))]
        args += list(x)
    else:
        in_specs.append(tok(D_MODEL))
        args.append(x)
    in_specs += [pl.BlockSpec((MOD_ROWS, 6 * D_MODEL), lambda i: (0, 0)),
                 pl.BlockSpec((1, D_MODEL), lambda i: (0, 0)),
                 pl.BlockSpec((1, D_MODEL, D_MODEL), lambda i: (layer, 0, 0))]
    args += [mod_l, g.reshape(1, D_MODEL), w_out]
    out_shape = [jax.ShapeDtypeStruct((N_TOK, D_MODEL), f32),
                 jax.ShapeDtypeStruct((N_TOK, D_MODEL), f32 if moe else bf16)]
    out_specs = [tok(D_MODEL), tok(D_MODEL)]
    if moe:
        in_specs.append(pl.BlockSpec((D_MODEL, ROUTER_PAD), lambda i: (0, 0)))
        args.append(jnp.pad(router_l, ((0, 0), (0, ROUTER_PAD - N_EXPERTS))))
        out_shape.append(jax.ShapeDtypeStruct((N_TOK, ROUTER_PAD), f32))
        out_specs.append(tok(ROUTER_PAD))
    return pl.pallas_call(
        functools.partial(_outproj_kernel, tm=tm, moe=moe, split_x=split_x),
        out_shape=tuple(out_shape),
        grid=(N_TOK // tm,),
        in_specs=in_specs,
        out_specs=tuple(out_specs),
        scratch_shapes=[pltpu.VMEM((D_MODEL, D_MODEL), bf16)],
        compiler_params=_cparams(1),
        name="outproj_norm2",
    )(*args)


def _ffn_kernel(be_ref, nv_ref, idx_ref, idx_next_ref, src_ref, w1_ref, w3_ref, w2_ref, o_ref,
                xbf_ref, xbuf_ref, sem, *, tm, nf):
    i = pl.program_id(0)
    j = pl.program_id(1)
    live = i < nv_ref[0]
    slot = i % 2

    @pl.when(j == 0)
    def _():
        o_ref[...] = jnp.zeros_like(o_ref)

    @pl.when((j == 0) & (i == 0))
    def _():
        _start_rows(idx_ref, src_ref, xbuf_ref, 0, sem, tm, 1)

    @pl.when((j == 0) & live)
    def _():
        _wait_rows(xbuf_ref, slot, sem)

    @pl.when((j == 0) & (i + 1 < nv_ref[0]))
    def _():
        _start_rows(idx_next_ref, src_ref, xbuf_ref, 1 - slot, sem, tm, 1)

    @pl.when((j == 0) & live)
    def _():
        xbf_ref[...] = xbuf_ref[slot].astype(bf16)

    @pl.when(live)
    def _():
        x = xbf_ref[...]
        h1 = jnp.dot(x, w1_ref[0].astype(bf16), preferred_element_type=f32)
        h3 = jnp.dot(x, w3_ref[0].astype(bf16), preferred_element_type=f32)
        a = (h1 * jax.nn.sigmoid(h1)) * h3
        o_ref[...] += jnp.dot(a.astype(bf16), w2_ref[0].astype(bf16), preferred_element_type=f32)


def _ffn_routed(row_src, row_token, w1, w3, w2, blk_expert, n_valid, tm, tf):
    n_rows = row_token.shape[0]
    ffn = w1.shape[-1]
    nf = ffn // tf
    nblk = n_rows // tm
    assert nf * tf == ffn and nblk * tm == n_rows

    def wcol(i, j, be, nv):
        return (be[i], 0, jnp.where(i < nv[0], j, nf - 1))

    def wrow(i, j, be, nv):
        return (be[i], jnp.where(i < nv[0], j, nf - 1), 0)

    idx3 = row_token.reshape(nblk, 1, tm)
    return pl.pallas_call(
        functools.partial(_ffn_kernel, tm=tm, nf=nf),
        out_shape=jax.ShapeDtypeStruct((n_rows, D_MODEL), f32),
        grid_spec=pltpu.PrefetchScalarGridSpec(
            num_scalar_prefetch=2,
            grid=(nblk, nf),
            in_specs=[
                pl.BlockSpec((1, 1, tm), lambda i, j, be, nv: (i, 0, 0), memory_space=pltpu.SMEM),
                pl.BlockSpec((1, 1, tm), lambda i, j, be, nv: (jnp.minimum(i + 1, nblk - 1), 0, 0),
                             memory_space=pltpu.SMEM),
                pl.BlockSpec(memory_space=pl.ANY),
                pl.BlockSpec((1, D_MODEL, tf), wcol),
                pl.BlockSpec((1, D_MODEL, tf), wcol),
                pl.BlockSpec((1, tf, D_MODEL), wrow),
            ],
            out_specs=pl.BlockSpec((tm, D_MODEL), lambda i, j, be, nv: (i, 0)),
            scratch_shapes=[pltpu.VMEM((tm, D_MODEL), bf16), pltpu.VMEM((2, tm, D_MODEL), row_src.dtype),
                            pltpu.SemaphoreType.DMA((2,))],
        ),
        compiler_params=_cparams(2),
        name="swiglu_routed",
    )(blk_expert, n_valid, idx3, idx3, row_src, w1, w3, w2)


def _cast_kernel(a_ref, o_ref):
    o_ref[...] = a_ref[...].astype(o_ref.dtype)


def _to_bf16(a, block_rows):
    rows, cols = a.shape
    assert rows % block_rows == 0
    return pl.pallas_call(
        _cast_kernel,
        out_shape=jax.ShapeDtypeStruct(a.shape, bf16),
        grid=(rows // block_rows,),
        in_specs=[pl.BlockSpec((block_rows, cols), lambda i: (i, 0))],
        out_specs=pl.BlockSpec((block_rows, cols), lambda i: (i, 0)),
        compiler_params=_cparams(1),
        name="cast_bf16",
    )(a)


def _ffn_dense_kernel(x_ref, w1_ref, w3_ref, w2_ref, x1_ref, mod_ref, o_ref, *, tm, chunks):
    i = pl.program_id(0)
    x = x_ref[...]
    acc = jnp.zeros((tm, D_MODEL), f32)
    c0 = 0
    for width in chunks:
        h1 = jnp.dot(x, w1_ref[:, c0:c0 + width], preferred_element_type=f32)
        h3 = jnp.dot(x, w3_ref[:, c0:c0 + width], preferred_element_type=f32)
        a = (h1 * jax.nn.sigmoid(h1)) * h3
        acc = acc + jnp.dot(a.astype(bf16), w2_ref[c0:c0 + width, :], preferred_element_type=f32)
        c0 += width
    gate2 = mod_ref[pl.ds(_mod_row(i, tm), 1), 5 * D_MODEL:6 * D_MODEL]
    o_ref[...] = x1_ref[...] + gate2 * acc


def _ffn_dense(h2, w1, w3, w2, x1, mod_l, tm=512, chunk=512):
    ffn = w1.shape[1]
    chunks = [chunk] * (ffn // chunk) + ([ffn % chunk] if ffn % chunk else [])
    w1b, w3b, w2b = _to_bf16(w1, 256), _to_bf16(w3, 256), _to_bf16(w2, 256)
    tok = lambda dt: pl.BlockSpec((tm, D_MODEL), lambda i: (i, 0))
    full = lambda a: pl.BlockSpec(a.shape, lambda i: (0, 0))
    return pl.pallas_call(
        functools.partial(_ffn_dense_kernel, tm=tm, chunks=tuple(chunks)),
        out_shape=jax.ShapeDtypeStruct((N_TOK, D_MODEL), f32),
        grid=(N_TOK // tm,),
        in_specs=[tok(bf16), full(w1b), full(w3b), full(w2b), tok(f32),
                  pl.BlockSpec((MOD_ROWS, 6 * D_MODEL), lambda i: (0, 0))],
        out_specs=tok(f32),
        compiler_params=_cparams(1),
        name="swiglu_dense",
    )(h2, w1b, w3b, w2b, x1, mod_l)


def _row_copy(src_ref, src_row, buf_ref, slot, rr, k, sem):
    w = src_ref.shape[1]
    return pltpu.make_async_copy(src_ref.at[pl.ds(src_row, 1), :],
                                 buf_ref.at[slot, pl.ds(rr, 1), pl.ds(k * w, w)], sem.at[slot])


def _start_rows(idx_ref, src_ref, buf_ref, slot, sem, n, pack):
    def body(rr, carry):
        for k in range(pack):
            _row_copy(src_ref, idx_ref[0, 0, rr * pack + k], buf_ref, slot, rr, k, sem).start(
                priority=ROW_DMA_PRIORITY)
        return carry

    lax.fori_loop(0, n // pack, body, 0, unroll=8)


def _wait_rows(buf_ref, slot, sem):
    pltpu.make_async_copy(buf_ref.at[slot], buf_ref.at[slot], sem.at[slot]).wait()


def _combine_kernel(*refs, tm, final):
    if final:
        idx_ref, idx_next_ref, ys_ref, x1_ref, mod_ref, gt_ref, g_ref, op_ref, os_ref, ybuf_ref, sem = refs
    else:
        idx_ref, idx_next_ref, ys_ref, x1_ref, mod_ref, gt_ref, o_ref, ybuf_ref, sem = refs
    i = pl.program_id(0)
    n_steps = pl.num_programs(0)
    slot = i % 2
    n_rows = tm * TOP_K

    @pl.when(i == 0)
    def _():
        _start_rows(idx_ref, ys_ref, ybuf_ref, 0, sem, n_rows, TOP_K)

    _wait_rows(ybuf_ref, slot, sem)

    @pl.when(i + 1 < n_steps)
    def _():
        _start_rows(idx_next_ref, ys_ref, ybuf_ref, 1 - slot, sem, n_rows, TOP_K)

    gate2 = mod_ref[pl.ds(_mod_row(i, tm), 1), 5 * D_MODEL:6 * D_MODEL]
    gt = gt_ref[...]
    f = gt[:, 0:1] * ybuf_ref[slot, :, :D_MODEL] + gt[:, 1:2] * ybuf_ref[slot, :, D_MODEL:]
    x2 = x1_ref[...] + gate2 * f
    if final:
        ms = jnp.mean(x2 * x2, axis=-1, keepdims=True)
        y = x2 * lax.rsqrt(ms + EPS) * g_ref[...]

        @pl.when(i < N_PROMPT_TOK // tm)
        def _():
            op_ref[...] = y

        @pl.when(i >= N_PROMPT_TOK // tm)
        def _():
            os_ref[...] = y
    else:
        o_ref[...] = x2


def _combine(x1, mod_l, ys, slots, gates, final_g=None, tm=512):
    final = final_g is not None
    npb = N_PROMPT_TOK // tm
    nblk = N_TOK // tm
    tok = pl.BlockSpec((tm, D_MODEL), lambda i: (i, 0))
    idx3 = slots.reshape(nblk, 1, tm * TOP_K)
    in_specs = [pl.BlockSpec((1, 1, tm * TOP_K), lambda i: (i, 0, 0), memory_space=pltpu.SMEM),
                pl.BlockSpec((1, 1, tm * TOP_K), lambda i: (jnp.minimum(i + 1, nblk - 1), 0, 0),
                             memory_space=pltpu.SMEM),
                pl.BlockSpec(memory_space=pl.ANY),
                tok, pl.BlockSpec((MOD_ROWS, 6 * D_MODEL), lambda i: (0, 0)),
                pl.BlockSpec((tm, TOP_K), lambda i: (i, 0))]
    args = [idx3, idx3, ys, x1, mod_l, gates]
    if final:
        in_specs.append(pl.BlockSpec((1, D_MODEL), lambda i: (0, 0)))
        args.append(final_g.reshape(1, D_MODEL))
        out_shape = (jax.ShapeDtypeStruct((N_PROMPT_TOK, D_MODEL), f32),
                     jax.ShapeDtypeStruct((N_SAMPLE_TOK, D_MODEL), f32))
        out_specs = (pl.BlockSpec((tm, D_MODEL), lambda i: (jnp.minimum(i, npb - 1), 0)),
                     pl.BlockSpec((tm, D_MODEL), lambda i: (jnp.maximum(i - npb, 0), 0)))
    else:
        out_shape = jax.ShapeDtypeStruct((N_TOK, D_MODEL), f32)
        out_specs = tok
    return pl.pallas_call(
        functools.partial(_combine_kernel, tm=tm, final=final),
        out_shape=out_shape,
        grid=(N_TOK // tm,),
        in_specs=in_specs,
        out_specs=out_specs,
        scratch_shapes=[pltpu.VMEM((2, tm, TOP_K * D_MODEL), ys.dtype), pltpu.SemaphoreType.DMA((2,))],
        compiler_params=_cparams(1),
        name="expert_combine",
    )(*args)


def _route(logits, tm):
    eid = jnp.arange(N_EXPERTS, dtype=jnp.int32)[None, :]
    v0 = jnp.max(logits, axis=-1, keepdims=True)
    i0 = jnp.min(jnp.where(logits == v0, eid, N_EXPERTS), axis=-1, keepdims=True)
    rest = jnp.where(eid == i0, -jnp.inf, logits)
    v1 = jnp.max(rest, axis=-1, keepdims=True)
    i1 = jnp.min(jnp.where(rest == v1, eid, N_EXPERTS), axis=-1, keepdims=True)
    gates = jax.nn.softmax(jnp.concatenate([v0, v1], axis=-1), axis=-1)
    flat_e = jnp.concatenate([i0, i1], axis=-1).reshape(-1)
    onehot = (flat_e[:, None] == jnp.arange(N_EXPERTS)[None, :]).astype(jnp.int32)
    csum = jnp.cumsum(onehot, axis=0)
    rank = jnp.take_along_axis(csum, flat_e[:, None], axis=1)[:, 0] - 1
    counts = csum[-1]
    padded = ((counts + tm - 1) // tm) * tm
    pend = jnp.cumsum(padded)
    pstart = pend - padded
    dest = pstart[flat_e] + rank
    n_rows = N_TOK * TOP_K + N_EXPERTS * tm
    tok_of = jnp.arange(N_TOK * TOP_K, dtype=jnp.int32) // TOP_K
    row_token = jnp.zeros((n_rows,), jnp.int32).at[dest].set(tok_of, unique_indices=True)
    blk_start = jnp.arange(n_rows // tm, dtype=jnp.int32) * tm
    blk_expert = jnp.minimum(jnp.sum((blk_start[:, None] >= pend[None, :]).astype(jnp.int32), axis=1),
                             N_EXPERTS - 1)
    n_valid = (pend[-1] // tm).astype(jnp.int32).reshape(1)
    blk_expert = jnp.where(blk_start < pend[-1], blk_expert, blk_expert[jnp.maximum(n_valid[0] - 1, 0)])
    return row_token, blk_expert, n_valid, dest.reshape(N_TOK, TOP_K), gates


def _pool_weight(pool_w_l):
    w = jnp.zeros((GROUP_W, GROUP_W), f32)
    for g in range(len(POOL_WINDOWS)):
        w = w.at[g * POOL_GC:(g + 1) * POOL_GC, g * POOL_GC:(g + 1) * POOL_GC].set(pool_w_l[g])
    return w


def kernel(x_prompt, x_sample, cache_diff_k, cache_diff_v, cache_na_k, cache_na_v, c, c_ctx,
           norm1_g, norm2_g, final_g, ada_w, ada_b, w_in, w_out, diff_lam, diff_subln_g,
           hy_conv, hy_w1, hy_b1, hy_w2, hy_b2, hy_w3, hy_b3, hy_freq, hy_decay, hy_dbias,
           na_rpb, pool_w, pool_scale, ffn_w1, ffn_w3, ffn_w2,
           moe_router, moe_w1, moe_w3, moe_w2):
    cond = jnp.concatenate([c, c_ctx[None, :], jnp.zeros((MOD_ROWS - DEC_BATCH - 1, D_MODEL), f32)], axis=0)
    mod = _ada_mod(cond, ada_w, ada_b)
    cos_np, sin_np = _rope_tables()
    cos_t, sin_t = jnp.asarray(cos_np), jnp.asarray(sin_np)
    dft = {L: tuple(jnp.asarray(m, dtype=bf16) for m in _dft_matrices(L)) for L in (SEQ, DEC_SEQ)}

    P = N_PROMPT_TOK
    x = (x_prompt.reshape(P, D_MODEL), x_sample.reshape(N_SAMPLE_TOK, D_MODEL))
    new_kv = None
    for l in range(DEPTH):
        pa, pb, pc, pd, *new_kv = _inproj(x, l, new_kv, mod[l], norm1_g[l], w_in, cos_t, sin_t)

        lam_init = 0.8 - 0.6 * math.exp(-0.3 * l)
        lv = diff_lam[l].astype(f32)
        lam = (jnp.exp(jnp.sum(lv[0] * lv[1])) - jnp.exp(jnp.sum(lv[2] * lv[3])) + lam_init).reshape(1)
        gain = jnp.tile(diff_subln_g[l], A_HEADS).reshape(1, GROUP_W)
        a_kw = dict(n_heads=A_HEADS, diff=True, scale=A_DQK ** -0.5, lam=lam, gain=gain, out_scale=1.0 - lam_init)
        oa_p = _attention(pa, 0, 0, pa, 1, 0, pa, 2, 0, n_seq=BATCH, seq_len=SEQ, n_keys=SEQ, tq=SEQ, **a_kw)
        s0 = P // DEC_SEQ
        cache_a = (cache_diff_k[:, l].reshape(-1, GROUP_W), cache_diff_v[:, l].reshape(-1, GROUP_W))
        oa_s = _attention(pa, 0, s0, pa, 1, s0, pa, 2, s0, n_seq=DEC_BATCH, seq_len=DEC_SEQ, n_keys=DEC_SEQ,
                          tq=256, cache_kv=cache_a, **a_kw)

        hy = (hy_conv[l], hy_w1[l], hy_b1[l], hy_w2[l], hy_b2[l], hy_w3[l], hy_b3[l],
              hy_freq[l], hy_decay[l], hy_dbias[l])
        ob_p = _hyena(pb, 0, BATCH, SEQ, dft[SEQ], hy)
        ob_s = _hyena(pb, P // DEC_SEQ, DEC_BATCH, DEC_SEQ, dft[DEC_SEQ], hy)

        oc_p = _attention(pc, 0, 0, pc, 1, 0, pc, 2, 0, n_seq=BATCH, seq_len=SEQ, n_keys=SEQ, tq=SEQ,
                          n_heads=C_HEADS, diff=False, scale=C_DH ** -0.5)
        oc_s = _na_attention(pc, cache_na_k[:, l].reshape(-1, GROUP_W), cache_na_v[:, l].reshape(-1, GROUP_W),
                             _na_bias_tables(na_rpb[l]))

        w_bd = _pool_weight(pool_w[l])
        od_p = _pool(pd, w_bd, pool_scale[l], 0, BATCH, SEQ)
        od_s = _pool(pd, w_bd, pool_scale[l], P // DEC_SEQ, DEC_BATCH, DEC_SEQ)

        mixers = [(oa_p, oa_s), (ob_p, ob_s), (oc_p, oc_s), (od_p, od_s)]
        tm = 1024
        if l % 2 == 0:
            assert l != DEPTH - 1
            x1, h2 = _outproj(mixers, x, l, mod[l], norm2_g[l], w_out)
            x = _ffn_dense(h2, ffn_w1[l // 2], ffn_w3[l // 2], ffn_w2[l // 2], x1, mod[l])
        else:
            x1, h2, logits = _outproj(mixers, x, l, mod[l], norm2_g[l], w_out, moe_router[l // 2])
            row_token, blk_expert, n_valid, dest, gates = _route(logits[:, :N_EXPERTS], tm)
            ys = _ffn_routed(h2, row_token, moe_w1[l // 2], moe_w3[l // 2], moe_w2[l // 2], blk_expert, n_valid,
                             tm=tm, tf=512)
            x = _combine(x1, mod[l], ys, dest, gates, final_g if l == DEPTH - 1 else None)

    assert isinstance(x, tuple)
    y_prompt = x[0].reshape(BATCH, SEQ, D_MODEL)
    y_sample = x[1].reshape(DEC_BATCH, DEC_SEQ, D_MODEL)
    kak, kav, kck, kcv = new_kv
    return (y_prompt, y_sample,
            kak.reshape(BATCH, DEPTH, SEQ, A_HEADS, 2 * A_DQK), kav.reshape(BATCH, DEPTH, SEQ, A_HEADS, A_DV),
            kck.reshape(BATCH, DEPTH, SEQ, C_HEADS, C_DH), kcv.reshape(BATCH, DEPTH, SEQ, C_HEADS, C_DH))
```

```python
import functools
import math

import numpy as np
import jax
import jax.numpy as jnp
from jax import lax
from jax.experimental import pallas as pl
from jax.experimental.pallas import tpu as pltpu

D_MODEL = 1024
BATCH = 32
SEQ = 256
DEPTH = 2
DEC_BATCH = 4
DEC_SEQ = 2048
PAST_LEN = 512
GRID_W = 64
GRID_H = DEC_SEQ // GRID_W
GROUP_W = D_MODEL // 4
A_HEADS = 4
A_DQK = GROUP_W // (2 * A_HEADS)
A_DV = GROUP_W // A_HEADS
ROPE_BASE = 10000.0
HY_CH = GROUP_W
HY_ORDER = 2
HY_BANDS = 8
HY_FEAT = 1 + 2 * HY_BANDS
HY_HID = 64
C_HEADS = 4
C_DH = GROUP_W // C_HEADS
NA_KH = 8
NA_KW = 16
POOL_WINDOWS = (2, 4, 8, 16)
POOL_GC = GROUP_W // 4
PROJ_W = 3 * GROUP_W + 3 * HY_CH + 3 * GROUP_W + GROUP_W
N_EXPERTS = 8
TOP_K = 2
EPS = 1e-6
NEG = -1e30
LOG2E = math.log2(math.e)

N_PROMPT_TOK = BATCH * SEQ
N_SAMPLE_TOK = DEC_BATCH * DEC_SEQ
N_TOK = N_PROMPT_TOK + N_SAMPLE_TOK
MOD_ROWS = 8
CTX_ROW = DEC_BATCH
ROUTER_PAD = 128
LANES = 128

NA_QROWS = 4
NA_SLAB_ROWS = 12
NA_GROUPS = GRID_H // NA_QROWS

f32 = jnp.float32
bf16 = jnp.bfloat16

VMEM_LIMIT = 56 * 1024 * 1024
ROW_DMA_PRIORITY = 1
HY_SPECTRUM_TILE = 512
HY_EPILOGUE_ROWS = 2048
_NT = (((1,), (1,)), ((), ()))


def _cparams(n_axes):
    return pltpu.CompilerParams(
        dimension_semantics=("arbitrary",) * n_axes, vmem_limit_bytes=VMEM_LIMIT)


def _mod_row(i, tm):
    n_prompt_blocks = N_PROMPT_TOK // tm
    blocks_per_seq = DEC_SEQ // tm
    return jnp.where(i < n_prompt_blocks, CTX_ROW, (i - n_prompt_blocks) // blocks_per_seq)


def _split3(a):
    a0 = a.astype(bf16)
    r1 = a - a0.astype(f32)
    a1 = r1.astype(bf16)
    a2 = (r1 - a1.astype(f32)).astype(bf16)
    return a0, a1, a2


def _dot_bf16x3(a, b):
    a0 = a.astype(bf16)
    a1 = (a - a0.astype(f32)).astype(bf16)
    b0 = b.astype(bf16)
    b1 = (b - b0.astype(f32)).astype(bf16)
    d = functools.partial(jnp.dot, preferred_element_type=f32)
    return (d(a1, b0) + d(a0, b1)) + d(a0, b0)


def _dot_f32(a, b):
    a0, a1, a2 = _split3(a)
    b0, b1, b2 = _split3(b)
    d = functools.partial(jnp.dot, preferred_element_type=f32)
    return ((d(a2, b0) + d(a1, b1) + d(a0, b2)) + (d(a1, b0) + d(a0, b1))) + d(a0, b0)


def _lane_group(shape, width):
    return lax.shift_right_logical(lax.broadcasted_iota(jnp.int32, shape, 1), int(math.log2(width)))


def _shift_rows(x, d):
    n = x.shape[0]
    r = pltpu.roll(x, d % n, 0)
    row = lax.broadcasted_iota(jnp.int32, x.shape, 0)
    keep = (row >= d) if d > 0 else (row < n + d)
    return jnp.where(keep, r, 0.0)


def _ada_kernel(cond_ref, w_ref, b_ref, o_ref):
    c = cond_ref[...]
    s = c * jax.nn.sigmoid(c)
    o_ref[0] = jnp.dot(s.astype(bf16), w_ref[0].astype(bf16), preferred_element_type=f32) + b_ref[0]


def _ada_mod(cond, ada_w, ada_b):
    tn = 1536
    n6 = 6 * D_MODEL
    return pl.pallas_call(
        _ada_kernel,
        out_shape=jax.ShapeDtypeStruct((DEPTH, MOD_ROWS, n6), f32),
        grid=(DEPTH, n6 // tn),
        in_specs=[
            pl.BlockSpec((MOD_ROWS, D_MODEL), lambda l, j: (0, 0)),
            pl.BlockSpec((1, D_MODEL, tn), lambda l, j: (l, 0, j)),
            pl.BlockSpec((1, 1, tn), lambda l, j: (l, 0, j)),
        ],
        out_specs=pl.BlockSpec((1, MOD_ROWS, tn), lambda l, j: (l, 0, j)),
        compiler_params=_cparams(2),
        name="ada_mod",
    )(cond, ada_w, ada_b.reshape(DEPTH, 1, n6))


def _rope_tables():
    t = np.arange(DEC_SEQ)
    nf = A_DQK // 4
    inv = ROPE_BASE ** (-np.arange(nf, dtype=np.float64) / nf)
    ar = (t // GRID_W)[:, None] * inv
    ac = (t % GRID_W)[:, None] * inv
    cos = np.concatenate([np.cos(ar), np.cos(ar), np.cos(ac), np.cos(ac)], axis=1)
    sin = np.concatenate([-np.sin(ar), np.sin(ar), -np.sin(ac), np.sin(ac)], axis=1)
    reps = GROUP_W // A_DQK
    return (np.tile(cos, (1, reps)).astype(np.float32), np.tile(sin, (1, reps)).astype(np.float32))


def _modulated_norm(x, g, shift, scale):
    ms = jnp.mean(x * x, axis=-1, keepdims=True)
    return (x * lax.rsqrt(ms + EPS) * g) * (1.0 + scale) + shift


def _inproj_kernel(*refs, tm, split_x, n_alias):
    n_x = 2 if split_x else 1
    x_refs = refs[:n_x]
    mod_ref, g_ref, w_ref, cos_ref, sin_ref = refs[n_x:n_x + 5]
    outs = refs[n_x + 5 + n_alias:]
    pa_ref, pb_ref, pc_ref, pd_ref, kak_ref, kav_ref, kck_ref, kcv_ref, wbf_ref = outs
    i = pl.program_id(0)
    is_prompt = i < N_PROMPT_TOK // tm

    @pl.when(i == 0)
    def _():
        wbf_ref[...] = w_ref[0].astype(bf16)

    row = _mod_row(i, tm)
    shift = mod_ref[pl.ds(row, 1), 0:D_MODEL]
    scale = mod_ref[pl.ds(row, 1), D_MODEL:2 * D_MODEL]
    x = jnp.where(is_prompt, x_refs[0][...], x_refs[1][...]) if split_x else x_refs[0][...]
    h = _modulated_norm(x, g_ref[...], shift, scale)
    proj = jnp.dot(h.astype(bf16), wbf_ref[...], preferred_element_type=f32)
    w3 = 3 * GROUP_W
    pb_ref[...] = proj[:, w3:2 * w3]
    pc_ref[...] = proj[:, 2 * w3:3 * w3]
    pd_ref[...] = proj[:, 3 * w3:]
    pa_ref[:, 2 * GROUP_W:] = proj[:, 2 * GROUP_W:w3]

    @pl.when(is_prompt)
    def _():
        pa_ref[:, :2 * GROUP_W] = proj[:, :2 * GROUP_W]
        kv_shape = (tm // SEQ, 1, SEQ, GROUP_W)
        cols = (GROUP_W, 2 * GROUP_W, 2 * w3 + GROUP_W, 2 * w3 + 2 * GROUP_W)
        for ref, c0 in zip((kak_ref, kav_ref, kck_ref, kcv_ref), cols):
            ref[:, 0:1] = proj[:, c0:c0 + GROUP_W].reshape(kv_shape)
            if ref.shape[1] > 1:
                ref[:, 1:] = jnp.zeros((tm // SEQ, ref.shape[1] - 1, SEQ, GROUP_W), f32)

    @pl.when(i >= N_PROMPT_TOK // tm)
    def _():
        cos = cos_ref[...]
        sin = sin_ref[...]
        lane = lax.broadcasted_iota(jnp.int32, (tm, GROUP_W), 1)
        first = (lane % 16) < 8
        for s in range(2):
            v = proj[:, s * GROUP_W:(s + 1) * GROUP_W]
            partner = jnp.where(first, pltpu.roll(v, GROUP_W - 8, 1), pltpu.roll(v, 8, 1))
            pa_ref[:, s * GROUP_W:(s + 1) * GROUP_W] = v * cos + partner * sin


def _inproj(x, layer, kv_prev, mod_l, g, w_in, cos_t, sin_t, tm=512):
    npb = N_PROMPT_TOK // tm
    blocks_per_seq = DEC_SEQ // tm
    w3 = 3 * GROUP_W
    split_x = isinstance(x, tuple)

    def rope_idx(i):
        return (jnp.maximum(i - npb, 0) % blocks_per_seq, 0)

    if split_x:
        x_specs = [pl.BlockSpec((tm, D_MODEL), lambda i: (jnp.minimum(i, npb - 1), 0)),
                   pl.BlockSpec((tm, D_MODEL), lambda i: (jnp.maximum(i - npb, 0), 0))]
        x_args = list(x)
    else:
        x_specs = [pl.BlockSpec((tm, D_MODEL), lambda i: (i, 0))]
        x_args = [x]
    kv_args = list(kv_prev) if kv_prev is not None else []
    n_in = len(x_args) + 5
    kv_shape = jax.ShapeDtypeStruct((BATCH, DEPTH, SEQ, GROUP_W), f32)
    kv_layers = DEPTH if layer == 0 else 1
    kv_spec = pl.BlockSpec((tm // SEQ, kv_layers, SEQ, GROUP_W), lambda i: (jnp.minimum(i, npb - 1), layer, 0, 0))
    tok = lambda w: pl.BlockSpec((tm, w), lambda i: (i, 0))
    return pl.pallas_call(
        functools.partial(_inproj_kernel, tm=tm, split_x=split_x, n_alias=len(kv_args)),
        out_shape=(jax.ShapeDtypeStruct((N_TOK, w3), f32), jax.ShapeDtypeStruct((N_TOK, w3), f32),
                   jax.ShapeDtypeStruct((N_TOK, w3), f32), jax.ShapeDtypeStruct((N_TOK, GROUP_W), f32),
                   kv_shape, kv_shape, kv_shape, kv_shape),
        grid=(N_TOK // tm,),
        in_specs=x_specs + [
            pl.BlockSpec((MOD_ROWS, 6 * D_MODEL), lambda i: (0, 0)),
            pl.BlockSpec((1, D_MODEL), lambda i: (0, 0)),
            pl.BlockSpec((1, D_MODEL, PROJ_W), lambda i: (layer, 0, 0)),
            pl.BlockSpec((tm, GROUP_W), rope_idx),
            pl.BlockSpec((tm, GROUP_W), rope_idx),
        ] + [pl.BlockSpec(memory_space=pl.ANY)] * len(kv_args),
        out_specs=(tok(w3), tok(w3), tok(w3), tok(GROUP_W), kv_spec, kv_spec, kv_spec, kv_spec),
        scratch_shapes=[pltpu.VMEM((D_MODEL, PROJ_W), bf16)],
        input_output_aliases={n_in + j: 4 + j for j in range(len(kv_args))},
        compiler_params=_cparams(1),
        name="norm1_inproj",
    )(*x_args, mod_l, g.reshape(1, D_MODEL), w_in, cos_t, sin_t, *kv_args)


def _values_with_ones(v, vhead, vlane, h, hw):
    ones_col = ((h + 1) * hw) % GROUP_W
    vm = jnp.where(vhead == h, v, jnp.where(vlane == ones_col, 1.0, 0.0))
    return vm.astype(bf16), ones_col


def _attn_kernel(*refs, n_heads, diff, scale, out_scale, cached):
    if cached:
        lam_ref, q_ref, k_ref, v_ref, kx_ref, vx_ref, g_ref, o_ref = refs
        k = jnp.concatenate([k_ref[...], kx_ref[...]], axis=0).astype(bf16)
        v = jnp.concatenate([v_ref[...], vx_ref[...]], axis=0)
    else:
        lam_ref, q_ref, k_ref, v_ref, g_ref, o_ref = refs
        k = k_ref[...].astype(bf16)
        v = v_ref[...]
    tq = q_ref.shape[0]
    n = k.shape[0]
    q = q_ref[...] * (scale * LOG2E)
    hw = GROUP_W // n_heads
    n_maps = 2 if diff else 1
    qgrp = _lane_group((tq, GROUP_W), hw // n_maps)
    qhead = _lane_group((tq, GROUP_W), hw)
    vhead = _lane_group((n, GROUP_W), hw)
    vlane = lax.broadcasted_iota(jnp.int32, (n, GROUP_W), 1)
    acc = jnp.zeros((tq, GROUP_W), f32)
    for h in range(n_heads):
        vm, ones_col = _values_with_ones(v, vhead, vlane, h, hw)
        maps = []
        for m in range(n_maps):
            qm = jnp.where(qgrp == n_maps * h + m, q, 0.0).astype(bf16)
            s = lax.dot_general(qm, k, _NT, preferred_element_type=f32)
            e = jnp.exp2(s - jnp.max(s, axis=-1, keepdims=True)).astype(bf16)
            o = jnp.dot(e, vm, preferred_element_type=f32)
            maps.append(o * (1.0 / o[:, ones_col:ones_col + 1]))
        oh = maps[0] - lam_ref[0] * maps[1] if diff else maps[0]
        acc = acc + jnp.where(qhead == h, oh, 0.0)
    if diff:
        r = lax.shift_right_logical(lax.broadcasted_iota(jnp.int32, (GROUP_W, GROUP_W), 0), 6)
        c = lax.shift_right_logical(lax.broadcasted_iota(jnp.int32, (GROUP_W, GROUP_W), 1), 6)
        bd = jnp.where(r == c, 1.0, 0.0).astype(bf16)
        sq = acc * acc
        hi = sq.astype(bf16)
        lo = (sq - hi.astype(f32)).astype(bf16)
        ms = (jnp.dot(hi, bd, preferred_element_type=f32) + jnp.dot(lo, bd, preferred_element_type=f32)) * (1.0 / A_DV)
        acc = (acc * lax.rsqrt(ms + EPS) * g_ref[...]) * out_scale
    o_ref[...] = acc.astype(o_ref.dtype)


def _attention(q_src, q_col, q_row0, k_src, k_col, k_row0, v_src, v_col, v_row0, *,
               n_seq, seq_len, n_keys, tq, n_heads, diff, scale, lam=None, gain=None, out_scale=1.0,
               cache_kv=None):
    qb = seq_len // tq
    if lam is None:
        lam = jnp.zeros((1,), f32)
    if gain is None:
        gain = jnp.ones((1, GROUP_W), f32)
    in_specs = [
        pl.BlockSpec(memory_space=pltpu.SMEM),
        pl.BlockSpec((tq, GROUP_W), lambda b, i: ((q_row0 + b) * qb + i, q_col)),
        pl.BlockSpec((n_keys, GROUP_W), lambda b, i: (k_row0 + b, k_col)),
        pl.BlockSpec((n_keys, GROUP_W), lambda b, i: (v_row0 + b, v_col)),
    ]
    args = [lam, q_src, k_src, v_src]
    if cache_kv is not None:
        n_cached = cache_kv[0].shape[0] // n_seq
        in_specs += [pl.BlockSpec((n_cached, GROUP_W), lambda b, i: (b, 0))] * 2
        args += list(cache_kv)
    in_specs.append(pl.BlockSpec((1, GROUP_W), lambda b, i: (0, 0)))
    args.append(gain)
    return pl.pallas_call(
        functools.partial(_attn_kernel, n_heads=n_heads, diff=diff, scale=scale, out_scale=out_scale,
                          cached=cache_kv is not None),
        out_shape=jax.ShapeDtypeStruct((n_seq * seq_len, GROUP_W), bf16),
        grid=(n_seq, qb),
        in_specs=in_specs,
        out_specs=pl.BlockSpec((tq, GROUP_W), lambda b, i: (b * qb + i, 0)),
        compiler_params=_cparams(2),
        name="diff_attention" if diff else "softmax_attention",
    )(*args)


def _na_group_geometry(g):
    r0 = g * NA_QROWS
    slab0 = min(max(r0 - NA_KH // 2, 0), GRID_H - NA_SLAB_ROWS)
    return r0, slab0


def _na_bias_tables(rpb):
    c = np.arange(GRID_W)
    ws = np.clip(c - NA_KW // 2, 0, GRID_W - NA_KW)
    kc = np.arange(GRID_W)
    col_ok = (kc[None, :] >= ws[:, None]) & (kc[None, :] < ws[:, None] + NA_KW)
    pad = GRID_W - NA_KW
    rp = jnp.pad(rpb.astype(f32) * LOG2E, ((0, 0), (0, 0), (pad, pad)))
    tc = jnp.stack([rp[:, :, GRID_W - 1 - ci:2 * GRID_W - 1 - ci] for ci in range(GRID_W)], axis=2)
    tc = jnp.where(col_ok[None, None], tc, NEG)
    neg_blk = jnp.full((C_HEADS, GRID_W, GRID_W), NEG, f32)
    tables = []
    for g in (0, 1, NA_GROUPS - 1):
        r0, slab0 = _na_group_geometry(g)
        rows = []
        for rq in range(NA_QROWS):
            r = r0 + rq
            rs = min(max(r - NA_KH // 2, 0), GRID_H - NA_KH)
            blks = []
            for kl in range(NA_SLAB_ROWS):
                kr = slab0 + kl
                blks.append(tc[:, kr - r + NA_KH - 1] if rs <= kr < rs + NA_KH else neg_blk)
            rows.append(jnp.concatenate(blks, axis=-1))
        tables.append(jnp.concatenate(rows, axis=-2))
    return jnp.stack(tables, axis=0)


def _na_kernel(q_ref, k_ref, v_ref, kx_ref, vx_ref, bias_ref, o_ref):
    g = pl.program_id(1)
    tq = NA_QROWS * GRID_W
    ns = NA_SLAB_ROWS * GRID_W
    slab0 = jnp.clip(g * NA_QROWS - NA_KH // 2, 0, GRID_H - NA_SLAB_ROWS)
    start = pl.multiple_of(slab0 * GRID_W, GRID_W)
    q = q_ref[...] * (C_DH ** -0.5 * LOG2E)
    ks = k_ref[pl.ds(start, ns), :].astype(bf16)
    vs = v_ref[pl.ds(start, ns), :]
    kx = kx_ref[...].astype(bf16)
    vx = vx_ref[...]
    qhead = _lane_group((tq, GROUP_W), C_DH)
    vshead = _lane_group((ns, GROUP_W), C_DH)
    vslane = lax.broadcasted_iota(jnp.int32, (ns, GROUP_W), 1)
    vxhead = _lane_group((PAST_LEN, GROUP_W), C_DH)
    vxlane = lax.broadcasted_iota(jnp.int32, (PAST_LEN, GROUP_W), 1)
    acc = jnp.zeros((tq, GROUP_W), f32)
    for h in range(C_HEADS):
        qm = jnp.where(qhead == h, q, 0.0).astype(bf16)
        sl = lax.dot_general(qm, ks, _NT, preferred_element_type=f32)
        b = bias_ref[0, h]
        sl = jnp.where(b > 0.5 * NEG, sl + b, NEG)
        sx = lax.dot_general(qm, kx, _NT, preferred_element_type=f32)
        mx = jnp.maximum(jnp.max(sl, axis=-1, keepdims=True), jnp.max(sx, axis=-1, keepdims=True))
        el = jnp.exp2(sl - mx).astype(bf16)
        ex = jnp.exp2(sx - mx).astype(bf16)
        vsm, ones_col = _values_with_ones(vs, vshead, vslane, h, C_DH)
        vxm, _ = _values_with_ones(vx, vxhead, vxlane, h, C_DH)
        o = jnp.dot(el, vsm, preferred_element_type=f32) + jnp.dot(ex, vxm, preferred_element_type=f32)
        acc = acc + jnp.where(qhead == h, o * (1.0 / o[:, ones_col:ones_col + 1]), 0.0)
    o_ref[...] = acc.astype(o_ref.dtype)


def _na_attention(pc, kx, vx, bias):
    tq = NA_QROWS * GRID_W
    ns = NA_SLAB_ROWS * GRID_W
    q_blk0 = N_PROMPT_TOK // tq
    s_blk0 = N_PROMPT_TOK // DEC_SEQ

    def bias_idx(b, g):
        return (jnp.where(g == 0, 0, jnp.where(g == NA_GROUPS - 1, 2, 1)), 0, 0, 0)

    return pl.pallas_call(
        _na_kernel,
        out_shape=jax.ShapeDtypeStruct((N_SAMPLE_TOK, GROUP_W), bf16),
        grid=(DEC_BATCH, NA_GROUPS),
        in_specs=[
            pl.BlockSpec((tq, GROUP_W), lambda b, g: (q_blk0 + b * NA_GROUPS + g, 0)),
            pl.BlockSpec((DEC_SEQ, GROUP_W), lambda b, g: (s_blk0 + b, 1)),
            pl.BlockSpec((DEC_SEQ, GROUP_W), lambda b, g: (s_blk0 + b, 2)),
            pl.BlockSpec((PAST_LEN, GROUP_W), lambda b, g: (b, 0)),
            pl.BlockSpec((PAST_LEN, GROUP_W), lambda b, g: (b, 0)),
            pl.BlockSpec((1, C_HEADS, tq, ns), bias_idx),
        ],
        out_specs=pl.BlockSpec((tq, GROUP_W), lambda b, g: (b * NA_GROUPS + g, 0)),
        compiler_params=_cparams(2),
        name="neighbourhood_attention",
    )(pc, pc, pc, kx, vx, bias)


def _dft_matrices(L):
    n = 2 * L
    k = np.arange(L)[:, None]
    s = np.arange(L)[None, :]
    ang = 2.0 * np.pi * ((k * s) % n) / n
    cos, sin = np.cos(ang), np.sin(ang)
    sin[0, :] = (-1.0) ** np.arange(L)
    fwd = np.concatenate([cos, sin], axis=0)
    wk = np.where(np.arange(L) == 0, 1.0, 2.0)[None, :]
    inv = np.concatenate([cos.T * wk, sin.T * wk], axis=1) / n
    inv[:, L] = ((-1.0) ** np.arange(L)) / n
    return fwd, inv


def _hy_filter_kernel(w1_ref, b1_ref, w2_ref, b2_ref, w3_ref, b3_ref, fr_ref, dec_ref, fa_ref, fb_ref,
                      p_ref, q_ref, r_ref, g_ref, nrm_ref, *, L, tk):
    s = pl.program_id(0)

    @pl.when(s == 0)
    def _():
        row = lax.broadcasted_iota(jnp.int32, (L, LANES), 0)
        lane = lax.broadcasted_iota(jnp.int32, (L, LANES), 1)
        t = row.astype(f32) / L
        band = jnp.where(lane <= HY_BANDS, lane, lane - HY_BANDS).astype(f32)
        ang = (2.0 * math.pi * band) * t
        feat = jnp.where(lane == 0, t, jnp.where(lane <= HY_BANDS, jnp.sin(ang),
                                                 jnp.where(lane <= 2 * HY_BANDS, jnp.cos(ang), 0.0)))
        z = jnp.sin(fr_ref[0:1, :] * (_dot_f32(feat, w1_ref[...]) + b1_ref[...]))
        z = jnp.sin(fr_ref[1:2, :] * (_dot_f32(z, w2_ref[...]) + b2_ref[...]))
        z = _dot_f32(z, w3_ref[...]) + b3_ref[...]
        wide = (L, HY_ORDER * 2 * HY_CH)
        tw = lax.broadcasted_iota(jnp.int32, wide, 0).astype(f32) / L
        taps = z * jnp.exp(-tw * jnp.abs(dec_ref[...]))
        bwd = (_lane_group(wide, HY_CH) % 2) == 1
        first = lax.broadcasted_iota(jnp.int32, wide, 0) == 0
        taps = jnp.where(first, jnp.where(bwd, 0.0, taps), taps)
        g_ref[...] = taps.astype(bf16)
        nrm_ref[...] = jnp.sum(jnp.abs(taps), axis=0, keepdims=True)

    ga = jnp.dot(fa_ref[...], g_ref[...], preferred_element_type=f32)
    gb = jnp.dot(fb_ref[...], g_ref[...], preferred_element_type=f32)
    top = (lax.broadcasted_iota(jnp.int32, (tk, HY_CH), 0) + s * tk) == 0
    for o in range(HY_ORDER):
        c0 = o * 2 * HY_CH
        inv = 1.0 / (nrm_ref[:, c0:c0 + HY_CH] + nrm_ref[:, c0 + HY_CH:c0 + 2 * HY_CH])
        hc = (ga[:, c0:c0 + HY_CH] + ga[:, c0 + HY_CH:c0 + 2 * HY_CH]) * inv
        bf_, bb_ = gb[:, c0:c0 + HY_CH], gb[:, c0 + HY_CH:c0 + 2 * HY_CH]
        hs = jnp.where(top, bf_ + bb_, bf_ - bb_) * inv
        oc = slice(o * HY_CH, (o + 1) * HY_CH)
        p_ref[:, oc] = hc
        q_ref[:, oc] = jnp.where(top, 0.0, hs)
        r_ref[:, oc] = jnp.where(top, hs, hc)


def _hy_filter_spectra(L, fwd_bf, w1, b1, w2, b2, w3, b3, freq, decay, tk=256):
    nk = L // tk
    wide = HY_ORDER * 2 * HY_CH
    full = lambda a: pl.BlockSpec(a.shape, lambda s: (0,) * a.ndim)
    w1p = jnp.pad(w1, ((0, LANES - HY_FEAT), (0, 0)))
    args = [w1p, b1.reshape(1, HY_HID), w2, b2.reshape(1, HY_HID), w3, b3.reshape(1, wide), freq,
            decay.reshape(1, wide)]
    out = jax.ShapeDtypeStruct((L, HY_ORDER * HY_CH), f32)
    plane = pl.BlockSpec((tk, HY_ORDER * HY_CH), lambda s: (s, 0))
    return pl.pallas_call(
        functools.partial(_hy_filter_kernel, L=L, tk=tk),
        out_shape=(out, out, out),
        grid=(nk,),
        in_specs=[full(a) for a in args] + [pl.BlockSpec((tk, L), lambda s: (s, 0)),
                                            pl.BlockSpec((tk, L), lambda s: (s + nk, 0))],
        out_specs=(plane, plane, plane),
        scratch_shapes=[pltpu.VMEM((L, wide), bf16), pltpu.VMEM((1, wide), f32)],
        compiler_params=_cparams(1),
        name="hyena_filter_spectra",
    )(*args, fwd_bf, fwd_bf)


def _hy_pre_kernel(u_ref, w_ref, x1_ref, x2_ref, v_ref, vbf_ref):
    u = u_ref[...]
    w = w_ref[...]
    y = _shift_rows(u, 1) * w[0:1, :] + u * w[1:2, :] + _shift_rows(u, -1) * w[2:3, :]
    x1_ref[...] = y[:, :HY_CH]
    x2_ref[...] = y[:, HY_CH:2 * HY_CH]
    v = y[:, 2 * HY_CH:]
    v_ref[...] = v
    vbf_ref[...] = v.astype(bf16)


def _hy_pre(pb, conv_w, seq0, n_seq, L):
    n = n_seq * L
    blk = pl.BlockSpec((L, HY_CH), lambda b: (b, 0))
    o32 = jax.ShapeDtypeStruct((n, HY_CH), f32)
    return pl.pallas_call(
        _hy_pre_kernel,
        out_shape=(o32, o32, o32, jax.ShapeDtypeStruct((n, HY_CH), bf16)),
        grid=(n_seq,),
        in_specs=[pl.BlockSpec((L, 3 * HY_CH), lambda b: (seq0 + b, 0)),
                  pl.BlockSpec((3, 3 * HY_CH), lambda b: (0, 0))],
        out_specs=(blk, blk, blk, blk),
        compiler_params=_cparams(1),
        name="hyena_short_conv",
    )(pb, conv_w)


def _hy_conv_kernel(*refs, n_seq, L, nk, te, n_out):
    (zbf_ref, z_ref, m_ref, fa_ref, fb_ref, ic_ref, is_ref, p_ref, q_ref, r_ref, db_ref) = refs[:11]
    out_refs = refs[11:11 + n_out]
    acc_ref = refs[11 + n_out]
    s = pl.program_id(0)

    @pl.when(s == 0)
    def _():
        acc_ref[...] = jnp.zeros_like(acc_ref)

    @pl.when(s < nk)
    def _():
        fa, fb, ic, isn = fa_ref[...], fb_ref[...], ic_ref[...], is_ref[...]
        p, q, r = p_ref[...], q_ref[...], r_ref[...]
        for b in range(n_seq):
            rows = slice(b * L, (b + 1) * L)
            zb = zbf_ref[rows, :]
            a = jnp.dot(fa, zb, preferred_element_type=f32)
            bb = jnp.dot(fb, zb, preferred_element_type=f32)
            yc = (a * p - bb * q).astype(bf16)
            ys = (a * q + bb * r).astype(bf16)
            acc_ref[rows, :] += (jnp.dot(ic, yc, preferred_element_type=f32)
                                 + jnp.dot(isn, ys, preferred_element_type=f32))

    @pl.when(s >= nk)
    def _():
        start = pl.multiple_of((s - nk) * te, te)
        y = acc_ref[pl.ds(start, te), :]
        res = m_ref[...] * (y + db_ref[...] * z_ref[...])
        for o_ref in out_refs:
            o_ref[...] = res.astype(o_ref.dtype)


def _hy_longconv(zbf, z, mult, fwd_bf, inv_bf, planes, order, dbias_o, out_dtypes, n_seq, L, tk):
    nk = L // tk
    n = n_seq * L
    te = max(L, HY_EPILOGUE_ROWS)
    assert n % te == 0
    kt = lambda s: jnp.minimum(s, nk - 1)
    ep = lambda s: (jnp.maximum(s - nk, 0), 0)
    plane = pl.BlockSpec((tk, HY_CH), lambda s: (kt(s), order))
    return pl.pallas_call(
        functools.partial(_hy_conv_kernel, n_seq=n_seq, L=L, nk=nk, te=te, n_out=len(out_dtypes)),
        out_shape=tuple(jax.ShapeDtypeStruct((n, HY_CH), dt) for dt in out_dtypes),
        grid=(nk + n // te,),
        in_specs=[
            pl.BlockSpec((n, HY_CH), lambda s: (0, 0)),
            pl.BlockSpec((te, HY_CH), ep),
            pl.BlockSpec((te, HY_CH), ep),
            pl.BlockSpec((tk, L), lambda s: (kt(s), 0)),
            pl.BlockSpec((tk, L), lambda s: (kt(s) + nk, 0)),
            pl.BlockSpec((L, tk), lambda s: (0, kt(s))),
            pl.BlockSpec((L, tk), lambda s: (0, kt(s) + nk)),
            plane, plane, plane,
            pl.BlockSpec((1, HY_CH), lambda s: (0, 0)),
        ],
        out_specs=tuple(pl.BlockSpec((te, HY_CH), ep) for _ in out_dtypes),
        scratch_shapes=[pltpu.VMEM((n, HY_CH), f32)],
        compiler_params=_cparams(1),
        name="hyena_longconv",
    )(zbf, z, mult, fwd_bf, fwd_bf, inv_bf, inv_bf, *planes, dbias_o.reshape(1, HY_CH))


def _hyena(pb, seq0, n_seq, L, dft, hy):
    conv_w, w1, b1, w2, b2, w3, b3, freq, decay, dbias = hy
    fwd_bf, inv_bf = dft
    tk = min(L, HY_SPECTRUM_TILE)
    planes = _hy_filter_spectra(L, fwd_bf, w1, b1, w2, b2, w3, b3, freq, decay, tk=tk)
    x1, x2, v, vbf = _hy_pre(pb, conv_w, seq0, n_seq, L)
    z, zbf = _hy_longconv(vbf, v, x1, fwd_bf, inv_bf, planes, 0, dbias[0], (f32, bf16), n_seq, L, tk)
    (ob,) = _hy_longconv(zbf, z, x2, fwd_bf, inv_bf, planes, 1, dbias[1], (bf16,), n_seq, L, tk)
    return ob


def _pool_kernel(u_ref, w_ref, sc_ref, o_ref):
    u = u_ref[...]
    L = u.shape[0]
    back = _shift_rows(u, 1)
    fwd = u
    sums = [back + fwd]
    for k in (1, 2, 4):
        back = back + _shift_rows(back, k)
        fwd = fwd + _shift_rows(fwd, -k)
        sums.append(back + fwd)
    a2, a4, a8, a16 = sums
    grp = _lane_group(u.shape, POOL_GC)
    t = lax.broadcasted_iota(jnp.int32, u.shape, 0)
    half = jnp.left_shift(1, grp)
    cnt = jnp.minimum(t + half, L) - jnp.maximum(t - half, 0)
    tot = jnp.where(grp == 0, a2, jnp.where(grp == 1, a4, jnp.where(grp == 2, a8, a16)))
    pooled = tot / cnt.astype(f32) - u
    y = jnp.dot(pooled.astype(bf16), w_ref[...].astype(bf16), preferred_element_type=f32)
    o_ref[...] = (y * sc_ref[...]).astype(o_ref.dtype)


def _pool(pd, w_bd, scale, seq0, n_seq, L):
    return pl.pallas_call(
        _pool_kernel,
        out_shape=jax.ShapeDtypeStruct((n_seq * L, GROUP_W), bf16),
        grid=(n_seq,),
        in_specs=[pl.BlockSpec((L, GROUP_W), lambda b: (seq0 + b, 0)),
                  pl.BlockSpec((GROUP_W, GROUP_W), lambda b: (0, 0)),
                  pl.BlockSpec((1, GROUP_W), lambda b: (0, 0))],
        out_specs=pl.BlockSpec((L, GROUP_W), lambda b: (b, 0)),
        compiler_params=_cparams(1),
        name="pool_mixer",
    )(pd, w_bd, scale.reshape(1, GROUP_W))


def _outproj_kernel(*refs, tm, moe, split_x):
    mix_refs = refs[:8]
    n_x = 2 if split_x else 1
    x_refs = refs[8:8 + n_x]
    if moe:
        mod_ref, g_ref, w_ref, r_ref, x1_ref, h2_ref, lg_ref, wbf_ref = refs[8 + n_x:]
    else:
        mod_ref, g_ref, w_ref, x1_ref, h2_ref, wbf_ref = refs[8 + n_x:]
    i = pl.program_id(0)

    @pl.when(i == 0)
    def _():
        wbf_ref[...] = w_ref[0].astype(bf16)

    row = _mod_row(i, tm)
    d = D_MODEL
    gate1 = mod_ref[pl.ds(row, 1), 2 * d:3 * d]
    shift2 = mod_ref[pl.ds(row, 1), 3 * d:4 * d]
    scale2 = mod_ref[pl.ds(row, 1), 4 * d:5 * d]
    is_prompt = i < N_PROMPT_TOK // tm
    mixed = jnp.concatenate(
        [jnp.where(is_prompt, mix_refs[2 * j][...], mix_refs[2 * j + 1][...]) for j in range(4)], axis=-1)
    mix = jnp.dot(mixed, wbf_ref[...], preferred_element_type=f32)
    x = jnp.where(is_prompt, x_refs[0][...], x_refs[1][...]) if split_x else x_refs[0][...]
    x1 = x + gate1 * mix
    x1_ref[...] = x1
    h = _modulated_norm(x1, g_ref[...], shift2, scale2)
    h2_ref[...] = h.astype(h2_ref.dtype)
    if moe:
        lg_ref[...] = _dot_bf16x3(h, r_ref[...])


def _outproj(mixers, x, layer, mod_l, g, w_out, router_l=None, tm=512):
    moe = router_l is not None
    npb = N_PROMPT_TOK // tm
    tok = lambda w: pl.BlockSpec((tm, w), lambda i: (i, 0))
    in_specs, args = [], []
    for op, os_ in mixers:
        in_specs.append(pl.BlockSpec((tm, GROUP_W), lambda i: (jnp.minimum(i, npb - 1), 0)))
        in_specs.append(pl.BlockSpec((tm, GROUP_W), lambda i: (jnp.maximum(i - npb, 0), 0)))
        args += [op, os_]
    split_x = isinstance(x, tuple)
    if split_x:
        in_specs += [pl.BlockSpec((tm, D_MODEL), lambda i: (jnp.minimum(i, npb - 1), 0)),
                     pl.BlockSpec((tm, D_MODEL), lambda i: (jnp.maximum(i - npb, 0), 0))]
        args += list(x)
    else:
        in_specs.append(tok(D_MODEL))
        args.append(x)
    in_specs += [pl.BlockSpec((MOD_ROWS, 6 * D_MODEL), lambda i: (0, 0)),
                 pl.BlockSpec((1, D_MODEL), lambda i: (0, 0)),
                 pl.BlockSpec((1, D_MODEL, D_MODEL), lambda i: (layer, 0, 0))]
    args += [mod_l, g.reshape(1, D_MODEL), w_out]
    out_shape = [jax.ShapeDtypeStruct((N_TOK, D_MODEL), f32),
                 jax.ShapeDtypeStruct((N_TOK, D_MODEL), f32 if moe else bf16)]
    out_specs = [tok(D_MODEL), tok(D_MODEL)]
    if moe:
        in_specs.append(pl.BlockSpec((D_MODEL, ROUTER_PAD), lambda i: (0, 0)))
        args.append(jnp.pad(router_l, ((0, 0), (0, ROUTER_PAD - N_EXPERTS))))
        out_shape.append(jax.ShapeDtypeStruct((N_TOK, ROUTER_PAD), f32))
        out_specs.append(tok(ROUTER_PAD))
    return pl.pallas_call(
        functools.partial(_outproj_kernel, tm=tm, moe=moe, split_x=split_x),
        out_shape=tuple(out_shape),
        grid=(N_TOK // tm,),
        in_specs=in_specs,
        out_specs=tuple(out_specs),
        scratch_shapes=[pltpu.VMEM((D_MODEL, D_MODEL), bf16)],
        compiler_params=_cparams(1),
        name="outproj_norm2",
    )(*args)


def _ffn_kernel(be_ref, nv_ref, idx_ref, idx_next_ref, src_ref, w1_ref, w3_ref, w2_ref, o_ref,
                xbf_ref, xbuf_ref, sem, *, tm, nf):
    i = pl.program_id(0)
    j = pl.program_id(1)
    last = pl.num_programs(0) - 1
    live = i < nv_ref[0]
    slot = i % 2
    share = (tm // nf) // 8 * 8
    head = tm - nf * share

    @pl.when(j == 0)
    def _():
        o_ref[...] = jnp.zeros_like(o_ref)

    @pl.when((j == 0) & (i == 0))
    def _():
        _start_rows(idx_ref, src_ref, xbuf_ref, 0, sem, tm)

    @pl.when((j == 0) & (i <= nv_ref[0]))
    def _():
        _wait_rows(xbuf_ref, slot, sem)

    @pl.when((j == 0) & live)
    def _():
        _start_rows(idx_next_ref, src_ref, xbuf_ref, 1 - slot, sem, head)
        xbf_ref[...] = xbuf_ref[slot].astype(bf16)

    @pl.when(live)
    def _():
        first = head + j * share
        for u in range(share):
            _row_copy(src_ref, idx_next_ref[0, 0, first + u], xbuf_ref, 1 - slot, first + u, 0, sem).start(
                priority=ROW_DMA_PRIORITY)
        x = xbf_ref[...]
        h1 = jnp.dot(x, w1_ref[0].astype(bf16), preferred_element_type=f32)
        h3 = jnp.dot(x, w3_ref[0].astype(bf16), preferred_element_type=f32)
        a = (h1 * jax.nn.sigmoid(h1)) * h3
        o_ref[...] += jnp.dot(a.astype(bf16), w2_ref[0].astype(bf16), preferred_element_type=f32)

    @pl.when(live & (i == last) & (j == nf - 1))
    def _():
        _wait_rows(xbuf_ref, 1 - slot, sem)


def _ffn_routed(row_src, row_token, w1, w3, w2, blk_expert, n_valid, tm, tf):
    n_rows = row_token.shape[0]
    ffn = w1.shape[-1]
    nf = ffn // tf
    nblk = n_rows // tm
    assert nf * tf == ffn and nblk * tm == n_rows

    def wcol(i, j, be, nv):
        return (be[i], 0, jnp.where(i < nv[0], j, nf - 1))

    def wrow(i, j, be, nv):
        return (be[i], jnp.where(i < nv[0], j, nf - 1), 0)

    idx3 = row_token.reshape(nblk, 1, tm)
    return pl.pallas_call(
        functools.partial(_ffn_kernel, tm=tm, nf=nf),
        out_shape=jax.ShapeDtypeStruct((n_rows, D_MODEL), f32),
        grid_spec=pltpu.PrefetchScalarGridSpec(
            num_scalar_prefetch=2,
            grid=(nblk, nf),
            in_specs=[
                pl.BlockSpec((1, 1, tm), lambda i, j, be, nv: (i, 0, 0), memory_space=pltpu.SMEM),
                pl.BlockSpec((1, 1, tm), lambda i, j, be, nv: (jnp.minimum(i + 1, nblk - 1), 0, 0),
                             memory_space=pltpu.SMEM),
                pl.BlockSpec(memory_space=pl.ANY),
                pl.BlockSpec((1, D_MODEL, tf), wcol),
                pl.BlockSpec((1, D_MODEL, tf), wcol),
                pl.BlockSpec((1, tf, D_MODEL), wrow),
            ],
            out_specs=pl.BlockSpec((tm, D_MODEL), lambda i, j, be, nv: (i, 0)),
            scratch_shapes=[pltpu.VMEM((tm, D_MODEL), bf16), pltpu.VMEM((2, tm, D_MODEL), row_src.dtype),
                            pltpu.SemaphoreType.DMA((2,))],
        ),
        compiler_params=_cparams(2),
        name="swiglu_routed",
    )(blk_expert, n_valid, idx3, idx3, row_src, w1, w3, w2)


def _cast_kernel(a_ref, o_ref):
    o_ref[...] = a_ref[...].astype(o_ref.dtype)


def _to_bf16(a, block_rows):
    rows, cols = a.shape
    assert rows % block_rows == 0
    return pl.pallas_call(
        _cast_kernel,
        out_shape=jax.ShapeDtypeStruct(a.shape, bf16),
        grid=(rows // block_rows,),
        in_specs=[pl.BlockSpec((block_rows, cols), lambda i: (i, 0))],
        out_specs=pl.BlockSpec((block_rows, cols), lambda i: (i, 0)),
        compiler_params=_cparams(1),
        name="cast_bf16",
    )(a)


def _ffn_dense_kernel(x_ref, w1_ref, w3_ref, w2_ref, x1_ref, mod_ref, o_ref, *, tm, chunks):
    i = pl.program_id(0)
    x = x_ref[...]
    acc = jnp.zeros((tm, D_MODEL), f32)
    c0 = 0
    for width in chunks:
        h1 = jnp.dot(x, w1_ref[:, c0:c0 + width], preferred_element_type=f32)
        h3 = jnp.dot(x, w3_ref[:, c0:c0 + width], preferred_element_type=f32)
        a = (h1 * jax.nn.sigmoid(h1)) * h3
        acc = acc + jnp.dot(a.astype(bf16), w2_ref[c0:c0 + width, :], preferred_element_type=f32)
        c0 += width
    gate2 = mod_ref[pl.ds(_mod_row(i, tm), 1), 5 * D_MODEL:6 * D_MODEL]
    o_ref[...] = x1_ref[...] + gate2 * acc


def _ffn_dense(h2, w1, w3, w2, x1, mod_l, tm=512, chunk=512):
    ffn = w1.shape[1]
    chunks = [chunk] * (ffn // chunk) + ([ffn % chunk] if ffn % chunk else [])
    w1b, w3b, w2b = _to_bf16(w1, 256), _to_bf16(w3, 256), _to_bf16(w2, 256)
    tok = lambda dt: pl.BlockSpec((tm, D_MODEL), lambda i: (i, 0))
    full = lambda a: pl.BlockSpec(a.shape, lambda i: (0, 0))
    return pl.pallas_call(
        functools.partial(_ffn_dense_kernel, tm=tm, chunks=tuple(chunks)),
        out_shape=jax.ShapeDtypeStruct((N_TOK, D_MODEL), f32),
        grid=(N_TOK // tm,),
        in_specs=[tok(bf16), full(w1b), full(w3b), full(w2b), tok(f32),
                  pl.BlockSpec((MOD_ROWS, 6 * D_MODEL), lambda i: (0, 0))],
        out_specs=tok(f32),
        compiler_params=_cparams(1),
        name="swiglu_dense",
    )(h2, w1b, w3b, w2b, x1, mod_l)


def _row_copy(src_ref, src_row, buf_ref, slot, rr, k, sem):
    w = src_ref.shape[1]
    return pltpu.make_async_copy(src_ref.at[pl.ds(src_row, 1), :],
                                 buf_ref.at[slot, pl.ds(rr, 1), pl.ds(k * w, w)], sem.at[slot])


def _start_rows(idx_ref, src_ref, buf_ref, slot, sem, n, pack=1):
    def body(rr, carry):
        for k in range(pack):
            _row_copy(src_ref, idx_ref[0, 0, rr * pack + k], buf_ref, slot, rr, k, sem).start(
                priority=ROW_DMA_PRIORITY)
        return carry

    lax.fori_loop(0, n // pack, body, 0, unroll=8)


def _wait_rows(buf_ref, slot, sem):
    pltpu.make_async_copy(buf_ref.at[slot], buf_ref.at[slot], sem.at[slot]).wait()


def _combine_kernel(*refs, tm, final):
    if final:
        idx_ref, idx_next_ref, ys_ref, x1_ref, mod_ref, gt_ref, g_ref, op_ref, os_ref, ybuf_ref, sem = refs
    else:
        idx_ref, idx_next_ref, ys_ref, x1_ref, mod_ref, gt_ref, o_ref, ybuf_ref, sem = refs
    i = pl.program_id(0)
    last = pl.num_programs(0) - 1
    slot = i % 2
    n_rows = tm * TOP_K

    @pl.when(i == 0)
    def _():
        _start_rows(idx_ref, ys_ref, ybuf_ref, 0, sem, n_rows, TOP_K)

    _wait_rows(ybuf_ref, slot, sem)

    for r in range(n_rows):
        _row_copy(ys_ref, idx_next_ref[0, 0, r], ybuf_ref, 1 - slot, r // TOP_K, r % TOP_K, sem).start(
            priority=ROW_DMA_PRIORITY)

    gate2 = mod_ref[pl.ds(_mod_row(i, tm), 1), 5 * D_MODEL:6 * D_MODEL]
    gt = gt_ref[...]
    f = gt[:, 0:1] * ybuf_ref[slot, :, :D_MODEL] + gt[:, 1:2] * ybuf_ref[slot, :, D_MODEL:]
    x2 = x1_ref[...] + gate2 * f
    if final:
        ms = jnp.mean(x2 * x2, axis=-1, keepdims=True)
        y = x2 * lax.rsqrt(ms + EPS) * g_ref[...]

        @pl.when(i < N_PROMPT_TOK // tm)
        def _():
            op_ref[...] = y

        @pl.when(i >= N_PROMPT_TOK // tm)
        def _():
            os_ref[...] = y
    else:
        o_ref[...] = x2

    @pl.when(i == last)
    def _():
        _wait_rows(ybuf_ref, 1 - slot, sem)


def _combine(x1, mod_l, ys, slots, gates, final_g=None, tm=512):
    final = final_g is not None
    npb = N_PROMPT_TOK // tm
    nblk = N_TOK // tm
    tok = pl.BlockSpec((tm, D_MODEL), lambda i: (i, 0))
    idx3 = slots.reshape(nblk, 1, tm * TOP_K)
    in_specs = [pl.BlockSpec((1, 1, tm * TOP_K), lambda i: (i, 0, 0), memory_space=pltpu.SMEM),
                pl.BlockSpec((1, 1, tm * TOP_K), lambda i: (jnp.minimum(i + 1, nblk - 1), 0, 0),
                             memory_space=pltpu.SMEM),
                pl.BlockSpec(memory_space=pl.ANY),
                tok, pl.BlockSpec((MOD_ROWS, 6 * D_MODEL), lambda i: (0, 0)),
                pl.BlockSpec((tm, TOP_K), lambda i: (i, 0))]
    args = [idx3, idx3, ys, x1, mod_l, gates]
    if final:
        in_specs.append(pl.BlockSpec((1, D_MODEL), lambda i: (0, 0)))
        args.append(final_g.reshape(1, D_MODEL))
        out_shape = (jax.ShapeDtypeStruct((N_PROMPT_TOK, D_MODEL), f32),
                     jax.ShapeDtypeStruct((N_SAMPLE_TOK, D_MODEL), f32))
        out_specs = (pl.BlockSpec((tm, D_MODEL), lambda i: (jnp.minimum(i, npb - 1), 0)),
                     pl.BlockSpec((tm, D_MODEL), lambda i: (jnp.maximum(i - npb, 0), 0)))
    else:
        out_shape = jax.ShapeDtypeStruct((N_TOK, D_MODEL), f32)
        out_specs = tok
    return pl.pallas_call(
        functools.partial(_combine_kernel, tm=tm, final=final),
        out_shape=out_shape,
        grid=(N_TOK // tm,),
        in_specs=in_specs,
        out_specs=out_specs,
        scratch_shapes=[pltpu.VMEM((2, tm, TOP_K * D_MODEL), ys.dtype), pltpu.SemaphoreType.DMA((2,))],
        compiler_params=_cparams(1),
        name="expert_combine",
    )(*args)


def _route(logits, tm):
    eid = jnp.arange(N_EXPERTS, dtype=jnp.int32)[None, :]
    v0 = jnp.max(logits, axis=-1, keepdims=True)
    i0 = jnp.min(jnp.where(logits == v0, eid, N_EXPERTS), axis=-1, keepdims=True)
    rest = jnp.where(eid == i0, -jnp.inf, logits)
    v1 = jnp.max(rest, axis=-1, keepdims=True)
    i1 = jnp.min(jnp.where(rest == v1, eid, N_EXPERTS), axis=-1, keepdims=True)
    gates = jax.nn.softmax(jnp.concatenate([v0, v1], axis=-1), axis=-1)
    flat_e = jnp.concatenate([i0, i1], axis=-1).reshape(-1)
    onehot = (flat_e[:, None] == jnp.arange(N_EXPERTS)[None, :]).astype(jnp.int32)
    csum = jnp.cumsum(onehot, axis=0)
    rank = jnp.take_along_axis(csum, flat_e[:, None], axis=1)[:, 0] - 1
    counts = csum[-1]
    padded = ((counts + tm - 1) // tm) * tm
    pend = jnp.cumsum(padded)
    pstart = pend - padded
    dest = pstart[flat_e] + rank
    n_rows = N_TOK * TOP_K + N_EXPERTS * tm
    tok_of = jnp.arange(N_TOK * TOP_K, dtype=jnp.int32) // TOP_K
    row_token = jnp.zeros((n_rows,), jnp.int32).at[dest].set(tok_of, unique_indices=True)
    blk_start = jnp.arange(n_rows // tm, dtype=jnp.int32) * tm
    blk_expert = jnp.minimum(jnp.sum((blk_start[:, None] >= pend[None, :]).astype(jnp.int32), axis=1),
                             N_EXPERTS - 1)
    n_valid = (pend[-1] // tm).astype(jnp.int32).reshape(1)
    blk_expert = jnp.where(blk_start < pend[-1], blk_expert, blk_expert[jnp.maximum(n_valid[0] - 1, 0)])
    return row_token, blk_expert, n_valid, dest.reshape(N_TOK, TOP_K), gates


def _pool_weight(pool_w_l):
    w = jnp.zeros((GROUP_W, GROUP_W), f32)
    for g in range(len(POOL_WINDOWS)):
        w = w.at[g * POOL_GC:(g + 1) * POOL_GC, g * POOL_GC:(g + 1) * POOL_GC].set(pool_w_l[g])
    return w


def kernel(x_prompt, x_sample, cache_diff_k, cache_diff_v, cache_na_k, cache_na_v, c, c_ctx,
           norm1_g, norm2_g, final_g, ada_w, ada_b, w_in, w_out, diff_lam, diff_subln_g,
           hy_conv, hy_w1, hy_b1, hy_w2, hy_b2, hy_w3, hy_b3, hy_freq, hy_decay, hy_dbias,
           na_rpb, pool_w, pool_scale, ffn_w1, ffn_w3, ffn_w2,
           moe_router, moe_w1, moe_w3, moe_w2):
    cond = jnp.concatenate([c, c_ctx[None, :], jnp.zeros((MOD_ROWS - DEC_BATCH - 1, D_MODEL), f32)], axis=0)
    mod = _ada_mod(cond, ada_w, ada_b)
    cos_np, sin_np = _rope_tables()
    cos_t, sin_t = jnp.asarray(cos_np), jnp.asarray(sin_np)
    dft = {L: tuple(jnp.asarray(m, dtype=bf16) for m in _dft_matrices(L)) for L in (SEQ, DEC_SEQ)}

    P = N_PROMPT_TOK
    x = (x_prompt.reshape(P, D_MODEL), x_sample.reshape(N_SAMPLE_TOK, D_MODEL))
    new_kv = None
    for l in range(DEPTH):
        pa, pb, pc, pd, *new_kv = _inproj(x, l, new_kv, mod[l], norm1_g[l], w_in, cos_t, sin_t)

        lam_init = 0.8 - 0.6 * math.exp(-0.3 * l)
        lv = diff_lam[l].astype(f32)
        lam = (jnp.exp(jnp.sum(lv[0] * lv[1])) - jnp.exp(jnp.sum(lv[2] * lv[3])) + lam_init).reshape(1)
        gain = jnp.tile(diff_subln_g[l], A_HEADS).reshape(1, GROUP_W)
        a_kw = dict(n_heads=A_HEADS, diff=True, scale=A_DQK ** -0.5, lam=lam, gain=gain, out_scale=1.0 - lam_init)
        oa_p = _attention(pa, 0, 0, pa, 1, 0, pa, 2, 0, n_seq=BATCH, seq_len=SEQ, n_keys=SEQ, tq=SEQ, **a_kw)
        s0 = P // DEC_SEQ
        cache_a = (cache_diff_k[:, l].reshape(-1, GROUP_W), cache_diff_v[:, l].reshape(-1, GROUP_W))
        oa_s = _attention(pa, 0, s0, pa, 1, s0, pa, 2, s0, n_seq=DEC_BATCH, seq_len=DEC_SEQ, n_keys=DEC_SEQ,
                          tq=256, cache_kv=cache_a, **a_kw)

        hy = (hy_conv[l], hy_w1[l], hy_b1[l], hy_w2[l], hy_b2[l], hy_w3[l], hy_b3[l],
              hy_freq[l], hy_decay[l], hy_dbias[l])
        ob_p = _hyena(pb, 0, BATCH, SEQ, dft[SEQ], hy)
        ob_s = _hyena(pb, P // DEC_SEQ, DEC_BATCH, DEC_SEQ, dft[DEC_SEQ], hy)

        oc_p = _attention(pc, 0, 0, pc, 1, 0, pc, 2, 0, n_seq=BATCH, seq_len=SEQ, n_keys=SEQ, tq=SEQ,
                          n_heads=C_HEADS, diff=False, scale=C_DH ** -0.5)
        oc_s = _na_attention(pc, cache_na_k[:, l].reshape(-1, GROUP_W), cache_na_v[:, l].reshape(-1, GROUP_W),
                             _na_bias_tables(na_rpb[l]))

        w_bd = _pool_weight(pool_w[l])
        od_p = _pool(pd, w_bd, pool_scale[l], 0, BATCH, SEQ)
        od_s = _pool(pd, w_bd, pool_scale[l], P // DEC_SEQ, DEC_BATCH, DEC_SEQ)

        mixers = [(oa_p, oa_s), (ob_p, ob_s), (oc_p, oc_s), (od_p, od_s)]
        tm = 1024
        if l % 2 == 0:
            assert l != DEPTH - 1
            x1, h2 = _outproj(mixers, x, l, mod[l], norm2_g[l], w_out)
            x = _ffn_dense(h2, ffn_w1[l // 2], ffn_w3[l // 2], ffn_w2[l // 2], x1, mod[l])
        else:
            x1, h2, logits = _outproj(mixers, x, l, mod[l], norm2_g[l], w_out, moe_router[l // 2])
            row_token, blk_expert, n_valid, dest, gates = _route(logits[:, :N_EXPERTS], tm)
            ys = _ffn_routed(h2, row_token, moe_w1[l // 2], moe_w3[l // 2], moe_w2[l // 2], blk_expert, n_valid,
                             tm=tm, tf=512)
            x = _combine(x1, mod[l], ys, dest, gates, final_g if l == DEPTH - 1 else None)

    assert isinstance(x, tuple)
    y_prompt = x[0].reshape(BATCH, SEQ, D_MODEL)
    y_sample = x[1].reshape(DEC_BATCH, DEC_SEQ, D_MODEL)
    kak, kav, kck, kcv = new_kv
    return (y_prompt, y_sample,
            kak.reshape(BATCH, DEPTH, SEQ, A_HEADS, 2 * A_DQK), kav.reshape(BATCH, DEPTH, SEQ, A_HEADS, A_DV),
            kck.reshape(BATCH, DEPTH, SEQ, C_HEADS, C_DH), kcv.reshape(BATCH, DEPTH, SEQ, C_HEADS, C_DH))
```

```python
import functools
import math

import numpy as np
import jax
import jax.numpy as jnp
from jax import lax
from jax.experimental import pallas as pl
from jax.experimental.pallas import tpu as pltpu

D_MODEL = 1024
BATCH = 32
SEQ = 256
DEPTH = 2
DEC_BATCH = 4
DEC_SEQ = 2048
PAST_LEN = 512
GRID_W = 64
GRID_H = DEC_SEQ // GRID_W
GROUP_W = D_MODEL // 4
A_HEADS = 4
A_DQK = GROUP_W // (2 * A_HEADS)
A_DV = GROUP_W // A_HEADS
ROPE_BASE = 10000.0
HY_CH = GROUP_W
HY_ORDER = 2
HY_BANDS = 8
HY_FEAT = 1 + 2 * HY_BANDS
HY_HID = 64
C_HEADS = 4
C_DH = GROUP_W // C_HEADS
NA_KH = 8
NA_KW = 16
POOL_WINDOWS = (2, 4, 8, 16)
POOL_GC = GROUP_W // 4
PROJ_W = 3 * GROUP_W + 3 * HY_CH + 3 * GROUP_W + GROUP_W
N_EXPERTS = 8
TOP_K = 2
EPS = 1e-6
NEG = -1e30
LOG2E = math.log2(math.e)

N_PROMPT_TOK = BATCH * SEQ
N_SAMPLE_TOK = DEC_BATCH * DEC_SEQ
N_TOK = N_PROMPT_TOK + N_SAMPLE_TOK
MOD_ROWS = 8
CTX_ROW = DEC_BATCH
ROUTER_PAD = 128
LANES = 128

NA_QROWS = 4
NA_SLAB_ROWS = 12
NA_GROUPS = GRID_H // NA_QROWS

f32 = jnp.float32
bf16 = jnp.bfloat16

VMEM_LIMIT = 56 * 1024 * 1024
ROW_DMA_PRIORITY = 1
CTX_SEQS_PER_STEP = 4
HY_SPECTRUM_TILE = 512
HY_EPILOGUE_ROWS = 2048
_NT = (((1,), (1,)), ((), ()))


def _cparams(n_axes):
    return pltpu.CompilerParams(
        dimension_semantics=("arbitrary",) * n_axes, vmem_limit_bytes=VMEM_LIMIT)


def _mod_row(i, tm):
    n_prompt_blocks = N_PROMPT_TOK // tm
    blocks_per_seq = DEC_SEQ // tm
    return jnp.where(i < n_prompt_blocks, CTX_ROW, (i - n_prompt_blocks) // blocks_per_seq)


def _split3(a):
    a0 = a.astype(bf16)
    r1 = a - a0.astype(f32)
    a1 = r1.astype(bf16)
    a2 = (r1 - a1.astype(f32)).astype(bf16)
    return a0, a1, a2


def _dot_bf16x3(a, b):
    a0 = a.astype(bf16)
    a1 = (a - a0.astype(f32)).astype(bf16)
    b0 = b.astype(bf16)
    b1 = (b - b0.astype(f32)).astype(bf16)
    d = functools.partial(jnp.dot, preferred_element_type=f32)
    return (d(a1, b0) + d(a0, b1)) + d(a0, b0)


def _dot_f32(a, b):
    a0, a1, a2 = _split3(a)
    b0, b1, b2 = _split3(b)
    d = functools.partial(jnp.dot, preferred_element_type=f32)
    return ((d(a2, b0) + d(a1, b1) + d(a0, b2)) + (d(a1, b0) + d(a0, b1))) + d(a0, b0)


def _lane_group(shape, width):
    return lax.shift_right_logical(lax.broadcasted_iota(jnp.int32, shape, 1), int(math.log2(width)))


def _shift_rows(x, d):
    n = x.shape[0]
    r = pltpu.roll(x, d % n, 0)
    row = lax.broadcasted_iota(jnp.int32, x.shape, 0)
    keep = (row >= d) if d > 0 else (row < n + d)
    return jnp.where(keep, r, 0.0)


def _ada_kernel(cond_ref, w_ref, b_ref, o_ref):
    c = cond_ref[...]
    s = c * jax.nn.sigmoid(c)
    o_ref[0] = jnp.dot(s.astype(bf16), w_ref[0].astype(bf16), preferred_element_type=f32) + b_ref[0]


def _ada_mod(cond, ada_w, ada_b):
    tn = 1536
    n6 = 6 * D_MODEL
    return pl.pallas_call(
        _ada_kernel,
        out_shape=jax.ShapeDtypeStruct((DEPTH, MOD_ROWS, n6), f32),
        grid=(DEPTH, n6 // tn),
        in_specs=[
            pl.BlockSpec((MOD_ROWS, D_MODEL), lambda l, j: (0, 0)),
            pl.BlockSpec((1, D_MODEL, tn), lambda l, j: (l, 0, j)),
            pl.BlockSpec((1, 1, tn), lambda l, j: (l, 0, j)),
        ],
        out_specs=pl.BlockSpec((1, MOD_ROWS, tn), lambda l, j: (l, 0, j)),
        compiler_params=_cparams(2),
        name="ada_mod",
    )(cond, ada_w, ada_b.reshape(DEPTH, 1, n6))


def _rope_tables():
    t = np.arange(DEC_SEQ)
    nf = A_DQK // 4
    inv = ROPE_BASE ** (-np.arange(nf, dtype=np.float64) / nf)
    ar = (t // GRID_W)[:, None] * inv
    ac = (t % GRID_W)[:, None] * inv
    cos = np.concatenate([np.cos(ar), np.cos(ar), np.cos(ac), np.cos(ac)], axis=1)
    sin = np.concatenate([-np.sin(ar), np.sin(ar), -np.sin(ac), np.sin(ac)], axis=1)
    reps = GROUP_W // A_DQK
    return (np.tile(cos, (1, reps)).astype(np.float32), np.tile(sin, (1, reps)).astype(np.float32))


def _modulated_norm(x, g, shift, scale):
    ms = jnp.mean(x * x, axis=-1, keepdims=True)
    return (x * lax.rsqrt(ms + EPS) * g) * (1.0 + scale) + shift


def _inproj_kernel(*refs, tm, split_x, n_alias):
    n_x = 2 if split_x else 1
    x_refs = refs[:n_x]
    mod_ref, g_ref, w_ref, cos_ref, sin_ref = refs[n_x:n_x + 5]
    outs = refs[n_x + 5 + n_alias:]
    pa_ref, pb_ref, pc_ref, pd_ref, kak_ref, kav_ref, kck_ref, kcv_ref, wbf_ref = outs
    i = pl.program_id(0)
    is_prompt = i < N_PROMPT_TOK // tm

    @pl.when(i == 0)
    def _():
        wbf_ref[...] = w_ref[0].astype(bf16)

    row = _mod_row(i, tm)
    shift = mod_ref[pl.ds(row, 1), 0:D_MODEL]
    scale = mod_ref[pl.ds(row, 1), D_MODEL:2 * D_MODEL]
    x = jnp.where(is_prompt, x_refs[0][...], x_refs[1][...]) if split_x else x_refs[0][...]
    h = _modulated_norm(x, g_ref[...], shift, scale)
    proj = jnp.dot(h.astype(bf16), wbf_ref[...], preferred_element_type=f32)
    w3 = 3 * GROUP_W
    pb_ref[...] = proj[:, w3:2 * w3]
    pc_ref[...] = proj[:, 2 * w3:3 * w3]
    pd_ref[...] = proj[:, 3 * w3:]
    pa_ref[:, 2 * GROUP_W:] = proj[:, 2 * GROUP_W:w3]

    @pl.when(is_prompt)
    def _():
        pa_ref[:, :2 * GROUP_W] = proj[:, :2 * GROUP_W]
        kv_shape = (tm // SEQ, 1, SEQ, GROUP_W)
        cols = (GROUP_W, 2 * GROUP_W, 2 * w3 + GROUP_W, 2 * w3 + 2 * GROUP_W)
        for ref, c0 in zip((kak_ref, kav_ref, kck_ref, kcv_ref), cols):
            ref[:, 0:1] = proj[:, c0:c0 + GROUP_W].reshape(kv_shape)
            if ref.shape[1] > 1:
                ref[:, 1:] = jnp.zeros((tm // SEQ, ref.shape[1] - 1, SEQ, GROUP_W), f32)

    @pl.when(i >= N_PROMPT_TOK // tm)
    def _():
        cos = cos_ref[...]
        sin = sin_ref[...]
        lane = lax.broadcasted_iota(jnp.int32, (tm, GROUP_W), 1)
        first = (lane % 16) < 8
        for s in range(2):
            v = proj[:, s * GROUP_W:(s + 1) * GROUP_W]
            partner = jnp.where(first, pltpu.roll(v, GROUP_W - 8, 1), pltpu.roll(v, 8, 1))
            pa_ref[:, s * GROUP_W:(s + 1) * GROUP_W] = v * cos + partner * sin


def _inproj(x, layer, kv_prev, mod_l, g, w_in, cos_t, sin_t, tm=512):
    npb = N_PROMPT_TOK // tm
    blocks_per_seq = DEC_SEQ // tm
    w3 = 3 * GROUP_W
    split_x = isinstance(x, tuple)

    def rope_idx(i):
        return (jnp.maximum(i - npb, 0) % blocks_per_seq, 0)

    if split_x:
        x_specs = [pl.BlockSpec((tm, D_MODEL), lambda i: (jnp.minimum(i, npb - 1), 0)),
                   pl.BlockSpec((tm, D_MODEL), lambda i: (jnp.maximum(i - npb, 0), 0))]
        x_args = list(x)
    else:
        x_specs = [pl.BlockSpec((tm, D_MODEL), lambda i: (i, 0))]
        x_args = [x]
    kv_args = list(kv_prev) if kv_prev is not None else []
    n_in = len(x_args) + 5
    kv_shape = jax.ShapeDtypeStruct((BATCH, DEPTH, SEQ, GROUP_W), f32)
    kv_layers = DEPTH if layer == 0 else 1
    kv_spec = pl.BlockSpec((tm // SEQ, kv_layers, SEQ, GROUP_W), lambda i: (jnp.minimum(i, npb - 1), layer, 0, 0))
    tok = lambda w: pl.BlockSpec((tm, w), lambda i: (i, 0))
    return pl.pallas_call(
        functools.partial(_inproj_kernel, tm=tm, split_x=split_x, n_alias=len(kv_args)),
        out_shape=(jax.ShapeDtypeStruct((N_TOK, w3), f32), jax.ShapeDtypeStruct((N_TOK, w3), f32),
                   jax.ShapeDtypeStruct((N_TOK, w3), f32), jax.ShapeDtypeStruct((N_TOK, GROUP_W), f32),
                   kv_shape, kv_shape, kv_shape, kv_shape),
        grid=(N_TOK // tm,),
        in_specs=x_specs + [
            pl.BlockSpec((MOD_ROWS, 6 * D_MODEL), lambda i: (0, 0)),
            pl.BlockSpec((1, D_MODEL), lambda i: (0, 0)),
            pl.BlockSpec((1, D_MODEL, PROJ_W), lambda i: (layer, 0, 0)),
            pl.BlockSpec((tm, GROUP_W), rope_idx),
            pl.BlockSpec((tm, GROUP_W), rope_idx),
        ] + [pl.BlockSpec(memory_space=pl.ANY)] * len(kv_args),
        out_specs=(tok(w3), tok(w3), tok(w3), tok(GROUP_W), kv_spec, kv_spec, kv_spec, kv_spec),
        scratch_shapes=[pltpu.VMEM((D_MODEL, PROJ_W), bf16)],
        input_output_aliases={n_in + j: 4 + j for j in range(len(kv_args))},
        compiler_params=_cparams(1),
        name="norm1_inproj",
    )(*x_args, mod_l, g.reshape(1, D_MODEL), w_in, cos_t, sin_t, *kv_args)


def _values_with_ones(v, vhead, vlane, h, hw):
    ones_col = ((h + 1) * hw) % GROUP_W
    vm = jnp.where(vhead == h, v, jnp.where(vlane == ones_col, 1.0, 0.0))
    return vm.astype(bf16), ones_col


def _attn_kernel(*refs, n_heads, diff, scale, out_scale, cached, seqs):
    if cached:
        lam_ref, q_ref, k_ref, v_ref, kx_ref, vx_ref, g_ref, o_ref = refs
    else:
        lam_ref, q_ref, k_ref, v_ref, g_ref, o_ref = refs
    tq = q_ref.shape[0] // seqs
    nk = k_ref.shape[0] // seqs
    for sq_i in range(seqs):
        qrows = slice(sq_i * tq, (sq_i + 1) * tq)
        krows = slice(sq_i * nk, (sq_i + 1) * nk)
        if cached:
            k = jnp.concatenate([k_ref[krows, :], kx_ref[...]], axis=0).astype(bf16)
            v = jnp.concatenate([v_ref[krows, :], vx_ref[...]], axis=0)
        else:
            k = k_ref[krows, :].astype(bf16)
            v = v_ref[krows, :]
        acc = _attend(q_ref[qrows, :], k, v, lam_ref, g_ref, n_heads=n_heads, diff=diff, scale=scale,
                      out_scale=out_scale)
        o_ref[qrows, :] = acc.astype(o_ref.dtype)


def _attend(q, k, v, lam_ref, g_ref, *, n_heads, diff, scale, out_scale):
    tq = q.shape[0]
    n = k.shape[0]
    q = q * (scale * LOG2E)
    hw = GROUP_W // n_heads
    n_maps = 2 if diff else 1
    qgrp = _lane_group((tq, GROUP_W), hw // n_maps)
    qhead = _lane_group((tq, GROUP_W), hw)
    vhead = _lane_group((n, GROUP_W), hw)
    vlane = lax.broadcasted_iota(jnp.int32, (n, GROUP_W), 1)
    acc = jnp.zeros((tq, GROUP_W), f32)
    for h in range(n_heads):
        vm, ones_col = _values_with_ones(v, vhead, vlane, h, hw)
        maps = []
        for m in range(n_maps):
            qm = jnp.where(qgrp == n_maps * h + m, q, 0.0).astype(bf16)
            s = lax.dot_general(qm, k, _NT, preferred_element_type=f32)
            e = jnp.exp2(s - jnp.max(s, axis=-1, keepdims=True)).astype(bf16)
            o = jnp.dot(e, vm, preferred_element_type=f32)
            maps.append(o * (1.0 / o[:, ones_col:ones_col + 1]))
        oh = maps[0] - lam_ref[0] * maps[1] if diff else maps[0]
        acc = acc + jnp.where(qhead == h, oh, 0.0)
    if diff:
        r = lax.shift_right_logical(lax.broadcasted_iota(jnp.int32, (GROUP_W, GROUP_W), 0), 6)
        c = lax.shift_right_logical(lax.broadcasted_iota(jnp.int32, (GROUP_W, GROUP_W), 1), 6)
        bd = jnp.where(r == c, 1.0, 0.0).astype(bf16)
        sq = acc * acc
        hi = sq.astype(bf16)
        lo = (sq - hi.astype(f32)).astype(bf16)
        ms = (jnp.dot(hi, bd, preferred_element_type=f32) + jnp.dot(lo, bd, preferred_element_type=f32)) * (1.0 / A_DV)
        acc = (acc * lax.rsqrt(ms + EPS) * g_ref[...]) * out_scale
    return acc


def _attention(q_src, q_col, q_row0, k_src, k_col, k_row0, v_src, v_col, v_row0, *,
               n_seq, seq_len, n_keys, tq, n_heads, diff, scale, lam=None, gain=None, out_scale=1.0,
               cache_kv=None, seqs=1):
    qb = seq_len // tq
    assert seqs == 1 or (qb == 1 and cache_kv is None and n_seq % seqs == 0
                         and q_row0 % seqs == 0 and k_row0 % seqs == 0 and v_row0 % seqs == 0)
    if lam is None:
        lam = jnp.zeros((1,), f32)
    if gain is None:
        gain = jnp.ones((1, GROUP_W), f32)
    in_specs = [
        pl.BlockSpec(memory_space=pltpu.SMEM),
        pl.BlockSpec((seqs * tq, GROUP_W), lambda b, i: ((q_row0 // seqs + b) * qb + i, q_col)),
        pl.BlockSpec((seqs * n_keys, GROUP_W), lambda b, i: (k_row0 // seqs + b, k_col)),
        pl.BlockSpec((seqs * n_keys, GROUP_W), lambda b, i: (v_row0 // seqs + b, v_col)),
    ]
    args = [lam, q_src, k_src, v_src]
    if cache_kv is not None:
        n_cached = cache_kv[0].shape[0] // n_seq
        in_specs += [pl.BlockSpec((n_cached, GROUP_W), lambda b, i: (b, 0))] * 2
        args += list(cache_kv)
    in_specs.append(pl.BlockSpec((1, GROUP_W), lambda b, i: (0, 0)))
    args.append(gain)
    return pl.pallas_call(
        functools.partial(_attn_kernel, n_heads=n_heads, diff=diff, scale=scale, out_scale=out_scale,
                          cached=cache_kv is not None, seqs=seqs),
        out_shape=jax.ShapeDtypeStruct((n_seq * seq_len, GROUP_W), bf16),
        grid=(n_seq // seqs, qb),
        in_specs=in_specs,
        out_specs=pl.BlockSpec((seqs * tq, GROUP_W), lambda b, i: (b * qb + i, 0)),
        compiler_params=_cparams(2),
        name="diff_attention" if diff else "softmax_attention",
    )(*args)


def _na_group_geometry(g):
    r0 = g * NA_QROWS
    slab0 = min(max(r0 - NA_KH // 2, 0), GRID_H - NA_SLAB_ROWS)
    return r0, slab0


def _na_bias_tables(rpb):
    c = np.arange(GRID_W)
    ws = np.clip(c - NA_KW // 2, 0, GRID_W - NA_KW)
    kc = np.arange(GRID_W)
    col_ok = (kc[None, :] >= ws[:, None]) & (kc[None, :] < ws[:, None] + NA_KW)
    pad = GRID_W - NA_KW
    rp = jnp.pad(rpb.astype(f32) * LOG2E, ((0, 0), (0, 0), (pad, pad)))
    tc = jnp.stack([rp[:, :, GRID_W - 1 - ci:2 * GRID_W - 1 - ci] for ci in range(GRID_W)], axis=2)
    tc = jnp.where(col_ok[None, None], tc, NEG)
    neg_blk = jnp.full((C_HEADS, GRID_W, GRID_W), NEG, f32)
    tables = []
    for g in (0, 1, NA_GROUPS - 1):
        r0, slab0 = _na_group_geometry(g)
        rows = []
        for rq in range(NA_QROWS):
            r = r0 + rq
            rs = min(max(r - NA_KH // 2, 0), GRID_H - NA_KH)
            blks = []
            for kl in range(NA_SLAB_ROWS):
                kr = slab0 + kl
                blks.append(tc[:, kr - r + NA_KH - 1] if rs <= kr < rs + NA_KH else neg_blk)
            rows.append(jnp.concatenate(blks, axis=-1))
        tables.append(jnp.concatenate(rows, axis=-2))
    return jnp.stack(tables, axis=0)


def _na_kernel(q_ref, k_ref, v_ref, kx_ref, vx_ref, bias_ref, o_ref):
    g = pl.program_id(1)
    tq = NA_QROWS * GRID_W
    ns = NA_SLAB_ROWS * GRID_W
    slab0 = jnp.clip(g * NA_QROWS - NA_KH // 2, 0, GRID_H - NA_SLAB_ROWS)
    start = pl.multiple_of(slab0 * GRID_W, GRID_W)
    q = q_ref[...] * (C_DH ** -0.5 * LOG2E)
    ks = k_ref[pl.ds(start, ns), :].astype(bf16)
    vs = v_ref[pl.ds(start, ns), :]
    kx = kx_ref[...].astype(bf16)
    vx = vx_ref[...]
    qhead = _lane_group((tq, GROUP_W), C_DH)
    vshead = _lane_group((ns, GROUP_W), C_DH)
    vslane = lax.broadcasted_iota(jnp.int32, (ns, GROUP_W), 1)
    vxhead = _lane_group((PAST_LEN, GROUP_W), C_DH)
    vxlane = lax.broadcasted_iota(jnp.int32, (PAST_LEN, GROUP_W), 1)
    acc = jnp.zeros((tq, GROUP_W), f32)
    for h in range(C_HEADS):
        qm = jnp.where(qhead == h, q, 0.0).astype(bf16)
        sl = lax.dot_general(qm, ks, _NT, preferred_element_type=f32)
        b = bias_ref[0, h]
        sl = jnp.where(b > 0.5 * NEG, sl + b, NEG)
        sx = lax.dot_general(qm, kx, _NT, preferred_element_type=f32)
        mx = jnp.maximum(jnp.max(sl, axis=-1, keepdims=True), jnp.max(sx, axis=-1, keepdims=True))
        el = jnp.exp2(sl - mx).astype(bf16)
        ex = jnp.exp2(sx - mx).astype(bf16)
        vsm, ones_col = _values_with_ones(vs, vshead, vslane, h, C_DH)
        vxm, _ = _values_with_ones(vx, vxhead, vxlane, h, C_DH)
        o = jnp.dot(el, vsm, preferred_element_type=f32) + jnp.dot(ex, vxm, preferred_element_type=f32)
        acc = acc + jnp.where(qhead == h, o * (1.0 / o[:, ones_col:ones_col + 1]), 0.0)
    o_ref[...] = acc.astype(o_ref.dtype)


def _na_attention(pc, kx, vx, bias):
    tq = NA_QROWS * GRID_W
    ns = NA_SLAB_ROWS * GRID_W
    q_blk0 = N_PROMPT_TOK // tq
    s_blk0 = N_PROMPT_TOK // DEC_SEQ

    def bias_idx(b, g):
        return (jnp.where(g == 0, 0, jnp.where(g == NA_GROUPS - 1, 2, 1)), 0, 0, 0)

    return pl.pallas_call(
        _na_kernel,
        out_shape=jax.ShapeDtypeStruct((N_SAMPLE_TOK, GROUP_W), bf16),
        grid=(DEC_BATCH, NA_GROUPS),
        in_specs=[
            pl.BlockSpec((tq, GROUP_W), lambda b, g: (q_blk0 + b * NA_GROUPS + g, 0)),
            pl.BlockSpec((DEC_SEQ, GROUP_W), lambda b, g: (s_blk0 + b, 1)),
            pl.BlockSpec((DEC_SEQ, GROUP_W), lambda b, g: (s_blk0 + b, 2)),
            pl.BlockSpec((PAST_LEN, GROUP_W), lambda b, g: (b, 0)),
            pl.BlockSpec((PAST_LEN, GROUP_W), lambda b, g: (b, 0)),
            pl.BlockSpec((1, C_HEADS, tq, ns), bias_idx),
        ],
        out_specs=pl.BlockSpec((tq, GROUP_W), lambda b, g: (b * NA_GROUPS + g, 0)),
        compiler_params=_cparams(2),
        name="neighbourhood_attention",
    )(pc, pc, pc, kx, vx, bias)


def _dft_matrices(L):
    n = 2 * L
    k = np.arange(L)[:, None]
    s = np.arange(L)[None, :]
    ang = 2.0 * np.pi * ((k * s) % n) / n
    cos, sin = np.cos(ang), np.sin(ang)
    sin[0, :] = (-1.0) ** np.arange(L)
    fwd = np.concatenate([cos, sin], axis=0)
    wk = np.where(np.arange(L) == 0, 1.0, 2.0)[None, :]
    inv = np.concatenate([cos.T * wk, sin.T * wk], axis=1) / n
    inv[:, L] = ((-1.0) ** np.arange(L)) / n
    return fwd, inv


def _hy_filter_kernel(w1_ref, b1_ref, w2_ref, b2_ref, w3_ref, b3_ref, fr_ref, dec_ref, fa_ref, fb_ref,
                      p_ref, q_ref, r_ref, g_ref, nrm_ref, *, L, tk):
    s = pl.program_id(0)

    @pl.when(s == 0)
    def _():
        row = lax.broadcasted_iota(jnp.int32, (L, LANES), 0)
        lane = lax.broadcasted_iota(jnp.int32, (L, LANES), 1)
        t = row.astype(f32) / L
        band = jnp.where(lane <= HY_BANDS, lane, lane - HY_BANDS).astype(f32)
        ang = (2.0 * math.pi * band) * t
        feat = jnp.where(lane == 0, t, jnp.where(lane <= HY_BANDS, jnp.sin(ang),
                                                 jnp.where(lane <= 2 * HY_BANDS, jnp.cos(ang), 0.0)))
        z = jnp.sin(fr_ref[0:1, :] * (_dot_f32(feat, w1_ref[...]) + b1_ref[...]))
        z = jnp.sin(fr_ref[1:2, :] * (_dot_f32(z, w2_ref[...]) + b2_ref[...]))
        z = _dot_f32(z, w3_ref[...]) + b3_ref[...]
        wide = (L, HY_ORDER * 2 * HY_CH)
        tw = lax.broadcasted_iota(jnp.int32, wide, 0).astype(f32) / L
        taps = z * jnp.exp(-tw * jnp.abs(dec_ref[...]))
        bwd = (_lane_group(wide, HY_CH) % 2) == 1
        first = lax.broadcasted_iota(jnp.int32, wide, 0) == 0
        taps = jnp.where(first, jnp.where(bwd, 0.0, taps), taps)
        g_ref[...] = taps.astype(bf16)
        nrm_ref[...] = jnp.sum(jnp.abs(taps), axis=0, keepdims=True)

    ga = jnp.dot(fa_ref[...], g_ref[...], preferred_element_type=f32)
    gb = jnp.dot(fb_ref[...], g_ref[...], preferred_element_type=f32)
    top = (lax.broadcasted_iota(jnp.int32, (tk, HY_CH), 0) + s * tk) == 0
    for o in range(HY_ORDER):
        c0 = o * 2 * HY_CH
        inv = 1.0 / (nrm_ref[:, c0:c0 + HY_CH] + nrm_ref[:, c0 + HY_CH:c0 + 2 * HY_CH])
        hc = (ga[:, c0:c0 + HY_CH] + ga[:, c0 + HY_CH:c0 + 2 * HY_CH]) * inv
        bf_, bb_ = gb[:, c0:c0 + HY_CH], gb[:, c0 + HY_CH:c0 + 2 * HY_CH]
        hs = jnp.where(top, bf_ + bb_, bf_ - bb_) * inv
        oc = slice(o * HY_CH, (o + 1) * HY_CH)
        p_ref[:, oc] = hc
        q_ref[:, oc] = jnp.where(top, 0.0, hs)
        r_ref[:, oc] = jnp.where(top, hs, hc)


def _hy_filter_spectra(L, fwd_bf, w1, b1, w2, b2, w3, b3, freq, decay, tk=256):
    nk = L // tk
    wide = HY_ORDER * 2 * HY_CH
    full = lambda a: pl.BlockSpec(a.shape, lambda s: (0,) * a.ndim)
    w1p = jnp.pad(w1, ((0, LANES - HY_FEAT), (0, 0)))
    args = [w1p, b1.reshape(1, HY_HID), w2, b2.reshape(1, HY_HID), w3, b3.reshape(1, wide), freq,
            decay.reshape(1, wide)]
    out = jax.ShapeDtypeStruct((L, HY_ORDER * HY_CH), f32)
    plane = pl.BlockSpec((tk, HY_ORDER * HY_CH), lambda s: (s, 0))
    return pl.pallas_call(
        functools.partial(_hy_filter_kernel, L=L, tk=tk),
        out_shape=(out, out, out),
        grid=(nk,),
        in_specs=[full(a) for a in args] + [pl.BlockSpec((tk, L), lambda s: (s, 0)),
                                            pl.BlockSpec((tk, L), lambda s: (s + nk, 0))],
        out_specs=(plane, plane, plane),
        scratch_shapes=[pltpu.VMEM((L, wide), bf16), pltpu.VMEM((1, wide), f32)],
        compiler_params=_cparams(1),
        name="hyena_filter_spectra",
    )(*args, fwd_bf, fwd_bf)


def _hy_pre_kernel(u_ref, w_ref, x1_ref, x2_ref, v_ref, vbf_ref):
    u = u_ref[...]
    w = w_ref[...]
    y = _shift_rows(u, 1) * w[0:1, :] + u * w[1:2, :] + _shift_rows(u, -1) * w[2:3, :]
    x1_ref[...] = y[:, :HY_CH]
    x2_ref[...] = y[:, HY_CH:2 * HY_CH]
    v = y[:, 2 * HY_CH:]
    v_ref[...] = v
    vbf_ref[...] = v.astype(bf16)


def _hy_pre(pb, conv_w, seq0, n_seq, L):
    n = n_seq * L
    blk = pl.BlockSpec((L, HY_CH), lambda b: (b, 0))
    o32 = jax.ShapeDtypeStruct((n, HY_CH), f32)
    return pl.pallas_call(
        _hy_pre_kernel,
        out_shape=(o32, o32, o32, jax.ShapeDtypeStruct((n, HY_CH), bf16)),
        grid=(n_seq,),
        in_specs=[pl.BlockSpec((L, 3 * HY_CH), lambda b: (seq0 + b, 0)),
                  pl.BlockSpec((3, 3 * HY_CH), lambda b: (0, 0))],
        out_specs=(blk, blk, blk, blk),
        compiler_params=_cparams(1),
        name="hyena_short_conv",
    )(pb, conv_w)


def _hy_conv_kernel(*refs, n_seq, L, nk, te, n_out):
    (zbf_ref, z_ref, m_ref, fa_ref, fb_ref, ic_ref, is_ref, p_ref, q_ref, r_ref, db_ref) = refs[:11]
    out_refs = refs[11:11 + n_out]
    acc_ref = refs[11 + n_out]
    s = pl.program_id(0)

    @pl.when(s == 0)
    def _():
        acc_ref[...] = jnp.zeros_like(acc_ref)

    @pl.when(s < nk)
    def _():
        fa, fb, ic, isn = fa_ref[...], fb_ref[...], ic_ref[...], is_ref[...]
        p, q, r = p_ref[...], q_ref[...], r_ref[...]
        for b in range(n_seq):
            rows = slice(b * L, (b + 1) * L)
            zb = zbf_ref[rows, :]
            a = jnp.dot(fa, zb, preferred_element_type=f32)
            bb = jnp.dot(fb, zb, preferred_element_type=f32)
            yc = (a * p - bb * q).astype(bf16)
            ys = (a * q + bb * r).astype(bf16)
            acc_ref[rows, :] += (jnp.dot(ic, yc, preferred_element_type=f32)
                                 + jnp.dot(isn, ys, preferred_element_type=f32))

    @pl.when(s >= nk)
    def _():
        start = pl.multiple_of((s - nk) * te, te)
        y = acc_ref[pl.ds(start, te), :]
        res = m_ref[...] * (y + db_ref[...] * z_ref[...])
        for o_ref in out_refs:
            o_ref[...] = res.astype(o_ref.dtype)


def _hy_longconv(zbf, z, mult, fwd_bf, inv_bf, planes, order, dbias_o, out_dtypes, n_seq, L, tk):
    nk = L // tk
    n = n_seq * L
    te = max(L, HY_EPILOGUE_ROWS)
    assert n % te == 0
    kt = lambda s: jnp.minimum(s, nk - 1)
    ep = lambda s: (jnp.maximum(s - nk, 0), 0)
    plane = pl.BlockSpec((tk, HY_CH), lambda s: (kt(s), order))
    return pl.pallas_call(
        functools.partial(_hy_conv_kernel, n_seq=n_seq, L=L, nk=nk, te=te, n_out=len(out_dtypes)),
        out_shape=tuple(jax.ShapeDtypeStruct((n, HY_CH), dt) for dt in out_dtypes),
        grid=(nk + n // te,),
        in_specs=[
            pl.BlockSpec((n, HY_CH), lambda s: (0, 0)),
            pl.BlockSpec((te, HY_CH), ep),
            pl.BlockSpec((te, HY_CH), ep),
            pl.BlockSpec((tk, L), lambda s: (kt(s), 0)),
            pl.BlockSpec((tk, L), lambda s: (kt(s) + nk, 0)),
            pl.BlockSpec((L, tk), lambda s: (0, kt(s))),
            pl.BlockSpec((L, tk), lambda s: (0, kt(s) + nk)),
            plane, plane, plane,
            pl.BlockSpec((1, HY_CH), lambda s: (0, 0)),
        ],
        out_specs=tuple(pl.BlockSpec((te, HY_CH), ep) for _ in out_dtypes),
        scratch_shapes=[pltpu.VMEM((n, HY_CH), f32)],
        compiler_params=_cparams(1),
        name="hyena_longconv",
    )(zbf, z, mult, fwd_bf, fwd_bf, inv_bf, inv_bf, *planes, dbias_o.reshape(1, HY_CH))


def _hyena(pb, seq0, n_seq, L, dft, hy):
    conv_w, w1, b1, w2, b2, w3, b3, freq, decay, dbias = hy
    fwd_bf, inv_bf = dft
    tk = min(L, HY_SPECTRUM_TILE)
    planes = _hy_filter_spectra(L, fwd_bf, w1, b1, w2, b2, w3, b3, freq, decay, tk=tk)
    x1, x2, v, vbf = _hy_pre(pb, conv_w, seq0, n_seq, L)
    z, zbf = _hy_longconv(vbf, v, x1, fwd_bf, inv_bf, planes, 0, dbias[0], (f32, bf16), n_seq, L, tk)
    (ob,) = _hy_longconv(zbf, z, x2, fwd_bf, inv_bf, planes, 1, dbias[1], (bf16,), n_seq, L, tk)
    return ob


def _pool_kernel(u_ref, w_ref, sc_ref, o_ref):
    u = u_ref[...]
    L = u.shape[0]
    back = _shift_rows(u, 1)
    fwd = u
    sums = [back + fwd]
    for k in (1, 2, 4):
        back = back + _shift_rows(back, k)
        fwd = fwd + _shift_rows(fwd, -k)
        sums.append(back + fwd)
    a2, a4, a8, a16 = sums
    grp = _lane_group(u.shape, POOL_GC)
    t = lax.broadcasted_iota(jnp.int32, u.shape, 0)
    half = jnp.left_shift(1, grp)
    cnt = jnp.minimum(t + half, L) - jnp.maximum(t - half, 0)
    tot = jnp.where(grp == 0, a2, jnp.where(grp == 1, a4, jnp.where(grp == 2, a8, a16)))
    pooled = tot / cnt.astype(f32) - u
    y = jnp.dot(pooled.astype(bf16), w_ref[...].astype(bf16), preferred_element_type=f32)
    o_ref[...] = (y * sc_ref[...]).astype(o_ref.dtype)


def _pool(pd, w_bd, scale, seq0, n_seq, L):
    return pl.pallas_call(
        _pool_kernel,
        out_shape=jax.ShapeDtypeStruct((n_seq * L, GROUP_W), bf16),
        grid=(n_seq,),
        in_specs=[pl.BlockSpec((L, GROUP_W), lambda b: (seq0 + b, 0)),
                  pl.BlockSpec((GROUP_W, GROUP_W), lambda b: (0, 0)),
                  pl.BlockSpec((1, GROUP_W), lambda b: (0, 0))],
        out_specs=pl.BlockSpec((L, GROUP_W), lambda b: (b, 0)),
        compiler_params=_cparams(1),
        name="pool_mixer",
    )(pd, w_bd, scale.reshape(1, GROUP_W))


def _outproj_kernel(*refs, tm, moe, split_x):
    mix_refs = refs[:8]
    n_x = 2 if split_x else 1
    x_refs = refs[8:8 + n_x]
    if moe:
        mod_ref, g_ref, w_ref, r_ref, x1_ref, h2_ref, lg_ref, wbf_ref = refs[8 + n_x:]
    else:
        mod_ref, g_ref, w_ref, x1_ref, h2_ref, wbf_ref = refs[8 + n_x:]
    i = pl.program_id(0)

    @pl.when(i == 0)
    def _():
        wbf_ref[...] = w_ref[0].astype(bf16)

    row = _mod_row(i, tm)
    d = D_MODEL
    gate1 = mod_ref[pl.ds(row, 1), 2 * d:3 * d]
    shift2 = mod_ref[pl.ds(row, 1), 3 * d:4 * d]
    scale2 = mod_ref[pl.ds(row, 1), 4 * d:5 * d]
    is_prompt = i < N_PROMPT_TOK // tm
    mixed = jnp.concatenate(
        [jnp.where(is_prompt, mix_refs[2 * j][...], mix_refs[2 * j + 1][...]) for j in range(4)], axis=-1)
    mix = jnp.dot(mixed, wbf_ref[...], preferred_element_type=f32)
    x = jnp.where(is_prompt, x_refs[0][...], x_refs[1][...]) if split_x else x_refs[0][...]
    x1 = x + gate1 * mix
    x1_ref[...] = x1
    h = _modulated_norm(x1, g_ref[...], shift2, scale2)
    h2_ref[...] = h.astype(h2_ref.dtype)
    if moe:
        lg_ref[...] = _dot_bf16x3(h, r_ref[...])


def _outproj(mixers, x, layer, mod_l, g, w_out, router_l=None, tm=512):
    moe = router_l is not None
    npb = N_PROMPT_TOK // tm
    tok = lambda w: pl.BlockSpec((tm, w), lambda i: (i, 0))
    in_specs, args = [], []
    for op, os_ in mixers:
        in_specs.append(pl.BlockSpec((tm, GROUP_W), lambda i: (jnp.minimum(i, npb - 1), 0)))
        in_specs.append(pl.BlockSpec((tm, GROUP_W), lambda i: (jnp.maximum(i - npb, 0), 0)))
        args += [op, os_]
    split_x = isinstance(x, tuple)
    if split_x:
        in_specs += [pl.BlockSpec((tm, D_MODEL), lambda i: (jnp.minimum(i, npb - 1), 0)),
                     pl.BlockSpec((tm, D_MODEL), lambda i: (jnp.maximum(i - npb, 0), 0))]
        args += list(x)
    else:
        in_specs.append(tok(D_MODEL))
        args.append(x)
    in_specs += [pl.BlockSpec((MOD_ROWS, 6 * D_MODEL), lambda i: (0, 0)),
                 pl.BlockSpec((1, D_MODEL), lambda i: (0, 0)),
                 pl.BlockSpec((1, D_MODEL, D_MODEL), lambda i: (layer, 0, 0))]
    args += [mod_l, g.reshape(1, D_MODEL), w_out]
    out_shape = [jax.ShapeDtypeStruct((N_TOK, D_MODEL), f32),
                 jax.ShapeDtypeStruct((N_TOK, D_MODEL), f32 if moe else bf16)]
    out_specs = [tok(D_MODEL), tok(D_MODEL)]
    if moe:
        in_specs.append(pl.BlockSpec((D_MODEL, ROUTER_PAD), lambda i: (0, 0)))
        args.append(jnp.pad(router_l, ((0, 0), (0, ROUTER_PAD - N_EXPERTS))))
        out_shape.append(jax.ShapeDtypeStruct((N_TOK, ROUTER_PAD), f32))
        out_specs.append(tok(ROUTER_PAD))
    return pl.pallas_call(
        functools.partial(_outproj_kernel, tm=tm, moe=moe, split_x=split_x),
        out_shape=tuple(out_shape),
        grid=(N_TOK // tm,),
        in_specs=in_specs,
        out_specs=tuple(out_specs),
        scratch_shapes=[pltpu.VMEM((D_MODEL, D_MODEL), bf16)],
        compiler_params=_cparams(1),
        name="outproj_norm2",
    )(*args)


def _ffn_kernel(be_ref, nv_ref, idx_ref, idx_next_ref, src_ref, w1_ref, w3_ref, w2_ref, o_ref,
                xbf_ref, xbuf_ref, sem, *, tm, nf):
    i = pl.program_id(0)
    j = pl.program_id(1)
    last = pl.num_programs(0) - 1
    live = i < nv_ref[0]
    slot = i % 2
    share = (tm // nf) // 8 * 8
    head = tm - nf * share

    @pl.when(j == 0)
    def _():
        o_ref[...] = jnp.zeros_like(o_ref)

    @pl.when((j == 0) & (i == 0))
    def _():
        _start_rows(idx_ref, src_ref, xbuf_ref, 0, sem, tm)

    @pl.when((j == 0) & (i <= nv_ref[0]))
    def _():
        _wait_rows(xbuf_ref, slot, sem)

    @pl.when((j == 0) & live)
    def _():
        _start_rows(idx_next_ref, src_ref, xbuf_ref, 1 - slot, sem, head)
        xbf_ref[...] = xbuf_ref[slot].astype(bf16)

    @pl.when(live)
    def _():
        first = head + j * share
        for u in range(share):
            _row_copy(src_ref, idx_next_ref[0, 0, first + u], xbuf_ref, 1 - slot, first + u, 0, sem).start(
                priority=ROW_DMA_PRIORITY)
        x = xbf_ref[...]
        h1 = jnp.dot(x, w1_ref[0].astype(bf16), preferred_element_type=f32)
        h3 = jnp.dot(x, w3_ref[0].astype(bf16), preferred_element_type=f32)
        a = (h1 * jax.nn.sigmoid(h1)) * h3
        o_ref[...] += jnp.dot(a.astype(bf16), w2_ref[0].astype(bf16), preferred_element_type=f32)

    @pl.when(live & (i == last) & (j == nf - 1))
    def _():
        _wait_rows(xbuf_ref, 1 - slot, sem)


def _ffn_routed(row_src, row_token, w1, w3, w2, blk_expert, n_valid, tm, tf):
    n_rows = row_token.shape[0]
    ffn = w1.shape[-1]
    nf = ffn // tf
    nblk = n_rows // tm
    assert nf * tf == ffn and nblk * tm == n_rows

    def wcol(i, j, be, nv):
        return (be[i], 0, jnp.where(i < nv[0], j, nf - 1))

    def wrow(i, j, be, nv):
        return (be[i], jnp.where(i < nv[0], j, nf - 1), 0)

    idx3 = row_token.reshape(nblk, 1, tm)
    return pl.pallas_call(
        functools.partial(_ffn_kernel, tm=tm, nf=nf),
        out_shape=jax.ShapeDtypeStruct((n_rows, D_MODEL), f32),
        grid_spec=pltpu.PrefetchScalarGridSpec(
            num_scalar_prefetch=2,
            grid=(nblk, nf),
            in_specs=[
                pl.BlockSpec((1, 1, tm), lambda i, j, be, nv: (i, 0, 0), memory_space=pltpu.SMEM),
                pl.BlockSpec((1, 1, tm), lambda i, j, be, nv: (jnp.minimum(i + 1, nblk - 1), 0, 0),
                             memory_space=pltpu.SMEM),
                pl.BlockSpec(memory_space=pl.ANY),
                pl.BlockSpec((1, D_MODEL, tf), wcol),
                pl.BlockSpec((1, D_MODEL, tf), wcol),
                pl.BlockSpec((1, tf, D_MODEL), wrow),
            ],
            out_specs=pl.BlockSpec((tm, D_MODEL), lambda i, j, be, nv: (i, 0)),
            scratch_shapes=[pltpu.VMEM((tm, D_MODEL), bf16), pltpu.VMEM((2, tm, D_MODEL), row_src.dtype),
                            pltpu.SemaphoreType.DMA((2,))],
        ),
        compiler_params=_cparams(2),
        name="swiglu_routed",
    )(blk_expert, n_valid, idx3, idx3, row_src, w1, w3, w2)


def _cast_kernel(a_ref, o_ref):
    o_ref[...] = a_ref[...].astype(o_ref.dtype)


def _to_bf16(a, block_rows):
    rows, cols = a.shape
    assert rows % block_rows == 0
    return pl.pallas_call(
        _cast_kernel,
        out_shape=jax.ShapeDtypeStruct(a.shape, bf16),
        grid=(rows // block_rows,),
        in_specs=[pl.BlockSpec((block_rows, cols), lambda i: (i, 0))],
        out_specs=pl.BlockSpec((block_rows, cols), lambda i: (i, 0)),
        compiler_params=_cparams(1),
        name="cast_bf16",
    )(a)


def _ffn_dense_kernel(x_ref, w1_ref, w3_ref, w2_ref, x1_ref, mod_ref, o_ref, *, tm, chunks):
    i = pl.program_id(0)
    x = x_ref[...]
    acc = jnp.zeros((tm, D_MODEL), f32)
    c0 = 0
    for width in chunks:
        h1 = jnp.dot(x, w1_ref[:, c0:c0 + width], preferred_element_type=f32)
        h3 = jnp.dot(x, w3_ref[:, c0:c0 + width], preferred_element_type=f32)
        a = (h1 * jax.nn.sigmoid(h1)) * h3
        acc = acc + jnp.dot(a.astype(bf16), w2_ref[c0:c0 + width, :], preferred_element_type=f32)
        c0 += width
    gate2 = mod_ref[pl.ds(_mod_row(i, tm), 1), 5 * D_MODEL:6 * D_MODEL]
    o_ref[...] = x1_ref[...] + gate2 * acc


def _ffn_dense(h2, w1, w3, w2, x1, mod_l, tm=512, chunk=512):
    ffn = w1.shape[1]
    chunks = [chunk] * (ffn // chunk) + ([ffn % chunk] if ffn % chunk else [])
    w1b, w3b, w2b = _to_bf16(w1, 256), _to_bf16(w3, 256), _to_bf16(w2, 256)
    tok = lambda dt: pl.BlockSpec((tm, D_MODEL), lambda i: (i, 0))
    full = lambda a: pl.BlockSpec(a.shape, lambda i: (0, 0))
    return pl.pallas_call(
        functools.partial(_ffn_dense_kernel, tm=tm, chunks=tuple(chunks)),
        out_shape=jax.ShapeDtypeStruct((N_TOK, D_MODEL), f32),
        grid=(N_TOK // tm,),
        in_specs=[tok(bf16), full(w1b), full(w3b), full(w2b), tok(f32),
                  pl.BlockSpec((MOD_ROWS, 6 * D_MODEL), lambda i: (0, 0))],
        out_specs=tok(f32),
        compiler_params=_cparams(1),
        name="swiglu_dense",
    )(h2, w1b, w3b, w2b, x1, mod_l)


def _row_copy(src_ref, src_row, buf_ref, slot, rr, k, sem):
    w = src_ref.shape[1]
    return pltpu.make_async_copy(src_ref.at[pl.ds(src_row, 1), :],
                                 buf_ref.at[slot, pl.ds(rr, 1), pl.ds(k * w, w)], sem.at[slot])


def _start_rows(idx_ref, src_ref, buf_ref, slot, sem, n, pack=1):
    def body(rr, carry):
        for k in range(pack):
            _row_copy(src_ref, idx_ref[0, 0, rr * pack + k], buf_ref, slot, rr, k, sem).start(
                priority=ROW_DMA_PRIORITY)
        return carry

    lax.fori_loop(0, n // pack, body, 0, unroll=8)


def _wait_rows(buf_ref, slot, sem):
    pltpu.make_async_copy(buf_ref.at[slot], buf_ref.at[slot], sem.at[slot]).wait()


def _combine_kernel(*refs, tm, final):
    if final:
        idx_ref, idx_next_ref, ys_ref, x1_ref, mod_ref, gt_ref, g_ref, op_ref, os_ref, ybuf_ref, sem = refs
    else:
        idx_ref, idx_next_ref, ys_ref, x1_ref, mod_ref, gt_ref, o_ref, ybuf_ref, sem = refs
    i = pl.program_id(0)
    last = pl.num_programs(0) - 1
    slot = i % 2
    n_rows = tm * TOP_K

    @pl.when(i == 0)
    def _():
        _start_rows(idx_ref, ys_ref, ybuf_ref, 0, sem, n_rows, TOP_K)

    _wait_rows(ybuf_ref, slot, sem)

    for r in range(n_rows):
        _row_copy(ys_ref, idx_next_ref[0, 0, r], ybuf_ref, 1 - slot, r // TOP_K, r % TOP_K, sem).start(
            priority=ROW_DMA_PRIORITY)

    gate2 = mod_ref[pl.ds(_mod_row(i, tm), 1), 5 * D_MODEL:6 * D_MODEL]
    gt = gt_ref[...]
    f = gt[:, 0:1] * ybuf_ref[slot, :, :D_MODEL] + gt[:, 1:2] * ybuf_ref[slot, :, D_MODEL:]
    x2 = x1_ref[...] + gate2 * f
    if final:
        ms = jnp.mean(x2 * x2, axis=-1, keepdims=True)
        y = x2 * lax.rsqrt(ms + EPS) * g_ref[...]

        @pl.when(i < N_PROMPT_TOK // tm)
        def _():
            op_ref[...] = y

        @pl.when(i >= N_PROMPT_TOK // tm)
        def _():
            os_ref[...] = y
    else:
        o_ref[...] = x2

    @pl.when(i == last)
    def _():
        _wait_rows(ybuf_ref, 1 - slot, sem)


def _combine(x1, mod_l, ys, slots, gates, final_g=None, tm=512):
    final = final_g is not None
    npb = N_PROMPT_TOK // tm
    nblk = N_TOK // tm
    tok = pl.BlockSpec((tm, D_MODEL), lambda i: (i, 0))
    idx3 = slots.reshape(nblk, 1, tm * TOP_K)
    in_specs = [pl.BlockSpec((1, 1, tm * TOP_K), lambda i: (i, 0, 0), memory_space=pltpu.SMEM),
                pl.BlockSpec((1, 1, tm * TOP_K), lambda i: (jnp.minimum(i + 1, nblk - 1), 0, 0),
                             memory_space=pltpu.SMEM),
                pl.BlockSpec(memory_space=pl.ANY),
                tok, pl.BlockSpec((MOD_ROWS, 6 * D_MODEL), lambda i: (0, 0)),
                pl.BlockSpec((tm, TOP_K), lambda i: (i, 0))]
    args = [idx3, idx3, ys, x1, mod_l, gates]
    if final:
        in_specs.append(pl.BlockSpec((1, D_MODEL), lambda i: (0, 0)))
        args.append(final_g.reshape(1, D_MODEL))
        out_shape = (jax.ShapeDtypeStruct((N_PROMPT_TOK, D_MODEL), f32),
                     jax.ShapeDtypeStruct((N_SAMPLE_TOK, D_MODEL), f32))
        out_specs = (pl.BlockSpec((tm, D_MODEL), lambda i: (jnp.minimum(i, npb - 1), 0)),
                     pl.BlockSpec((tm, D_MODEL), lambda i: (jnp.maximum(i - npb, 0), 0)))
    else:
        out_shape = jax.ShapeDtypeStruct((N_TOK, D_MODEL), f32)
        out_specs = tok
    return pl.pallas_call(
        functools.partial(_combine_kernel, tm=tm, final=final),
        out_shape=out_shape,
        grid=(N_TOK // tm,),
        in_specs=in_specs,
        out_specs=out_specs,
        scratch_shapes=[pltpu.VMEM((2, tm, TOP_K * D_MODEL), ys.dtype), pltpu.SemaphoreType.DMA((2,))],
        compiler_params=_cparams(1),
        name="expert_combine",
    )(*args)


def _row_token_kernel(dest_ref, o_ref, *, chunk, n_rows):
    c = pl.program_id(0)

    @pl.when(c == 0)
    def _():
        def clear(r, carry):
            o_ref[r] = 0
            return carry

        lax.fori_loop(0, n_rows, clear, 0, unroll=8)

    def place(a, carry):
        o_ref[dest_ref[0, 0, a]] = lax.shift_right_logical(c * chunk + a, TOP_K // 2)
        return carry

    lax.fori_loop(0, chunk, place, 0, unroll=8)


def _row_token(dest, n_rows, chunk=4096):
    assert TOP_K == 2 and dest.shape[0] % chunk == 0
    n_chunks = dest.shape[0] // chunk
    return pl.pallas_call(
        functools.partial(_row_token_kernel, chunk=chunk, n_rows=n_rows),
        out_shape=jax.ShapeDtypeStruct((n_rows,), jnp.int32),
        grid=(n_chunks,),
        in_specs=[pl.BlockSpec((1, 1, chunk), lambda c: (c, 0, 0), memory_space=pltpu.SMEM)],
        out_specs=pl.BlockSpec(memory_space=pltpu.SMEM),
        compiler_params=_cparams(1),
        name="row_token",
    )(dest.reshape(n_chunks, 1, chunk))


def _route(logits, tm):
    eid = jnp.arange(N_EXPERTS, dtype=jnp.int32)[None, :]
    v0 = jnp.max(logits, axis=-1, keepdims=True)
    i0 = jnp.min(jnp.where(logits == v0, eid, N_EXPERTS), axis=-1, keepdims=True)
    rest = jnp.where(eid == i0, -jnp.inf, logits)
    v1 = jnp.max(rest, axis=-1, keepdims=True)
    i1 = jnp.min(jnp.where(rest == v1, eid, N_EXPERTS), axis=-1, keepdims=True)
    gates = jax.nn.softmax(jnp.concatenate([v0, v1], axis=-1), axis=-1)
    flat_e = jnp.concatenate([i0, i1], axis=-1).reshape(-1)
    onehot = (flat_e[:, None] == jnp.arange(N_EXPERTS)[None, :]).astype(jnp.int32)
    csum = jnp.cumsum(onehot, axis=0)
    rank = jnp.take_along_axis(csum, flat_e[:, None], axis=1)[:, 0] - 1
    counts = csum[-1]
    padded = ((counts + tm - 1) // tm) * tm
    pend = jnp.cumsum(padded)
    pstart = pend - padded
    dest = pstart[flat_e] + rank
    n_rows = N_TOK * TOP_K + N_EXPERTS * tm
    row_token = _row_token(dest, n_rows)
    blk_start = jnp.arange(n_rows // tm, dtype=jnp.int32) * tm
    blk_expert = jnp.minimum(jnp.sum((blk_start[:, None] >= pend[None, :]).astype(jnp.int32), axis=1),
                             N_EXPERTS - 1)
    n_valid = (pend[-1] // tm).astype(jnp.int32).reshape(1)
    blk_expert = jnp.where(blk_start < pend[-1], blk_expert, blk_expert[jnp.maximum(n_valid[0] - 1, 0)])
    return row_token, blk_expert, n_valid, dest.reshape(N_TOK, TOP_K), gates


def _pool_weight(pool_w_l):
    w = jnp.zeros((GROUP_W, GROUP_W), f32)
    for g in range(len(POOL_WINDOWS)):
        w = w.at[g * POOL_GC:(g + 1) * POOL_GC, g * POOL_GC:(g + 1) * POOL_GC].set(pool_w_l[g])
    return w


def kernel(x_prompt, x_sample, cache_diff_k, cache_diff_v, cache_na_k, cache_na_v, c, c_ctx,
           norm1_g, norm2_g, final_g, ada_w, ada_b, w_in, w_out, diff_lam, diff_subln_g,
           hy_conv, hy_w1, hy_b1, hy_w2, hy_b2, hy_w3, hy_b3, hy_freq, hy_decay, hy_dbias,
           na_rpb, pool_w, pool_scale, ffn_w1, ffn_w3, ffn_w2,
           moe_router, moe_w1, moe_w3, moe_w2):
    cond = jnp.concatenate([c, c_ctx[None, :], jnp.zeros((MOD_ROWS - DEC_BATCH - 1, D_MODEL), f32)], axis=0)
    mod = _ada_mod(cond, ada_w, ada_b)
    cos_np, sin_np = _rope_tables()
    cos_t, sin_t = jnp.asarray(cos_np), jnp.asarray(sin_np)
    dft = {L: tuple(jnp.asarray(m, dtype=bf16) for m in _dft_matrices(L)) for L in (SEQ, DEC_SEQ)}

    P = N_PROMPT_TOK
    x = (x_prompt.reshape(P, D_MODEL), x_sample.reshape(N_SAMPLE_TOK, D_MODEL))
    new_kv = None
    for l in range(DEPTH):
        pa, pb, pc, pd, *new_kv = _inproj(x, l, new_kv, mod[l], norm1_g[l], w_in, cos_t, sin_t)

        lam_init = 0.8 - 0.6 * math.exp(-0.3 * l)
        lv = diff_lam[l].astype(f32)
        lam = (jnp.exp(jnp.sum(lv[0] * lv[1])) - jnp.exp(jnp.sum(lv[2] * lv[3])) + lam_init).reshape(1)
        gain = jnp.tile(diff_subln_g[l], A_HEADS).reshape(1, GROUP_W)
        a_kw = dict(n_heads=A_HEADS, diff=True, scale=A_DQK ** -0.5, lam=lam, gain=gain, out_scale=1.0 - lam_init)
        oa_p = _attention(pa, 0, 0, pa, 1, 0, pa, 2, 0, n_seq=BATCH, seq_len=SEQ, n_keys=SEQ, tq=SEQ,
                          seqs=CTX_SEQS_PER_STEP, **a_kw)
        s0 = P // DEC_SEQ
        cache_a = (cache_diff_k[:, l].reshape(-1, GROUP_W), cache_diff_v[:, l].reshape(-1, GROUP_W))
        oa_s = _attention(pa, 0, s0, pa, 1, s0, pa, 2, s0, n_seq=DEC_BATCH, seq_len=DEC_SEQ, n_keys=DEC_SEQ,
                          tq=256, cache_kv=cache_a, **a_kw)

        hy = (hy_conv[l], hy_w1[l], hy_b1[l], hy_w2[l], hy_b2[l], hy_w3[l], hy_b3[l],
              hy_freq[l], hy_decay[l], hy_dbias[l])
        ob_p = _hyena(pb, 0, BATCH, SEQ, dft[SEQ], hy)
        ob_s = _hyena(pb, P // DEC_SEQ, DEC_BATCH, DEC_SEQ, dft[DEC_SEQ], hy)

        oc_p = _attention(pc, 0, 0, pc, 1, 0, pc, 2, 0, n_seq=BATCH, seq_len=SEQ, n_keys=SEQ, tq=SEQ,
                          n_heads=C_HEADS, diff=False, scale=C_DH ** -0.5, seqs=CTX_SEQS_PER_STEP)
        oc_s = _na_attention(pc, cache_na_k[:, l].reshape(-1, GROUP_W), cache_na_v[:, l].reshape(-1, GROUP_W),
                             _na_bias_tables(na_rpb[l]))

        w_bd = _pool_weight(pool_w[l])
        od_p = _pool(pd, w_bd, pool_scale[l], 0, BATCH, SEQ)
        od_s = _pool(pd, w_bd, pool_scale[l], P // DEC_SEQ, DEC_BATCH, DEC_SEQ)

        mixers = [(oa_p, oa_s), (ob_p, ob_s), (oc_p, oc_s), (od_p, od_s)]
        tm = 1024
        if l % 2 == 0:
            assert l != DEPTH - 1
            x1, h2 = _outproj(mixers, x, l, mod[l], norm2_g[l], w_out)
            x = _ffn_dense(h2, ffn_w1[l // 2], ffn_w3[l // 2], ffn_w2[l // 2], x1, mod[l])
        else:
            x1, h2, logits = _outproj(mixers, x, l, mod[l], norm2_g[l], w_out, moe_router[l // 2])
            row_token, blk_expert, n_valid, dest, gates = _route(logits[:, :N_EXPERTS], tm)
            ys = _ffn_routed(h2, row_token, moe_w1[l // 2], moe_w3[l // 2], moe_w2[l // 2], blk_expert, n_valid,
                             tm=tm, tf=512)
            x = _combine(x1, mod[l], ys, dest, gates, final_g if l == DEPTH - 1 else None)

    assert isinstance(x, tuple)
    y_prompt = x[0].reshape(BATCH, SEQ, D_MODEL)
    y_sample = x[1].reshape(DEC_BATCH, DEC_SEQ, D_MODEL)
    kak, kav, kck, kcv = new_kv
    return (y_prompt, y_sample,
            kak.reshape(BATCH, DEPTH, SEQ, A_HEADS, 2 * A_DQK), kav.reshape(BATCH, DEPTH, SEQ, A_HEADS, A_DV),
            kck.reshape(BATCH, DEPTH, SEQ, C_HEADS, C_DH), kcv.reshape(BATCH, DEPTH, SEQ, C_HEADS, C_DH))
```

```python
import functools
import math

import numpy as np
import jax
import jax.numpy as jnp
from jax import lax
from jax.experimental import pallas as pl
from jax.experimental.pallas import tpu as pltpu

D_MODEL = 1024
BATCH = 32
SEQ = 256
DEPTH = 2
DEC_BATCH = 4
DEC_SEQ = 2048
PAST_LEN = 512
GRID_W = 64
GRID_H = DEC_SEQ // GRID_W
GROUP_W = D_MODEL // 4
A_HEADS = 4
A_DQK = GROUP_W // (2 * A_HEADS)
A_DV = GROUP_W // A_HEADS
ROPE_BASE = 10000.0
HY_CH = GROUP_W
HY_ORDER = 2
HY_BANDS = 8
HY_FEAT = 1 + 2 * HY_BANDS
HY_HID = 64
C_HEADS = 4
C_DH = GROUP_W // C_HEADS
NA_KH = 8
NA_KW = 16
POOL_WINDOWS = (2, 4, 8, 16)
POOL_GC = GROUP_W // 4
PROJ_W = 3 * GROUP_W + 3 * HY_CH + 3 * GROUP_W + GROUP_W
N_EXPERTS = 8
TOP_K = 2
EPS = 1e-6
NEG = -1e30
LOG2E = math.log2(math.e)

N_PROMPT_TOK = BATCH * SEQ
N_SAMPLE_TOK = DEC_BATCH * DEC_SEQ
N_TOK = N_PROMPT_TOK + N_SAMPLE_TOK
MOD_ROWS = 8
CTX_ROW = DEC_BATCH
ROUTER_PAD = 128
LANES = 128

NA_QROWS = 4
NA_SLAB_ROWS = 12
NA_GROUPS = GRID_H // NA_QROWS

f32 = jnp.float32
bf16 = jnp.bfloat16

VMEM_LIMIT = 56 * 1024 * 1024
ROW_DMA_PRIORITY = 1
HY_SPECTRUM_TILE = 512
HY_EPILOGUE_ROWS = 2048
_NT = (((1,), (1,)), ((), ()))


def _cparams(n_axes):
    return pltpu.CompilerParams(
        dimension_semantics=("arbitrary",) * n_axes, vmem_limit_bytes=VMEM_LIMIT)


def _mod_row(i, tm):
    n_prompt_blocks = N_PROMPT_TOK // tm
    blocks_per_seq = DEC_SEQ // tm
    return jnp.where(i < n_prompt_blocks, CTX_ROW, (i - n_prompt_blocks) // blocks_per_seq)


def _split3(a):
    a0 = a.astype(bf16)
    r1 = a - a0.astype(f32)
    a1 = r1.astype(bf16)
    a2 = (r1 - a1.astype(f32)).astype(bf16)
    return a0, a1, a2


def _dot_bf16x3(a, b):
    a0 = a.astype(bf16)
    a1 = (a - a0.astype(f32)).astype(bf16)
    b0 = b.astype(bf16)
    b1 = (b - b0.astype(f32)).astype(bf16)
    d = functools.partial(jnp.dot, preferred_element_type=f32)
    return (d(a1, b0) + d(a0, b1)) + d(a0, b0)


def _dot_f32(a, b):
    a0, a1, a2 = _split3(a)
    b0, b1, b2 = _split3(b)
    d = functools.partial(jnp.dot, preferred_element_type=f32)
    return ((d(a2, b0) + d(a1, b1) + d(a0, b2)) + (d(a1, b0) + d(a0, b1))) + d(a0, b0)


def _lane_group(shape, width):
    return lax.shift_right_logical(lax.broadcasted_iota(jnp.int32, shape, 1), int(math.log2(width)))


def _shift_rows(x, d):
    n = x.shape[0]
    r = pltpu.roll(x, d % n, 0)
    row = lax.broadcasted_iota(jnp.int32, x.shape, 0)
    keep = (row >= d) if d > 0 else (row < n + d)
    return jnp.where(keep, r, 0.0)


def _ada_kernel(cond_ref, w_ref, b_ref, o_ref):
    c = cond_ref[...]
    s = c * jax.nn.sigmoid(c)
    o_ref[0] = jnp.dot(s.astype(bf16), w_ref[0].astype(bf16), preferred_element_type=f32) + b_ref[0]


def _ada_mod(cond, ada_w, ada_b):
    tn = 1536
    n6 = 6 * D_MODEL
    return pl.pallas_call(
        _ada_kernel,
        out_shape=jax.ShapeDtypeStruct((DEPTH, MOD_ROWS, n6), f32),
        grid=(DEPTH, n6 // tn),
        in_specs=[
            pl.BlockSpec((MOD_ROWS, D_MODEL), lambda l, j: (0, 0)),
            pl.BlockSpec((1, D_MODEL, tn), lambda l, j: (l, 0, j)),
            pl.BlockSpec((1, 1, tn), lambda l, j: (l, 0, j)),
        ],
        out_specs=pl.BlockSpec((1, MOD_ROWS, tn), lambda l, j: (l, 0, j)),
        compiler_params=_cparams(2),
        name="ada_mod",
    )(cond, ada_w, ada_b.reshape(DEPTH, 1, n6))


def _rope_tables():
    t = np.arange(DEC_SEQ)
    nf = A_DQK // 4
    inv = ROPE_BASE ** (-np.arange(nf, dtype=np.float64) / nf)
    ar = (t // GRID_W)[:, None] * inv
    ac = (t % GRID_W)[:, None] * inv
    cos = np.concatenate([np.cos(ar), np.cos(ar), np.cos(ac), np.cos(ac)], axis=1)
    sin = np.concatenate([-np.sin(ar), np.sin(ar), -np.sin(ac), np.sin(ac)], axis=1)
    reps = GROUP_W // A_DQK
    return (np.tile(cos, (1, reps)).astype(np.float32), np.tile(sin, (1, reps)).astype(np.float32))


def _modulated_norm(x, g, shift, scale):
    ms = jnp.mean(x * x, axis=-1, keepdims=True)
    return (x * lax.rsqrt(ms + EPS) * g) * (1.0 + scale) + shift


def _inproj_kernel(*refs, tm, split_x, n_alias):
    n_x = 2 if split_x else 1
    x_refs = refs[:n_x]
    mod_ref, g_ref, w_ref, cos_ref, sin_ref = refs[n_x:n_x + 5]
    outs = refs[n_x + 5 + n_alias:]
    pa_ref, pb_ref, pc_ref, pd_ref, kak_ref, kav_ref, kck_ref, kcv_ref, wbf_ref = outs
    i = pl.program_id(0)
    is_prompt = i < N_PROMPT_TOK // tm

    @pl.when(i == 0)
    def _():
        wbf_ref[...] = w_ref[0].astype(bf16)

    row = _mod_row(i, tm)
    shift = mod_ref[pl.ds(row, 1), 0:D_MODEL]
    scale = mod_ref[pl.ds(row, 1), D_MODEL:2 * D_MODEL]
    x = jnp.where(is_prompt, x_refs[0][...], x_refs[1][...]) if split_x else x_refs[0][...]
    h = _modulated_norm(x, g_ref[...], shift, scale)
    proj = jnp.dot(h.astype(bf16), wbf_ref[...], preferred_element_type=f32)
    w3 = 3 * GROUP_W
    pb_ref[...] = proj[:, w3:2 * w3]
    pc_ref[...] = proj[:, 2 * w3:3 * w3]
    pd_ref[...] = proj[:, 3 * w3:]
    pa_ref[:, 2 * GROUP_W:] = proj[:, 2 * GROUP_W:w3]

    @pl.when(is_prompt)
    def _():
        pa_ref[:, :2 * GROUP_W] = proj[:, :2 * GROUP_W]
        kv_shape = (tm // SEQ, 1, SEQ, GROUP_W)
        cols = (GROUP_W, 2 * GROUP_W, 2 * w3 + GROUP_W, 2 * w3 + 2 * GROUP_W)
        for ref, c0 in zip((kak_ref, kav_ref, kck_ref, kcv_ref), cols):
            ref[:, 0:1] = proj[:, c0:c0 + GROUP_W].reshape(kv_shape)
            if ref.shape[1] > 1:
                ref[:, 1:] = jnp.zeros((tm // SEQ, ref.shape[1] - 1, SEQ, GROUP_W), f32)

    @pl.when(i >= N_PROMPT_TOK // tm)
    def _():
        cos = cos_ref[...]
        sin = sin_ref[...]
        lane = lax.broadcasted_iota(jnp.int32, (tm, GROUP_W), 1)
        first = (lane % 16) < 8
        for s in range(2):
            v = proj[:, s * GROUP_W:(s + 1) * GROUP_W]
            partner = jnp.where(first, pltpu.roll(v, GROUP_W - 8, 1), pltpu.roll(v, 8, 1))
            pa_ref[:, s * GROUP_W:(s + 1) * GROUP_W] = v * cos + partner * sin


def _inproj(x, layer, kv_prev, mod_l, g, w_in, cos_t, sin_t, tm=512):
    npb = N_PROMPT_TOK // tm
    blocks_per_seq = DEC_SEQ // tm
    w3 = 3 * GROUP_W
    split_x = isinstance(x, tuple)

    def rope_idx(i):
        return (jnp.maximum(i - npb, 0) % blocks_per_seq, 0)

    if split_x:
        x_specs = [pl.BlockSpec((tm, D_MODEL), lambda i: (jnp.minimum(i, npb - 1), 0)),
                   pl.BlockSpec((tm, D_MODEL), lambda i: (jnp.maximum(i - npb, 0), 0))]
        x_args = list(x)
    else:
        x_specs = [pl.BlockSpec((tm, D_MODEL), lambda i: (i, 0))]
        x_args = [x]
    kv_args = list(kv_prev) if kv_prev is not None else []
    n_in = len(x_args) + 5
    kv_shape = jax.ShapeDtypeStruct((BATCH, DEPTH, SEQ, GROUP_W), f32)
    kv_layers = DEPTH if layer == 0 else 1
    kv_spec = pl.BlockSpec((tm // SEQ, kv_layers, SEQ, GROUP_W), lambda i: (jnp.minimum(i, npb - 1), layer, 0, 0))
    tok = lambda w: pl.BlockSpec((tm, w), lambda i: (i, 0))
    return pl.pallas_call(
        functools.partial(_inproj_kernel, tm=tm, split_x=split_x, n_alias=len(kv_args)),
        out_shape=(jax.ShapeDtypeStruct((N_TOK, w3), f32), jax.ShapeDtypeStruct((N_TOK, w3), f32),
                   jax.ShapeDtypeStruct((N_TOK, w3), f32), jax.ShapeDtypeStruct((N_TOK, GROUP_W), f32),
                   kv_shape, kv_shape, kv_shape, kv_shape),
        grid=(N_TOK // tm,),
        in_specs=x_specs + [
            pl.BlockSpec((MOD_ROWS, 6 * D_MODEL), lambda i: (0, 0)),
            pl.BlockSpec((1, D_MODEL), lambda i: (0, 0)),
            pl.BlockSpec((1, D_MODEL, PROJ_W), lambda i: (layer, 0, 0)),
            pl.BlockSpec((tm, GROUP_W), rope_idx),
            pl.BlockSpec((tm, GROUP_W), rope_idx),
        ] + [pl.BlockSpec(memory_space=pl.ANY)] * len(kv_args),
        out_specs=(tok(w3), tok(w3), tok(w3), tok(GROUP_W), kv_spec, kv_spec, kv_spec, kv_spec),
        scratch_shapes=[pltpu.VMEM((D_MODEL, PROJ_W), bf16)],
        input_output_aliases={n_in + j: 4 + j for j in range(len(kv_args))},
        compiler_params=_cparams(1),
        name="norm1_inproj",
    )(*x_args, mod_l, g.reshape(1, D_MODEL), w_in, cos_t, sin_t, *kv_args)


def _values_with_ones(v, vhead, vlane, h, hw):
    ones_col = ((h + 1) * hw) % GROUP_W
    vm = jnp.where(vhead == h, v, jnp.where(vlane == ones_col, 1.0, 0.0))
    return vm.astype(bf16), ones_col


def _attn_kernel(*refs, n_heads, diff, scale, out_scale, cached):
    if cached:
        lam_ref, q_ref, k_ref, v_ref, kx_ref, vx_ref, g_ref, o_ref = refs
        k = jnp.concatenate([k_ref[...], kx_ref[...]], axis=0).astype(bf16)
        v = jnp.concatenate([v_ref[...], vx_ref[...]], axis=0)
    else:
        lam_ref, q_ref, k_ref, v_ref, g_ref, o_ref = refs
        k = k_ref[...].astype(bf16)
        v = v_ref[...]
    acc = _attend(q_ref[...], k, v, lam_ref, g_ref, n_heads=n_heads, diff=diff, scale=scale, out_scale=out_scale)
    o_ref[...] = acc.astype(o_ref.dtype)


def _attend(q, k, v, lam_ref, g_ref, *, n_heads, diff, scale, out_scale):
    tq = q.shape[0]
    n = k.shape[0]
    q = q * (scale * LOG2E)
    hw = GROUP_W // n_heads
    n_maps = 2 if diff else 1
    qgrp = _lane_group((tq, GROUP_W), hw // n_maps)
    qhead = _lane_group((tq, GROUP_W), hw)
    vhead = _lane_group((n, GROUP_W), hw)
    vlane = lax.broadcasted_iota(jnp.int32, (n, GROUP_W), 1)
    acc = jnp.zeros((tq, GROUP_W), f32)
    for h in range(n_heads):
        vm, ones_col = _values_with_ones(v, vhead, vlane, h, hw)
        maps = []
        for m in range(n_maps):
            qm = jnp.where(qgrp == n_maps * h + m, q, 0.0).astype(bf16)
            s = lax.dot_general(qm, k, _NT, preferred_element_type=f32)
            e = jnp.exp2(s - jnp.max(s, axis=-1, keepdims=True)).astype(bf16)
            o = jnp.dot(e, vm, preferred_element_type=f32)
            maps.append(o * (1.0 / o[:, ones_col:ones_col + 1]))
        oh = maps[0] - lam_ref[0] * maps[1] if diff else maps[0]
        acc = acc + jnp.where(qhead == h, oh, 0.0)
    if diff:
        r = lax.shift_right_logical(lax.broadcasted_iota(jnp.int32, (GROUP_W, GROUP_W), 0), 6)
        c = lax.shift_right_logical(lax.broadcasted_iota(jnp.int32, (GROUP_W, GROUP_W), 1), 6)
        bd = jnp.where(r == c, 1.0, 0.0).astype(bf16)
        sq = acc * acc
        hi = sq.astype(bf16)
        lo = (sq - hi.astype(f32)).astype(bf16)
        ms = (jnp.dot(hi, bd, preferred_element_type=f32) + jnp.dot(lo, bd, preferred_element_type=f32)) * (1.0 / A_DV)
        acc = (acc * lax.rsqrt(ms + EPS) * g_ref[...]) * out_scale
    return acc


def _attention(q_src, q_col, q_row0, k_src, k_col, k_row0, v_src, v_col, v_row0, *,
               n_seq, seq_len, n_keys, tq, n_heads, diff, scale, lam=None, gain=None, out_scale=1.0,
               cache_kv=None):
    qb = seq_len // tq
    if lam is None:
        lam = jnp.zeros((1,), f32)
    if gain is None:
        gain = jnp.ones((1, GROUP_W), f32)
    in_specs = [
        pl.BlockSpec(memory_space=pltpu.SMEM),
        pl.BlockSpec((tq, GROUP_W), lambda b, i: ((q_row0 + b) * qb + i, q_col)),
        pl.BlockSpec((n_keys, GROUP_W), lambda b, i: (k_row0 + b, k_col)),
        pl.BlockSpec((n_keys, GROUP_W), lambda b, i: (v_row0 + b, v_col)),
    ]
    args = [lam, q_src, k_src, v_src]
    if cache_kv is not None:
        n_cached = cache_kv[0].shape[0] // n_seq
        in_specs += [pl.BlockSpec((n_cached, GROUP_W), lambda b, i: (b, 0))] * 2
        args += list(cache_kv)
    in_specs.append(pl.BlockSpec((1, GROUP_W), lambda b, i: (0, 0)))
    args.append(gain)
    return pl.pallas_call(
        functools.partial(_attn_kernel, n_heads=n_heads, diff=diff, scale=scale, out_scale=out_scale,
                          cached=cache_kv is not None),
        out_shape=jax.ShapeDtypeStruct((n_seq * seq_len, GROUP_W), bf16),
        grid=(n_seq, qb),
        in_specs=in_specs,
        out_specs=pl.BlockSpec((tq, GROUP_W), lambda b, i: (b * qb + i, 0)),
        compiler_params=_cparams(2),
        name="diff_attention" if diff else "softmax_attention",
    )(*args)


def _na_group_geometry(g):
    r0 = g * NA_QROWS
    slab0 = min(max(r0 - NA_KH // 2, 0), GRID_H - NA_SLAB_ROWS)
    return r0, slab0


def _na_bias_tables(rpb):
    c = np.arange(GRID_W)
    ws = np.clip(c - NA_KW // 2, 0, GRID_W - NA_KW)
    kc = np.arange(GRID_W)
    col_ok = (kc[None, :] >= ws[:, None]) & (kc[None, :] < ws[:, None] + NA_KW)
    pad = GRID_W - NA_KW
    rp = jnp.pad(rpb.astype(f32) * LOG2E, ((0, 0), (0, 0), (pad, pad)))
    tc = jnp.stack([rp[:, :, GRID_W - 1 - ci:2 * GRID_W - 1 - ci] for ci in range(GRID_W)], axis=2)
    tc = jnp.where(col_ok[None, None], tc, NEG)
    neg_blk = jnp.full((C_HEADS, GRID_W, GRID_W), NEG, f32)
    tables = []
    for g in (0, 1, NA_GROUPS - 1):
        r0, slab0 = _na_group_geometry(g)
        rows = []
        for rq in range(NA_QROWS):
            r = r0 + rq
            rs = min(max(r - NA_KH // 2, 0), GRID_H - NA_KH)
            blks = []
            for kl in range(NA_SLAB_ROWS):
                kr = slab0 + kl
                blks.append(tc[:, kr - r + NA_KH - 1] if rs <= kr < rs + NA_KH else neg_blk)
            rows.append(jnp.concatenate(blks, axis=-1))
        tables.append(jnp.concatenate(rows, axis=-2))
    return jnp.stack(tables, axis=0)


def _na_kernel(q_ref, k_ref, v_ref, kx_ref, vx_ref, bias_ref, o_ref):
    g = pl.program_id(1)
    tq = NA_QROWS * GRID_W
    ns = NA_SLAB_ROWS * GRID_W
    slab0 = jnp.clip(g * NA_QROWS - NA_KH // 2, 0, GRID_H - NA_SLAB_ROWS)
    start = pl.multiple_of(slab0 * GRID_W, GRID_W)
    q = q_ref[...] * (C_DH ** -0.5 * LOG2E)
    ks = k_ref[pl.ds(start, ns), :].astype(bf16)
    vs = v_ref[pl.ds(start, ns), :]
    kx = kx_ref[...].astype(bf16)
    vx = vx_ref[...]
    qhead = _lane_group((tq, GROUP_W), C_DH)
    vshead = _lane_group((ns, GROUP_W), C_DH)
    vslane = lax.broadcasted_iota(jnp.int32, (ns, GROUP_W), 1)
    vxhead = _lane_group((PAST_LEN, GROUP_W), C_DH)
    vxlane = lax.broadcasted_iota(jnp.int32, (PAST_LEN, GROUP_W), 1)
    acc = jnp.zeros((tq, GROUP_W), f32)
    for h in range(C_HEADS):
        qm = jnp.where(qhead == h, q, 0.0).astype(bf16)
        sl = lax.dot_general(qm, ks, _NT, preferred_element_type=f32)
        b = bias_ref[0, h]
        sl = jnp.where(b > 0.5 * NEG, sl + b, NEG)
        sx = lax.dot_general(qm, kx, _NT, preferred_element_type=f32)
        mx = jnp.maximum(jnp.max(sl, axis=-1, keepdims=True), jnp.max(sx, axis=-1, keepdims=True))
        el = jnp.exp2(sl - mx).astype(bf16)
        ex = jnp.exp2(sx - mx).astype(bf16)
        vsm, ones_col = _values_with_ones(vs, vshead, vslane, h, C_DH)
        vxm, _ = _values_with_ones(vx, vxhead, vxlane, h, C_DH)
        o = jnp.dot(el, vsm, preferred_element_type=f32) + jnp.dot(ex, vxm, preferred_element_type=f32)
        acc = acc + jnp.where(qhead == h, o * (1.0 / o[:, ones_col:ones_col + 1]), 0.0)
    o_ref[...] = acc.astype(o_ref.dtype)


def _na_attention(pc, kx, vx, bias):
    tq = NA_QROWS * GRID_W
    ns = NA_SLAB_ROWS * GRID_W
    q_blk0 = N_PROMPT_TOK // tq
    s_blk0 = N_PROMPT_TOK // DEC_SEQ

    def bias_idx(b, g):
        return (jnp.where(g == 0, 0, jnp.where(g == NA_GROUPS - 1, 2, 1)), 0, 0, 0)

    return pl.pallas_call(
        _na_kernel,
        out_shape=jax.ShapeDtypeStruct((N_SAMPLE_TOK, GROUP_W), bf16),
        grid=(DEC_BATCH, NA_GROUPS),
        in_specs=[
            pl.BlockSpec((tq, GROUP_W), lambda b, g: (q_blk0 + b * NA_GROUPS + g, 0)),
            pl.BlockSpec((DEC_SEQ, GROUP_W), lambda b, g: (s_blk0 + b, 1)),
            pl.BlockSpec((DEC_SEQ, GROUP_W), lambda b, g: (s_blk0 + b, 2)),
            pl.BlockSpec((PAST_LEN, GROUP_W), lambda b, g: (b, 0)),
            pl.BlockSpec((PAST_LEN, GROUP_W), lambda b, g: (b, 0)),
            pl.BlockSpec((1, C_HEADS, tq, ns), bias_idx),
        ],
        out_specs=pl.BlockSpec((tq, GROUP_W), lambda b, g: (b * NA_GROUPS + g, 0)),
        compiler_params=_cparams(2),
        name="neighbourhood_attention",
    )(pc, pc, pc, kx, vx, bias)


def _dft_matrices(L):
    n = 2 * L
    k = np.arange(L)[:, None]
    s = np.arange(L)[None, :]
    ang = 2.0 * np.pi * ((k * s) % n) / n
    cos, sin = np.cos(ang), np.sin(ang)
    sin[0, :] = (-1.0) ** np.arange(L)
    fwd = np.concatenate([cos, sin], axis=0)
    wk = np.where(np.arange(L) == 0, 1.0, 2.0)[None, :]
    inv = np.concatenate([cos.T * wk, sin.T * wk], axis=1) / n
    inv[:, L] = ((-1.0) ** np.arange(L)) / n
    return fwd, inv


def _hy_filter_kernel(w1_ref, b1_ref, w2_ref, b2_ref, w3_ref, b3_ref, fr_ref, dec_ref, fa_ref, fb_ref,
                      p_ref, q_ref, r_ref, g_ref, nrm_ref, *, L, tk):
    s = pl.program_id(0)

    @pl.when(s == 0)
    def _():
        row = lax.broadcasted_iota(jnp.int32, (L, LANES), 0)
        lane = lax.broadcasted_iota(jnp.int32, (L, LANES), 1)
        t = row.astype(f32) / L
        band = jnp.where(lane <= HY_BANDS, lane, lane - HY_BANDS).astype(f32)
        ang = (2.0 * math.pi * band) * t
        feat = jnp.where(lane == 0, t, jnp.where(lane <= HY_BANDS, jnp.sin(ang),
                                                 jnp.where(lane <= 2 * HY_BANDS, jnp.cos(ang), 0.0)))
        z = jnp.sin(fr_ref[0:1, :] * (_dot_f32(feat, w1_ref[...]) + b1_ref[...]))
        z = jnp.sin(fr_ref[1:2, :] * (_dot_f32(z, w2_ref[...]) + b2_ref[...]))
        z = _dot_f32(z, w3_ref[...]) + b3_ref[...]
        wide = (L, HY_ORDER * 2 * HY_CH)
        tw = lax.broadcasted_iota(jnp.int32, wide, 0).astype(f32) / L
        taps = z * jnp.exp(-tw * jnp.abs(dec_ref[...]))
        bwd = (_lane_group(wide, HY_CH) % 2) == 1
        first = lax.broadcasted_iota(jnp.int32, wide, 0) == 0
        taps = jnp.where(first, jnp.where(bwd, 0.0, taps), taps)
        g_ref[...] = taps.astype(bf16)
        nrm_ref[...] = jnp.sum(jnp.abs(taps), axis=0, keepdims=True)

    ga = jnp.dot(fa_ref[...], g_ref[...], preferred_element_type=f32)
    gb = jnp.dot(fb_ref[...], g_ref[...], preferred_element_type=f32)
    top = (lax.broadcasted_iota(jnp.int32, (tk, HY_CH), 0) + s * tk) == 0
    for o in range(HY_ORDER):
        c0 = o * 2 * HY_CH
        inv = 1.0 / (nrm_ref[:, c0:c0 + HY_CH] + nrm_ref[:, c0 + HY_CH:c0 + 2 * HY_CH])
        hc = (ga[:, c0:c0 + HY_CH] + ga[:, c0 + HY_CH:c0 + 2 * HY_CH]) * inv
        bf_, bb_ = gb[:, c0:c0 + HY_CH], gb[:, c0 + HY_CH:c0 + 2 * HY_CH]
        hs = jnp.where(top, bf_ + bb_, bf_ - bb_) * inv
        oc = slice(o * HY_CH, (o + 1) * HY_CH)
        p_ref[:, oc] = hc
        q_ref[:, oc] = jnp.where(top, 0.0, hs)
        r_ref[:, oc] = jnp.where(top, hs, hc)


def _hy_filter_spectra(L, fwd_bf, w1, b1, w2, b2, w3, b3, freq, decay, tk=256):
    nk = L // tk
    wide = HY_ORDER * 2 * HY_CH
    full = lambda a: pl.BlockSpec(a.shape, lambda s: (0,) * a.ndim)
    w1p = jnp.pad(w1, ((0, LANES - HY_FEAT), (0, 0)))
    args = [w1p, b1.reshape(1, HY_HID), w2, b2.reshape(1, HY_HID), w3, b3.reshape(1, wide), freq,
            decay.reshape(1, wide)]
    out = jax.ShapeDtypeStruct((L, HY_ORDER * HY_CH), f32)
    plane = pl.BlockSpec((tk, HY_ORDER * HY_CH), lambda s: (s, 0))
    return pl.pallas_call(
        functools.partial(_hy_filter_kernel, L=L, tk=tk),
        out_shape=(out, out, out),
        grid=(nk,),
        in_specs=[full(a) for a in args] + [pl.BlockSpec((tk, L), lambda s: (s, 0)),
                                            pl.BlockSpec((tk, L), lambda s: (s + nk, 0))],
        out_specs=(plane, plane, plane),
        scratch_shapes=[pltpu.VMEM((L, wide), bf16), pltpu.VMEM((1, wide), f32)],
        compiler_params=_cparams(1),
        name="hyena_filter_spectra",
    )(*args, fwd_bf, fwd_bf)


def _hy_pre_kernel(u_ref, w_ref, x1_ref, x2_ref, v_ref, vbf_ref):
    u = u_ref[...]
    w = w_ref[...]
    y = _shift_rows(u, 1) * w[0:1, :] + u * w[1:2, :] + _shift_rows(u, -1) * w[2:3, :]
    x1_ref[...] = y[:, :HY_CH]
    x2_ref[...] = y[:, HY_CH:2 * HY_CH]
    v = y[:, 2 * HY_CH:]
    v_ref[...] = v
    vbf_ref[...] = v.astype(bf16)


def _hy_pre(pb, conv_w, seq0, n_seq, L):
    n = n_seq * L
    blk = pl.BlockSpec((L, HY_CH), lambda b: (b, 0))
    o32 = jax.ShapeDtypeStruct((n, HY_CH), f32)
    return pl.pallas_call(
        _hy_pre_kernel,
        out_shape=(o32, o32, o32, jax.ShapeDtypeStruct((n, HY_CH), bf16)),
        grid=(n_seq,),
        in_specs=[pl.BlockSpec((L, 3 * HY_CH), lambda b: (seq0 + b, 0)),
                  pl.BlockSpec((3, 3 * HY_CH), lambda b: (0, 0))],
        out_specs=(blk, blk, blk, blk),
        compiler_params=_cparams(1),
        name="hyena_short_conv",
    )(pb, conv_w)


def _hy_conv_kernel(*refs, n_seq, L, nk, te, n_out):
    (zbf_ref, z_ref, m_ref, fa_ref, fb_ref, ic_ref, is_ref, p_ref, q_ref, r_ref, db_ref) = refs[:11]
    out_refs = refs[11:11 + n_out]
    acc_ref = refs[11 + n_out]
    s = pl.program_id(0)

    @pl.when(s == 0)
    def _():
        acc_ref[...] = jnp.zeros_like(acc_ref)

    @pl.when(s < nk)
    def _():
        fa, fb, ic, isn = fa_ref[...], fb_ref[...], ic_ref[...], is_ref[...]
        p, q, r = p_ref[...], q_ref[...], r_ref[...]
        for b in range(n_seq):
            rows = slice(b * L, (b + 1) * L)
            zb = zbf_ref[rows, :]
            a = jnp.dot(fa, zb, preferred_element_type=f32)
            bb = jnp.dot(fb, zb, preferred_element_type=f32)
            yc = (a * p - bb * q).astype(bf16)
            ys = (a * q + bb * r).astype(bf16)
            acc_ref[rows, :] += (jnp.dot(ic, yc, preferred_element_type=f32)
                                 + jnp.dot(isn, ys, preferred_element_type=f32))

    @pl.when(s >= nk)
    def _():
        start = pl.multiple_of((s - nk) * te, te)
        y = acc_ref[pl.ds(start, te), :]
        res = m_ref[...] * (y + db_ref[...] * z_ref[...])
        for o_ref in out_refs:
            o_ref[...] = res.astype(o_ref.dtype)


def _hy_longconv(zbf, z, mult, fwd_bf, inv_bf, planes, order, dbias_o, out_dtypes, n_seq, L, tk):
    nk = L // tk
    n = n_seq * L
    te = max(L, HY_EPILOGUE_ROWS)
    assert n % te == 0
    kt = lambda s: jnp.minimum(s, nk - 1)
    ep = lambda s: (jnp.maximum(s - nk, 0), 0)
    plane = pl.BlockSpec((tk, HY_CH), lambda s: (kt(s), order))
    return pl.pallas_call(
        functools.partial(_hy_conv_kernel, n_seq=n_seq, L=L, nk=nk, te=te, n_out=len(out_dtypes)),
        out_shape=tuple(jax.ShapeDtypeStruct((n, HY_CH), dt) for dt in out_dtypes),
        grid=(nk + n // te,),
        in_specs=[
            pl.BlockSpec((n, HY_CH), lambda s: (0, 0)),
            pl.BlockSpec((te, HY_CH), ep),
            pl.BlockSpec((te, HY_CH), ep),
            pl.BlockSpec((tk, L), lambda s: (kt(s), 0)),
            pl.BlockSpec((tk, L), lambda s: (kt(s) + nk, 0)),
            pl.BlockSpec((L, tk), lambda s: (0, kt(s))),
            pl.BlockSpec((L, tk), lambda s: (0, kt(s) + nk)),
            plane, plane, plane,
            pl.BlockSpec((1, HY_CH), lambda s: (0, 0)),
        ],
        out_specs=tuple(pl.BlockSpec((te, HY_CH), ep) for _ in out_dtypes),
        scratch_shapes=[pltpu.VMEM((n, HY_CH), f32)],
        compiler_params=_cparams(1),
        name="hyena_longconv",
    )(zbf, z, mult, fwd_bf, fwd_bf, inv_bf, inv_bf, *planes, dbias_o.reshape(1, HY_CH))


def _hyena(pb, seq0, n_seq, L, dft, hy):
    conv_w, w1, b1, w2, b2, w3, b3, freq, decay, dbias = hy
    fwd_bf, inv_bf = dft
    tk = min(L, HY_SPECTRUM_TILE)
    planes = _hy_filter_spectra(L, fwd_bf, w1, b1, w2, b2, w3, b3, freq, decay, tk=tk)
    x1, x2, v, vbf = _hy_pre(pb, conv_w, seq0, n_seq, L)
    z, zbf = _hy_longconv(vbf, v, x1, fwd_bf, inv_bf, planes, 0, dbias[0], (f32, bf16), n_seq, L, tk)
    (ob,) = _hy_longconv(zbf, z, x2, fwd_bf, inv_bf, planes, 1, dbias[1], (bf16,), n_seq, L, tk)
    return ob


def _pool_kernel(u_ref, w_ref, sc_ref, o_ref):
    u = u_ref[...]
    L = u.shape[0]
    back = _shift_rows(u, 1)
    fwd = u
    sums = [back + fwd]
    for k in (1, 2, 4):
        back = back + _shift_rows(back, k)
        fwd = fwd + _shift_rows(fwd, -k)
        sums.append(back + fwd)
    a2, a4, a8, a16 = sums
    grp = _lane_group(u.shape, POOL_GC)
    t = lax.broadcasted_iota(jnp.int32, u.shape, 0)
    half = jnp.left_shift(1, grp)
    cnt = jnp.minimum(t + half, L) - jnp.maximum(t - half, 0)
    tot = jnp.where(grp == 0, a2, jnp.where(grp == 1, a4, jnp.where(grp == 2, a8, a16)))
    pooled = tot / cnt.astype(f32) - u
    y = jnp.dot(pooled.astype(bf16), w_ref[...].astype(bf16), preferred_element_type=f32)
    o_ref[...] = (y * sc_ref[...]).astype(o_ref.dtype)


def _pool(pd, w_bd, scale, seq0, n_seq, L):
    return pl.pallas_call(
        _pool_kernel,
        out_shape=jax.ShapeDtypeStruct((n_seq * L, GROUP_W), bf16),
        grid=(n_seq,),
        in_specs=[pl.BlockSpec((L, GROUP_W), lambda b: (seq0 + b, 0)),
                  pl.BlockSpec((GROUP_W, GROUP_W), lambda b: (0, 0)),
                  pl.BlockSpec((1, GROUP_W), lambda b: (0, 0))],
        out_specs=pl.BlockSpec((L, GROUP_W), lambda b: (b, 0)),
        compiler_params=_cparams(1),
        name="pool_mixer",
    )(pd, w_bd, scale.reshape(1, GROUP_W))


def _outproj_kernel(*refs, tm, moe, split_x):
    mix_refs = refs[:8]
    n_x = 2 if split_x else 1
    x_refs = refs[8:8 + n_x]
    if moe:
        mod_ref, g_ref, w_ref, r_ref, x1_ref, h2_ref, lg_ref, wbf_ref = refs[8 + n_x:]
    else:
        mod_ref, g_ref, w_ref, x1_ref, h2_ref, wbf_ref = refs[8 + n_x:]
    i = pl.program_id(0)

    @pl.when(i == 0)
    def _():
        wbf_ref[...] = w_ref[0].astype(bf16)

    row = _mod_row(i, tm)
    d = D_MODEL
    gate1 = mod_ref[pl.ds(row, 1), 2 * d:3 * d]
    shift2 = mod_ref[pl.ds(row, 1), 3 * d:4 * d]
    scale2 = mod_ref[pl.ds(row, 1), 4 * d:5 * d]
    is_prompt = i < N_PROMPT_TOK // tm
    mixed = jnp.concatenate(
        [jnp.where(is_prompt, mix_refs[2 * j][...], mix_refs[2 * j + 1][...]) for j in range(4)], axis=-1)
    mix = jnp.dot(mixed, wbf_ref[...], preferred_element_type=f32)
    x = jnp.where(is_prompt, x_refs[0][...], x_refs[1][...]) if split_x else x_refs[0][...]
    x1 = x + gate1 * mix
    x1_ref[...] = x1
    h = _modulated_norm(x1, g_ref[...], shift2, scale2)
    h2_ref[...] = h.astype(h2_ref.dtype)
    if moe:
        lg_ref[...] = _dot_bf16x3(h, r_ref[...])


def _outproj(mixers, x, layer, mod_l, g, w_out, router_l=None, tm=512):
    moe = router_l is not None
    npb = N_PROMPT_TOK // tm
    tok = lambda w: pl.BlockSpec((tm, w), lambda i: (i, 0))
    in_specs, args = [], []
    for op, os_ in mixers:
        in_specs.append(pl.BlockSpec((tm, GROUP_W), lambda i: (jnp.minimum(i, npb - 1), 0)))
        in_specs.append(pl.BlockSpec((tm, GROUP_W), lambda i: (jnp.maximum(i - npb, 0), 0)))
        args += [op, os_]
    split_x = isinstance(x, tuple)
    if split_x:
        in_specs += [pl.BlockSpec((tm, D_MODEL), lambda i: (jnp.minimum(i, npb - 1), 0)),
                     pl.BlockSpec((tm, D_MODEL), lambda i: (jnp.maximum(i - npb, 0), 0))]
        args += list(x)
    else:
        in_specs.append(tok(D_MODEL))
        args.append(x)
    in_specs += [pl.BlockSpec((MOD_ROWS, 6 * D_MODEL), lambda i: (0, 0)),
                 pl.BlockSpec((1, D_MODEL), lambda i: (0, 0)),
                 pl.BlockSpec((1, D_MODEL, D_MODEL), lambda i: (layer, 0, 0))]
    args += [mod_l, g.reshape(1, D_MODEL), w_out]
    out_shape = [jax.ShapeDtypeStruct((N_TOK, D_MODEL), f32),
                 jax.ShapeDtypeStruct((N_TOK, D_MODEL), f32 if moe else bf16)]
    out_specs = [tok(D_MODEL), tok(D_MODEL)]
    if moe:
        in_specs.append(pl.BlockSpec((D_MODEL, ROUTER_PAD), lambda i: (0, 0)))
        args.append(jnp.pad(router_l, ((0, 0), (0, ROUTER_PAD - N_EXPERTS))))
        out_shape.append(jax.ShapeDtypeStruct((N_TOK, ROUTER_PAD), f32))
        out_specs.append(tok(ROUTER_PAD))
    return pl.pallas_call(
        functools.partial(_outproj_kernel, tm=tm, moe=moe, split_x=split_x),
        out_shape=tuple(out_shape),
        grid=(N_TOK // tm,),
        in_specs=in_specs,
        out_specs=tuple(out_specs),
        scratch_shapes=[pltpu.VMEM((D_MODEL, D_MODEL), bf16)],
        compiler_params=_cparams(1),
        name="outproj_norm2",
    )(*args)


def _ffn_kernel(be_ref, nv_ref, idx_ref, idx_next_ref, src_ref, w1_ref, w3_ref, w2_ref, o_ref,
                xbf_ref, xbuf_ref, sem, *, tm, nf):
    i = pl.program_id(0)
    j = pl.program_id(1)
    last = pl.num_programs(0) - 1
    live = i < nv_ref[0]
    slot = i % 2
    share = (tm // nf) // 8 * 8
    head = tm - nf * share

    @pl.when(j == 0)
    def _():
        o_ref[...] = jnp.zeros_like(o_ref)

    @pl.when((j == 0) & (i == 0))
    def _():
        _start_rows(idx_ref, src_ref, xbuf_ref, 0, sem, tm)

    @pl.when((j == 0) & (i <= nv_ref[0]))
    def _():
        _wait_rows(xbuf_ref, slot, sem)

    @pl.when((j == 0) & live)
    def _():
        _start_rows(idx_next_ref, src_ref, xbuf_ref, 1 - slot, sem, head)
        xbf_ref[...] = xbuf_ref[slot].astype(bf16)

    @pl.when(live)
    def _():
        first = head + j * share
        for u in range(share):
            _row_copy(src_ref, idx_next_ref[0, 0, first + u], xbuf_ref, 1 - slot, first + u, 0, sem).start(
                priority=ROW_DMA_PRIORITY)
        x = xbf_ref[...]
        h1 = jnp.dot(x, w1_ref[0].astype(bf16), preferred_element_type=f32)
        h3 = jnp.dot(x, w3_ref[0].astype(bf16), preferred_element_type=f32)
        a = (h1 * jax.nn.sigmoid(h1)) * h3
        o_ref[...] += jnp.dot(a.astype(bf16), w2_ref[0].astype(bf16), preferred_element_type=f32)

    @pl.when(live & (i == last) & (j == nf - 1))
    def _():
        _wait_rows(xbuf_ref, 1 - slot, sem)


def _ffn_routed(row_src, row_token, w1, w3, w2, blk_expert, n_valid, tm, tf):
    n_rows = row_token.shape[0]
    ffn = w1.shape[-1]
    nf = ffn // tf
    nblk = n_rows // tm
    assert nf * tf == ffn and nblk * tm == n_rows

    def wcol(i, j, be, nv):
        return (be[i], 0, jnp.where(i < nv[0], j, nf - 1))

    def wrow(i, j, be, nv):
        return (be[i], jnp.where(i < nv[0], j, nf - 1), 0)

    idx3 = row_token.reshape(nblk, 1, tm)
    return pl.pallas_call(
        functools.partial(_ffn_kernel, tm=tm, nf=nf),
        out_shape=jax.ShapeDtypeStruct((n_rows, D_MODEL), f32),
        grid_spec=pltpu.PrefetchScalarGridSpec(
            num_scalar_prefetch=2,
            grid=(nblk, nf),
            in_specs=[
                pl.BlockSpec((1, 1, tm), lambda i, j, be, nv: (i, 0, 0), memory_space=pltpu.SMEM),
                pl.BlockSpec((1, 1, tm), lambda i, j, be, nv: (jnp.minimum(i + 1, nblk - 1), 0, 0),
                             memory_space=pltpu.SMEM),
                pl.BlockSpec(memory_space=pl.ANY),
                pl.BlockSpec((1, D_MODEL, tf), wcol),
                pl.BlockSpec((1, D_MODEL, tf), wcol),
                pl.BlockSpec((1, tf, D_MODEL), wrow),
            ],
            out_specs=pl.BlockSpec((tm, D_MODEL), lambda i, j, be, nv: (i, 0)),
            scratch_shapes=[pltpu.VMEM((tm, D_MODEL), bf16), pltpu.VMEM((2, tm, D_MODEL), row_src.dtype),
                            pltpu.SemaphoreType.DMA((2,))],
        ),
        compiler_params=_cparams(2),
        name="swiglu_routed",
    )(blk_expert, n_valid, idx3, idx3, row_src, w1, w3, w2)


def _cast_kernel(a_ref, o_ref):
    o_ref[...] = a_ref[...].astype(o_ref.dtype)


def _to_bf16(a, block_rows):
    rows, cols = a.shape
    assert rows % block_rows == 0
    return pl.pallas_call(
        _cast_kernel,
        out_shape=jax.ShapeDtypeStruct(a.shape, bf16),
        grid=(rows // block_rows,),
        in_specs=[pl.BlockSpec((block_rows, cols), lambda i: (i, 0))],
        out_specs=pl.BlockSpec((block_rows, cols), lambda i: (i, 0)),
        compiler_params=_cparams(1),
        name="cast_bf16",
    )(a)


def _ffn_dense_kernel(x_ref, w1_ref, w3_ref, w2_ref, x1_ref, mod_ref, o_ref, *, tm, chunks):
    i = pl.program_id(0)
    x = x_ref[...]
    acc = jnp.zeros((tm, D_MODEL), f32)
    c0 = 0
    for width in chunks:
        h1 = jnp.dot(x, w1_ref[:, c0:c0 + width], preferred_element_type=f32)
        h3 = jnp.dot(x, w3_ref[:, c0:c0 + width], preferred_element_type=f32)
        a = (h1 * jax.nn.sigmoid(h1)) * h3
        acc = acc + jnp.dot(a.astype(bf16), w2_ref[c0:c0 + width, :], preferred_element_type=f32)
        c0 += width
    gate2 = mod_ref[pl.ds(_mod_row(i, tm), 1), 5 * D_MODEL:6 * D_MODEL]
    o_ref[...] = x1_ref[...] + gate2 * acc


def _ffn_dense(h2, w1, w3, w2, x1, mod_l, tm=1024, chunk=512):
    ffn = w1.shape[1]
    chunks = [chunk] * (ffn // chunk) + ([ffn % chunk] if ffn % chunk else [])
    w1b, w3b, w2b = _to_bf16(w1, 256), _to_bf16(w3, 256), _to_bf16(w2, 256)
    tok = lambda dt: pl.BlockSpec((tm, D_MODEL), lambda i: (i, 0))
    full = lambda a: pl.BlockSpec(a.shape, lambda i: (0, 0))
    return pl.pallas_call(
        functools.partial(_ffn_dense_kernel, tm=tm, chunks=tuple(chunks)),
        out_shape=jax.ShapeDtypeStruct((N_TOK, D_MODEL), f32),
        grid=(N_TOK // tm,),
        in_specs=[tok(bf16), full(w1b), full(w3b), full(w2b), tok(f32),
                  pl.BlockSpec((MOD_ROWS, 6 * D_MODEL), lambda i: (0, 0))],
        out_specs=tok(f32),
        compiler_params=_cparams(1),
        name="swiglu_dense",
    )(h2, w1b, w3b, w2b, x1, mod_l)


def _row_copy(src_ref, src_row, buf_ref, slot, rr, k, sem):
    w = src_ref.shape[1]
    return pltpu.make_async_copy(src_ref.at[pl.ds(src_row, 1), :],
                                 buf_ref.at[slot, pl.ds(rr, 1), pl.ds(k * w, w)], sem.at[slot])


def _start_rows(idx_ref, src_ref, buf_ref, slot, sem, n, pack=1):
    def body(rr, carry):
        for k in range(pack):
            _row_copy(src_ref, idx_ref[0, 0, rr * pack + k], buf_ref, slot, rr, k, sem).start(
                priority=ROW_DMA_PRIORITY)
        return carry

    lax.fori_loop(0, n // pack, body, 0, unroll=8)


def _wait_rows(buf_ref, slot, sem):
    pltpu.make_async_copy(buf_ref.at[slot], buf_ref.at[slot], sem.at[slot]).wait()


def _combine_kernel(*refs, tm, final):
    if final:
        idx_ref, idx_next_ref, ys_ref, x1_ref, mod_ref, gt_ref, g_ref, op_ref, os_ref, ybuf_ref, sem = refs
    else:
        idx_ref, idx_next_ref, ys_ref, x1_ref, mod_ref, gt_ref, o_ref, ybuf_ref, sem = refs
    i = pl.program_id(0)
    last = pl.num_programs(0) - 1
    slot = i % 2
    n_rows = tm * TOP_K

    @pl.when(i == 0)
    def _():
        _start_rows(idx_ref, ys_ref, ybuf_ref, 0, sem, n_rows, TOP_K)

    _wait_rows(ybuf_ref, slot, sem)

    for r in range(n_rows):
        _row_copy(ys_ref, idx_next_ref[0, 0, r], ybuf_ref, 1 - slot, r // TOP_K, r % TOP_K, sem).start(
            priority=ROW_DMA_PRIORITY)

    gate2 = mod_ref[pl.ds(_mod_row(i, tm), 1), 5 * D_MODEL:6 * D_MODEL]
    gt = gt_ref[...]
    f = gt[:, 0:1] * ybuf_ref[slot, :, :D_MODEL] + gt[:, 1:2] * ybuf_ref[slot, :, D_MODEL:]
    x2 = x1_ref[...] + gate2 * f
    if final:
        ms = jnp.mean(x2 * x2, axis=-1, keepdims=True)
        y = x2 * lax.rsqrt(ms + EPS) * g_ref[...]

        @pl.when(i < N_PROMPT_TOK // tm)
        def _():
            op_ref[...] = y

        @pl.when(i >= N_PROMPT_TOK // tm)
        def _():
            os_ref[...] = y
    else:
        o_ref[...] = x2

    @pl.when(i == last)
    def _():
        _wait_rows(ybuf_ref, 1 - slot, sem)


def _combine(x1, mod_l, ys, slots, gates, final_g=None, tm=512):
    final = final_g is not None
    npb = N_PROMPT_TOK // tm
    nblk = N_TOK // tm
    tok = pl.BlockSpec((tm, D_MODEL), lambda i: (i, 0))
    idx3 = slots.reshape(nblk, 1, tm * TOP_K)
    in_specs = [pl.BlockSpec((1, 1, tm * TOP_K), lambda i: (i, 0, 0), memory_space=pltpu.SMEM),
                pl.BlockSpec((1, 1, tm * TOP_K), lambda i: (jnp.minimum(i + 1, nblk - 1), 0, 0),
                             memory_space=pltpu.SMEM),
                pl.BlockSpec(memory_space=pl.ANY),
                tok, pl.BlockSpec((MOD_ROWS, 6 * D_MODEL), lambda i: (0, 0)),
                pl.BlockSpec((tm, TOP_K), lambda i: (i, 0))]
    args = [idx3, idx3, ys, x1, mod_l, gates]
    if final:
        in_specs.append(pl.BlockSpec((1, D_MODEL), lambda i: (0, 0)))
        args.append(final_g.reshape(1, D_MODEL))
        out_shape = (jax.ShapeDtypeStruct((N_PROMPT_TOK, D_MODEL), f32),
                     jax.ShapeDtypeStruct((N_SAMPLE_TOK, D_MODEL), f32))
        out_specs = (pl.BlockSpec((tm, D_MODEL), lambda i: (jnp.minimum(i, npb - 1), 0)),
                     pl.BlockSpec((tm, D_MODEL), lambda i: (jnp.maximum(i - npb, 0), 0)))
    else:
        out_shape = jax.ShapeDtypeStruct((N_TOK, D_MODEL), f32)
        out_specs = tok
    return pl.pallas_call(
        functools.partial(_combine_kernel, tm=tm, final=final),
        out_shape=out_shape,
        grid=(N_TOK // tm,),
        in_specs=in_specs,
        out_specs=out_specs,
        scratch_shapes=[pltpu.VMEM((2, tm, TOP_K * D_MODEL), ys.dtype), pltpu.SemaphoreType.DMA((2,))],
        compiler_params=_cparams(1),
        name="expert_combine",
    )(*args)


def _route(logits, tm):
    eid = jnp.arange(N_EXPERTS, dtype=jnp.int32)[None, :]
    v0 = jnp.max(logits, axis=-1, keepdims=True)
    i0 = jnp.min(jnp.where(logits == v0, eid, N_EXPERTS), axis=-1, keepdims=True)
    rest = jnp.where(eid == i0, -jnp.inf, logits)
    v1 = jnp.max(rest, axis=-1, keepdims=True)
    i1 = jnp.min(jnp.where(rest == v1, eid, N_EXPERTS), axis=-1, keepdims=True)
    gates = jax.nn.softmax(jnp.concatenate([v0, v1], axis=-1), axis=-1)
    flat_e = jnp.concatenate([i0, i1], axis=-1).reshape(-1)
    onehot = (flat_e[:, None] == jnp.arange(N_EXPERTS)[None, :]).astype(jnp.int32)
    csum = jnp.cumsum(onehot, axis=0)
    rank = jnp.take_along_axis(csum, flat_e[:, None], axis=1)[:, 0] - 1
    counts = csum[-1]
    padded = ((counts + tm - 1) // tm) * tm
    pend = jnp.cumsum(padded)
    pstart = pend - padded
    dest = pstart[flat_e] + rank
    n_rows = N_TOK * TOP_K + N_EXPERTS * tm
    tok_of = jnp.arange(N_TOK * TOP_K, dtype=jnp.int32) // TOP_K
    row_token = jnp.zeros((n_rows,), jnp.int32).at[dest].set(tok_of, unique_indices=True)
    blk_start = jnp.arange(n_rows // tm, dtype=jnp.int32) * tm
    blk_expert = jnp.minimum(jnp.sum((blk_start[:, None] >= pend[None, :]).astype(jnp.int32), axis=1),
                             N_EXPERTS - 1)
    n_valid = (pend[-1] // tm).astype(jnp.int32).reshape(1)
    blk_expert = jnp.where(blk_start < pend[-1], blk_expert, blk_expert[jnp.maximum(n_valid[0] - 1, 0)])
    return row_token, blk_expert, n_valid, dest.reshape(N_TOK, TOP_K), gates


def _pool_weight(pool_w_l):
    w = jnp.zeros((GROUP_W, GROUP_W), f32)
    for g in range(len(POOL_WINDOWS)):
        w = w.at[g * POOL_GC:(g + 1) * POOL_GC, g * POOL_GC:(g + 1) * POOL_GC].set(pool_w_l[g])
    return w


def kernel(x_prompt, x_sample, cache_diff_k, cache_diff_v, cache_na_k, cache_na_v, c, c_ctx,
           norm1_g, norm2_g, final_g, ada_w, ada_b, w_in, w_out, diff_lam, diff_subln_g,
           hy_conv, hy_w1, hy_b1, hy_w2, hy_b2, hy_w3, hy_b3, hy_freq, hy_decay, hy_dbias,
           na_rpb, pool_w, pool_scale, ffn_w1, ffn_w3, ffn_w2,
           moe_router, moe_w1, moe_w3, moe_w2):
    cond = jnp.concatenate([c, c_ctx[None, :], jnp.zeros((MOD_ROWS - DEC_BATCH - 1, D_MODEL), f32)], axis=0)
    mod = _ada_mod(cond, ada_w, ada_b)
    cos_np, sin_np = _rope_tables()
    cos_t, sin_t = jnp.asarray(cos_np), jnp.asarray(sin_np)
    dft = {L: tuple(jnp.asarray(m, dtype=bf16) for m in _dft_matrices(L)) for L in (SEQ, DEC_SEQ)}

    P = N_PROMPT_TOK
    x = (x_prompt.reshape(P, D_MODEL), x_sample.reshape(N_SAMPLE_TOK, D_MODEL))
    new_kv = None
    for l in range(DEPTH):
        pa, pb, pc, pd, *new_kv = _inproj(x, l, new_kv, mod[l], norm1_g[l], w_in, cos_t, sin_t)

        lam_init = 0.8 - 0.6 * math.exp(-0.3 * l)
        lv = diff_lam[l].astype(f32)
        lam = (jnp.exp(jnp.sum(lv[0] * lv[1])) - jnp.exp(jnp.sum(lv[2] * lv[3])) + lam_init).reshape(1)
        gain = jnp.tile(diff_subln_g[l], A_HEADS).reshape(1, GROUP_W)
        a_kw = dict(n_heads=A_HEADS, diff=True, scale=A_DQK ** -0.5, lam=lam, gain=gain, out_scale=1.0 - lam_init)
        oa_p = _attention(pa, 0, 0, pa, 1, 0, pa, 2, 0, n_seq=BATCH, seq_len=SEQ, n_keys=SEQ, tq=SEQ, **a_kw)
        s0 = P // DEC_SEQ
        cache_a = (cache_diff_k[:, l].reshape(-1, GROUP_W), cache_diff_v[:, l].reshape(-1, GROUP_W))
        oa_s = _attention(pa, 0, s0, pa, 1, s0, pa, 2, s0, n_seq=DEC_BATCH, seq_len=DEC_SEQ, n_keys=DEC_SEQ,
                          tq=256, cache_kv=cache_a, **a_kw)

        hy = (hy_conv[l], hy_w1[l], hy_b1[l], hy_w2[l], hy_b2[l], hy_w3[l], hy_b3[l],
              hy_freq[l], hy_decay[l], hy_dbias[l])
        ob_p = _hyena(pb, 0, BATCH, SEQ, dft[SEQ], hy)
        ob_s = _hyena(pb, P // DEC_SEQ, DEC_BATCH, DEC_SEQ, dft[DEC_SEQ], hy)

        oc_p = _attention(pc, 0, 0, pc, 1, 0, pc, 2, 0, n_seq=BATCH, seq_len=SEQ, n_keys=SEQ, tq=SEQ,
                          n_heads=C_HEADS, diff=False, scale=C_DH ** -0.5)
        oc_s = _na_attention(pc, cache_na_k[:, l].reshape(-1, GROUP_W), cache_na_v[:, l].reshape(-1, GROUP_W),
                             _na_bias_tables(na_rpb[l]))

        w_bd = _pool_weight(pool_w[l])
        od_p = _pool(pd, w_bd, pool_scale[l], 0, BATCH, SEQ)
        od_s = _pool(pd, w_bd, pool_scale[l], P // DEC_SEQ, DEC_BATCH, DEC_SEQ)

        mixers = [(oa_p, oa_s), (ob_p, ob_s), (oc_p, oc_s), (od_p, od_s)]
        tm = 1024
        if l % 2 == 0:
            assert l != DEPTH - 1
            x1, h2 = _outproj(mixers, x, l, mod[l], norm2_g[l], w_out)
            x = _ffn_dense(h2, ffn_w1[l // 2], ffn_w3[l // 2], ffn_w2[l // 2], x1, mod[l])
        else:
            x1, h2, logits = _outproj(mixers, x, l, mod[l], norm2_g[l], w_out, moe_router[l // 2])
            row_token, blk_expert, n_valid, dest, gates = _route(logits[:, :N_EXPERTS], tm)
            ys = _ffn_routed(h2, row_token, moe_w1[l // 2], moe_w3[l // 2], moe_w2[l // 2], blk_expert, n_valid,
                             tm=tm, tf=512)
            x = _combine(x1, mod[l], ys, dest, gates, final_g if l == DEPTH - 1 else None)

    assert isinstance(x, tuple)
    y_prompt = x[0].reshape(BATCH, SEQ, D_MODEL)
    y_sample = x[1].reshape(DEC_BATCH, DEC_SEQ, D_MODEL)
    kak, kav, kck, kcv = new_kv
    return (y_prompt, y_sample,
            kak.reshape(BATCH, DEPTH, SEQ, A_HEADS, 2 * A_DQK), kav.reshape(BATCH, DEPTH, SEQ, A_HEADS, A_DV),
            kck.reshape(BATCH, DEPTH, SEQ, C_HEADS, C_DH), kcv.reshape(BATCH, DEPTH, SEQ, C_HEADS, C_DH))
```
